```python
import math
import jax, jax.numpy as jnp
from jax import lax
import numpy as np

D_MODEL = 2048
BATCH = 4
SEQ = 2048
DEPTH = 1

D_SSM = D_MODEL // 2
D_ATTN = D_MODEL - D_SSM
SSM_CH = 16
SSM_GROUPS = D_SSM // SSM_CH
SSM_STATE = 64
ATTN_HEAD_DIM = 64
N_HEADS = D_ATTN // ATTN_HEAD_DIM
DILATED_PATTERNS = ((128, 1), (512, 4), (2048, 16))
ATTN_BLOCK = 128
N_EXPERTS = 64
TOP_K = 8
N_EXPERT_GROUPS = 8
TOPK_GROUPS = 4
D_EXPERT = 512
ROUTED_SCALE = 2.5
MOE_BLOCK = 128
DT_MIN = 0.001
DT_MAX = 0.1
NORM_EPS = 1e-5
DEEPNORM_ALPHA = (2.0 * DEPTH) ** 0.25
DEEPNORM_BETA = (8.0 * DEPTH) ** -0.25

kernel_name = 'hymba_s5_longnet_alibi_deepnorm_moe'


def _layer_norm(x, g, b):
    xf = x.astype(jnp.float32)
    mu = jnp.mean(xf, axis=-1, keepdims=True)
    var = jnp.mean(jnp.square(xf - mu), axis=-1, keepdims=True)
    return ((xf - mu) * lax.rsqrt(var + NORM_EPS) * g.astype(jnp.float32) + b.astype(jnp.float32)).astype(x.dtype)


def _rms_norm(x, g):
    xf = x.astype(jnp.float32)
    return (xf * lax.rsqrt(jnp.mean(jnp.square(xf), axis=-1, keepdims=True) + NORM_EPS) * g.astype(jnp.float32)).astype(x.dtype)


def _s5_mixer(u, log_dt, a_re, a_im, b_re, b_im, c_re, c_im, d_skip, w_glu):
    bsz, seq, _ = u.shape
    f32 = jnp.float32
    uf = u.astype(f32).reshape(bsz, seq, SSM_GROUPS, SSM_CH)
    lam = lax.complex(jnp.minimum(a_re.astype(f32), -1e-4), a_im.astype(f32))
    dt = jnp.exp(log_dt.astype(f32))
    a_bar = jnp.exp(lam * dt)
    b_c = lax.complex(b_re.astype(f32), b_im.astype(f32))
    b_bar = ((a_bar - 1.0) / lam)[..., None] * b_c
    bu = jnp.einsum('bsgc,gpc->bsgp', uf.astype(jnp.complex64), b_bar)
    a_seq = jnp.broadcast_to(a_bar, (1, seq, SSM_GROUPS, SSM_STATE))

    def combine(e1, e2):
        a1, b1 = e1
        a2, b2 = e2
        return a2 * a1, a2 * b1 + b2

    _, h = lax.associative_scan(combine, (a_seq, bu), axis=1)
    c_c = lax.complex(c_re.astype(f32), c_im.astype(f32))
    y = jnp.einsum('bsgp,gcp->bsgc', h, c_c).real + d_skip.astype(f32) * uf
    y = y.reshape(bsz, seq, D_SSM).astype(u.dtype)
    z = jax.nn.gelu(y) @ w_glu
    val, gate = jnp.split(z, 2, axis=-1)
    return val * jax.nn.sigmoid(gate)


def _dilated_window_attention(q, k, v, slopes, window, dilation):
    bsz, seq, n_h, dh = q.shape
    steps = window // dilation
    sub_len = seq // dilation
    nb = -(-sub_len // ATTN_BLOCK)
    sub_pad = nb * ATTN_BLOCK

    def to_sub(t):
        t = t.reshape(bsz, sub_len, dilation, n_h, dh).transpose(0, 2, 3, 1, 4)
        t = jnp.pad(t, ((0, 0), (0, 0), (0, 0), (0, sub_pad - sub_len), (0, 0)))
        return t.reshape(bsz, dilation, n_h, nb, ATTN_BLOCK, dh)

    def with_prev(t):
        prev = jnp.pad(t[:, :, :, :-1], ((0, 0), (0, 0), (0, 0), (1, 0), (0, 0), (0, 0)))
        return jnp.concatenate([prev, t], axis=4)

    qb = to_sub(q)
    kk = with_prev(to_sub(k))
    vv = with_prev(to_sub(v))
    s = jnp.einsum('bdhnqc,bdhnkc->bdhnqk', qb, kk) * (dh ** -0.5)
    blk_ids = jnp.arange(nb)[:, None, None] * ATTN_BLOCK
    q_idx = blk_ids + jnp.arange(ATTN_BLOCK)[None, :, None]
    k_idx = blk_ids - ATTN_BLOCK + jnp.arange(2 * ATTN_BLOCK)[None, None, :]
    delta = q_idx - k_idx
    valid = (delta >= 0) & (delta <= steps) & (k_idx >= 0)
    dist = (delta * dilation).astype(jnp.float32)
    alibi = -slopes[:, None, None, None] * dist
    s = jnp.where(valid, s + alibi, -jnp.inf)
    lse = jax.nn.logsumexp(s, axis=-1)
    o = jnp.einsum('bdhnqk,bdhnkc->bdhnqc', jnp.exp(s - lse[..., None]), vv)

    def from_sub(t, tail):
        t = t.reshape((bsz, dilation, n_h, sub_pad) + tail)[:, :, :, :sub_len]
        t = jnp.moveaxis(t, 3, 1)
        return t.reshape((bsz, seq, n_h) + tail)

    return from_sub(o, (dh,)), from_sub(lse, ())


def _attention_mixer(q, k, v, slopes):
    bsz, seq, _ = q.shape
    shp = (bsz, seq, N_HEADS, ATTN_HEAD_DIM)
    qf, kf, vf = (t.astype(jnp.float32).reshape(shp) for t in (q, k, v))
    outs, lses = [], []
    for window, dilation in DILATED_PATTERNS:
        o, l = _dilated_window_attention(qf, kf, vf, slopes, window, dilation)
        outs.append(o)
        lses.append(l)
    wts = jax.nn.softmax(jnp.stack(lses), axis=0)
    out = jnp.sum(wts[..., None] * jnp.stack(outs), axis=0)
    return out.reshape(bsz, seq, D_ATTN).astype(q.dtype)


def _moe(h, w_router, router_bias, w_gate, w_up, w_down, ws_gate, ws_up, ws_down):
    bsz, seq, d = h.shape
    n_tok = bsz * seq
    xf = h.reshape(n_tok, d)
    scores = jax.nn.sigmoid(xf.astype(jnp.float32) @ w_router.astype(jnp.float32))
    sel = scores + router_bias.astype(jnp.float32)
    grouped = sel.reshape(n_tok, N_EXPERT_GROUPS, N_EXPERTS // N_EXPERT_GROUPS)
    group_score = jnp.sum(lax.top_k(grouped, 2)[0], axis=-1)
    _, top_groups = lax.top_k(group_score, TOPK_GROUPS)
    group_keep = jnp.any(top_groups[..., None] == jnp.arange(N_EXPERT_GROUPS), axis=1)
    expert_keep = jnp.repeat(group_keep, N_EXPERTS // N_EXPERT_GROUPS, axis=1)
    _, top_idx = lax.top_k(jnp.where(expert_keep, sel, -jnp.inf), TOP_K)
    top_w = jnp.take_along_axis(scores, top_idx, axis=1)
    top_w = top_w / jnp.sum(top_w, axis=-1, keepdims=True) * ROUTED_SCALE

    nk = n_tok * TOP_K
    flat_e = top_idx.reshape(nk)
    flat_tok = jnp.repeat(jnp.arange(n_tok, dtype=jnp.int32), TOP_K)
    flat_w = top_w.reshape(nk)
    order = jnp.argsort(flat_e)
    se, stok, sw = flat_e[order], flat_tok[order], flat_w[order]
    counts = jnp.bincount(flat_e, length=N_EXPERTS)
    padded = (counts + MOE_BLOCK - 1) // MOE_BLOCK * MOE_BLOCK
    pad_end = jnp.cumsum(padded)
    pad_start = pad_end - padded
    start = jnp.cumsum(counts) - counts
    dest = pad_start[se] + jnp.arange(nk) - start[se]
    n_blocks = -(-(nk + N_EXPERTS * (MOE_BLOCK - 1)) // MOE_BLOCK)
    n_rows = n_blocks * MOE_BLOCK
    row_tok = jnp.zeros((n_rows,), jnp.int32).at[dest].set(stok)
    row_w = jnp.zeros((n_rows,), jnp.float32).at[dest].set(sw)
    block_e = jnp.minimum(jnp.searchsorted(pad_end, jnp.arange(n_blocks) * MOE_BLOCK, side='right'), N_EXPERTS - 1)

    def expert_block(args):
        tok_blk, e, w_blk = args
        xb = xf[tok_blk]
        yb = (jax.nn.silu(xb @ w_gate[e]) * (xb @ w_up[e])) @ w_down[e]
        return yb * w_blk[:, None].astype(yb.dtype)

    yb = lax.map(expert_block, (row_tok.reshape(n_blocks, MOE_BLOCK), block_e, row_w.reshape(n_blocks, MOE_BLOCK)))
    routed = jax.ops.segment_sum(yb.reshape(n_rows, d), row_tok, num_segments=n_tok)
    shared = (jax.nn.silu(xf @ ws_gate) * (xf @ ws_up)) @ ws_down
    return (routed + shared).reshape(bsz, seq, d)


def setup_inputs(seed: int = 0) -> dict:
    key = jax.random.key(seed)
    ks = jax.random.split(key, 32)
    L = DEPTH
    f32 = jnp.float32

    def nrm(k, shape, scale):
        return jax.random.normal(k, shape, f32) * scale

    x = nrm(ks[0], (BATCH, SEQ, D_MODEL), 1.0)
    w_in = nrm(ks[1], (L, D_MODEL, D_SSM + 3 * D_ATTN), D_MODEL ** -0.5)
    w_in = w_in.at[:, :, D_SSM + 2 * D_ATTN:].multiply(DEEPNORM_BETA)
    u01 = jax.random.uniform(ks[2], (L, SSM_GROUPS, SSM_STATE), f32)
    ssm_log_dt = math.log(DT_MIN) + u01 * (math.log(DT_MAX) - math.log(DT_MIN))
    ssm_a_re = -0.5 + nrm(ks[3], (L, SSM_GROUPS, SSM_STATE), 0.01)
    ssm_a_im = jnp.pi * jnp.arange(SSM_STATE, dtype=f32) + nrm(ks[4], (L, SSM_GROUPS, SSM_STATE), 0.01)
    ssm_b_re = nrm(ks[5], (L, SSM_GROUPS, SSM_STATE, SSM_CH), (2.0 * SSM_CH) ** -0.5)
    ssm_b_im = nrm(ks[6], (L, SSM_GROUPS, SSM_STATE, SSM_CH), (2.0 * SSM_CH) ** -0.5)
    ssm_c_re = nrm(ks[7], (L, SSM_GROUPS, SSM_CH, SSM_STATE), (2.0 * SSM_STATE) ** -0.5)
    ssm_c_im = nrm(ks[8], (L, SSM_GROUPS, SSM_CH, SSM_STATE), (2.0 * SSM_STATE) ** -0.5)
    ssm_d = nrm(ks[9], (L, SSM_GROUPS, SSM_CH), 1.0)
    w_glu = nrm(ks[10], (L, D_SSM, 2 * D_SSM), D_SSM ** -0.5)
    g_ssm_out = 1.0 + nrm(ks[11], (L, D_SSM), 0.02)
    g_attn_out = 1.0 + nrm(ks[12], (L, D_ATTN), 0.02)
    w_out = nrm(ks[13], (L, D_SSM + D_ATTN, D_MODEL), (D_SSM + D_ATTN) ** -0.5 * DEEPNORM_BETA)
    ln1_g = 1.0 + nrm(ks[14], (L, D_MODEL), 0.02)
    ln1_b = nrm(ks[15], (L, D_MODEL), 0.02)
    w_router = nrm(ks[16], (L, D_MODEL, N_EXPERTS), D_MODEL ** -0.5)
    router_bias = nrm(ks[17], (L, N_EXPERTS), 0.01)
    w_gate = nrm(ks[18], (L, N_EXPERTS, D_MODEL, D_EXPERT), D_MODEL ** -0.5)
    w_up = nrm(ks[19], (L, N_EXPERTS, D_MODEL, D_EXPERT), D_MODEL ** -0.5)
    w_down = nrm(ks[20], (L, N_EXPERTS, D_EXPERT, D_MODEL), D_EXPERT ** -0.5 * DEEPNORM_BETA)
    ws_gate = nrm(ks[21], (L, D_MODEL, D_EXPERT), D_MODEL ** -0.5)
    ws_up = nrm(ks[22], (L, D_MODEL, D_EXPERT), D_MODEL ** -0.5)
    ws_down = nrm(ks[23], (L, D_EXPERT, D_MODEL), D_EXPERT ** -0.5 * DEEPNORM_BETA)
    ln2_g = 1.0 + nrm(ks[24], (L, D_MODEL), 0.02)
    ln2_b = nrm(ks[25], (L, D_MODEL), 0.02)
    return {'x': x, 'w_in': w_in, 'ssm_log_dt': ssm_log_dt, 'ssm_a_re': ssm_a_re, 'ssm_a_im': ssm_a_im,
            'ssm_b_re': ssm_b_re, 'ssm_b_im': ssm_b_im, 'ssm_c_re': ssm_c_re, 'ssm_c_im': ssm_c_im,
            'ssm_d': ssm_d, 'w_glu': w_glu, 'g_ssm_out': g_ssm_out, 'g_attn_out': g_attn_out,
            'w_out': w_out, 'ln1_g': ln1_g, 'ln1_b': ln1_b, 'w_router': w_router,
            'router_bias': router_bias, 'w_gate': w_gate, 'w_up': w_up, 'w_down': w_down,
            'ws_gate': ws_gate, 'ws_up': ws_up, 'ws_down': ws_down, 'ln2_g': ln2_g, 'ln2_b': ln2_b}


def reference(x, w_in, ssm_log_dt, ssm_a_re, ssm_a_im, ssm_b_re, ssm_b_im, ssm_c_re, ssm_c_im,
              ssm_d, w_glu, g_ssm_out, g_attn_out, w_out, ln1_g, ln1_b, w_router, router_bias,
              w_gate, w_up, w_down, ws_gate, ws_up, ws_down, ln2_g, ln2_b):
    slopes = 2.0 ** (-8.0 * jnp.arange(1, N_HEADS + 1, dtype=jnp.float32) / N_HEADS)
    h = x
    for layer in range(DEPTH):
        proj = h @ w_in[layer]
        u, q, k, v = jnp.split(proj, [D_SSM, D_SSM + D_ATTN, D_SSM + 2 * D_ATTN], axis=-1)
        y_ssm = _s5_mixer(u, ssm_log_dt[layer], ssm_a_re[layer], ssm_a_im[layer], ssm_b_re[layer],
                          ssm_b_im[layer], ssm_c_re[layer], ssm_c_im[layer], ssm_d[layer], w_glu[layer])
        y_attn = _attention_mixer(q, k, v, slopes)
        merged = jnp.concatenate([_rms_norm(y_ssm, g_ssm_out[layer]), _rms_norm(y_attn, g_attn_out[layer])], axis=-1)
        h = _layer_norm(DEEPNORM_ALPHA * h + merged @ w_out[layer], ln1_g[layer], ln1_b[layer])
        ffn = _moe(h, w_router[layer], router_bias[layer], w_gate[layer], w_up[layer], w_down[layer],
                   ws_gate[layer], ws_up[layer], ws_down[layer])
        h = _layer_norm(DEEPNORM_ALPHA * h + ffn, ln2_g[layer], ln2_b[layer])
    return h
```

```python
import functools
import math

import jax
import jax.numpy as jnp
from jax import lax
from jax.experimental import pallas as pl
from jax.experimental.pallas import tpu as pltpu

D_MODEL = 2048
D_SSM = 1024
D_ATTN = 1024
SSM_CH = 16
SSM_GROUPS = 64
SSM_STATE = 64
HEAD_DIM = 64
N_HEADS = 16
PATTERNS = ((128, 1), (512, 4), (2048, 16))
ATTN_BLOCK = 128
N_EXPERTS = 64
TOP_K = 8
N_EXPERT_GROUPS = 8
TOPK_GROUPS = 4
D_EXPERT = 512
ROUTED_SCALE = 2.5
NORM_EPS = 1e-5
DEPTH = 1
DEEPNORM_ALPHA = (2.0 * DEPTH) ** 0.25

S5_CHUNK = 16
S5_GROUPS_PER_BLOCK = 8
MOE_ROWS = 256
MASK_VALUE = -1e30
VMEM_LIMIT = 56 * 1024 * 1024

bf16 = jnp.bfloat16
f32 = jnp.float32


def _params(*sem):
    return pltpu.CompilerParams(dimension_semantics=sem, vmem_limit_bytes=VMEM_LIMIT)


def _matmul_kernel(a_ref, b_ref, o_ref):
    a = a_ref[...].astype(bf16)
    o_ref[...] = jnp.dot(a, b_ref[...], preferred_element_type=f32).astype(o_ref.dtype)


def _matmul(a, b, out_dtype, tm=512, tn=1024):
    m, k = a.shape
    _, n = b.shape
    tn = min(tn, n)
    return pl.pallas_call(
        _matmul_kernel,
        grid=(m // tm, n // tn),
        in_specs=[pl.BlockSpec((tm, k), lambda i, j: (i, 0)),
                  pl.BlockSpec((k, tn), lambda i, j: (0, j))],
        out_specs=pl.BlockSpec((tm, tn), lambda i, j: (i, j)),
        out_shape=jax.ShapeDtypeStruct((m, n), out_dtype),
        compiler_params=_params("parallel", "arbitrary"),
        name="matmul",
    )(a, b)


def _s5_tables(log_dt, a_re, a_im, b_re, b_im, c_re, c_im, d_skip):
    t = S5_CHUNK
    gpb = S5_GROUPS_PER_BLOCK
    nblk = SSM_GROUPS // gpb
    hp = lax.Precision.HIGHEST
    lr = jnp.minimum(a_re.astype(f32), -1e-4)
    li = a_im.astype(f32)
    dt = jnp.exp(log_dt.astype(f32))
    kk = jnp.arange(t + 1, dtype=f32)[:, None, None]
    mag = jnp.exp(kk * (lr * dt))
    pr = mag * jnp.cos(kk * (li * dt))
    pi = mag * jnp.sin(kk * (li * dt))
    xr, xi = pr[1] - 1.0, pi[1]
    den = lr * lr + li * li
    cr = (xr * lr + xi * li) / den
    ci = (xi * lr - xr * li) / den
    bbr = cr[..., None] * b_re - ci[..., None] * b_im
    bbi = cr[..., None] * b_im + ci[..., None] * b_re
    eye = jnp.eye(gpb, dtype=f32)

    wr = pr[:t, :, :, None] * bbr - pi[:t, :, :, None] * bbi
    wi = pr[:t, :, :, None] * bbi + pi[:t, :, :, None] * bbr
    taps = (jnp.einsum('gop,tgpc->tgco', c_re, wr, precision=hp)
            - jnp.einsum('gop,tgpc->tgco', c_im, wi, precision=hp))
    taps = taps.reshape(t, nblk, gpb, SSM_CH, SSM_CH)
    ktab = jnp.einsum('tbgco,gh->btgcho', taps, eye).reshape(nblk, t, gpb * SSM_CH, gpb * SSM_CH)

    rev = jnp.arange(t - 1, -1, -1)
    sr = pr[rev][..., None] * bbr - pi[rev][..., None] * bbi
    si = pr[rev][..., None] * bbi + pi[rev][..., None] * bbr
    sb = jnp.stack([sr, si], axis=0).reshape(2, t, nblk, gpb, SSM_STATE, SSM_CH)
    bpow = jnp.einsum('zsbgpc,gh->bsgczhp', sb, eye)
    bpow = bpow.reshape(nblk, t * gpb * SSM_CH, 2 * gpb * SSM_STATE)

    er = c_re[None] * pr[1:, :, None, :] - c_im[None] * pi[1:, :, None, :]
    ei = c_re[None] * pi[1:, :, None, :] + c_im[None] * pr[1:, :, None, :]
    eb = jnp.stack([er, -ei], axis=0).reshape(2, t, nblk, gpb, SSM_CH, SSM_STATE)
    cpow = jnp.einsum('ztbgop,gh->bzgptho', eb, eye)
    cpow = cpow.reshape(nblk, 2 * gpb * SSM_STATE, t * gpb * SSM_CH)

    a_chunk = jnp.stack([pr[t], pi[t]], axis=0).reshape(2, nblk, 1, gpb * SSM_STATE)
    a_chunk = a_chunk.transpose(1, 0, 2, 3).reshape(nblk, 2, gpb * SSM_STATE)
    dvec = jnp.tile(d_skip.astype(f32).reshape(nblk, 1, gpb * SSM_CH), (1, 1, t))
    return ktab.astype(bf16), bpow.astype(bf16), cpow.astype(bf16), a_chunk, dvec


def _s5_kernel(u_ref, ktab_ref, bpow_ref, cpow_ref, a_ref, d_ref, y_ref,
               toep_ref, s_ref, h_ref, *, n_batch, n_chunk):
    t = S5_CHUNK
    w = S5_GROUPS_PER_BLOCK * SSM_CH
    ns = S5_GROUPS_PER_BLOCK * SSM_STATE
    for tt in range(t):
        for ss in range(tt + 1):
            toep_ref[ss * w:(ss + 1) * w, tt * w:(tt + 1) * w] = ktab_ref[0, tt - ss]
        if tt % 2 == 0:
            toep_ref[(tt + 1) * w:(tt + 2) * w, tt * w:(tt + 1) * w] = jnp.zeros((w, w), bf16)

    u = u_ref[0]
    s_ref[...] = jnp.dot(u, bpow_ref[0], preferred_element_type=f32)

    ar = a_ref[0, 0:1, :]
    ai = a_ref[0, 1:2, :]

    def step(j, carry):
        new = []
        for b in range(n_batch):
            hr, hi = carry[2 * b], carry[2 * b + 1]
            row = b * n_chunk + j
            h_ref[pl.ds(row, 1), 0:ns] = hr
            h_ref[pl.ds(row, 1), ns:2 * ns] = hi
            sr = s_ref[pl.ds(row, 1), 0:ns]
            si = s_ref[pl.ds(row, 1), ns:2 * ns]
            new.append(ar * hr - ai * hi + sr)
            new.append(ar * hi + ai * hr + si)
        return tuple(new)

    zero = jnp.zeros((1, ns), f32)
    lax.fori_loop(0, n_chunk, step, (zero,) * (2 * n_batch))

    hprev = h_ref[...].astype(bf16)
    for tp in range(t // 2):
        c0, c1 = 2 * tp * w, (2 * tp + 2) * w
        y = jnp.dot(u[:, :c1], toep_ref[0:c1, c0:c1], preferred_element_type=f32)
        y = y + jnp.dot(hprev, cpow_ref[0, :, c0:c1], preferred_element_type=f32)
        y = y + d_ref[0, :, c0:c1] * u[:, c0:c1].astype(f32)
        y_ref[0, :, c0:c1] = jax.nn.gelu(y, approximate=True).astype(y_ref.dtype)


def _s5_mixer(u_blocks, tables, n_batch, n_chunk):
    ktab, bpow, cpow, a_chunk, dvec = tables
    nblk, rows, cols = u_blocks.shape
    ns = S5_GROUPS_PER_BLOCK * SSM_STATE
    kern = functools.partial(_s5_kernel, n_batch=n_batch, n_chunk=n_chunk)
    return pl.pallas_call(
        kern,
        grid=(nblk,),
        in_specs=[pl.BlockSpec((1, rows, cols), lambda g: (g, 0, 0)),
                  pl.BlockSpec((1,) + ktab.shape[1:], lambda g: (g, 0, 0, 0)),
                  pl.BlockSpec((1,) + bpow.shape[1:], lambda g: (g, 0, 0)),
                  pl.BlockSpec((1,) + cpow.shape[1:], lambda g: (g, 0, 0)),
                  pl.BlockSpec((1, 2, ns), lambda g: (g, 0, 0)),
                  pl.BlockSpec((1, 1, cols), lambda g: (g, 0, 0))],
        out_specs=pl.BlockSpec((1, rows, cols), lambda g: (g, 0, 0)),
        out_shape=jax.ShapeDtypeStruct((nblk, rows, cols), bf16),
        scratch_shapes=[pltpu.VMEM((cols, cols), bf16),
                        pltpu.VMEM((rows, 2 * ns), f32),
                        pltpu.VMEM((rows, 2 * ns), f32)],
        compiler_params=_params("parallel"),
        name="s5_mixer",
    )(u_blocks, ktab, bpow, cpow, a_chunk, dvec)


def _attn_kernel(q_ref, kp_ref, kc_ref, vp_ref, vc_ref, bias_ref, o_ref, lse_ref):
    i = pl.program_id(1)
    q = q_ref[0]
    dims = (((1,), (1,)), ((), ()))
    scale = HEAD_DIM ** -0.5
    s_c = lax.dot_general(q, kc_ref[0], dims, preferred_element_type=f32) * scale
    s_p = lax.dot_general(q, kp_ref[0], dims, preferred_element_type=f32) * scale
    s_c = s_c + bias_ref[0, :, ATTN_BLOCK:]
    s_p = s_p + bias_ref[0, :, :ATTN_BLOCK]
    s_p = jnp.where(i > 0, s_p, MASK_VALUE)
    m = jnp.maximum(jnp.max(s_c, axis=-1, keepdims=True), jnp.max(s_p, axis=-1, keepdims=True))
    p_c = jnp.exp(s_c - m)
    p_p = jnp.exp(s_p - m)
    l = jnp.sum(p_c, axis=-1, keepdims=True) + jnp.sum(p_p, axis=-1, keepdims=True)
    acc = jnp.dot(p_c.astype(bf16), vc_ref[0], preferred_element_type=f32)
    acc = acc + jnp.dot(p_p.astype(bf16), vp_ref[0], preferred_element_type=f32)
    o_ref[0] = acc / l
    lse_ref[0] = jnp.broadcast_to(m + jnp.log(l), acc.shape)


def _attn_bias(slopes, dilation):
    blk = ATTN_BLOCK
    qi = jnp.arange(blk)[:, None]
    ki = jnp.arange(2 * blk)[None, :] - blk
    delta = qi - ki
    valid = (delta >= 0) & (delta <= blk)
    dist = (delta * dilation).astype(f32)
    bias = -slopes[:, None, None] * dist[None]
    return jnp.where(valid[None], bias, MASK_VALUE).astype(f32)


def _dilated_attention(q, k, v, bias, n_heads):
    n, l, dh = q.shape
    nb = l // ATTN_BLOCK
    cur = lambda r, i: (r, i, 0)
    prev = lambda r, i: (r, jnp.maximum(i - 1, 0), 0)
    blk = (1, ATTN_BLOCK, dh)
    return pl.pallas_call(
        _attn_kernel,
        grid=(n, nb),
        in_specs=[pl.BlockSpec(blk, cur), pl.BlockSpec(blk, prev), pl.BlockSpec(blk, cur),
                  pl.BlockSpec(blk, prev), pl.BlockSpec(blk, cur),
                  pl.BlockSpec((1, ATTN_BLOCK, 2 * ATTN_BLOCK), lambda r, i: (r % n_heads, 0, 0))],
        out_specs=[pl.BlockSpec(blk, cur), pl.BlockSpec(blk, cur)],
        out_shape=[jax.ShapeDtypeStruct((n, l, dh), f32), jax.ShapeDtypeStruct((n, l, dh), f32)],
        compiler_params=_params("parallel", "arbitrary"),
        name="dilated_attention",
    )(q, k, k, v, v, bias)


def _layer_norm_rows(x, g, b):
    mu = jnp.mean(x, axis=-1, keepdims=True)
    xc = x - mu
    var = jnp.mean(xc * xc, axis=-1, keepdims=True)
    return xc * lax.rsqrt(var + NORM_EPS) * g + b


def _rms_rows(x, g):
    return x * lax.rsqrt(jnp.mean(x * x, axis=-1, keepdims=True) + NORM_EPS) * g


def _mix_out_kernel(z_ref, o1_ref, o2_ref, o3_ref, l1_ref, l2_ref, l3_ref, x_ref, w_ref,
                    gs_ref, ga_ref, lg_ref, lb_ref, h_ref, hb_ref):
    z = z_ref[...]
    y_ssm = z[:, :D_SSM] * jax.nn.sigmoid(z[:, D_SSM:])
    l1, l2, l3 = l1_ref[...], l2_ref[...], l3_ref[...]
    m = jnp.maximum(jnp.maximum(l1, l2), l3)
    e1, e2, e3 = jnp.exp(l1 - m), jnp.exp(l2 - m), jnp.exp(l3 - m)
    y_attn = (e1 * o1_ref[...] + e2 * o2_ref[...] + e3 * o3_ref[...]) / (e1 + e2 + e3)
    ns = _rms_rows(y_ssm, gs_ref[...]).astype(bf16)
    na = _rms_rows(y_attn, ga_ref[...]).astype(bf16)
    proj = jnp.dot(ns, w_ref[0:D_SSM, :], preferred_element_type=f32)
    proj = proj + jnp.dot(na, w_ref[D_SSM:, :], preferred_element_type=f32)
    h = _layer_norm_rows(DEEPNORM_ALPHA * x_ref[...] + proj, lg_ref[...], lb_ref[...])
    h_ref[...] = h
    hb_ref[...] = h.astype(bf16)


def _mix_out(z, o1, o2, o3, l1, l2, l3, x, w_out, g_ssm, g_attn, ln_g, ln_b, tm=256):
    n = x.shape[0]
    row = lambda c: pl.BlockSpec((tm, c), lambda i: (i, 0))
    full = lambda a: pl.BlockSpec(a.shape, lambda i: (0,) * a.ndim)
    return pl.pallas_call(
        _mix_out_kernel,
        grid=(n // tm,),
        in_specs=[row(2 * D_SSM), row(D_ATTN), row(D_ATTN), row(D_ATTN), row(D_ATTN), row(D_ATTN),
                  row(D_ATTN), row(D_MODEL), full(w_out), full(g_ssm), full(g_attn), full(ln_g),
                  full(ln_b)],
        out_specs=[row(D_MODEL), row(D_MODEL)],
        out_shape=[jax.ShapeDtypeStruct((n, D_MODEL), f32), jax.ShapeDtypeStruct((n, D_MODEL), bf16)],
        compiler_params=_params("parallel"),
        name="mix_out",
    )(z, o1, o2, o3, l1, l2, l3, x, w_out, g_ssm, g_attn, ln_g, ln_b)


def _router_kernel(h_ref, w_ref, s_ref):
    logits = jnp.dot(h_ref[...], w_ref[...], preferred_element_type=f32,
                     precision=lax.Precision.HIGHEST)
    s_ref[...] = jax.nn.sigmoid(logits)


def _router_scores(h, w_router, tm=512):
    n = h.shape[0]
    return pl.pallas_call(
        _router_kernel,
        grid=(n // tm,),
        in_specs=[pl.BlockSpec((tm, D_MODEL), lambda i: (i, 0)),
                  pl.BlockSpec(w_router.shape, lambda i: (0, 0))],
        out_specs=pl.BlockSpec((tm, N_EXPERTS), lambda i: (i, 0)),
        out_shape=jax.ShapeDtypeStruct((n, N_EXPERTS), f32),
        compiler_params=_params("parallel"),
        name="router",
    )(h, w_router)


def _experts_kernel(be_ref, x_ref, rw_ref, wg_ref, wu_ref, wd_ref, y_ref, wgu_scr, wd_scr):
    i = pl.program_id(0)
    changed = jnp.logical_or(i == 0, be_ref[i] != be_ref[jnp.maximum(i - 1, 0)])

    @pl.when(changed)
    def _():
        wgu_scr[:, :D_EXPERT] = wg_ref[0].astype(bf16)
        wgu_scr[:, D_EXPERT:] = wu_ref[0].astype(bf16)
        wd_scr[...] = wd_ref[0].astype(bf16)

    gu = jnp.dot(x_ref[...], wgu_scr[...], preferred_element_type=f32)
    g, u = gu[:, :D_EXPERT], gu[:, D_EXPERT:]
    mid = (g * jax.nn.sigmoid(g) * u * rw_ref[...]).astype(bf16)
    y_ref[...] = jnp.dot(mid, wd_scr[...], preferred_element_type=f32).astype(y_ref.dtype)


def _experts(block_e, xs, row_w, w_gate, w_up, w_down):
    n_rows = xs.shape[0]
    n_blocks = n_rows // MOE_ROWS
    wmap = lambda i, be: (be[i], 0, 0)
    grid_spec = pltpu.PrefetchScalarGridSpec(
        num_scalar_prefetch=1,
        grid=(n_blocks,),
        in_specs=[pl.BlockSpec((MOE_ROWS, D_MODEL), lambda i, be: (i, 0)),
                  pl.BlockSpec((MOE_ROWS, 1), lambda i, be: (i, 0)),
                  pl.BlockSpec((1, D_MODEL, D_EXPERT), wmap),
                  pl.BlockSpec((1, D_MODEL, D_EXPERT), wmap),
                  pl.BlockSpec((1, D_EXPERT, D_MODEL), wmap)],
        out_specs=pl.BlockSpec((MOE_ROWS, D_MODEL), lambda i, be: (i, 0)),
        scratch_shapes=[pltpu.VMEM((D_MODEL, 2 * D_EXPERT), bf16),
                        pltpu.VMEM((D_EXPERT, D_MODEL), bf16)])
    return pl.pallas_call(
        _experts_kernel,
        grid_spec=grid_spec,
        out_shape=jax.ShapeDtypeStruct((n_rows, D_MODEL), f32),
        compiler_params=_params("arbitrary"),
        name="experts",
    )(block_e, xs, row_w, w_gate, w_up, w_down)


def _final_kernel(hb_ref, h_ref, r_ref, wgu_ref, wd_ref, lg_ref, lb_ref, o_ref):
    gu = jnp.dot(hb_ref[...], wgu_ref[...], preferred_element_type=f32)
    g, u = gu[:, :D_EXPERT], gu[:, D_EXPERT:]
    mid = (g * jax.nn.sigmoid(g) * u).astype(bf16)
    shared = jnp.dot(mid, wd_ref[...], preferred_element_type=f32)
    o_ref[...] = _layer_norm_rows(DEEPNORM_ALPHA * h_ref[...] + r_ref[...] + shared,
                                  lg_ref[...], lb_ref[...])


def _final(hb, h, routed, wgu, wd, ln_g, ln_b, tm=256):
    n = h.shape[0]
    row = lambda: pl.BlockSpec((tm, D_MODEL), lambda i: (i, 0))
    full = lambda a: pl.BlockSpec(a.shape, lambda i: (0,) * a.ndim)
    return pl.pallas_call(
        _final_kernel,
        grid=(n // tm,),
        in_specs=[row(), row(), row(), full(wgu), full(wd), full(ln_g), full(ln_b)],
        out_specs=row(),
        out_shape=jax.ShapeDtypeStruct((n, D_MODEL), f32),
        compiler_params=_params("parallel"),
        name="shared_final",
    )(hb, h, routed, wgu, wd, ln_g, ln_b)


def _route(scores, router_bias):
    n_tok = scores.shape[0]
    sel = scores + router_bias.astype(f32)
    grouped = sel.reshape(n_tok, N_EXPERT_GROUPS, N_EXPERTS // N_EXPERT_GROUPS)
    group_score = jnp.sum(lax.top_k(grouped, 2)[0], axis=-1)
    _, top_groups = lax.top_k(group_score, TOPK_GROUPS)
    group_keep = jnp.any(top_groups[..., None] == jnp.arange(N_EXPERT_GROUPS), axis=1)
    expert_keep = jnp.repeat(group_keep, N_EXPERTS // N_EXPERT_GROUPS, axis=1)
    _, top_idx = lax.top_k(jnp.where(expert_keep, sel, -jnp.inf), TOP_K)
    top_w = jnp.take_along_axis(scores, top_idx, axis=1)
    top_w = top_w / jnp.sum(top_w, axis=-1, keepdims=True) * ROUTED_SCALE
    return top_idx, top_w


def _dispatch(top_idx, top_w):
    n_tok = top_idx.shape[0]
    nk = n_tok * TOP_K
    flat_e = top_idx.reshape(nk)
    flat_tok = jnp.repeat(jnp.arange(n_tok, dtype=jnp.int32), TOP_K)
    flat_w = top_w.reshape(nk)
    order = jnp.argsort(flat_e)
    se, stok, sw = flat_e[order], flat_tok[order], flat_w[order]
    counts = jnp.bincount(flat_e, length=N_EXPERTS)
    padded = (counts + MOE_ROWS - 1) // MOE_ROWS * MOE_ROWS
    pad_end = jnp.cumsum(padded)
    pad_start = pad_end - padded
    start = jnp.cumsum(counts) - counts
    dest = pad_start[se] + jnp.arange(nk) - start[se]
    n_blocks = -(-(nk + N_EXPERTS * (MOE_ROWS - 1)) // MOE_ROWS)
    n_rows = n_blocks * MOE_ROWS
    row_tok = jnp.zeros((n_rows,), jnp.int32).at[dest].set(stok)
    row_w = jnp.zeros((n_rows,), f32).at[dest].set(sw)
    block_e = jnp.minimum(jnp.searchsorted(pad_end, jnp.arange(n_blocks) * MOE_ROWS, side='right'),
                          N_EXPERTS - 1).astype(jnp.int32)
    return row_tok, row_w, block_e


def kernel(x, w_in, ssm_log_dt, ssm_a_re, ssm_a_im, ssm_b_re, ssm_b_im, ssm_c_re, ssm_c_im, ssm_d,
           w_glu, g_ssm_out, g_attn_out, w_out, ln1_g, ln1_b, w_router, router_bias, w_gate, w_up,
           w_down, ws_gate, ws_up, ws_down, ln2_g, ln2_b):
    bsz, seq, d = x.shape
    n_tok = bsz * seq
    slopes = 2.0 ** (-8.0 * jnp.arange(1, N_HEADS + 1, dtype=f32) / N_HEADS)
    h = x.reshape(n_tok, d)
    for layer in range(DEPTH):
        proj = _matmul(h, w_in[layer].astype(bf16), bf16)
        u = proj[:, :D_SSM]
        q = proj[:, D_SSM:D_SSM + D_ATTN]
        k = proj[:, D_SSM + D_ATTN:D_SSM + 2 * D_ATTN]
        v = proj[:, D_SSM + 2 * D_ATTN:]

        t, gpb = S5_CHUNK, S5_GROUPS_PER_BLOCK
        nblk, n_chunk, w = SSM_GROUPS // gpb, seq // S5_CHUNK, gpb * SSM_CH
        tables = _s5_tables(ssm_log_dt[layer], ssm_a_re[layer], ssm_a_im[layer], ssm_b_re[layer],
                            ssm_b_im[layer], ssm_c_re[layer], ssm_c_im[layer], ssm_d[layer])
        ub = u.reshape(bsz * n_chunk, t, nblk, w).transpose(2, 0, 1, 3).reshape(nblk, bsz * n_chunk, t * w)
        yb = _s5_mixer(ub, tables, bsz, n_chunk)
        y = yb.reshape(nblk, bsz * n_chunk, t, w).transpose(1, 2, 0, 3).reshape(n_tok, D_SSM)
        z = _matmul(y, w_glu[layer].astype(bf16), f32)

        outs, lses = [], []
        for _, dil in PATTERNS:
            sub = seq // dil

            def to_sub(a):
                a = a.reshape(bsz, sub, dil, N_HEADS, HEAD_DIM).transpose(0, 2, 3, 1, 4)
                return a.reshape(bsz * dil * N_HEADS, sub, HEAD_DIM)

            def from_sub(a):
                a = a.reshape(bsz, dil, N_HEADS, sub, HEAD_DIM).transpose(0, 3, 1, 2, 4)
                return a.reshape(n_tok, D_ATTN)

            o, lse = _dilated_attention(to_sub(q), to_sub(k), to_sub(v), _attn_bias(slopes, dil), N_HEADS)
            outs.append(from_sub(o))
            lses.append(from_sub(lse))

        row2 = lambda a: a.astype(f32).reshape(1, -1)
        h, hb = _mix_out(z, outs[0], outs[1], outs[2], lses[0], lses[1], lses[2], h,
                         w_out[layer].astype(bf16), row2(g_ssm_out[layer]), row2(g_attn_out[layer]),
                         row2(ln1_g[layer]), row2(ln1_b[layer]))

        scores = _router_scores(h, w_router[layer].astype(f32))
        top_idx, top_w = _route(scores, router_bias[layer])
        row_tok, row_w, block_e = _dispatch(top_idx, top_w)
        xs = jnp.take(hb, row_tok, axis=0)
        ys = _experts(block_e, xs, row_w[:, None], w_gate[layer], w_up[layer], w_down[layer])
        routed = jax.ops.segment_sum(ys, row_tok, num_segments=n_tok)
        wgu = jnp.concatenate([ws_gate[layer], ws_up[layer]], axis=1).astype(bf16)
        h = _final(hb, h, routed, wgu, ws_down[layer].astype(bf16), row2(ln2_g[layer]), row2(ln2_b[layer]))
    return h.reshape(bsz, seq, d)
```

```python
import functools
import math

import jax
import jax.numpy as jnp
from jax import lax
from jax.experimental import pallas as pl
from jax.experimental.pallas import tpu as pltpu

D_MODEL = 2048
D_SSM = 1024
D_ATTN = 1024
SSM_CH = 16
SSM_GROUPS = 64
SSM_STATE = 64
HEAD_DIM = 64
N_HEADS = 16
PATTERNS = ((128, 1), (512, 4), (2048, 16))
ATTN_BLOCK = 128
N_EXPERTS = 64
TOP_K = 8
N_EXPERT_GROUPS = 8
TOPK_GROUPS = 4
D_EXPERT = 512
ROUTED_SCALE = 2.5
NORM_EPS = 1e-5
DEPTH = 1
DEEPNORM_ALPHA = (2.0 * DEPTH) ** 0.25

S5_CHUNK = 16
S5_GROUPS_PER_BLOCK = 8
MOE_ROWS = 256
MASK_VALUE = -1e30
PACK_ROWS = D_MODEL // 2 // 128
N_TOKENS = 8192
TOK_BITS = 13
PAD_CODE = N_TOKENS * TOP_K
MOE_BLOCKS = -(-(N_TOKENS * TOP_K + N_EXPERTS * (MOE_ROWS - 1)) // MOE_ROWS)
VMEM_LIMIT = 56 * 1024 * 1024

bf16 = jnp.bfloat16
f32 = jnp.float32


def _params(*sem):
    return pltpu.CompilerParams(dimension_semantics=sem, vmem_limit_bytes=VMEM_LIMIT)


def _matmul_kernel(a_ref, b_ref, o_ref):
    a = a_ref[...].astype(bf16)
    o_ref[...] = jnp.dot(a, b_ref[...], preferred_element_type=f32).astype(o_ref.dtype)


def _matmul(a, b, out_dtype, tm=512, tn=1024):
    m, k = a.shape
    _, n = b.shape
    tn = min(tn, n)
    return pl.pallas_call(
        _matmul_kernel,
        grid=(m // tm, n // tn),
        in_specs=[pl.BlockSpec((tm, k), lambda i, j: (i, 0)),
                  pl.BlockSpec((k, tn), lambda i, j: (0, j))],
        out_specs=pl.BlockSpec((tm, tn), lambda i, j: (i, j)),
        out_shape=jax.ShapeDtypeStruct((m, n), out_dtype),
        compiler_params=_params("parallel", "arbitrary"),
        name="matmul",
    )(a, b)


def _s5_tables(log_dt, a_re, a_im, b_re, b_im, c_re, c_im, d_skip):
    t = S5_CHUNK
    gpb = S5_GROUPS_PER_BLOCK
    nblk = SSM_GROUPS // gpb
    hp = lax.Precision.HIGHEST
    lr = jnp.minimum(a_re.astype(f32), -1e-4)
    li = a_im.astype(f32)
    dt = jnp.exp(log_dt.astype(f32))
    kk = jnp.arange(t + 1, dtype=f32)[:, None, None]
    mag = jnp.exp(kk * (lr * dt))
    pr = mag * jnp.cos(kk * (li * dt))
    pi = mag * jnp.sin(kk * (li * dt))
    xr, xi = pr[1] - 1.0, pi[1]
    den = lr * lr + li * li
    cr = (xr * lr + xi * li) / den
    ci = (xi * lr - xr * li) / den
    bbr = cr[..., None] * b_re - ci[..., None] * b_im
    bbi = cr[..., None] * b_im + ci[..., None] * b_re
    eye = jnp.eye(gpb, dtype=f32)

    wr = pr[:t, :, :, None] * bbr - pi[:t, :, :, None] * bbi
    wi = pr[:t, :, :, None] * bbi + pi[:t, :, :, None] * bbr
    taps = (jnp.einsum('gop,tgpc->tgco', c_re, wr, precision=hp)
            - jnp.einsum('gop,tgpc->tgco', c_im, wi, precision=hp))
    taps = taps.reshape(t, nblk, gpb, SSM_CH, SSM_CH)
    ktab = jnp.einsum('tbgco,gh->btgcho', taps, eye).reshape(nblk, t, gpb * SSM_CH, gpb * SSM_CH)

    rev = jnp.arange(t - 1, -1, -1)
    sr = pr[rev][..., None] * bbr - pi[rev][..., None] * bbi
    si = pr[rev][..., None] * bbi + pi[rev][..., None] * bbr
    sb = jnp.stack([sr, si], axis=0).reshape(2, t, nblk, gpb, SSM_STATE, SSM_CH)
    bpow = jnp.einsum('zsbgpc,gh->bsgczhp', sb, eye)
    bpow = bpow.reshape(nblk, t * gpb * SSM_CH, 2 * gpb * SSM_STATE)

    er = c_re[None] * pr[1:, :, None, :] - c_im[None] * pi[1:, :, None, :]
    ei = c_re[None] * pi[1:, :, None, :] + c_im[None] * pr[1:, :, None, :]
    eb = jnp.stack([er, -ei], axis=0).reshape(2, t, nblk, gpb, SSM_CH, SSM_STATE)
    cpow = jnp.einsum('ztbgop,gh->bzgptho', eb, eye)
    cpow = cpow.reshape(nblk, 2 * gpb * SSM_STATE, t * gpb * SSM_CH)

    a_chunk = jnp.stack([pr[t], pi[t]], axis=0).reshape(2, nblk, 1, gpb * SSM_STATE)
    a_chunk = a_chunk.transpose(1, 0, 2, 3).reshape(nblk, 2, gpb * SSM_STATE)
    dvec = jnp.tile(d_skip.astype(f32).reshape(nblk, 1, gpb * SSM_CH), (1, 1, t))
    return ktab.astype(bf16), bpow.astype(bf16), cpow.astype(bf16), a_chunk, dvec


def _s5_kernel(u_ref, ktab_ref, bpow_ref, cpow_ref, a_ref, d_ref, y_ref,
               toep_ref, s_ref, h_ref, *, n_batch, n_chunk):
    t = S5_CHUNK
    w = S5_GROUPS_PER_BLOCK * SSM_CH
    ns = S5_GROUPS_PER_BLOCK * SSM_STATE
    for tt in range(t):
        for ss in range(tt + 1):
            toep_ref[ss * w:(ss + 1) * w, tt * w:(tt + 1) * w] = ktab_ref[0, tt - ss]
        if tt % 2 == 0:
            toep_ref[(tt + 1) * w:(tt + 2) * w, tt * w:(tt + 1) * w] = jnp.zeros((w, w), bf16)

    u = u_ref[0]
    s_ref[...] = jnp.dot(u, bpow_ref[0], preferred_element_type=f32)

    ar = a_ref[0, 0:1, :]
    ai = a_ref[0, 1:2, :]

    def step(j, carry):
        new = []
        for b in range(n_batch):
            hr, hi = carry[2 * b], carry[2 * b + 1]
            row = b * n_chunk + j
            h_ref[pl.ds(row, 1), 0:ns] = hr
            h_ref[pl.ds(row, 1), ns:2 * ns] = hi
            sr = s_ref[pl.ds(row, 1), 0:ns]
            si = s_ref[pl.ds(row, 1), ns:2 * ns]
            new.append(ar * hr - ai * hi + sr)
            new.append(ar * hi + ai * hr + si)
        return tuple(new)

    zero = jnp.zeros((1, ns), f32)
    lax.fori_loop(0, n_chunk, step, (zero,) * (2 * n_batch))

    hprev = h_ref[...].astype(bf16)
    for tp in range(t // 2):
        c0, c1 = 2 * tp * w, (2 * tp + 2) * w
        y = jnp.dot(u[:, :c1], toep_ref[0:c1, c0:c1], preferred_element_type=f32)
        y = y + jnp.dot(hprev, cpow_ref[0, :, c0:c1], preferred_element_type=f32)
        y = y + d_ref[0, :, c0:c1] * u[:, c0:c1].astype(f32)
        y_ref[0, :, c0:c1] = jax.nn.gelu(y, approximate=True).astype(y_ref.dtype)


def _s5_mixer(u_blocks, tables, n_batch, n_chunk):
    ktab, bpow, cpow, a_chunk, dvec = tables
    nblk, rows, cols = u_blocks.shape
    ns = S5_GROUPS_PER_BLOCK * SSM_STATE
    kern = functools.partial(_s5_kernel, n_batch=n_batch, n_chunk=n_chunk)
    return pl.pallas_call(
        kern,
        grid=(nblk,),
        in_specs=[pl.BlockSpec((1, rows, cols), lambda g: (g, 0, 0)),
                  pl.BlockSpec((1,) + ktab.shape[1:], lambda g: (g, 0, 0, 0)),
                  pl.BlockSpec((1,) + bpow.shape[1:], lambda g: (g, 0, 0)),
                  pl.BlockSpec((1,) + cpow.shape[1:], lambda g: (g, 0, 0)),
                  pl.BlockSpec((1, 2, ns), lambda g: (g, 0, 0)),
                  pl.BlockSpec((1, 1, cols), lambda g: (g, 0, 0))],
        out_specs=pl.BlockSpec((1, rows, cols), lambda g: (g, 0, 0)),
        out_shape=jax.ShapeDtypeStruct((nblk, rows, cols), bf16),
        scratch_shapes=[pltpu.VMEM((cols, cols), bf16),
                        pltpu.VMEM((rows, 2 * ns), f32),
                        pltpu.VMEM((rows, 2 * ns), f32)],
        compiler_params=_params("parallel"),
        name="s5_mixer",
    )(u_blocks, ktab, bpow, cpow, a_chunk, dvec)


def _attn_kernel(q_ref, kp_ref, kc_ref, vp_ref, vc_ref, bias_ref, o_ref, lse_ref):
    i = pl.program_id(1)
    q = q_ref[0]
    dims = (((1,), (1,)), ((), ()))
    scale = HEAD_DIM ** -0.5
    s_c = lax.dot_general(q, kc_ref[0], dims, preferred_element_type=f32) * scale
    s_p = lax.dot_general(q, kp_ref[0], dims, preferred_element_type=f32) * scale
    s_c = s_c + bias_ref[0, :, ATTN_BLOCK:]
    s_p = s_p + bias_ref[0, :, :ATTN_BLOCK]
    s_p = jnp.where(i > 0, s_p, MASK_VALUE)
    m = jnp.maximum(jnp.max(s_c, axis=-1, keepdims=True), jnp.max(s_p, axis=-1, keepdims=True))
    p_c = jnp.exp(s_c - m)
    p_p = jnp.exp(s_p - m)
    l = jnp.sum(p_c, axis=-1, keepdims=True) + jnp.sum(p_p, axis=-1, keepdims=True)
    acc = jnp.dot(p_c.astype(bf16), vc_ref[0], preferred_element_type=f32)
    acc = acc + jnp.dot(p_p.astype(bf16), vp_ref[0], preferred_element_type=f32)
    o_ref[0] = acc / l
    lse_ref[0] = jnp.broadcast_to(m + jnp.log(l), acc.shape)


def _attn_bias(slopes, dilation):
    blk = ATTN_BLOCK
    qi = jnp.arange(blk)[:, None]
    ki = jnp.arange(2 * blk)[None, :] - blk
    delta = qi - ki
    valid = (delta >= 0) & (delta <= blk)
    dist = (delta * dilation).astype(f32)
    bias = -slopes[:, None, None] * dist[None]
    return jnp.where(valid[None], bias, MASK_VALUE).astype(f32)


def _dilated_attention(q, k, v, bias, n_heads):
    n, l, dh = q.shape
    nb = l // ATTN_BLOCK
    cur = lambda r, i: (r, i, 0)
    prev = lambda r, i: (r, jnp.maximum(i - 1, 0), 0)
    blk = (1, ATTN_BLOCK, dh)
    return pl.pallas_call(
        _attn_kernel,
        grid=(n, nb),
        in_specs=[pl.BlockSpec(blk, cur), pl.BlockSpec(blk, prev), pl.BlockSpec(blk, cur),
                  pl.BlockSpec(blk, prev), pl.BlockSpec(blk, cur),
                  pl.BlockSpec((1, ATTN_BLOCK, 2 * ATTN_BLOCK), lambda r, i: (r % n_heads, 0, 0))],
        out_specs=[pl.BlockSpec(blk, cur), pl.BlockSpec(blk, cur)],
        out_shape=[jax.ShapeDtypeStruct((n, l, dh), f32), jax.ShapeDtypeStruct((n, l, dh), f32)],
        compiler_params=_params("parallel", "arbitrary"),
        name="dilated_attention",
    )(q, k, k, v, v, bias)


def _layer_norm_rows(x, g, b):
    mu = jnp.mean(x, axis=-1, keepdims=True)
    xc = x - mu
    var = jnp.mean(xc * xc, axis=-1, keepdims=True)
    return xc * lax.rsqrt(var + NORM_EPS) * g + b


def _rms_rows(x, g):
    return x * lax.rsqrt(jnp.mean(x * x, axis=-1, keepdims=True) + NORM_EPS) * g


def _mix_out_kernel(z_ref, o1_ref, o2_ref, o3_ref, l1_ref, l2_ref, l3_ref, x_ref, w_ref,
                    gs_ref, ga_ref, lg_ref, lb_ref, h_ref, hb_ref, hp_ref):
    z = z_ref[...]
    y_ssm = z[:, :D_SSM] * jax.nn.sigmoid(z[:, D_SSM:])
    l1, l2, l3 = l1_ref[...], l2_ref[...], l3_ref[...]
    m = jnp.maximum(jnp.maximum(l1, l2), l3)
    e1, e2, e3 = jnp.exp(l1 - m), jnp.exp(l2 - m), jnp.exp(l3 - m)
    y_attn = (e1 * o1_ref[...] + e2 * o2_ref[...] + e3 * o3_ref[...]) / (e1 + e2 + e3)
    ns = _rms_rows(y_ssm, gs_ref[...]).astype(bf16)
    na = _rms_rows(y_attn, ga_ref[...]).astype(bf16)
    proj = jnp.dot(ns, w_ref[0:D_SSM, :], preferred_element_type=f32)
    proj = proj + jnp.dot(na, w_ref[D_SSM:, :], preferred_element_type=f32)
    h = _layer_norm_rows(DEEPNORM_ALPHA * x_ref[...] + proj, lg_ref[...], lb_ref[...])
    h_ref[...] = h
    hb = h.astype(bf16)
    hb_ref[...] = hb
    half = D_MODEL // 2
    lo = pltpu.bitcast(hb[:, :half].astype(f32), jnp.uint32) >> 16
    hi = pltpu.bitcast(hb[:, half:].astype(f32), jnp.uint32) & jnp.uint32(0xFFFF0000)
    word = pltpu.bitcast(hi | lo, jnp.int32)
    tm = word.shape[0]
    for s in range(PACK_ROWS):
        hp_ref[pl.ds(s, tm, stride=PACK_ROWS), :] = word[:, s * 128:(s + 1) * 128]


def _mix_out(z, o1, o2, o3, l1, l2, l3, x, w_out, g_ssm, g_attn, ln_g, ln_b, tm=256):
    n = x.shape[0]
    row = lambda c: pl.BlockSpec((tm, c), lambda i: (i, 0))
    full = lambda a: pl.BlockSpec(a.shape, lambda i: (0,) * a.ndim)
    return pl.pallas_call(
        _mix_out_kernel,
        grid=(n // tm,),
        in_specs=[row(2 * D_SSM), row(D_ATTN), row(D_ATTN), row(D_ATTN), row(D_ATTN), row(D_ATTN),
                  row(D_ATTN), row(D_MODEL), full(w_out), full(g_ssm), full(g_attn), full(ln_g),
                  full(ln_b)],
        out_specs=[row(D_MODEL), row(D_MODEL), pl.BlockSpec((tm * PACK_ROWS, 128), lambda i: (i, 0))],
        out_shape=[jax.ShapeDtypeStruct((n, D_MODEL), f32), jax.ShapeDtypeStruct((n, D_MODEL), bf16),
                   jax.ShapeDtypeStruct((n * PACK_ROWS, 128), jnp.int32)],
        compiler_params=_params("parallel"),
        name="mix_out",
    )(z, o1, o2, o3, l1, l2, l3, x, w_out, g_ssm, g_attn, ln_g, ln_b)


def _router_kernel(h_ref, wrt_ref, bias_ref, tri_ref, trie_ref, e8_ref, pos8_ref, w8_ref, cnt_ref):
    gsz = N_EXPERTS // N_EXPERT_GROUPS
    tm = h_ref.shape[0]
    ninf = -jnp.inf

    @pl.when(pl.program_id(0) == 0)
    def _():
        cnt_ref[...] = jnp.zeros_like(cnt_ref)

    logits = lax.dot_general(wrt_ref[...], h_ref[...], (((1,), (1,)), ((), ())),
                             preferred_element_type=f32, precision=lax.Precision.HIGHEST)
    scores = jax.nn.sigmoid(logits)
    sel = scores + bias_ref[...]
    io = lax.broadcasted_iota(jnp.int32, (gsz, tm), 0)

    blks, gs_rows = [], []
    for g in range(N_EXPERT_GROUPS):
        blk = sel[g * gsz:(g + 1) * gsz, :]
        m1 = jnp.max(blk, axis=0, keepdims=True)
        first = jnp.min(jnp.where(blk == m1, io, gsz), axis=0, keepdims=True)
        m2 = jnp.max(jnp.where(io == first, ninf, blk), axis=0, keepdims=True)
        blks.append(blk)
        gs_rows.append(m1 + m2)
    gs = jnp.concatenate(gs_rows, axis=0)

    iog = lax.broadcasted_iota(jnp.int32, (N_EXPERT_GROUPS, tm), 0)
    beaten = jnp.zeros((N_EXPERT_GROUPS, tm), f32)
    for gp in range(N_EXPERT_GROUPS):
        row = gs_rows[gp]
        tie = jnp.where(iog > gp, 1.0, 0.0)
        beaten = beaten + jnp.where(row > gs, 1.0, jnp.where(row == gs, tie, 0.0))
    keep = beaten < TOPK_GROUPS
    masked = [jnp.where(keep[g:g + 1, :], blks[g], ninf) for g in range(N_EXPERT_GROUPS)]

    ranks = [jnp.zeros((gsz, tm), f32) for _ in range(N_EXPERT_GROUPS)]
    tie_in = [jnp.where(io > j, 1.0, 0.0) for j in range(gsz)]
    for gp in range(N_EXPERT_GROUPS):
        for j in range(gsz):
            row = masked[gp][j:j + 1, :]
            for g in range(N_EXPERT_GROUPS):
                if gp < g:
                    inc = jnp.where(row >= masked[g], 1.0, 0.0)
                elif gp > g:
                    inc = jnp.where(row > masked[g], 1.0, 0.0)
                else:
                    inc = jnp.where(row > masked[g], 1.0, jnp.where(row == masked[g], tie_in[j], 0.0))
                ranks[g] = ranks[g] + inc
    selb = jnp.concatenate([jnp.where(r < TOP_K, 1.0, 0.0) for r in ranks], axis=0)
    wsel = selb * scores
    wn = wsel / jnp.sum(wsel, axis=0, keepdims=True) * ROUTED_SCALE

    maskb = selb.astype(bf16)
    pos = jnp.dot(maskb, tri_ref[...], preferred_element_type=f32) + cnt_ref[:, 0:1]
    cnt_ref[...] = cnt_ref[...] + jnp.sum(selb, axis=1, keepdims=True)
    slot = jnp.dot(trie_ref[...], maskb, preferred_element_type=f32)
    ioe = lax.broadcasted_iota(jnp.int32, (N_EXPERTS, tm), 0).astype(f32)
    e_rows, p_rows, w_rows = [], [], []
    for k in range(TOP_K):
        hit = jnp.where(slot == k, selb, 0.0)
        e_rows.append(jnp.sum(hit * ioe, axis=0, keepdims=True))
        p_rows.append(jnp.sum(hit * pos, axis=0, keepdims=True))
        w_rows.append(jnp.sum(hit * wn, axis=0, keepdims=True))
    e8_ref[...] = jnp.concatenate(e_rows, axis=0).astype(jnp.int32)
    pos8_ref[...] = jnp.concatenate(p_rows, axis=0).astype(jnp.int32)
    w8_ref[...] = jnp.concatenate(w_rows, axis=0)


def _router(h, w_router, router_bias, tm=512):
    n = h.shape[0]
    wrt = w_router.astype(f32).T
    bias = router_bias.astype(f32).reshape(N_EXPERTS, 1)
    tri = (jnp.arange(tm)[:, None] < jnp.arange(tm)[None, :]).astype(bf16)
    trie = (jnp.arange(N_EXPERTS)[None, :] < jnp.arange(N_EXPERTS)[:, None]).astype(bf16)
    full = lambda a: pl.BlockSpec(a.shape, lambda i: (0,) * a.ndim)
    tok = lambda: pl.BlockSpec((TOP_K, tm), lambda i: (0, i))
    return pl.pallas_call(
        _router_kernel,
        grid=(n // tm,),
        in_specs=[pl.BlockSpec((tm, D_MODEL), lambda i: (i, 0)), full(wrt), full(bias), full(tri), full(trie)],
        out_specs=[tok(), tok(), tok(), pl.BlockSpec((N_EXPERTS, 128), lambda i: (0, 0))],
        out_shape=[jax.ShapeDtypeStruct((TOP_K, n), jnp.int32), jax.ShapeDtypeStruct((TOP_K, n), jnp.int32),
                   jax.ShapeDtypeStruct((TOP_K, n), f32), jax.ShapeDtypeStruct((N_EXPERTS, 128), f32)],
        compiler_params=_params("arbitrary"),
        name="router",
    )(h, wrt, bias, tri, trie)


def _invert_kernel(dest_ref, seg_end_ref, pad_end_ref, code_ref):
    def fill(lo, hi):
        def body(r, c):
            code_ref[r] = PAD_CODE
            return c
        lax.fori_loop(lo, hi, body, 0)

    def per_expert(e, c):
        fill(seg_end_ref[e], pad_end_ref[e])
        return c
    lax.fori_loop(0, N_EXPERTS, per_expert, 0)
    fill(pad_end_ref[N_EXPERTS - 1], code_ref.shape[0])

    def body(p, c):
        code_ref[dest_ref[p]] = p
        return c
    lax.fori_loop(0, dest_ref.shape[0], body, 0, unroll=8)


def _invert(dest, seg_end, pad_end, n_rows):
    smem = lambda: pl.BlockSpec(memory_space=pltpu.SMEM)
    return pl.pallas_call(
        _invert_kernel,
        in_specs=[smem(), smem(), smem()],
        out_specs=smem(),
        out_shape=jax.ShapeDtypeStruct((n_rows,), jnp.int32),
        name="invert_dispatch",
    )(dest, seg_end, pad_end)


def _expert_changed(be_ref, i):
    return jnp.logical_or(i == 0, be_ref[i] != be_ref[jnp.maximum(i - 1, 0)])


def _moe_up_kernel(be_ref, code_ref, nb_ref, hp_ref, wg_ref, wu_ref, mid_ref, xg_scr, wgu_scr):
    i = pl.program_id(0)
    rows = MOE_ROWS

    @pl.when(i < nb_ref[0])
    def _():
        @pl.when(_expert_changed(be_ref, i))
        def _():
            wgu_scr[:, :D_EXPERT] = wg_ref[0].astype(bf16)
            wgu_scr[:, D_EXPERT:] = wu_ref[0].astype(bf16)

        def gather(r, c):
            tok = code_ref[i * rows + r] & (N_TOKENS - 1)
            v = hp_ref[pl.ds(pl.multiple_of(tok * PACK_ROWS, PACK_ROWS), PACK_ROWS), :]
            xg_scr[pl.ds(r, PACK_ROWS, stride=rows), :] = v
            return c
        lax.fori_loop(0, rows, gather, 0, unroll=8)

        lo, hi = [], []
        for s in range(PACK_ROWS):
            wds = xg_scr[s * rows:(s + 1) * rows, :]
            lo.append(pltpu.bitcast(wds << 16, f32).astype(bf16))
            hi.append(pltpu.bitcast(wds & jnp.int32(-65536), f32).astype(bf16))
        x = jnp.concatenate(lo + hi, axis=1)
        gu = jnp.dot(x, wgu_scr[...], preferred_element_type=f32)
        g, u = gu[:, :D_EXPERT], gu[:, D_EXPERT:]
        mid_ref[...] = (g * jax.nn.sigmoid(g) * u).astype(bf16)

    @pl.when(i >= nb_ref[0])
    def _():
        mid_ref[...] = jnp.zeros_like(mid_ref)


def _moe_up(block_e, code, nb, hp, w_gate, w_up):
    wmap = lambda i, be, cd, nb: (be[i], 0, 0)
    grid_spec = pltpu.PrefetchScalarGridSpec(
        num_scalar_prefetch=3,
        grid=(MOE_BLOCKS,),
        in_specs=[pl.BlockSpec(memory_space=pltpu.VMEM),
                  pl.BlockSpec((1, D_MODEL, D_EXPERT), wmap),
                  pl.BlockSpec((1, D_MODEL, D_EXPERT), wmap)],
        out_specs=pl.BlockSpec((MOE_ROWS, D_EXPERT), lambda i, be, cd, nb: (i, 0)),
        scratch_shapes=[pltpu.VMEM((PACK_ROWS * MOE_ROWS, 128), jnp.int32),
                        pltpu.VMEM((D_MODEL, 2 * D_EXPERT), bf16)])
    return pl.pallas_call(
        _moe_up_kernel,
        grid_spec=grid_spec,
        out_shape=jax.ShapeDtypeStruct((MOE_BLOCKS * MOE_ROWS, D_EXPERT), bf16),
        compiler_params=pltpu.CompilerParams(dimension_semantics=("arbitrary",),
                                             vmem_limit_bytes=60 * 1024 * 1024),
        name="moe_up",
    )(block_e, code, nb, hp, w_gate, w_up)


SCATTER_UNROLL = 4


def _moe_down_kernel(be_ref, code_ref, nb_ref, w_ref, mid_ref, wd_ref, acc_ref, ybuf, wd_scr):
    i = pl.program_id(1)
    rows = MOE_ROWS
    half = D_MODEL // 2

    @pl.when(i == 0)
    def _():
        acc_ref[...] = jnp.zeros_like(acc_ref)

    @pl.when(i < nb_ref[0])
    def _():
        @pl.when(_expert_changed(be_ref, i))
        def _():
            wd_scr[...] = wd_ref[0].astype(bf16)

        y = jnp.dot(mid_ref[...], wd_scr[...], preferred_element_type=f32)
        for c in range(PACK_ROWS):
            ybuf[c * rows:(c + 1) * rows, :] = y[:, c * 128:(c + 1) * 128]

        def scatter(g, c):
            sums, addrs = [], []
            for j in range(SCATTER_UNROLL):
                r = g * SCATTER_UNROLL + j
                code = code_ref[i * rows + r]
                pad = code == PAD_CODE
                tok = jnp.where(pad, N_TOKENS, code & (N_TOKENS - 1))
                w = jnp.where(pad, 0.0, w_ref[jnp.minimum(code, PAD_CODE - 1)])
                a = pl.multiple_of(tok * PACK_ROWS, PACK_ROWS)
                v = ybuf[pl.ds(r, PACK_ROWS, stride=rows), :]
                sums.append(acc_ref[pl.ds(a, PACK_ROWS), :] + w * v)
                addrs.append(a)
            for j in range(SCATTER_UNROLL):
                acc_ref[pl.ds(addrs[j], PACK_ROWS), :] = sums[j]
            return c
        lax.fori_loop(0, rows // SCATTER_UNROLL, scatter, 0)


def _moe_down(block_e, code, nb, w8, mid, w_down):
    half = D_MODEL // 2
    acc_rows = (N_TOKENS + 1) * PACK_ROWS
    grid_spec = pltpu.PrefetchScalarGridSpec(
        num_scalar_prefetch=4,
        grid=(2, MOE_BLOCKS),
        in_specs=[pl.BlockSpec((MOE_ROWS, D_EXPERT), lambda p, i, be, cd, nb, w: (i, 0)),
                  pl.BlockSpec((1, D_EXPERT, half), lambda p, i, be, cd, nb, w: (be[i], 0, p))],
        out_specs=pl.BlockSpec((None, acc_rows, 128), lambda p, i, be, cd, nb, w: (p, 0, 0),
                               pipeline_mode=pl.Buffered(1)),
        scratch_shapes=[pltpu.VMEM((PACK_ROWS * MOE_ROWS, 128), f32),
                        pltpu.VMEM((D_EXPERT, half), bf16)])
    return pl.pallas_call(
        _moe_down_kernel,
        grid_spec=grid_spec,
        out_shape=jax.ShapeDtypeStruct((2, acc_rows, 128), f32),
        compiler_params=_params("arbitrary", "arbitrary"),
        name="moe_down",
    )(block_e, code, nb, w8, mid, w_down)


def _final_kernel(hb_ref, h_ref, r0_ref, r1_ref, wgu_ref, wd_ref, lg_ref, lb_ref, o_ref):
    tm = h_ref.shape[0]
    gu = jnp.dot(hb_ref[...], wgu_ref[...], preferred_element_type=f32)
    g, u = gu[:, :D_EXPERT], gu[:, D_EXPERT:]
    mid = (g * jax.nn.sigmoid(g) * u).astype(bf16)
    shared = jnp.dot(mid, wd_ref[...], preferred_element_type=f32)
    routed = jnp.concatenate([r[pl.ds(c, tm, stride=PACK_ROWS), :]
                              for r in (r0_ref, r1_ref) for c in range(PACK_ROWS)], axis=1)
    o_ref[...] = _layer_norm_rows(DEEPNORM_ALPHA * h_ref[...] + routed + shared,
                                  lg_ref[...], lb_ref[...])


def _final(hb, h, racc, wgu, wd, ln_g, ln_b, tm=256):
    n = h.shape[0]
    row = lambda: pl.BlockSpec((tm, D_MODEL), lambda i: (i, 0))
    full = lambda a: pl.BlockSpec(a.shape, lambda i: (0,) * a.ndim)
    acc = lambda p: pl.BlockSpec((None, tm * PACK_ROWS, 128), lambda i: (p, i, 0))
    return pl.pallas_call(
        _final_kernel,
        grid=(n // tm,),
        in_specs=[row(), row(), acc(0), acc(1), full(wgu), full(wd), full(ln_g), full(ln_b)],
        out_specs=row(),
        out_shape=jax.ShapeDtypeStruct((n, D_MODEL), f32),
        compiler_params=_params("parallel"),
        name="shared_final",
    )(hb, h, racc, racc, wgu, wd, ln_g, ln_b)


def _dispatch_plan(e8, pos8, cnt):
    counts = cnt[:, 0].astype(jnp.int32)
    padded = (counts + MOE_ROWS - 1) // MOE_ROWS * MOE_ROWS
    pad_end = jnp.cumsum(padded).astype(jnp.int32)
    pad_start = pad_end - padded
    seg_end = pad_start + counts
    dest = (pad_start[e8] + pos8).reshape(-1)
    block_e = jnp.minimum(jnp.searchsorted(pad_end, jnp.arange(MOE_BLOCKS) * MOE_ROWS, side='right'),
                          N_EXPERTS - 1).astype(jnp.int32)
    nb = (pad_end[-1:] // MOE_ROWS).astype(jnp.int32)
    return dest, seg_end, pad_end, block_e, nb


def kernel(x, w_in, ssm_log_dt, ssm_a_re, ssm_a_im, ssm_b_re, ssm_b_im, ssm_c_re, ssm_c_im, ssm_d,
           w_glu, g_ssm_out, g_attn_out, w_out, ln1_g, ln1_b, w_router, router_bias, w_gate, w_up,
           w_down, ws_gate, ws_up, ws_down, ln2_g, ln2_b):
    bsz, seq, d = x.shape
    n_tok = bsz * seq
    slopes = 2.0 ** (-8.0 * jnp.arange(1, N_HEADS + 1, dtype=f32) / N_HEADS)
    h = x.reshape(n_tok, d)
    for layer in range(DEPTH):
        proj = _matmul(h, w_in[layer].astype(bf16), bf16)
        u = proj[:, :D_SSM]
        q = proj[:, D_SSM:D_SSM + D_ATTN]
        k = proj[:, D_SSM + D_ATTN:D_SSM + 2 * D_ATTN]
        v = proj[:, D_SSM + 2 * D_ATTN:]

        t, gpb = S5_CHUNK, S5_GROUPS_PER_BLOCK
        nblk, n_chunk, w = SSM_GROUPS // gpb, seq // S5_CHUNK, gpb * SSM_CH
        tables = _s5_tables(ssm_log_dt[layer], ssm_a_re[layer], ssm_a_im[layer], ssm_b_re[layer],
                            ssm_b_im[layer], ssm_c_re[layer], ssm_c_im[layer], ssm_d[layer])
        ub = u.reshape(bsz * n_chunk, t, nblk, w).transpose(2, 0, 1, 3).reshape(nblk, bsz * n_chunk, t * w)
        yb = _s5_mixer(ub, tables, bsz, n_chunk)
        y = yb.reshape(nblk, bsz * n_chunk, t, w).transpose(1, 2, 0, 3).reshape(n_tok, D_SSM)
        z = _matmul(y, w_glu[layer].astype(bf16), f32)

        outs, lses = [], []
        for _, dil in PATTERNS:
            sub = seq // dil

            def to_sub(a):
                a = a.reshape(bsz, sub, dil, N_HEADS, HEAD_DIM).transpose(0, 2, 3, 1, 4)
                return a.reshape(bsz * dil * N_HEADS, sub, HEAD_DIM)

            def from_sub(a):
                a = a.reshape(bsz, dil, N_HEADS, sub, HEAD_DIM).transpose(0, 3, 1, 2, 4)
                return a.reshape(n_tok, D_ATTN)

            o, lse = _dilated_attention(to_sub(q), to_sub(k), to_sub(v), _attn_bias(slopes, dil), N_HEADS)
            outs.append(from_sub(o))
            lses.append(from_sub(lse))

        row2 = lambda a: a.astype(f32).reshape(1, -1)
        h, hb, hp = _mix_out(z, outs[0], outs[1], outs[2], lses[0], lses[1], lses[2], h,
                             w_out[layer].astype(bf16), row2(g_ssm_out[layer]), row2(g_attn_out[layer]),
                             row2(ln1_g[layer]), row2(ln1_b[layer]))

        assert n_tok == N_TOKENS
        e8, pos8, w8, cnt = _router(h, w_router[layer], router_bias[layer])
        dest, seg_end, pad_end, block_e, nb = _dispatch_plan(e8, pos8, cnt)
        code = _invert(dest, seg_end, pad_end, MOE_BLOCKS * MOE_ROWS)
        mid = _moe_up(block_e, code, nb, hp, w_gate[layer], w_up[layer])
        racc = _moe_down(block_e, code, nb, w8.reshape(-1), mid, w_down[layer])
        wgu = jnp.concatenate([ws_gate[layer], ws_up[layer]], axis=1).astype(bf16)
        h = _final(hb, h, racc, wgu, ws_down[layer].astype(bf16), row2(ln2_g[layer]), row2(ln2_b[layer]))
    return h.reshape(bsz, seq, d)
```

```python
import functools
import math

import jax
import jax.numpy as jnp
import numpy as np
from jax import lax
from jax.experimental import pallas as pl
from jax.experimental.pallas import tpu as pltpu

D_MODEL = 2048
D_SSM = 1024
D_ATTN = 1024
SSM_CH = 16
SSM_GROUPS = 64
SSM_STATE = 64
HEAD_DIM = 64
N_HEADS = 16
PATTERNS = ((128, 1), (512, 4), (2048, 16))
ATTN_BLOCK = 128
N_EXPERTS = 64
TOP_K = 8
N_EXPERT_GROUPS = 8
TOPK_GROUPS = 4
D_EXPERT = 512
ROUTED_SCALE = 2.5
NORM_EPS = 1e-5
DEPTH = 1
DEEPNORM_ALPHA = (2.0 * DEPTH) ** 0.25

S5_CHUNK = 16
S5_GROUPS_PER_BLOCK = 8
MOE_ROWS = 256
MASK_VALUE = -1e30
PACK_ROWS = D_MODEL // 2 // 128
N_TOKENS = 8192
TOK_BITS = 13
PAD_CODE = N_TOKENS * TOP_K
MOE_BLOCKS = -(-(N_TOKENS * TOP_K + N_EXPERTS * (MOE_ROWS - 1)) // MOE_ROWS)
VMEM_LIMIT = 56 * 1024 * 1024

bf16 = jnp.bfloat16
f32 = jnp.float32


def _params(*sem):
    return pltpu.CompilerParams(dimension_semantics=sem, vmem_limit_bytes=VMEM_LIMIT)


def _matmul_kernel(a_ref, b_ref, o_ref):
    a = a_ref[...].astype(bf16)
    o_ref[...] = jnp.dot(a, b_ref[...], preferred_element_type=f32).astype(o_ref.dtype)


def _matmul(a, b, out_dtype, tm=512, tn=1024):
    m, k = a.shape
    _, n = b.shape
    tn = min(tn, n)
    return pl.pallas_call(
        _matmul_kernel,
        grid=(m // tm, n // tn),
        in_specs=[pl.BlockSpec((tm, k), lambda i, j: (i, 0)),
                  pl.BlockSpec((k, tn), lambda i, j: (0, j))],
        out_specs=pl.BlockSpec((tm, tn), lambda i, j: (i, j)),
        out_shape=jax.ShapeDtypeStruct((m, n), out_dtype),
        compiler_params=_params("parallel", "arbitrary"),
        name="matmul",
    )(a, b)


def _s5_tables(log_dt, a_re, a_im, b_re, b_im, c_re, c_im, d_skip):
    t = S5_CHUNK
    gpb = S5_GROUPS_PER_BLOCK
    nblk = SSM_GROUPS // gpb
    hp = lax.Precision.HIGHEST
    lr = jnp.minimum(a_re.astype(f32), -1e-4)
    li = a_im.astype(f32)
    dt = jnp.exp(log_dt.astype(f32))
    kk = jnp.arange(t + 1, dtype=f32)[:, None, None]
    mag = jnp.exp(kk * (lr * dt))
    pr = mag * jnp.cos(kk * (li * dt))
    pi = mag * jnp.sin(kk * (li * dt))
    xr, xi = pr[1] - 1.0, pi[1]
    den = lr * lr + li * li
    cr = (xr * lr + xi * li) / den
    ci = (xi * lr - xr * li) / den
    bbr = cr[..., None] * b_re - ci[..., None] * b_im
    bbi = cr[..., None] * b_im + ci[..., None] * b_re
    eye = jnp.eye(gpb, dtype=f32)

    wr = pr[:t, :, :, None] * bbr - pi[:t, :, :, None] * bbi
    wi = pr[:t, :, :, None] * bbi + pi[:t, :, :, None] * bbr
    taps = (jnp.einsum('gop,tgpc->tgco', c_re, wr, precision=hp)
            - jnp.einsum('gop,tgpc->tgco', c_im, wi, precision=hp))
    taps = taps.reshape(t, nblk, gpb, SSM_CH, SSM_CH)
    ktab = jnp.einsum('tbgco,gh->btgcho', taps, eye).reshape(nblk, t, gpb * SSM_CH, gpb * SSM_CH)

    rev = jnp.arange(t - 1, -1, -1)
    sr = pr[rev][..., None] * bbr - pi[rev][..., None] * bbi
    si = pr[rev][..., None] * bbi + pi[rev][..., None] * bbr
    sb = jnp.stack([sr, si], axis=0).reshape(2, t, nblk, gpb, SSM_STATE, SSM_CH)
    bpow = jnp.einsum('zsbgpc,gh->bsgczhp', sb, eye)
    bpow = bpow.reshape(nblk, t * gpb * SSM_CH, 2 * gpb * SSM_STATE)

    er = c_re[None] * pr[1:, :, None, :] - c_im[None] * pi[1:, :, None, :]
    ei = c_re[None] * pi[1:, :, None, :] + c_im[None] * pr[1:, :, None, :]
    eb = jnp.stack([er, -ei], axis=0).reshape(2, t, nblk, gpb, SSM_CH, SSM_STATE)
    cpow = jnp.einsum('ztbgop,gh->bzgptho', eb, eye)
    cpow = cpow.reshape(nblk, 2 * gpb * SSM_STATE, t * gpb * SSM_CH)

    a_chunk = jnp.stack([pr[t], pi[t]], axis=0).reshape(2, nblk, 1, gpb * SSM_STATE)
    a_chunk = a_chunk.transpose(1, 0, 2, 3).reshape(nblk, 2, gpb * SSM_STATE)
    dvec = jnp.tile(d_skip.astype(f32).reshape(nblk, 1, gpb * SSM_CH), (1, 1, t))
    return ktab.astype(bf16), bpow.astype(bf16), cpow.astype(bf16), a_chunk, dvec


def _s5_kernel(u_ref, ktab_ref, bpow_ref, cpow_ref, a_ref, d_ref, y_ref,
               toep_ref, s_ref, h_ref, *, n_batch, n_chunk):
    t = S5_CHUNK
    w = S5_GROUPS_PER_BLOCK * SSM_CH
    ns = S5_GROUPS_PER_BLOCK * SSM_STATE
    for tt in range(t):
        for ss in range(tt + 1):
            toep_ref[ss * w:(ss + 1) * w, tt * w:(tt + 1) * w] = ktab_ref[0, tt - ss]
        if tt % 2 == 0:
            toep_ref[(tt + 1) * w:(tt + 2) * w, tt * w:(tt + 1) * w] = jnp.zeros((w, w), bf16)

    u = u_ref[0]
    s_ref[...] = jnp.dot(u, bpow_ref[0], preferred_element_type=f32)

    ar = a_ref[0, 0:1, :]
    ai = a_ref[0, 1:2, :]

    def step(j, carry):
        new = []
        for b in range(n_batch):
            hr, hi = carry[2 * b], carry[2 * b + 1]
            row = b * n_chunk + j
            h_ref[pl.ds(row, 1), 0:ns] = hr
            h_ref[pl.ds(row, 1), ns:2 * ns] = hi
            sr = s_ref[pl.ds(row, 1), 0:ns]
            si = s_ref[pl.ds(row, 1), ns:2 * ns]
            new.append(ar * hr - ai * hi + sr)
            new.append(ar * hi + ai * hr + si)
        return tuple(new)

    zero = jnp.zeros((1, ns), f32)
    lax.fori_loop(0, n_chunk, step, (zero,) * (2 * n_batch))

    hprev = h_ref[...].astype(bf16)
    for tp in range(t // 2):
        c0, c1 = 2 * tp * w, (2 * tp + 2) * w
        y = jnp.dot(u[:, :c1], toep_ref[0:c1, c0:c1], preferred_element_type=f32)
        y = y + jnp.dot(hprev, cpow_ref[0, :, c0:c1], preferred_element_type=f32)
        y = y + d_ref[0, :, c0:c1] * u[:, c0:c1].astype(f32)
        y_ref[0, :, c0:c1] = jax.nn.gelu(y, approximate=True).astype(y_ref.dtype)


def _s5_mixer(u_blocks, tables, n_batch, n_chunk):
    ktab, bpow, cpow, a_chunk, dvec = tables
    nblk, rows, cols = u_blocks.shape
    ns = S5_GROUPS_PER_BLOCK * SSM_STATE
    kern = functools.partial(_s5_kernel, n_batch=n_batch, n_chunk=n_chunk)
    return pl.pallas_call(
        kern,
        grid=(nblk,),
        in_specs=[pl.BlockSpec((1, rows, cols), lambda g: (g, 0, 0)),
                  pl.BlockSpec((1,) + ktab.shape[1:], lambda g: (g, 0, 0, 0)),
                  pl.BlockSpec((1,) + bpow.shape[1:], lambda g: (g, 0, 0)),
                  pl.BlockSpec((1,) + cpow.shape[1:], lambda g: (g, 0, 0)),
                  pl.BlockSpec((1, 2, ns), lambda g: (g, 0, 0)),
                  pl.BlockSpec((1, 1, cols), lambda g: (g, 0, 0))],
        out_specs=pl.BlockSpec((1, rows, cols), lambda g: (g, 0, 0)),
        out_shape=jax.ShapeDtypeStruct((nblk, rows, cols), bf16),
        scratch_shapes=[pltpu.VMEM((cols, cols), bf16),
                        pltpu.VMEM((rows, 2 * ns), f32),
                        pltpu.VMEM((rows, 2 * ns), f32)],
        compiler_params=_params("parallel"),
        name="s5_mixer",
    )(u_blocks, ktab, bpow, cpow, a_chunk, dvec)


def _attn_bias_table():
    blk = ATTN_BLOCK
    slopes = 2.0 ** (-8.0 * jnp.arange(1, N_HEADS + 1, dtype=f32) / N_HEADS)
    delta = np.arange(blk)[:, None] - (np.arange(2 * blk)[None, :] - blk)
    tabs = []
    for window, dil in PATTERNS:
        assert window // dil == blk
        valid = (delta >= 0) & (delta <= window // dil)
        dist = jnp.asarray(delta * dil, dtype=f32)
        bias = jnp.where(valid[None], -slopes[:, None, None] * dist[None], MASK_VALUE)
        tabs.append(bias.reshape(N_HEADS // 2, 2 * blk, 2 * blk))
    return jnp.stack(tabs, axis=1)


def _attn_kernel(q_ref, k_ref, v_ref, bias_ref, o_ref, *scr):
    blk = ATTN_BLOCK
    seq = q_ref.shape[0]
    first_head = lax.broadcasted_iota(jnp.int32, (blk, 2 * HEAD_DIM), 1) < HEAD_DIM
    dims = (((1,), (1,)), ((), ()))
    for pi, (_, dil) in enumerate(PATTERNS):
        o_scr, l_scr = scr[2 * pi], scr[2 * pi + 1]
        sub = seq // dil
        for r in range(dil):
            rows = (lambda st, n: pl.ds(st, n)) if dil == 1 else (lambda st, n: pl.ds(st, n, stride=dil))
            qd = (q_ref[rows(r, sub), :] * HEAD_DIM ** -0.5).astype(bf16)
            kd = k_ref[rows(r, sub), :].astype(bf16)
            vd = v_ref[rows(r, sub), :].astype(bf16)
            for i in range(sub // blk):
                qb = qd[i * blk:(i + 1) * blk]
                zero = jnp.zeros_like(qb)
                q2 = jnp.concatenate([jnp.where(first_head, qb, zero), jnp.where(first_head, zero, qb)], axis=0)
                k0 = max(i - 1, 0) * blk
                nk = (i + 1) * blk - k0
                s = lax.dot_general(q2, kd[k0:k0 + nk], dims, preferred_element_type=f32)
                s = s + bias_ref[0, pi, :, 2 * blk - nk:]
                m = jnp.max(s, axis=-1, keepdims=True)
                p = jnp.exp(s - m)
                l = jnp.sum(p, axis=-1, keepdims=True)
                o = jnp.dot(p.astype(bf16), vd[k0:k0 + nk], preferred_element_type=f32) / l
                lse = m + jnp.log(l)
                dst = rows(r + dil * blk * i, blk)
                o_scr[dst, :] = jnp.where(first_head, o[:blk], o[blk:])
                l_scr[dst, :] = jnp.where(first_head, lse[:blk], lse[blk:])
    l1, l2, l3 = scr[1][...], scr[3][...], scr[5][...]
    m = jnp.maximum(jnp.maximum(l1, l2), l3)
    e1, e2, e3 = jnp.exp(l1 - m), jnp.exp(l2 - m), jnp.exp(l3 - m)
    o_ref[...] = (e1 * scr[0][...] + e2 * scr[2][...] + e3 * scr[4][...]) / (e1 + e2 + e3)


def _dilated_attention(qkv, bsz, seq):
    pairs = N_HEADS // 2
    width = 2 * HEAD_DIM
    bias = _attn_bias_table()
    col = lambda off: pl.BlockSpec((seq, width), lambda hp, b: (b, off + hp))
    return pl.pallas_call(
        _attn_kernel,
        grid=(pairs, bsz),
        in_specs=[col(0), col(pairs), col(2 * pairs),
                  pl.BlockSpec((1,) + bias.shape[1:], lambda hp, b: (hp, 0, 0, 0))],
        out_specs=pl.BlockSpec((seq, width), lambda hp, b: (b, hp)),
        out_shape=jax.ShapeDtypeStruct((bsz * seq, D_ATTN), f32),
        scratch_shapes=[pltpu.VMEM((seq, width), f32)] * (2 * len(PATTERNS)),
        compiler_params=_params("parallel", "parallel"),
        name="dilated_attention",
    )(qkv, qkv, qkv, bias)


def _layer_norm_rows(x, g, b):
    mu = jnp.mean(x, axis=-1, keepdims=True)
    xc = x - mu
    var = jnp.mean(xc * xc, axis=-1, keepdims=True)
    return xc * lax.rsqrt(var + NORM_EPS) * g + b


def _rms_rows(x, g):
    return x * lax.rsqrt(jnp.mean(x * x, axis=-1, keepdims=True) + NORM_EPS) * g


def _mix_out_kernel(z_ref, ya_ref, x_ref, w_ref, gs_ref, ga_ref, lg_ref, lb_ref, h_ref, hb_ref, hp_ref):
    z = z_ref[...]
    y_ssm = z[:, :D_SSM] * jax.nn.sigmoid(z[:, D_SSM:])
    ns = _rms_rows(y_ssm, gs_ref[...]).astype(bf16)
    na = _rms_rows(ya_ref[...], ga_ref[...]).astype(bf16)
    proj = jnp.dot(ns, w_ref[0:D_SSM, :], preferred_element_type=f32)
    proj = proj + jnp.dot(na, w_ref[D_SSM:, :], preferred_element_type=f32)
    h = _layer_norm_rows(DEEPNORM_ALPHA * x_ref[...] + proj, lg_ref[...], lb_ref[...])
    h_ref[...] = h
    hb_ref[...] = h.astype(bf16)
    half = D_MODEL // 2
    word = pltpu.pack_elementwise([h[:, :half], h[:, half:]], packed_dtype=bf16)
    tm = word.shape[0]
    for s in range(PACK_ROWS):
        hp_ref[pl.ds(s, tm, stride=PACK_ROWS), :] = word[:, s * 128:(s + 1) * 128]


def _mix_out(z, y_attn, x, w_out, g_ssm, g_attn, ln_g, ln_b, tm=256):
    n = x.shape[0]
    row = lambda c: pl.BlockSpec((tm, c), lambda i: (i, 0))
    full = lambda a: pl.BlockSpec(a.shape, lambda i: (0,) * a.ndim)
    return pl.pallas_call(
        _mix_out_kernel,
        grid=(n // tm,),
        in_specs=[row(2 * D_SSM), row(D_ATTN), row(D_MODEL), full(w_out), full(g_ssm), full(g_attn),
                  full(ln_g), full(ln_b)],
        out_specs=[row(D_MODEL), row(D_MODEL), pl.BlockSpec((tm * PACK_ROWS, 128), lambda i: (i, 0))],
        out_shape=[jax.ShapeDtypeStruct((n, D_MODEL), f32), jax.ShapeDtypeStruct((n, D_MODEL), bf16),
                   jax.ShapeDtypeStruct((n * PACK_ROWS, 128), jnp.int32)],
        compiler_params=_params("parallel"),
        name="mix_out",
    )(z, y_attn, x, w_out, g_ssm, g_attn, ln_g, ln_b)


def _router_kernel(h_ref, wrt_ref, bias_ref, tri_ref, trie_ref, e8_ref, pos8_ref, w8_ref, cnt_ref):
    gsz = N_EXPERTS // N_EXPERT_GROUPS
    tm = h_ref.shape[0]
    ninf = -jnp.inf

    @pl.when(pl.program_id(0) == 0)
    def _():
        cnt_ref[...] = jnp.zeros_like(cnt_ref)

    logits = lax.dot_general(wrt_ref[...], h_ref[...], (((1,), (1,)), ((), ())),
                             preferred_element_type=f32, precision=lax.Precision.HIGHEST)
    scores = jax.nn.sigmoid(logits)
    sel = scores + bias_ref[...]
    io = lax.broadcasted_iota(jnp.int32, (gsz, tm), 0)

    blks, gs_rows = [], []
    for g in range(N_EXPERT_GROUPS):
        blk = sel[g * gsz:(g + 1) * gsz, :]
        m1 = jnp.max(blk, axis=0, keepdims=True)
        first = jnp.min(jnp.where(blk == m1, io, gsz), axis=0, keepdims=True)
        m2 = jnp.max(jnp.where(io == first, ninf, blk), axis=0, keepdims=True)
        blks.append(blk)
        gs_rows.append(m1 + m2)
    gs = jnp.concatenate(gs_rows, axis=0)

    iog = lax.broadcasted_iota(jnp.int32, (N_EXPERT_GROUPS, tm), 0)
    beaten = jnp.zeros((N_EXPERT_GROUPS, tm), f32)
    for gp in range(N_EXPERT_GROUPS):
        row = gs_rows[gp]
        tie = jnp.where(iog > gp, 1.0, 0.0)
        beaten = beaten + jnp.where(row > gs, 1.0, jnp.where(row == gs, tie, 0.0))
    keep = beaten < TOPK_GROUPS
    masked = [jnp.where(keep[g:g + 1, :], blks[g], ninf) for g in range(N_EXPERT_GROUPS)]

    ranks = [jnp.zeros((gsz, tm), f32) for _ in range(N_EXPERT_GROUPS)]
    tie_in = [jnp.where(io > j, 1.0, 0.0) for j in range(gsz)]
    for gp in range(N_EXPERT_GROUPS):
        for j in range(gsz):
            row = masked[gp][j:j + 1, :]
            for g in range(N_EXPERT_GROUPS):
                if gp < g:
                    inc = jnp.where(row >= masked[g], 1.0, 0.0)
                elif gp > g:
                    inc = jnp.where(row > masked[g], 1.0, 0.0)
                else:
                    inc = jnp.where(row > masked[g], 1.0, jnp.where(row == masked[g], tie_in[j], 0.0))
                ranks[g] = ranks[g] + inc
    selb = jnp.concatenate([jnp.where(r < TOP_K, 1.0, 0.0) for r in ranks], axis=0)
    wsel = selb * scores
    wn = wsel / jnp.sum(wsel, axis=0, keepdims=True) * ROUTED_SCALE

    maskb = selb.astype(bf16)
    pos = jnp.dot(maskb, tri_ref[...], preferred_element_type=f32) + cnt_ref[:, 0:1]
    cnt_ref[...] = cnt_ref[...] + jnp.sum(selb, axis=1, keepdims=True)
    slot = jnp.dot(trie_ref[...], maskb, preferred_element_type=f32)
    ioe = lax.broadcasted_iota(jnp.int32, (N_EXPERTS, tm), 0).astype(f32)
    e_rows, p_rows, w_rows = [], [], []
    for k in range(TOP_K):
        hit = jnp.where(slot == k, selb, 0.0)
        e_rows.append(jnp.sum(hit * ioe, axis=0, keepdims=True))
        p_rows.append(jnp.sum(hit * pos, axis=0, keepdims=True))
        w_rows.append(jnp.sum(hit * wn, axis=0, keepdims=True))
    e8_ref[...] = jnp.concatenate(e_rows, axis=0).astype(jnp.int32)
    pos8_ref[...] = jnp.concatenate(p_rows, axis=0).astype(jnp.int32)
    w8_ref[...] = jnp.concatenate(w_rows, axis=0)


def _router(h, w_router, router_bias, tm=512):
    n = h.shape[0]
    wrt = w_router.astype(f32).T
    bias = router_bias.astype(f32).reshape(N_EXPERTS, 1)
    tri = (jnp.arange(tm)[:, None] < jnp.arange(tm)[None, :]).astype(bf16)
    trie = (jnp.arange(N_EXPERTS)[None, :] < jnp.arange(N_EXPERTS)[:, None]).astype(bf16)
    full = lambda a: pl.BlockSpec(a.shape, lambda i: (0,) * a.ndim)
    tok = lambda: pl.BlockSpec((TOP_K, tm), lambda i: (0, i))
    return pl.pallas_call(
        _router_kernel,
        grid=(n // tm,),
        in_specs=[pl.BlockSpec((tm, D_MODEL), lambda i: (i, 0)), full(wrt), full(bias), full(tri), full(trie)],
        out_specs=[tok(), tok(), tok(), pl.BlockSpec((N_EXPERTS, 128), lambda i: (0, 0))],
        out_shape=[jax.ShapeDtypeStruct((TOP_K, n), jnp.int32), jax.ShapeDtypeStruct((TOP_K, n), jnp.int32),
                   jax.ShapeDtypeStruct((TOP_K, n), f32), jax.ShapeDtypeStruct((N_EXPERTS, 128), f32)],
        compiler_params=_params("arbitrary"),
        name="router",
    )(h, wrt, bias, tri, trie)


def _invert_kernel(dest_ref, seg_end_ref, pad_end_ref, code_ref):
    def fill(lo, hi):
        def body(r, c):
            code_ref[r] = PAD_CODE
            return c
        lax.fori_loop(lo, hi, body, 0)

    def per_expert(e, c):
        fill(seg_end_ref[e], pad_end_ref[e])
        return c
    lax.fori_loop(0, N_EXPERTS, per_expert, 0)
    fill(pad_end_ref[N_EXPERTS - 1], code_ref.shape[0])

    def body(p, c):
        code_ref[dest_ref[p]] = p
        return c
    lax.fori_loop(0, dest_ref.shape[0], body, 0, unroll=8)


def _invert(dest, seg_end, pad_end, n_rows):
    smem = lambda: pl.BlockSpec(memory_space=pltpu.SMEM)
    return pl.pallas_call(
        _invert_kernel,
        in_specs=[smem(), smem(), smem()],
        out_specs=smem(),
        out_shape=jax.ShapeDtypeStruct((n_rows,), jnp.int32),
        name="invert_dispatch",
    )(dest, seg_end, pad_end)


def _expert_changed(be_ref, i):
    return jnp.logical_or(i == 0, be_ref[i] != be_ref[jnp.maximum(i - 1, 0)])


GATHER_UNROLL = 8
SCATTER_UNROLL = 8


def _moe_up_kernel(be_ref, code_ref, nb_ref, w_ref, hp_ref, wg_ref, wu_ref, mid_ref, xg_scr, wcol_scr, wgu_scr):
    i = pl.program_id(0)
    rows = MOE_ROWS

    @pl.when(i < nb_ref[0])
    def _():
        @pl.when(_expert_changed(be_ref, i))
        def _():
            wgu_scr[:, :D_EXPERT] = wg_ref[0].astype(bf16)
            wgu_scr[:, D_EXPERT:] = wu_ref[0].astype(bf16)

        def gather(g, c):
            for j in range(GATHER_UNROLL):
                r = g * GATHER_UNROLL + j
                code = code_ref[i * rows + r]
                tok = (code >> 3) & (N_TOKENS - 1)
                v = hp_ref[pl.ds(pl.multiple_of(tok * PACK_ROWS, PACK_ROWS), PACK_ROWS), :]
                xg_scr[pl.ds(r, PACK_ROWS, stride=rows), :] = v
                w = jnp.where(code < PAD_CODE, w_ref[jnp.minimum(code, PAD_CODE - 1)], 0.0)
                wcol_scr[pl.ds(r, 1), :] = jnp.full((1, 128), w, f32)
            return c
        lax.fori_loop(0, rows // GATHER_UNROLL, gather, 0)

        lo, hi = [], []
        for s in range(PACK_ROWS):
            wds = xg_scr[s * rows:(s + 1) * rows, :]
            lo.append(pltpu.unpack_elementwise(wds, index=0, packed_dtype=bf16, unpacked_dtype=f32).astype(bf16))
            hi.append(pltpu.unpack_elementwise(wds, index=1, packed_dtype=bf16, unpacked_dtype=f32).astype(bf16))
        x = jnp.concatenate(lo + hi, axis=1)
        gu = jnp.dot(x, wgu_scr[...], preferred_element_type=f32)
        g, u = gu[:, :D_EXPERT], gu[:, D_EXPERT:]
        wt = jnp.concatenate([wcol_scr[...]] * (D_EXPERT // 128), axis=1)
        mid_ref[...] = (g * jax.nn.sigmoid(g) * u * wt).astype(bf16)

    @pl.when(i >= nb_ref[0])
    def _():
        mid_ref[...] = jnp.zeros_like(mid_ref)


def _moe_up(block_e, code, nb, w8, hp, w_gate, w_up):
    wmap = lambda i, be, cd, nb, w: (be[i], 0, 0)
    grid_spec = pltpu.PrefetchScalarGridSpec(
        num_scalar_prefetch=4,
        grid=(MOE_BLOCKS,),
        in_specs=[pl.BlockSpec(memory_space=pltpu.VMEM),
                  pl.BlockSpec((1, D_MODEL, D_EXPERT), wmap),
                  pl.BlockSpec((1, D_MODEL, D_EXPERT), wmap)],
        out_specs=pl.BlockSpec((MOE_ROWS, D_EXPERT), lambda i, be, cd, nb, w: (i, 0)),
        scratch_shapes=[pltpu.VMEM((PACK_ROWS * MOE_ROWS, 128), jnp.int32),
                        pltpu.VMEM((MOE_ROWS, 128), f32),
                        pltpu.VMEM((D_MODEL, 2 * D_EXPERT), bf16)])
    return pl.pallas_call(
        _moe_up_kernel,
        grid_spec=grid_spec,
        out_shape=jax.ShapeDtypeStruct((MOE_BLOCKS * MOE_ROWS, D_EXPERT), bf16),
        compiler_params=pltpu.CompilerParams(dimension_semantics=("arbitrary",),
                                             vmem_limit_bytes=60 * 1024 * 1024),
        name="moe_up",
    )(block_e, code, nb, w8, hp, w_gate, w_up)


def _moe_down_kernel(be_ref, code_ref, nb_ref, mid_ref, wd_ref, acc_ref, ybuf, wd_scr):
    i = pl.program_id(1)
    rows = MOE_ROWS

    @pl.when(i == 0)
    def _():
        acc_ref[...] = jnp.zeros_like(acc_ref)

    @pl.when(i < nb_ref[0])
    def _():
        @pl.when(_expert_changed(be_ref, i))
        def _():
            wd_scr[...] = wd_ref[0].astype(bf16)

        y = jnp.dot(mid_ref[...], wd_scr[...], preferred_element_type=f32)
        for c in range(PACK_ROWS):
            ybuf[c * rows:(c + 1) * rows, :] = y[:, c * 128:(c + 1) * 128]

        def scatter(g, c):
            sums, addrs = [], []
            for j in range(SCATTER_UNROLL):
                r = g * SCATTER_UNROLL + j
                a = pl.multiple_of(code_ref[i * rows + r] & -PACK_ROWS, PACK_ROWS)
                v = ybuf[pl.ds(r, PACK_ROWS, stride=rows), :]
                sums.append(acc_ref[pl.ds(a, PACK_ROWS), :] + v)
                addrs.append(a)
            for j in range(SCATTER_UNROLL):
                acc_ref[pl.ds(addrs[j], PACK_ROWS), :] = sums[j]
            return c
        lax.fori_loop(0, rows // SCATTER_UNROLL, scatter, 0)


def _moe_down(block_e, code, nb, mid, w_down):
    half = D_MODEL // 2
    acc_rows = (N_TOKENS + 1) * PACK_ROWS
    grid_spec = pltpu.PrefetchScalarGridSpec(
        num_scalar_prefetch=3,
        grid=(2, MOE_BLOCKS),
        in_specs=[pl.BlockSpec((MOE_ROWS, D_EXPERT), lambda p, i, be, cd, nb: (i, 0)),
                  pl.BlockSpec((1, D_EXPERT, half), lambda p, i, be, cd, nb: (be[i], 0, p))],
        out_specs=pl.BlockSpec((None, acc_rows, 128), lambda p, i, be, cd, nb: (p, 0, 0),
                               pipeline_mode=pl.Buffered(1)),
        scratch_shapes=[pltpu.VMEM((PACK_ROWS * MOE_ROWS, 128), f32),
                        pltpu.VMEM((D_EXPERT, half), bf16)])
    return pl.pallas_call(
        _moe_down_kernel,
        grid_spec=grid_spec,
        out_shape=jax.ShapeDtypeStruct((2, acc_rows, 128), f32),
        compiler_params=_params("arbitrary", "arbitrary"),
        name="moe_down",
    )(block_e, code, nb, mid, w_down)


def _final_kernel(hb_ref, h_ref, r0_ref, r1_ref, wgu_ref, wd_ref, lg_ref, lb_ref, o_ref):
    tm = h_ref.shape[0]
    gu = jnp.dot(hb_ref[...], wgu_ref[...], preferred_element_type=f32)
    g, u = gu[:, :D_EXPERT], gu[:, D_EXPERT:]
    mid = (g * jax.nn.sigmoid(g) * u).astype(bf16)
    shared = jnp.dot(mid, wd_ref[...], preferred_element_type=f32)
    routed = jnp.concatenate([r[pl.ds(c, tm, stride=PACK_ROWS), :]
                              for r in (r0_ref, r1_ref) for c in range(PACK_ROWS)], axis=1)
    o_ref[...] = _layer_norm_rows(DEEPNORM_ALPHA * h_ref[...] + routed + shared,
                                  lg_ref[...], lb_ref[...])


def _final(hb, h, racc, wgu, wd, ln_g, ln_b, tm=256):
    n = h.shape[0]
    row = lambda: pl.BlockSpec((tm, D_MODEL), lambda i: (i, 0))
    full = lambda a: pl.BlockSpec(a.shape, lambda i: (0,) * a.ndim)
    acc = lambda p: pl.BlockSpec((None, tm * PACK_ROWS, 128), lambda i: (p, i, 0))
    return pl.pallas_call(
        _final_kernel,
        grid=(n // tm,),
        in_specs=[row(), row(), acc(0), acc(1), full(wgu), full(wd), full(ln_g), full(ln_b)],
        out_specs=row(),
        out_shape=jax.ShapeDtypeStruct((n, D_MODEL), f32),
        compiler_params=_params("parallel"),
        name="shared_final",
    )(hb, h, racc, racc, wgu, wd, ln_g, ln_b)


def _dispatch_plan(e8, pos8, cnt):
    counts = cnt[:, 0].astype(jnp.int32)
    padded = (counts + MOE_ROWS - 1) // MOE_ROWS * MOE_ROWS
    pad_end = jnp.cumsum(padded).astype(jnp.int32)
    pad_start = pad_end - padded
    seg_end = pad_start + counts
    ids = jnp.arange(N_EXPERTS, dtype=jnp.int32)
    start8 = jnp.sum(jnp.where(e8[..., None] == ids, pad_start, 0), axis=-1)
    dest = (start8 + pos8).T.reshape(-1)
    block_start = jnp.arange(MOE_BLOCKS, dtype=jnp.int32) * MOE_ROWS
    block_e = jnp.minimum(jnp.sum((pad_end[None, :] <= block_start[:, None]).astype(jnp.int32), axis=1),
                          N_EXPERTS - 1)
    nb = (pad_end[-1:] // MOE_ROWS).astype(jnp.int32)
    return dest, seg_end, pad_end, block_e, nb


def kernel(x, w_in, ssm_log_dt, ssm_a_re, ssm_a_im, ssm_b_re, ssm_b_im, ssm_c_re, ssm_c_im, ssm_d,
           w_glu, g_ssm_out, g_attn_out, w_out, ln1_g, ln1_b, w_router, router_bias, w_gate, w_up,
           w_down, ws_gate, ws_up, ws_down, ln2_g, ln2_b):
    bsz, seq, d = x.shape
    n_tok = bsz * seq
    h = x.reshape(n_tok, d)
    for layer in range(DEPTH):
        w_in_b = w_in[layer].astype(bf16)
        u = _matmul(h, w_in_b[:, :D_SSM], bf16)
        qkv = _matmul(h, w_in_b[:, D_SSM:], f32)

        t, gpb = S5_CHUNK, S5_GROUPS_PER_BLOCK
        nblk, n_chunk, w = SSM_GROUPS // gpb, seq // S5_CHUNK, gpb * SSM_CH
        tables = _s5_tables(ssm_log_dt[layer], ssm_a_re[layer], ssm_a_im[layer], ssm_b_re[layer],
                            ssm_b_im[layer], ssm_c_re[layer], ssm_c_im[layer], ssm_d[layer])
        ub = u.reshape(bsz * n_chunk, t, nblk, w).transpose(2, 0, 1, 3).reshape(nblk, bsz * n_chunk, t * w)
        yb = _s5_mixer(ub, tables, bsz, n_chunk)
        y = yb.reshape(nblk, bsz * n_chunk, t, w).transpose(1, 2, 0, 3).reshape(n_tok, D_SSM)
        z = _matmul(y, w_glu[layer].astype(bf16), f32)

        y_attn = _dilated_attention(qkv, bsz, seq)

        row2 = lambda a: a.astype(f32).reshape(1, -1)
        h, hb, hp = _mix_out(z, y_attn, h, w_out[layer].astype(bf16), row2(g_ssm_out[layer]),
                             row2(g_attn_out[layer]), row2(ln1_g[layer]), row2(ln1_b[layer]))

        assert n_tok == N_TOKENS
        e8, pos8, w8, cnt = _router(h, w_router[layer], router_bias[layer])
        dest, seg_end, pad_end, block_e, nb = _dispatch_plan(e8, pos8, cnt)
        code = _invert(dest, seg_end, pad_end, MOE_BLOCKS * MOE_ROWS)
        mid = _moe_up(block_e, code, nb, w8.T.reshape(-1), hp, w_gate[layer], w_up[layer])
        racc = _moe_down(block_e, code, nb, mid, w_down[layer])
        wgu = jnp.concatenate([ws_gate[layer], ws_up[layer]], axis=1).astype(bf16)
        h = _final(hb, h, racc, wgu, ws_down[layer].astype(bf16), row2(ln2_g[layer]), row2(ln2_b[layer]))
    return h.reshape(bsz, seq, d)
```

```python
import functools
import math

import jax
import jax.numpy as jnp
import numpy as np
from jax import lax
from jax.experimental import pallas as pl
from jax.experimental.pallas import tpu as pltpu

D_MODEL = 2048
D_SSM = 1024
D_ATTN = 1024
SSM_CH = 16
SSM_GROUPS = 64
SSM_STATE = 64
HEAD_DIM = 64
N_HEADS = 16
PATTERNS = ((128, 1), (512, 4), (2048, 16))
ATTN_BLOCK = 128
N_EXPERTS = 64
TOP_K = 8
N_EXPERT_GROUPS = 8
TOPK_GROUPS = 4
D_EXPERT = 512
ROUTED_SCALE = 2.5
NORM_EPS = 1e-5
DEPTH = 1
DEEPNORM_ALPHA = (2.0 * DEPTH) ** 0.25

S5_CHUNK = 16
S5_GROUPS_PER_BLOCK = 8
MOE_ROWS = 256
MASK_VALUE = -1e30
PACK_ROWS = D_MODEL // 2 // 128
N_TOKENS = 8192
TOK_BITS = 13
PAD_CODE = N_TOKENS * TOP_K
MOE_BLOCKS = -(-(N_TOKENS * TOP_K + N_EXPERTS * (MOE_ROWS - 1)) // MOE_ROWS)
VMEM_LIMIT = 56 * 1024 * 1024

bf16 = jnp.bfloat16
f32 = jnp.float32


def _params(*sem):
    return pltpu.CompilerParams(dimension_semantics=sem, vmem_limit_bytes=VMEM_LIMIT)


def _matmul_kernel(a_ref, b_ref, o_ref):
    a = a_ref[...].astype(bf16)
    o_ref[...] = jnp.dot(a, b_ref[...], preferred_element_type=f32).astype(o_ref.dtype)


def _matmul(a, b, out_dtype, tm=512, tn=1024):
    m, k = a.shape
    _, n = b.shape
    tn = min(tn, n)
    return pl.pallas_call(
        _matmul_kernel,
        grid=(m // tm, n // tn),
        in_specs=[pl.BlockSpec((tm, k), lambda i, j: (i, 0)),
                  pl.BlockSpec((k, tn), lambda i, j: (0, j))],
        out_specs=pl.BlockSpec((tm, tn), lambda i, j: (i, j)),
        out_shape=jax.ShapeDtypeStruct((m, n), out_dtype),
        compiler_params=_params("parallel", "arbitrary"),
        name="matmul",
    )(a, b)


def _s5_tables(log_dt, a_re, a_im, b_re, b_im, c_re, c_im, d_skip):
    t = S5_CHUNK
    gpb = S5_GROUPS_PER_BLOCK
    nblk = SSM_GROUPS // gpb
    hp = lax.Precision.HIGHEST
    lr = jnp.minimum(a_re.astype(f32), -1e-4)
    li = a_im.astype(f32)
    dt = jnp.exp(log_dt.astype(f32))
    kk = jnp.arange(t + 1, dtype=f32)[:, None, None]
    mag = jnp.exp(kk * (lr * dt))
    pr = mag * jnp.cos(kk * (li * dt))
    pi = mag * jnp.sin(kk * (li * dt))
    xr, xi = pr[1] - 1.0, pi[1]
    den = lr * lr + li * li
    cr = (xr * lr + xi * li) / den
    ci = (xi * lr - xr * li) / den
    bbr = cr[..., None] * b_re - ci[..., None] * b_im
    bbi = cr[..., None] * b_im + ci[..., None] * b_re
    eye = jnp.eye(gpb, dtype=f32)

    wr = pr[:t, :, :, None] * bbr - pi[:t, :, :, None] * bbi
    wi = pr[:t, :, :, None] * bbi + pi[:t, :, :, None] * bbr
    taps = (jnp.einsum('gop,tgpc->tgco', c_re, wr, precision=hp)
            - jnp.einsum('gop,tgpc->tgco', c_im, wi, precision=hp))
    taps = taps.reshape(t, nblk, gpb, SSM_CH, SSM_CH)
    ktab = jnp.einsum('tbgco,gh->btgcho', taps, eye).reshape(nblk, t, gpb * SSM_CH, gpb * SSM_CH)

    rev = jnp.arange(t - 1, -1, -1)
    sr = pr[rev][..., None] * bbr - pi[rev][..., None] * bbi
    si = pr[rev][..., None] * bbi + pi[rev][..., None] * bbr
    sb = jnp.stack([sr, si], axis=0).reshape(2, t, nblk, gpb, SSM_STATE, SSM_CH)
    bpow = jnp.einsum('zsbgpc,gh->bsgczhp', sb, eye)
    bpow = bpow.reshape(nblk, t * gpb * SSM_CH, 2 * gpb * SSM_STATE)

    er = c_re[None] * pr[1:, :, None, :] - c_im[None] * pi[1:, :, None, :]
    ei = c_re[None] * pi[1:, :, None, :] + c_im[None] * pr[1:, :, None, :]
    eb = jnp.stack([er, -ei], axis=0).reshape(2, t, nblk, gpb, SSM_CH, SSM_STATE)
    cpow = jnp.einsum('ztbgop,gh->bzgptho', eb, eye)
    cpow = cpow.reshape(nblk, 2 * gpb * SSM_STATE, t * gpb * SSM_CH)

    a_chunk = jnp.stack([pr[t], pi[t]], axis=0).reshape(2, nblk, 1, gpb * SSM_STATE)
    a_chunk = a_chunk.transpose(1, 0, 2, 3).reshape(nblk, 2, gpb * SSM_STATE)
    dvec = jnp.tile(d_skip.astype(f32).reshape(nblk, 1, gpb * SSM_CH), (1, 1, t))
    return ktab.astype(bf16), bpow.astype(bf16), cpow.astype(bf16), a_chunk, dvec


def _s5_kernel(u_ref, ktab_ref, bpow_ref, cpow_ref, a_ref, d_ref, y_ref,
               toep_ref, s_ref, h_ref, *, n_batch, n_chunk):
    t = S5_CHUNK
    w = S5_GROUPS_PER_BLOCK * SSM_CH
    ns = S5_GROUPS_PER_BLOCK * SSM_STATE
    for tt in range(t):
        for ss in range(tt + 1):
            toep_ref[ss * w:(ss + 1) * w, tt * w:(tt + 1) * w] = ktab_ref[0, tt - ss]
        if tt % 2 == 0:
            toep_ref[(tt + 1) * w:(tt + 2) * w, tt * w:(tt + 1) * w] = jnp.zeros((w, w), bf16)

    u = u_ref[0]
    s_ref[...] = jnp.dot(u, bpow_ref[0], preferred_element_type=f32)

    ar = a_ref[0, 0:1, :]
    ai = a_ref[0, 1:2, :]

    def step(j, carry):
        new = []
        for b in range(n_batch):
            hr, hi = carry[2 * b], carry[2 * b + 1]
            row = b * n_chunk + j
            h_ref[pl.ds(row, 1), 0:ns] = hr
            h_ref[pl.ds(row, 1), ns:2 * ns] = hi
            sr = s_ref[pl.ds(row, 1), 0:ns]
            si = s_ref[pl.ds(row, 1), ns:2 * ns]
            new.append(ar * hr - ai * hi + sr)
            new.append(ar * hi + ai * hr + si)
        return tuple(new)

    zero = jnp.zeros((1, ns), f32)
    lax.fori_loop(0, n_chunk, step, (zero,) * (2 * n_batch))

    hprev = h_ref[...].astype(bf16)
    for tp in range(t // 2):
        c0, c1 = 2 * tp * w, (2 * tp + 2) * w
        y = jnp.dot(u[:, :c1], toep_ref[0:c1, c0:c1], preferred_element_type=f32)
        y = y + jnp.dot(hprev, cpow_ref[0, :, c0:c1], preferred_element_type=f32)
        y = y + d_ref[0, :, c0:c1] * u[:, c0:c1].astype(f32)
        y_ref[0, :, c0:c1] = jax.nn.gelu(y, approximate=True).astype(y_ref.dtype)


def _s5_mixer(u_blocks, tables, n_batch, n_chunk):
    ktab, bpow, cpow, a_chunk, dvec = tables
    nblk, rows, cols = u_blocks.shape
    ns = S5_GROUPS_PER_BLOCK * SSM_STATE
    kern = functools.partial(_s5_kernel, n_batch=n_batch, n_chunk=n_chunk)
    return pl.pallas_call(
        kern,
        grid=(nblk,),
        in_specs=[pl.BlockSpec((1, rows, cols), lambda g: (g, 0, 0)),
                  pl.BlockSpec((1,) + ktab.shape[1:], lambda g: (g, 0, 0, 0)),
                  pl.BlockSpec((1,) + bpow.shape[1:], lambda g: (g, 0, 0)),
                  pl.BlockSpec((1,) + cpow.shape[1:], lambda g: (g, 0, 0)),
                  pl.BlockSpec((1, 2, ns), lambda g: (g, 0, 0)),
                  pl.BlockSpec((1, 1, cols), lambda g: (g, 0, 0))],
        out_specs=pl.BlockSpec((1, rows, cols), lambda g: (g, 0, 0)),
        out_shape=jax.ShapeDtypeStruct((nblk, rows, cols), bf16),
        scratch_shapes=[pltpu.VMEM((cols, cols), bf16),
                        pltpu.VMEM((rows, 2 * ns), f32),
                        pltpu.VMEM((rows, 2 * ns), f32)],
        compiler_params=_params("parallel"),
        name="s5_mixer",
    )(u_blocks, ktab, bpow, cpow, a_chunk, dvec)


def _attn_bias_table():
    blk = ATTN_BLOCK
    slopes = 2.0 ** (-8.0 * jnp.arange(1, N_HEADS + 1, dtype=f32) / N_HEADS)
    delta = np.arange(blk)[:, None] - (np.arange(2 * blk)[None, :] - blk)
    tabs = []
    for window, dil in PATTERNS:
        assert window // dil == blk
        valid = (delta >= 0) & (delta <= window // dil)
        dist = jnp.asarray(delta * dil, dtype=f32)
        bias = jnp.where(valid[None], -slopes[:, None, None] * dist[None], MASK_VALUE)
        tabs.append(bias.reshape(N_HEADS // 2, 2 * blk, 2 * blk))
    return jnp.stack(tabs, axis=1)


def _attn_kernel(q_ref, k_ref, v_ref, bias_ref, o_ref, *scr):
    blk = ATTN_BLOCK
    seq = q_ref.shape[0]
    first_head = lax.broadcasted_iota(jnp.int32, (blk, 2 * HEAD_DIM), 1) < HEAD_DIM
    dims = (((1,), (1,)), ((), ()))
    for pi, (_, dil) in enumerate(PATTERNS):
        o_scr, l_scr = scr[2 * pi], scr[2 * pi + 1]
        sub = seq // dil
        for r in range(dil):
            rows = (lambda st, n: pl.ds(st, n)) if dil == 1 else (lambda st, n: pl.ds(st, n, stride=dil))
            qd = (q_ref[rows(r, sub), :] * HEAD_DIM ** -0.5).astype(bf16)
            kd = k_ref[rows(r, sub), :].astype(bf16)
            vd = v_ref[rows(r, sub), :].astype(bf16)
            for i in range(sub // blk):
                qb = qd[i * blk:(i + 1) * blk]
                zero = jnp.zeros_like(qb)
                q2 = jnp.concatenate([jnp.where(first_head, qb, zero), jnp.where(first_head, zero, qb)], axis=0)
                k0 = max(i - 1, 0) * blk
                nk = (i + 1) * blk - k0
                s = lax.dot_general(q2, kd[k0:k0 + nk], dims, preferred_element_type=f32)
                s = s + bias_ref[0, pi, :, 2 * blk - nk:]
                m = jnp.max(s, axis=-1, keepdims=True)
                p = jnp.exp(s - m)
                l = jnp.sum(p, axis=-1, keepdims=True)
                o = jnp.dot(p.astype(bf16), vd[k0:k0 + nk], preferred_element_type=f32) / l
                lse = m + jnp.log(l)
                dst = rows(r + dil * blk * i, blk)
                o_scr[dst, :] = jnp.where(first_head, o[:blk], o[blk:])
                l_scr[dst, :] = jnp.where(first_head, lse[:blk], lse[blk:])
    l1, l2, l3 = scr[1][...], scr[3][...], scr[5][...]
    m = jnp.maximum(jnp.maximum(l1, l2), l3)
    e1, e2, e3 = jnp.exp(l1 - m), jnp.exp(l2 - m), jnp.exp(l3 - m)
    o_ref[...] = (e1 * scr[0][...] + e2 * scr[2][...] + e3 * scr[4][...]) / (e1 + e2 + e3)


def _dilated_attention(qkv, bsz, seq):
    pairs = N_HEADS // 2
    width = 2 * HEAD_DIM
    bias = _attn_bias_table()
    col = lambda off: pl.BlockSpec((seq, width), lambda hp, b: (b, off + hp))
    return pl.pallas_call(
        _attn_kernel,
        grid=(pairs, bsz),
        in_specs=[col(0), col(pairs), col(2 * pairs),
                  pl.BlockSpec((1,) + bias.shape[1:], lambda hp, b: (hp, 0, 0, 0))],
        out_specs=pl.BlockSpec((seq, width), lambda hp, b: (b, hp)),
        out_shape=jax.ShapeDtypeStruct((bsz * seq, D_ATTN), f32),
        scratch_shapes=[pltpu.VMEM((seq, width), f32)] * (2 * len(PATTERNS)),
        compiler_params=_params("parallel", "parallel"),
        name="dilated_attention",
    )(qkv, qkv, qkv, bias)


def _layer_norm_rows(x, g, b):
    mu = jnp.mean(x, axis=-1, keepdims=True)
    xc = x - mu
    var = jnp.mean(xc * xc, axis=-1, keepdims=True)
    return xc * lax.rsqrt(var + NORM_EPS) * g + b


def _rms_rows(x, g):
    return x * lax.rsqrt(jnp.mean(x * x, axis=-1, keepdims=True) + NORM_EPS) * g


def _mix_out_kernel(z_ref, ya_ref, x_ref, w_ref, gs_ref, ga_ref, lg_ref, lb_ref, h_ref, hb_ref, hp_ref):
    z = z_ref[...]
    y_ssm = z[:, :D_SSM] * jax.nn.sigmoid(z[:, D_SSM:])
    ns = _rms_rows(y_ssm, gs_ref[...]).astype(bf16)
    na = _rms_rows(ya_ref[...], ga_ref[...]).astype(bf16)
    proj = jnp.dot(ns, w_ref[0:D_SSM, :], preferred_element_type=f32)
    proj = proj + jnp.dot(na, w_ref[D_SSM:, :], preferred_element_type=f32)
    h = _layer_norm_rows(DEEPNORM_ALPHA * x_ref[...] + proj, lg_ref[...], lb_ref[...])
    h_ref[...] = h
    hb_ref[...] = h.astype(bf16)
    half = D_MODEL // 2
    word = pltpu.pack_elementwise([h[:, :half], h[:, half:]], packed_dtype=bf16)
    tm = word.shape[0]
    for s in range(PACK_ROWS):
        hp_ref[pl.ds(s, tm, stride=PACK_ROWS), :] = word[:, s * 128:(s + 1) * 128]


def _mix_out(z, y_attn, x, w_out, g_ssm, g_attn, ln_g, ln_b, tm=256):
    n = x.shape[0]
    row = lambda c: pl.BlockSpec((tm, c), lambda i: (i, 0))
    full = lambda a: pl.BlockSpec(a.shape, lambda i: (0,) * a.ndim)
    return pl.pallas_call(
        _mix_out_kernel,
        grid=(n // tm,),
        in_specs=[row(2 * D_SSM), row(D_ATTN), row(D_MODEL), full(w_out), full(g_ssm), full(g_attn),
                  full(ln_g), full(ln_b)],
        out_specs=[row(D_MODEL), row(D_MODEL), pl.BlockSpec((tm * PACK_ROWS, 128), lambda i: (i, 0))],
        out_shape=[jax.ShapeDtypeStruct((n, D_MODEL), f32), jax.ShapeDtypeStruct((n, D_MODEL), bf16),
                   jax.ShapeDtypeStruct((n * PACK_ROWS, 128), jnp.int32)],
        compiler_params=_params("parallel"),
        name="mix_out",
    )(z, y_attn, x, w_out, g_ssm, g_attn, ln_g, ln_b)


def _router_kernel(h_ref, wrt_ref, bias_ref, tri_ref, trie_ref, e8_ref, pos8_ref, wtok_ref, cnt_ref):
    gsz = N_EXPERTS // N_EXPERT_GROUPS
    tm = h_ref.shape[0]
    ninf = -jnp.inf

    @pl.when(pl.program_id(0) == 0)
    def _():
        cnt_ref[...] = jnp.zeros_like(cnt_ref)

    logits = lax.dot_general(wrt_ref[...], h_ref[...], (((1,), (1,)), ((), ())),
                             preferred_element_type=f32, precision=lax.Precision.HIGHEST)
    scores = jax.nn.sigmoid(logits)
    sel = scores + bias_ref[...]
    io = lax.broadcasted_iota(jnp.int32, (gsz, tm), 0)

    blks, gs_rows = [], []
    for g in range(N_EXPERT_GROUPS):
        blk = sel[g * gsz:(g + 1) * gsz, :]
        m1 = jnp.max(blk, axis=0, keepdims=True)
        first = jnp.min(jnp.where(blk == m1, io, gsz), axis=0, keepdims=True)
        m2 = jnp.max(jnp.where(io == first, ninf, blk), axis=0, keepdims=True)
        blks.append(blk)
        gs_rows.append(m1 + m2)
    gs = jnp.concatenate(gs_rows, axis=0)

    iog = lax.broadcasted_iota(jnp.int32, (N_EXPERT_GROUPS, tm), 0)
    beaten = jnp.zeros((N_EXPERT_GROUPS, tm), f32)
    for gp in range(N_EXPERT_GROUPS):
        row = gs_rows[gp]
        tie = jnp.where(iog > gp, 1.0, 0.0)
        beaten = beaten + jnp.where(row > gs, 1.0, jnp.where(row == gs, tie, 0.0))
    keep = beaten < TOPK_GROUPS
    masked = [jnp.where(keep[g:g + 1, :], blks[g], ninf) for g in range(N_EXPERT_GROUPS)]

    ranks = [jnp.zeros((gsz, tm), f32) for _ in range(N_EXPERT_GROUPS)]
    tie_in = [jnp.where(io > j, 1.0, 0.0) for j in range(gsz)]
    for gp in range(N_EXPERT_GROUPS):
        for j in range(gsz):
            row = masked[gp][j:j + 1, :]
            for g in range(N_EXPERT_GROUPS):
                if gp < g:
                    inc = jnp.where(row >= masked[g], 1.0, 0.0)
                elif gp > g:
                    inc = jnp.where(row > masked[g], 1.0, 0.0)
                else:
                    inc = jnp.where(row > masked[g], 1.0, jnp.where(row == masked[g], tie_in[j], 0.0))
                ranks[g] = ranks[g] + inc
    selb = jnp.concatenate([jnp.where(r < TOP_K, 1.0, 0.0) for r in ranks], axis=0)
    wsel = selb * scores
    wn = wsel / jnp.sum(wsel, axis=0, keepdims=True) * ROUTED_SCALE

    maskb = selb.astype(bf16)
    pos = jnp.dot(maskb, tri_ref[...], preferred_element_type=f32) + cnt_ref[:, 0:1]
    cnt_ref[...] = cnt_ref[...] + jnp.sum(selb, axis=1, keepdims=True)
    slot = jnp.dot(trie_ref[...], maskb, preferred_element_type=f32)
    ioe = lax.broadcasted_iota(jnp.int32, (N_EXPERTS, tm), 0).astype(f32)
    e_rows, p_rows = [], []
    for k in range(TOP_K):
        hit = jnp.where(slot == k, selb, 0.0)
        e_rows.append(jnp.sum(hit * ioe, axis=0, keepdims=True))
        p_rows.append(jnp.sum(hit * pos, axis=0, keepdims=True))
    e8_ref[...] = jnp.concatenate(e_rows, axis=0).astype(jnp.int32)
    pos8_ref[...] = jnp.concatenate(p_rows, axis=0).astype(jnp.int32)
    wtok_ref[...] = jnp.concatenate([wn.T, jnp.zeros((tm, 128 - N_EXPERTS), f32)], axis=1)


def _router(h, w_router, router_bias, tm=512):
    n = h.shape[0]
    wrt = w_router.astype(f32).T
    bias = router_bias.astype(f32).reshape(N_EXPERTS, 1)
    tri = (jnp.arange(tm)[:, None] < jnp.arange(tm)[None, :]).astype(bf16)
    trie = (jnp.arange(N_EXPERTS)[None, :] < jnp.arange(N_EXPERTS)[:, None]).astype(bf16)
    full = lambda a: pl.BlockSpec(a.shape, lambda i: (0,) * a.ndim)
    tok = lambda: pl.BlockSpec((TOP_K, tm), lambda i: (0, i))
    return pl.pallas_call(
        _router_kernel,
        grid=(n // tm,),
        in_specs=[pl.BlockSpec((tm, D_MODEL), lambda i: (i, 0)), full(wrt), full(bias), full(tri), full(trie)],
        out_specs=[tok(), tok(), pl.BlockSpec((tm, 128), lambda i: (i, 0)),
                   pl.BlockSpec((N_EXPERTS, 128), lambda i: (0, 0))],
        out_shape=[jax.ShapeDtypeStruct((TOP_K, n), jnp.int32), jax.ShapeDtypeStruct((TOP_K, n), jnp.int32),
                   jax.ShapeDtypeStruct((n, 128), f32), jax.ShapeDtypeStruct((N_EXPERTS, 128), f32)],
        compiler_params=_params("arbitrary"),
        name="router",
    )(h, wrt, bias, tri, trie)


def _invert_kernel(dest_ref, seg_end_ref, pad_end_ref, code_ref):
    def fill(lo, hi):
        def body(r, c):
            code_ref[r] = PAD_CODE
            return c
        lax.fori_loop(lo, hi, body, 0)

    def per_expert(e, c):
        fill(seg_end_ref[e], pad_end_ref[e])
        return c
    lax.fori_loop(0, N_EXPERTS, per_expert, 0)
    fill(pad_end_ref[N_EXPERTS - 1], code_ref.shape[0])

    def body(p, c):
        code_ref[dest_ref[p]] = p
        return c
    lax.fori_loop(0, dest_ref.shape[0], body, 0, unroll=8)


def _invert(dest, seg_end, pad_end, n_rows):
    smem = lambda: pl.BlockSpec(memory_space=pltpu.SMEM)
    return pl.pallas_call(
        _invert_kernel,
        in_specs=[smem(), smem(), smem()],
        out_specs=smem(),
        out_shape=jax.ShapeDtypeStruct((n_rows,), jnp.int32),
        name="invert_dispatch",
    )(dest, seg_end, pad_end)


def _expert_changed(be_ref, i):
    return jnp.logical_or(i == 0, be_ref[i] != be_ref[jnp.maximum(i - 1, 0)])


GATHER_UNROLL = 8
SCATTER_UNROLL = 16


def _moe_up_kernel(be_ref, code_ref, nb_ref, hp_ref, wtok_ref, wg_ref, wu_ref, mid_ref, xg_scr, wrow_scr, wgu_scr):
    i = pl.program_id(0)
    rows = MOE_ROWS

    @pl.when(i < nb_ref[0])
    def _():
        @pl.when(_expert_changed(be_ref, i))
        def _():
            wgu_scr[:, :D_EXPERT] = wg_ref[0].astype(bf16)
            wgu_scr[:, D_EXPERT:] = wu_ref[0].astype(bf16)

        def gather(g, c):
            for j in range(GATHER_UNROLL):
                r = g * GATHER_UNROLL + j
                code = code_ref[i * rows + r]
                tok = (code >> 3) & (N_TOKENS - 1)
                dst = pl.multiple_of(g * (GATHER_UNROLL * PACK_ROWS), GATHER_UNROLL * PACK_ROWS) + j * PACK_ROWS
                xg_scr[pl.ds(dst, PACK_ROWS), :] = hp_ref[pl.ds(pl.multiple_of(tok * PACK_ROWS, PACK_ROWS), PACK_ROWS), :]
                wrow_scr[pl.ds(r, 1), :] = wtok_ref[pl.ds(tok, 1), :]
            return c
        lax.fori_loop(0, rows // GATHER_UNROLL, gather, 0)

        lo, hi = [], []
        for s in range(PACK_ROWS):
            wds = xg_scr[pl.ds(s, rows, stride=PACK_ROWS), :]
            lo.append(pltpu.unpack_elementwise(wds, index=0, packed_dtype=bf16, unpacked_dtype=f32).astype(bf16))
            hi.append(pltpu.unpack_elementwise(wds, index=1, packed_dtype=bf16, unpacked_dtype=f32).astype(bf16))
        x = jnp.concatenate(lo + hi, axis=1)
        gu = jnp.dot(x, wgu_scr[...], preferred_element_type=f32)
        g, u = gu[:, :D_EXPERT], gu[:, D_EXPERT:]
        lane = lax.broadcasted_iota(jnp.int32, (rows, 128), 1)
        w = jnp.sum(jnp.where(lane == be_ref[i], wrow_scr[...], 0.0), axis=1, keepdims=True)
        real = i * rows + lax.broadcasted_iota(jnp.int32, (rows, 1), 0) < nb_ref[1 + be_ref[i]]
        mid_ref[...] = (g * jax.nn.sigmoid(g) * u * jnp.where(real, w, 0.0)).astype(bf16)

    @pl.when(i >= nb_ref[0])
    def _():
        mid_ref[...] = jnp.zeros_like(mid_ref)


def _moe_up(block_e, code, nb, hp, wtok, w_gate, w_up):
    wmap = lambda i, be, cd, nb: (be[i], 0, 0)
    grid_spec = pltpu.PrefetchScalarGridSpec(
        num_scalar_prefetch=3,
        grid=(MOE_BLOCKS,),
        in_specs=[pl.BlockSpec(memory_space=pltpu.VMEM),
                  pl.BlockSpec(memory_space=pltpu.VMEM),
                  pl.BlockSpec((1, D_MODEL, D_EXPERT), wmap),
                  pl.BlockSpec((1, D_MODEL, D_EXPERT), wmap)],
        out_specs=pl.BlockSpec((MOE_ROWS, D_EXPERT), lambda i, be, cd, nb: (i, 0)),
        scratch_shapes=[pltpu.VMEM((PACK_ROWS * MOE_ROWS, 128), jnp.int32),
                        pltpu.VMEM((MOE_ROWS, 128), f32),
                        pltpu.VMEM((D_MODEL, 2 * D_EXPERT), bf16)])
    return pl.pallas_call(
        _moe_up_kernel,
        grid_spec=grid_spec,
        out_shape=jax.ShapeDtypeStruct((MOE_BLOCKS * MOE_ROWS, D_EXPERT), bf16),
        compiler_params=pltpu.CompilerParams(dimension_semantics=("arbitrary",),
                                             vmem_limit_bytes=60 * 1024 * 1024),
        name="moe_up",
    )(block_e, code, nb, hp, wtok, w_gate, w_up)


def _moe_down_kernel(be_ref, code_ref, nb_ref, mid_ref, wd_ref, acc_ref, ybuf, wd_scr):
    i = pl.program_id(1)
    rows = MOE_ROWS

    @pl.when(i == 0)
    def _():
        acc_ref[...] = jnp.zeros_like(acc_ref)

    @pl.when(i < nb_ref[0])
    def _():
        @pl.when(_expert_changed(be_ref, i))
        def _():
            wd_scr[...] = wd_ref[0].astype(bf16)

        y = jnp.dot(mid_ref[...], wd_scr[...], preferred_element_type=f32)
        for c in range(PACK_ROWS):
            ybuf[c * rows:(c + 1) * rows, :] = y[:, c * 128:(c + 1) * 128]

        for g in range(rows // SCATTER_UNROLL):
            sums, addrs = [], []
            for j in range(SCATTER_UNROLL):
                r = g * SCATTER_UNROLL + j
                a = pl.multiple_of(code_ref[i * rows + r] & -PACK_ROWS, PACK_ROWS)
                v = ybuf[pl.ds(r, PACK_ROWS, stride=rows), :]
                sums.append(acc_ref[pl.ds(a, PACK_ROWS), :] + v)
                addrs.append(a)
            for j in range(SCATTER_UNROLL):
                acc_ref[pl.ds(addrs[j], PACK_ROWS), :] = sums[j]


def _moe_down(block_e, code, nb, mid, w_down):
    half = D_MODEL // 2
    acc_rows = (N_TOKENS + 1) * PACK_ROWS
    grid_spec = pltpu.PrefetchScalarGridSpec(
        num_scalar_prefetch=3,
        grid=(2, MOE_BLOCKS),
        in_specs=[pl.BlockSpec((MOE_ROWS, D_EXPERT), lambda p, i, be, cd, nb: (i, 0)),
                  pl.BlockSpec((1, D_EXPERT, half), lambda p, i, be, cd, nb: (be[i], 0, p))],
        out_specs=pl.BlockSpec((None, acc_rows, 128), lambda p, i, be, cd, nb: (p, 0, 0),
                               pipeline_mode=pl.Buffered(1)),
        scratch_shapes=[pltpu.VMEM((PACK_ROWS * MOE_ROWS, 128), f32),
                        pltpu.VMEM((D_EXPERT, half), bf16)])
    return pl.pallas_call(
        _moe_down_kernel,
        grid_spec=grid_spec,
        out_shape=jax.ShapeDtypeStruct((2, acc_rows, 128), f32),
        compiler_params=_params("arbitrary", "arbitrary"),
        name="moe_down",
    )(block_e, code, nb, mid, w_down)


def _final_kernel(hb_ref, h_ref, r0_ref, r1_ref, wgu_ref, wd_ref, lg_ref, lb_ref, o_ref):
    tm = h_ref.shape[0]
    gu = jnp.dot(hb_ref[...], wgu_ref[...], preferred_element_type=f32)
    g, u = gu[:, :D_EXPERT], gu[:, D_EXPERT:]
    mid = (g * jax.nn.sigmoid(g) * u).astype(bf16)
    shared = jnp.dot(mid, wd_ref[...], preferred_element_type=f32)
    routed = jnp.concatenate([r[pl.ds(c, tm, stride=PACK_ROWS), :]
                              for r in (r0_ref, r1_ref) for c in range(PACK_ROWS)], axis=1)
    o_ref[...] = _layer_norm_rows(DEEPNORM_ALPHA * h_ref[...] + routed + shared,
                                  lg_ref[...], lb_ref[...])


def _final(hb, h, racc, wgu, wd, ln_g, ln_b, tm=256):
    n = h.shape[0]
    row = lambda: pl.BlockSpec((tm, D_MODEL), lambda i: (i, 0))
    full = lambda a: pl.BlockSpec(a.shape, lambda i: (0,) * a.ndim)
    acc = lambda p: pl.BlockSpec((None, tm * PACK_ROWS, 128), lambda i: (p, i, 0))
    return pl.pallas_call(
        _final_kernel,
        grid=(n // tm,),
        in_specs=[row(), row(), acc(0), acc(1), full(wgu), full(wd), full(ln_g), full(ln_b)],
        out_specs=row(),
        out_shape=jax.ShapeDtypeStruct((n, D_MODEL), f32),
        compiler_params=_params("parallel"),
        name="shared_final",
    )(hb, h, racc, racc, wgu, wd, ln_g, ln_b)


def _dispatch_plan(e8, pos8, cnt):
    counts = cnt[:, 0].astype(jnp.int32)
    padded = (counts + MOE_ROWS - 1) // MOE_ROWS * MOE_ROWS
    pad_end = jnp.cumsum(padded).astype(jnp.int32)
    pad_start = pad_end - padded
    seg_end = pad_start + counts
    ids = jnp.arange(N_EXPERTS, dtype=jnp.int32)
    start8 = jnp.sum(jnp.where(e8[..., None] == ids, pad_start, 0), axis=-1)
    dest = (start8 + pos8).T.reshape(-1)
    block_start = jnp.arange(MOE_BLOCKS, dtype=jnp.int32) * MOE_ROWS
    block_e = jnp.minimum(jnp.sum((pad_end[None, :] <= block_start[:, None]).astype(jnp.int32), axis=1),
                          N_EXPERTS - 1)
    nb = jnp.concatenate([pad_end[-1:] // MOE_ROWS, seg_end]).astype(jnp.int32)
    return dest, seg_end, pad_end, block_e, nb


def kernel(x, w_in, ssm_log_dt, ssm_a_re, ssm_a_im, ssm_b_re, ssm_b_im, ssm_c_re, ssm_c_im, ssm_d,
           w_glu, g_ssm_out, g_attn_out, w_out, ln1_g, ln1_b, w_router, router_bias, w_gate, w_up,
           w_down, ws_gate, ws_up, ws_down, ln2_g, ln2_b):
    bsz, seq, d = x.shape
    n_tok = bsz * seq
    h = x.reshape(n_tok, d)
    for layer in range(DEPTH):
        w_in_b = w_in[layer].astype(bf16)
        u = _matmul(h, w_in_b[:, :D_SSM], bf16)
        qkv = _matmul(h, w_in_b[:, D_SSM:], f32)

        t, gpb = S5_CHUNK, S5_GROUPS_PER_BLOCK
        nblk, n_chunk, w = SSM_GROUPS // gpb, seq // S5_CHUNK, gpb * SSM_CH
        tables = _s5_tables(ssm_log_dt[layer], ssm_a_re[layer], ssm_a_im[layer], ssm_b_re[layer],
                            ssm_b_im[layer], ssm_c_re[layer], ssm_c_im[layer], ssm_d[layer])
        ub = u.reshape(bsz * n_chunk, t, nblk, w).transpose(2, 0, 1, 3).reshape(nblk, bsz * n_chunk, t * w)
        yb = _s5_mixer(ub, tables, bsz, n_chunk)
        y = yb.reshape(nblk, bsz * n_chunk, t, w).transpose(1, 2, 0, 3).reshape(n_tok, D_SSM)
        z = _matmul(y, w_glu[layer].astype(bf16), f32)

        y_attn = _dilated_attention(qkv, bsz, seq)

        row2 = lambda a: a.astype(f32).reshape(1, -1)
        h, hb, hp = _mix_out(z, y_attn, h, w_out[layer].astype(bf16), row2(g_ssm_out[layer]),
                             row2(g_attn_out[layer]), row2(ln1_g[layer]), row2(ln1_b[layer]))

        assert n_tok == N_TOKENS
        e8, pos8, wtok, cnt = _router(h, w_router[layer], router_bias[layer])
        dest, seg_end, pad_end, block_e, nb = _dispatch_plan(e8, pos8, cnt)
        code = _invert(dest, seg_end, pad_end, MOE_BLOCKS * MOE_ROWS)
        mid = _moe_up(block_e, code, nb, hp, wtok, w_gate[layer], w_up[layer])
        racc = _moe_down(block_e, code, nb, mid, w_down[layer])
        wgu = jnp.concatenate([ws_gate[layer], ws_up[layer]], axis=1).astype(bf16)
        h = _final(hb, h, racc, wgu, ws_down[layer].astype(bf16), row2(ln2_g[layer]), row2(ln2_b[layer]))
    return h.reshape(bsz, seq, d)
```

```python
import functools

import jax
import jax.numpy as jnp
import numpy as np
from jax import lax
from jax.experimental import pallas as pl
from jax.experimental.pallas import tpu as pltpu

D_MODEL = 2048
D_SSM = 1024
D_ATTN = 1024
SSM_CH = 16
SSM_GROUPS = 64
SSM_STATE = 64
HEAD_DIM = 64
N_HEADS = 16
PATTERNS = ((128, 1), (512, 4), (2048, 16))
ATTN_BLOCK = 128
N_EXPERTS = 64
TOP_K = 8
N_EXPERT_GROUPS = 8
TOPK_GROUPS = 4
D_EXPERT = 512
ROUTED_SCALE = 2.5
NORM_EPS = 1e-5
DEPTH = 1
DEEPNORM_ALPHA = (2.0 * DEPTH) ** 0.25

S5_CHUNK = 16
S5_GROUPS_PER_BLOCK = 8
MOE_ROWS = 256
MASK_VALUE = -1e30
PACK_ROWS = D_MODEL // 2 // 128
N_TOKENS = 8192
PAD_CODE = N_TOKENS * TOP_K
MOE_BLOCKS = -(-(N_TOKENS * TOP_K + N_EXPERTS * (MOE_ROWS - 1)) // MOE_ROWS)
VMEM_LIMIT = 56 * 1024 * 1024
MOE_UP_VMEM_LIMIT = 62 * 1024 * 1024

bf16 = jnp.bfloat16
f32 = jnp.float32


def _params(*sem):
    return pltpu.CompilerParams(dimension_semantics=sem, vmem_limit_bytes=VMEM_LIMIT)


def _matmul_kernel(a_ref, b_ref, o_ref):
    a = a_ref[...].astype(bf16)
    o_ref[...] = jnp.dot(a, b_ref[...], preferred_element_type=f32).astype(o_ref.dtype)


def _matmul(a, b, out_dtype, tm=512, tn=1024):
    m, k = a.shape
    _, n = b.shape
    tn = min(tn, n)
    return pl.pallas_call(
        _matmul_kernel,
        grid=(m // tm, n // tn),
        in_specs=[pl.BlockSpec((tm, k), lambda i, j: (i, 0)),
                  pl.BlockSpec((k, tn), lambda i, j: (0, j))],
        out_specs=pl.BlockSpec((tm, tn), lambda i, j: (i, j)),
        out_shape=jax.ShapeDtypeStruct((m, n), out_dtype),
        compiler_params=_params("parallel", "arbitrary"),
        name="matmul",
    )(a, b)


def _s5_tables(log_dt, a_re, a_im, b_re, b_im, c_re, c_im, d_skip):
    t = S5_CHUNK
    gpb = S5_GROUPS_PER_BLOCK
    nblk = SSM_GROUPS // gpb
    hp = lax.Precision.HIGHEST
    lr = jnp.minimum(a_re.astype(f32), -1e-4)
    li = a_im.astype(f32)
    dt = jnp.exp(log_dt.astype(f32))
    kk = jnp.arange(t + 1, dtype=f32)[:, None, None]
    mag = jnp.exp(kk * (lr * dt))
    pr = mag * jnp.cos(kk * (li * dt))
    pi = mag * jnp.sin(kk * (li * dt))
    xr, xi = pr[1] - 1.0, pi[1]
    den = lr * lr + li * li
    cr = (xr * lr + xi * li) / den
    ci = (xi * lr - xr * li) / den
    bbr = cr[..., None] * b_re - ci[..., None] * b_im
    bbi = cr[..., None] * b_im + ci[..., None] * b_re
    eye = jnp.eye(gpb, dtype=f32)

    wr = pr[:t, :, :, None] * bbr - pi[:t, :, :, None] * bbi
    wi = pr[:t, :, :, None] * bbi + pi[:t, :, :, None] * bbr
    taps = (jnp.einsum('gop,tgpc->tgco', c_re, wr, precision=hp)
            - jnp.einsum('gop,tgpc->tgco', c_im, wi, precision=hp))
    taps = taps.reshape(t, nblk, gpb, SSM_CH, SSM_CH)
    ktab = jnp.einsum('tbgco,gh->btgcho', taps, eye).reshape(nblk, t, gpb * SSM_CH, gpb * SSM_CH)

    rev = jnp.arange(t - 1, -1, -1)
    sr = pr[rev][..., None] * bbr - pi[rev][..., None] * bbi
    si = pr[rev][..., None] * bbi + pi[rev][..., None] * bbr
    sb = jnp.stack([sr, si], axis=0).reshape(2, t, nblk, gpb, SSM_STATE, SSM_CH)
    bpow = jnp.einsum('zsbgpc,gh->bsgczhp', sb, eye)
    bpow = bpow.reshape(nblk, t * gpb * SSM_CH, 2 * gpb * SSM_STATE)

    er = c_re[None] * pr[1:, :, None, :] - c_im[None] * pi[1:, :, None, :]
    ei = c_re[None] * pi[1:, :, None, :] + c_im[None] * pr[1:, :, None, :]
    eb = jnp.stack([er, -ei], axis=0).reshape(2, t, nblk, gpb, SSM_CH, SSM_STATE)
    cpow = jnp.einsum('ztbgop,gh->bzgptho', eb, eye)
    cpow = cpow.reshape(nblk, 2 * gpb * SSM_STATE, t * gpb * SSM_CH)

    a_chunk = jnp.stack([pr[t], pi[t]], axis=0).reshape(2, nblk, 1, gpb * SSM_STATE)
    a_chunk = a_chunk.transpose(1, 0, 2, 3).reshape(nblk, 2, gpb * SSM_STATE)
    dvec = jnp.tile(d_skip.astype(f32).reshape(nblk, 1, gpb * SSM_CH), (1, 1, t))
    return ktab.astype(bf16), bpow.astype(bf16), cpow.astype(bf16), a_chunk, dvec


def _s5_kernel(u_ref, ktab_ref, bpow_ref, cpow_ref, a_ref, d_ref, y_ref,
               toep_ref, s_ref, h_ref, *, n_batch, n_chunk):
    t = S5_CHUNK
    w = S5_GROUPS_PER_BLOCK * SSM_CH
    ns = S5_GROUPS_PER_BLOCK * SSM_STATE
    for tt in range(t):
        for ss in range(tt + 1):
            toep_ref[ss * w:(ss + 1) * w, tt * w:(tt + 1) * w] = ktab_ref[0, tt - ss]
        if tt % 2 == 0:
            toep_ref[(tt + 1) * w:(tt + 2) * w, tt * w:(tt + 1) * w] = jnp.zeros((w, w), bf16)

    rows = n_batch * n_chunk
    uf = [u_ref[pl.ds(s, rows, stride=t), :] for s in range(t)]
    u = jnp.concatenate([p.astype(bf16) for p in uf], axis=1)
    s_ref[...] = jnp.dot(u, bpow_ref[0], preferred_element_type=f32)

    ar = a_ref[0, 0:1, :]
    ai = a_ref[0, 1:2, :]

    def step(j, carry):
        new = []
        for b in range(n_batch):
            hr, hi = carry[2 * b], carry[2 * b + 1]
            row = b * n_chunk + j
            h_ref[pl.ds(row, 1), 0:ns] = hr
            h_ref[pl.ds(row, 1), ns:2 * ns] = hi
            sr = s_ref[pl.ds(row, 1), 0:ns]
            si = s_ref[pl.ds(row, 1), ns:2 * ns]
            new.append(ar * hr - ai * hi + sr)
            new.append(ar * hi + ai * hr + si)
        return tuple(new)

    zero = jnp.zeros((1, ns), f32)
    lax.fori_loop(0, n_chunk, step, (zero,) * (2 * n_batch))

    hprev = h_ref[...].astype(bf16)
    for tp in range(t // 2):
        c0, c1 = 2 * tp * w, (2 * tp + 2) * w
        y = jnp.dot(u[:, :c1], toep_ref[0:c1, c0:c1], preferred_element_type=f32)
        y = y + jnp.dot(hprev, cpow_ref[0, :, c0:c1], preferred_element_type=f32)
        for k, tt in enumerate((2 * tp, 2 * tp + 1)):
            yk = y[:, k * w:(k + 1) * w] + d_ref[0, :, tt * w:(tt + 1) * w] * uf[tt]
            y_ref[pl.ds(tt, rows, stride=t), :] = jax.nn.gelu(yk, approximate=True)


def _s5_mixer(proj, tables, n_batch, n_chunk):
    ktab, bpow, cpow, a_chunk, dvec = tables
    nblk = ktab.shape[0]
    w = S5_GROUPS_PER_BLOCK * SSM_CH
    cols = S5_CHUNK * w
    rows = n_batch * n_chunk
    n_tok = rows * S5_CHUNK
    ns = S5_GROUPS_PER_BLOCK * SSM_STATE
    kern = functools.partial(_s5_kernel, n_batch=n_batch, n_chunk=n_chunk)
    return pl.pallas_call(
        kern,
        grid=(nblk,),
        in_specs=[pl.BlockSpec((n_tok, w), lambda g: (0, g)),
                  pl.BlockSpec((1,) + ktab.shape[1:], lambda g: (g, 0, 0, 0)),
                  pl.BlockSpec((1,) + bpow.shape[1:], lambda g: (g, 0, 0)),
                  pl.BlockSpec((1,) + cpow.shape[1:], lambda g: (g, 0, 0)),
                  pl.BlockSpec((1, 2, ns), lambda g: (g, 0, 0)),
                  pl.BlockSpec((1, 1, cols), lambda g: (g, 0, 0))],
        out_specs=pl.BlockSpec((n_tok, w), lambda g: (0, g)),
        out_shape=jax.ShapeDtypeStruct((n_tok, D_SSM), f32),
        scratch_shapes=[pltpu.VMEM((cols, cols), bf16),
                        pltpu.VMEM((rows, 2 * ns), f32),
                        pltpu.VMEM((rows, 2 * ns), f32)],
        compiler_params=_params("parallel"),
        name="s5_mixer",
    )(proj, ktab, bpow, cpow, a_chunk, dvec)


def _attn_bias_table():
    blk = ATTN_BLOCK
    slopes = 2.0 ** (-8.0 * jnp.arange(1, N_HEADS + 1, dtype=f32) / N_HEADS)
    delta = np.arange(blk)[:, None] - (np.arange(2 * blk)[None, :] - blk)
    tabs = []
    for window, dil in PATTERNS:
        assert window // dil == blk
        valid = (delta >= 0) & (delta <= window // dil)
        dist = jnp.asarray(delta * dil, dtype=f32)
        bias = jnp.where(valid[None], -slopes[:, None, None] * dist[None], MASK_VALUE)
        tabs.append(bias.reshape(N_HEADS // 2, 2 * blk, 2 * blk))
    return jnp.stack(tabs, axis=1)


def _attn_kernel(q_ref, k_ref, v_ref, bias_ref, o_ref, *scr):
    blk = ATTN_BLOCK
    seq = q_ref.shape[0]
    first_head = lax.broadcasted_iota(jnp.int32, (blk, 2 * HEAD_DIM), 1) < HEAD_DIM
    dims = (((1,), (1,)), ((), ()))
    for pi, (_, dil) in enumerate(PATTERNS):
        o_scr, l_scr = scr[2 * pi], scr[2 * pi + 1]
        sub = seq // dil
        for r in range(dil):
            rows = (lambda st, n: pl.ds(st, n)) if dil == 1 else (lambda st, n: pl.ds(st, n, stride=dil))
            qd = (q_ref[rows(r, sub), :] * HEAD_DIM ** -0.5).astype(bf16)
            kd = k_ref[rows(r, sub), :].astype(bf16)
            vd = v_ref[rows(r, sub), :].astype(bf16)
            for i in range(sub // blk):
                qb = qd[i * blk:(i + 1) * blk]
                zero = jnp.zeros_like(qb)
                q2 = jnp.concatenate([jnp.where(first_head, qb, zero), jnp.where(first_head, zero, qb)], axis=0)
                k0 = max(i - 1, 0) * blk
                nk = (i + 1) * blk - k0
                s = lax.dot_general(q2, kd[k0:k0 + nk], dims, preferred_element_type=f32)
                s = s + bias_ref[0, pi, :, 2 * blk - nk:]
                m = jnp.max(s, axis=-1, keepdims=True)
                p = jnp.exp(s - m)
                l = jnp.sum(p, axis=-1, keepdims=True)
                o = jnp.dot(p.astype(bf16), vd[k0:k0 + nk], preferred_element_type=f32) / l
                lse = m + jnp.log(l)
                dst = rows(r + dil * blk * i, blk)
                o_scr[dst, :] = jnp.where(first_head, o[:blk], o[blk:])
                l_scr[dst, :] = jnp.where(first_head, lse[:blk], lse[blk:])
    l1, l2, l3 = scr[1][...], scr[3][...], scr[5][...]
    m = jnp.maximum(jnp.maximum(l1, l2), l3)
    e1, e2, e3 = jnp.exp(l1 - m), jnp.exp(l2 - m), jnp.exp(l3 - m)
    o_ref[...] = (e1 * scr[0][...] + e2 * scr[2][...] + e3 * scr[4][...]) / (e1 + e2 + e3)


def _dilated_attention(qkv, first, bsz, seq):
    pairs = N_HEADS // 2
    width = 2 * HEAD_DIM
    bias = _attn_bias_table()
    col = lambda off: pl.BlockSpec((seq, width), lambda hp, b: (b, first + off + hp))
    return pl.pallas_call(
        _attn_kernel,
        grid=(pairs, bsz),
        in_specs=[col(0), col(pairs), col(2 * pairs),
                  pl.BlockSpec((1,) + bias.shape[1:], lambda hp, b: (hp, 0, 0, 0))],
        out_specs=pl.BlockSpec((seq, width), lambda hp, b: (b, hp)),
        out_shape=jax.ShapeDtypeStruct((bsz * seq, D_ATTN), f32),
        scratch_shapes=[pltpu.VMEM((seq, width), f32)] * (2 * len(PATTERNS)),
        compiler_params=_params("parallel", "parallel"),
        name="dilated_attention",
    )(qkv, qkv, qkv, bias)


def _layer_norm_rows(x, g, b):
    mu = jnp.mean(x, axis=-1, keepdims=True)
    xc = x - mu
    var = jnp.mean(xc * xc, axis=-1, keepdims=True)
    return xc * lax.rsqrt(var + NORM_EPS) * g + b


def _rms_rows(x, g):
    return x * lax.rsqrt(jnp.mean(x * x, axis=-1, keepdims=True) + NORM_EPS) * g


def _mix_out_kernel(z_ref, ya_ref, x_ref, w_ref, gs_ref, ga_ref, lg_ref, lb_ref, h_ref, hb_ref, hp_ref):
    z = z_ref[...]
    y_ssm = z[:, :D_SSM] * jax.nn.sigmoid(z[:, D_SSM:])
    ns = _rms_rows(y_ssm, gs_ref[...]).astype(bf16)
    na = _rms_rows(ya_ref[...], ga_ref[...]).astype(bf16)
    proj = jnp.dot(ns, w_ref[0:D_SSM, :], preferred_element_type=f32)
    proj = proj + jnp.dot(na, w_ref[D_SSM:, :], preferred_element_type=f32)
    h = _layer_norm_rows(DEEPNORM_ALPHA * x_ref[...] + proj, lg_ref[...], lb_ref[...])
    h_ref[...] = h
    hb_ref[...] = h.astype(bf16)
    half = D_MODEL // 2
    word = pltpu.pack_elementwise([h[:, :half], h[:, half:]], packed_dtype=bf16)
    tm = word.shape[0]
    for s in range(PACK_ROWS):
        hp_ref[pl.ds(s, tm, stride=PACK_ROWS), :] = word[:, s * 128:(s + 1) * 128]


def _mix_out(z, y_attn, x, w_out, g_ssm, g_attn, ln_g, ln_b, tm=256):
    n = x.shape[0]
    row = lambda c: pl.BlockSpec((tm, c), lambda i: (i, 0))
    full = lambda a: pl.BlockSpec(a.shape, lambda i: (0,) * a.ndim)
    return pl.pallas_call(
        _mix_out_kernel,
        grid=(n // tm,),
        in_specs=[row(2 * D_SSM), row(D_ATTN), row(D_MODEL), full(w_out), full(g_ssm), full(g_attn),
                  full(ln_g), full(ln_b)],
        out_specs=[row(D_MODEL), row(D_MODEL), pl.BlockSpec((tm * PACK_ROWS, 128), lambda i: (i, 0))],
        out_shape=[jax.ShapeDtypeStruct((n, D_MODEL), f32), jax.ShapeDtypeStruct((n, D_MODEL), bf16),
                   jax.ShapeDtypeStruct((n * PACK_ROWS, 128), jnp.int32)],
        compiler_params=_params("parallel"),
        name="mix_out",
    )(z, y_attn, x, w_out, g_ssm, g_attn, ln_g, ln_b)


def _router_kernel(h_ref, wrt_ref, bias_ref, tri_ref, trie_ref, e8_ref, pos8_ref, wtok_ref, cnt_ref):
    gsz = N_EXPERTS // N_EXPERT_GROUPS
    tm = h_ref.shape[0]
    ninf = -jnp.inf

    @pl.when(pl.program_id(0) == 0)
    def _():
        cnt_ref[...] = jnp.zeros_like(cnt_ref)

    logits = lax.dot_general(wrt_ref[...], h_ref[...], (((1,), (1,)), ((), ())),
                             preferred_element_type=f32, precision=lax.Precision.HIGHEST)
    scores = jax.nn.sigmoid(logits)
    sel = scores + bias_ref[...]
    io = lax.broadcasted_iota(jnp.int32, (gsz, tm), 0)

    blks, gs_rows = [], []
    for g in range(N_EXPERT_GROUPS):
        blk = sel[g * gsz:(g + 1) * gsz, :]
        m1 = jnp.max(blk, axis=0, keepdims=True)
        first = jnp.min(jnp.where(blk == m1, io, gsz), axis=0, keepdims=True)
        m2 = jnp.max(jnp.where(io == first, ninf, blk), axis=0, keepdims=True)
        blks.append(blk)
        gs_rows.append(m1 + m2)
    gs = jnp.concatenate(gs_rows, axis=0)

    iog = lax.broadcasted_iota(jnp.int32, (N_EXPERT_GROUPS, tm), 0)
    beaten = jnp.zeros((N_EXPERT_GROUPS, tm), f32)
    for gp in range(N_EXPERT_GROUPS):
        row = gs_rows[gp]
        tie = jnp.where(iog > gp, 1.0, 0.0)
        beaten = beaten + jnp.where(row > gs, 1.0, jnp.where(row == gs, tie, 0.0))
    keep = beaten < TOPK_GROUPS
    masked = [jnp.where(keep[g:g + 1, :], blks[g], ninf) for g in range(N_EXPERT_GROUPS)]

    ranks = [jnp.zeros((gsz, tm), f32) for _ in range(N_EXPERT_GROUPS)]
    tie_in = [jnp.where(io > j, 1.0, 0.0) for j in range(gsz)]
    for gp in range(N_EXPERT_GROUPS):
        for j in range(gsz):
            row = masked[gp][j:j + 1, :]
            for g in range(N_EXPERT_GROUPS):
                if gp < g:
                    inc = jnp.where(row >= masked[g], 1.0, 0.0)
                elif gp > g:
                    inc = jnp.where(row > masked[g], 1.0, 0.0)
                else:
                    inc = jnp.where(row > masked[g], 1.0, jnp.where(row == masked[g], tie_in[j], 0.0))
                ranks[g] = ranks[g] + inc
    selb = jnp.concatenate([jnp.where(r < TOP_K, 1.0, 0.0) for r in ranks], axis=0)
    wsel = selb * scores
    wn = wsel / jnp.sum(wsel, axis=0, keepdims=True) * ROUTED_SCALE

    maskb = selb.astype(bf16)
    pos = jnp.dot(maskb, tri_ref[...], preferred_element_type=f32) + cnt_ref[:, 0:1]
    cnt_ref[...] = cnt_ref[...] + jnp.sum(selb, axis=1, keepdims=True)
    slot = jnp.dot(trie_ref[...], maskb, preferred_element_type=f32)
    ioe = lax.broadcasted_iota(jnp.int32, (N_EXPERTS, tm), 0).astype(f32)
    e_rows, p_rows = [], []
    for k in range(TOP_K):
        hit = jnp.where(slot == k, selb, 0.0)
        e_rows.append(jnp.sum(hit * ioe, axis=0, keepdims=True))
        p_rows.append(jnp.sum(hit * pos, axis=0, keepdims=True))
    e8_ref[...] = jnp.concatenate(e_rows, axis=0).astype(jnp.int32)
    pos8_ref[...] = jnp.concatenate(p_rows, axis=0).astype(jnp.int32)
    wtok_ref[...] = jnp.concatenate([wn.T, jnp.zeros((tm, 128 - N_EXPERTS), f32)], axis=1)


def _router(h, w_router, router_bias, tm=512):
    n = h.shape[0]
    wrt = w_router.astype(f32).T
    bias = router_bias.astype(f32).reshape(N_EXPERTS, 1)
    tri = (jnp.arange(tm)[:, None] < jnp.arange(tm)[None, :]).astype(bf16)
    trie = (jnp.arange(N_EXPERTS)[None, :] < jnp.arange(N_EXPERTS)[:, None]).astype(bf16)
    full = lambda a: pl.BlockSpec(a.shape, lambda i: (0,) * a.ndim)
    tok = lambda: pl.BlockSpec((TOP_K, tm), lambda i: (0, i))
    return pl.pallas_call(
        _router_kernel,
        grid=(n // tm,),
        in_specs=[pl.BlockSpec((tm, D_MODEL), lambda i: (i, 0)), full(wrt), full(bias), full(tri), full(trie)],
        out_specs=[tok(), tok(), pl.BlockSpec((tm, 128), lambda i: (i, 0)),
                   pl.BlockSpec((N_EXPERTS, 128), lambda i: (0, 0))],
        out_shape=[jax.ShapeDtypeStruct((TOP_K, n), jnp.int32), jax.ShapeDtypeStruct((TOP_K, n), jnp.int32),
                   jax.ShapeDtypeStruct((n, 128), f32), jax.ShapeDtypeStruct((N_EXPERTS, 128), f32)],
        compiler_params=_params("arbitrary"),
        name="router",
    )(h, wrt, bias, tri, trie)


def _invert_kernel(dest_ref, seg_end_ref, pad_end_ref, code_ref):
    def fill(lo, hi):
        def body(r, c):
            code_ref[r] = PAD_CODE
            return c
        lax.fori_loop(lo, hi, body, 0)

    def per_expert(e, c):
        fill(seg_end_ref[e], pad_end_ref[e])
        return c
    lax.fori_loop(0, N_EXPERTS, per_expert, 0)
    fill(pad_end_ref[N_EXPERTS - 1], code_ref.shape[0])

    def body(p, c):
        code_ref[dest_ref[p]] = p
        return c
    lax.fori_loop(0, dest_ref.shape[0], body, 0, unroll=8)


def _invert(dest, seg_end, pad_end, n_rows):
    smem = lambda: pl.BlockSpec(memory_space=pltpu.SMEM)
    return pl.pallas_call(
        _invert_kernel,
        in_specs=[smem(), smem(), smem()],
        out_specs=smem(),
        out_shape=jax.ShapeDtypeStruct((n_rows,), jnp.int32),
        name="invert_dispatch",
    )(dest, seg_end, pad_end)


def _expert_changed(be_ref, i):
    return jnp.logical_or(i == 0, be_ref[i] != be_ref[jnp.maximum(i - 1, 0)])


SCATTER_UNROLL = 4


def _moe_up_kernel(be_ref, code_ref, nb_ref, hp_ref, wtok_ref, wg_ref, wu_ref, mid_ref,
                   xg_scr, xa_scr, xb_scr, wra_scr, wrb_scr, wgu_scr):
    i = pl.program_id(0)
    rows = MOE_ROWS

    def gather_block(blk, x_dst, wrow_dst):
        base = blk * rows
        for r in range(rows):
            tok = (code_ref[base + r] >> 3) & (N_TOKENS - 1)
            xg_scr[r * PACK_ROWS:(r + 1) * PACK_ROWS, :] = (
                hp_ref[pl.ds(pl.multiple_of(tok * PACK_ROWS, PACK_ROWS), PACK_ROWS), :])
            wrow_dst[r:r + 1, :] = wtok_ref[pl.ds(tok, 1), :]
        half = D_MODEL // 2
        for s in range(PACK_ROWS):
            wds = xg_scr[pl.ds(s, rows, stride=PACK_ROWS), :]
            lo = pltpu.unpack_elementwise(wds, index=0, packed_dtype=bf16, unpacked_dtype=f32)
            hi = pltpu.unpack_elementwise(wds, index=1, packed_dtype=bf16, unpacked_dtype=f32)
            x_dst[:, s * 128:(s + 1) * 128] = lo.astype(bf16)
            x_dst[:, half + s * 128:half + (s + 1) * 128] = hi.astype(bf16)

    def expert_block(x_cur, wrow_cur, x_nxt, wrow_nxt):
        gather_block(jnp.minimum(i + 1, MOE_BLOCKS - 1), x_nxt, wrow_nxt)
        gu = jnp.dot(x_cur[...], wgu_scr[...], preferred_element_type=f32)
        g, u = gu[:, :D_EXPERT], gu[:, D_EXPERT:]
        lane = lax.broadcasted_iota(jnp.int32, (rows, 128), 1)
        w = jnp.sum(jnp.where(lane == be_ref[i], wrow_cur[...], 0.0), axis=1, keepdims=True)
        real = i * rows + lax.broadcasted_iota(jnp.int32, (rows, 1), 0) < nb_ref[1 + be_ref[i]]
        mid_ref[...] = (g * jax.nn.sigmoid(g) * u * jnp.where(real, w, 0.0)).astype(bf16)

    @pl.when(i == 0)
    def _():
        gather_block(0, xa_scr, wra_scr)

    @pl.when(i < nb_ref[0])
    def _():
        @pl.when(_expert_changed(be_ref, i))
        def _():
            wgu_scr[:, :D_EXPERT] = wg_ref[0].astype(bf16)
            wgu_scr[:, D_EXPERT:] = wu_ref[0].astype(bf16)

        @pl.when(i % 2 == 0)
        def _():
            expert_block(xa_scr, wra_scr, xb_scr, wrb_scr)

        @pl.when(i % 2 == 1)
        def _():
            expert_block(xb_scr, wrb_scr, xa_scr, wra_scr)

    @pl.when(i >= nb_ref[0])
    def _():
        mid_ref[...] = jnp.zeros_like(mid_ref)


def _moe_up(block_e, code, nb, hp, wtok, w_gate, w_up):
    wmap = lambda i, be, cd, nb: (be[i], 0, 0)
    grid_spec = pltpu.PrefetchScalarGridSpec(
        num_scalar_prefetch=3,
        grid=(MOE_BLOCKS,),
        in_specs=[pl.BlockSpec(memory_space=pltpu.VMEM),
                  pl.BlockSpec(memory_space=pltpu.VMEM),
                  pl.BlockSpec((1, D_MODEL, D_EXPERT), wmap),
                  pl.BlockSpec((1, D_MODEL, D_EXPERT), wmap)],
        out_specs=pl.BlockSpec((MOE_ROWS, D_EXPERT), lambda i, be, cd, nb: (i, 0)),
        scratch_shapes=[pltpu.VMEM((PACK_ROWS * MOE_ROWS, 128), jnp.int32),
                        pltpu.VMEM((MOE_ROWS, D_MODEL), bf16), pltpu.VMEM((MOE_ROWS, D_MODEL), bf16),
                        pltpu.VMEM((MOE_ROWS, 128), f32), pltpu.VMEM((MOE_ROWS, 128), f32),
                        pltpu.VMEM((D_MODEL, 2 * D_EXPERT), bf16)])
    return pl.pallas_call(
        _moe_up_kernel,
        grid_spec=grid_spec,
        out_shape=jax.ShapeDtypeStruct((MOE_BLOCKS * MOE_ROWS, D_EXPERT), bf16),
        compiler_params=pltpu.CompilerParams(dimension_semantics=("arbitrary",),
                                             vmem_limit_bytes=MOE_UP_VMEM_LIMIT),
        name="moe_up",
    )(block_e, code, nb, hp, wtok, w_gate, w_up)


def _moe_down_kernel(be_ref, code_ref, nb_ref, mid_ref, wd_ref, acc_ref, ya_scr, yb_scr, wd_scr):
    i = pl.program_id(1)
    rows = MOE_ROWS

    @pl.when(i == 0)
    def _():
        acc_ref[...] = jnp.zeros_like(acc_ref)

    def scatter_block(blk, ybuf):
        for g in range(rows // SCATTER_UNROLL):
            sums, addrs = [], []
            for j in range(SCATTER_UNROLL):
                r = g * SCATTER_UNROLL + j
                a = pl.multiple_of(code_ref[blk * rows + r] & -PACK_ROWS, PACK_ROWS)
                v = ybuf[pl.ds(r, PACK_ROWS, stride=rows), :]
                sums.append(acc_ref[pl.ds(a, PACK_ROWS), :] + v)
                addrs.append(a)
            for j in range(SCATTER_UNROLL):
                acc_ref[pl.ds(addrs[j], PACK_ROWS), :] = sums[j]

    def down_block(ybuf):
        y = jnp.dot(mid_ref[...], wd_scr[...], preferred_element_type=f32)
        for c in range(PACK_ROWS):
            ybuf[c * rows:(c + 1) * rows, :] = y[:, c * 128:(c + 1) * 128]

    @pl.when(jnp.logical_and(i < nb_ref[0], _expert_changed(be_ref, jnp.minimum(i, MOE_BLOCKS - 1))))
    def _():
        wd_scr[...] = wd_ref[0].astype(bf16)

    even = i % 2 == 0
    for parity, y_cur, y_prev in ((True, ya_scr, yb_scr), (False, yb_scr, ya_scr)):
        mine = even if parity else jnp.logical_not(even)

        @pl.when(jnp.logical_and(mine, jnp.logical_and(i > 0, i < nb_ref[0])))
        def _():
            down_block(y_cur)
            scatter_block(i - 1, y_prev)

        @pl.when(jnp.logical_and(mine, i == nb_ref[0]))
        def _():
            scatter_block(i - 1, y_prev)

    @pl.when(i == 0)
    def _():
        down_block(ya_scr)


def _moe_down(block_e, code, nb, mid, w_down):
    half = D_MODEL // 2
    acc_rows = (N_TOKENS + 1) * PACK_ROWS
    last = MOE_BLOCKS - 1
    grid_spec = pltpu.PrefetchScalarGridSpec(
        num_scalar_prefetch=3,
        grid=(2, MOE_BLOCKS + 1),
        in_specs=[pl.BlockSpec((MOE_ROWS, D_EXPERT), lambda p, i, be, cd, nb: (jnp.minimum(i, last), 0)),
                  pl.BlockSpec((1, D_EXPERT, half), lambda p, i, be, cd, nb: (be[jnp.minimum(i, last)], 0, p))],
        out_specs=pl.BlockSpec((None, acc_rows, 128), lambda p, i, be, cd, nb: (p, 0, 0),
                               pipeline_mode=pl.Buffered(1)),
        scratch_shapes=[pltpu.VMEM((PACK_ROWS * MOE_ROWS, 128), f32),
                        pltpu.VMEM((PACK_ROWS * MOE_ROWS, 128), f32),
                        pltpu.VMEM((D_EXPERT, half), bf16)])
    return pl.pallas_call(
        _moe_down_kernel,
        grid_spec=grid_spec,
        out_shape=jax.ShapeDtypeStruct((2, acc_rows, 128), f32),
        compiler_params=_params("arbitrary", "arbitrary"),
        name="moe_down",
    )(block_e, code, nb, mid, w_down)


def _final_kernel(hb_ref, h_ref, r0_ref, r1_ref, wgu_ref, wd_ref, lg_ref, lb_ref, o_ref):
    tm = h_ref.shape[0]
    gu = jnp.dot(hb_ref[...], wgu_ref[...], preferred_element_type=f32)
    g, u = gu[:, :D_EXPERT], gu[:, D_EXPERT:]
    mid = (g * jax.nn.sigmoid(g) * u).astype(bf16)
    shared = jnp.dot(mid, wd_ref[...], preferred_element_type=f32)
    routed = jnp.concatenate([r[pl.ds(c, tm, stride=PACK_ROWS), :]
                              for r in (r0_ref, r1_ref) for c in range(PACK_ROWS)], axis=1)
    o_ref[...] = _layer_norm_rows(DEEPNORM_ALPHA * h_ref[...] + routed + shared,
                                  lg_ref[...], lb_ref[...])


def _final(hb, h, racc, wgu, wd, ln_g, ln_b, tm=256):
    n = h.shape[0]
    row = lambda: pl.BlockSpec((tm, D_MODEL), lambda i: (i, 0))
    full = lambda a: pl.BlockSpec(a.shape, lambda i: (0,) * a.ndim)
    acc = lambda p: pl.BlockSpec((None, tm * PACK_ROWS, 128), lambda i: (p, i, 0))
    return pl.pallas_call(
        _final_kernel,
        grid=(n // tm,),
        in_specs=[row(), row(), acc(0), acc(1), full(wgu), full(wd), full(ln_g), full(ln_b)],
        out_specs=row(),
        out_shape=jax.ShapeDtypeStruct((n, D_MODEL), f32),
        compiler_params=_params("parallel"),
        name="shared_final",
    )(hb, h, racc, racc, wgu, wd, ln_g, ln_b)


def _dispatch_plan(e8, pos8, cnt):
    counts = cnt[:, 0].astype(jnp.int32)
    padded = (counts + MOE_ROWS - 1) // MOE_ROWS * MOE_ROWS
    pad_end = jnp.cumsum(padded).astype(jnp.int32)
    pad_start = pad_end - padded
    seg_end = pad_start + counts
    ids = jnp.arange(N_EXPERTS, dtype=jnp.int32)
    start8 = jnp.sum(jnp.where(e8[..., None] == ids, pad_start, 0), axis=-1)
    dest = (start8 + pos8).T.reshape(-1)
    block_start = jnp.arange(MOE_BLOCKS, dtype=jnp.int32) * MOE_ROWS
    block_e = jnp.minimum(jnp.sum((pad_end[None, :] <= block_start[:, None]).astype(jnp.int32), axis=1),
                          N_EXPERTS - 1)
    nb = jnp.concatenate([pad_end[-1:] // MOE_ROWS, seg_end]).astype(jnp.int32)
    return dest, seg_end, pad_end, block_e, nb


def kernel(x, w_in, ssm_log_dt, ssm_a_re, ssm_a_im, ssm_b_re, ssm_b_im, ssm_c_re, ssm_c_im, ssm_d,
           w_glu, g_ssm_out, g_attn_out, w_out, ln1_g, ln1_b, w_router, router_bias, w_gate, w_up,
           w_down, ws_gate, ws_up, ws_down, ln2_g, ln2_b):
    bsz, seq, d = x.shape
    n_tok = bsz * seq
    h = x.reshape(n_tok, d)
    for layer in range(DEPTH):
        proj = _matmul(h, w_in[layer].astype(bf16), f32)

        tables = _s5_tables(ssm_log_dt[layer], ssm_a_re[layer], ssm_a_im[layer], ssm_b_re[layer],
                            ssm_b_im[layer], ssm_c_re[layer], ssm_c_im[layer], ssm_d[layer])
        y = _s5_mixer(proj, tables, bsz, seq // S5_CHUNK)
        z = _matmul(y, w_glu[layer].astype(bf16), f32)

        y_attn = _dilated_attention(proj, D_SSM // (2 * HEAD_DIM), bsz, seq)

        row2 = lambda a: a.astype(f32).reshape(1, -1)
        h, hb, hp = _mix_out(z, y_attn, h, w_out[layer].astype(bf16), row2(g_ssm_out[layer]),
                             row2(g_attn_out[layer]), row2(ln1_g[layer]), row2(ln1_b[layer]))

        assert n_tok == N_TOKENS
        e8, pos8, wtok, cnt = _router(h, w_router[layer], router_bias[layer])
        dest, seg_end, pad_end, block_e, nb = _dispatch_plan(e8, pos8, cnt)
        code = _invert(dest, seg_end, pad_end, MOE_BLOCKS * MOE_ROWS)
        mid = _moe_up(block_e, code, nb, hp, wtok, w_gate[layer], w_up[layer])
        racc = _moe_down(block_e, code, nb, mid, w_down[layer])
        wgu = jnp.concatenate([ws_gate[layer], ws_up[layer]], axis=1).astype(bf16)
        h = _final(hb, h, racc, wgu, ws_down[layer].astype(bf16), row2(ln2_g[layer]), row2(ln2_b[layer]))
    return h.reshape(bsz, seq, d)
```

```python
import functools

import jax
import jax.numpy as jnp
import numpy as np
from jax import lax
from jax.experimental import pallas as pl
from jax.experimental.pallas import tpu as pltpu

D_MODEL = 2048
D_SSM = 1024
D_ATTN = 1024
SSM_CH = 16
SSM_GROUPS = 64
SSM_STATE = 64
HEAD_DIM = 64
N_HEADS = 16
PATTERNS = ((128, 1), (512, 4), (2048, 16))
ATTN_BLOCK = 128
N_EXPERTS = 64
TOP_K = 8
N_EXPERT_GROUPS = 8
TOPK_GROUPS = 4
D_EXPERT = 512
ROUTED_SCALE = 2.5
NORM_EPS = 1e-5
DEPTH = 1
DEEPNORM_ALPHA = (2.0 * DEPTH) ** 0.25

S5_CHUNK = 16
S5_GROUPS_PER_BLOCK = 8
MOE_ROWS = 256
MASK_VALUE = -1e30
PACK_ROWS = D_MODEL // 2 // 128
N_TOKENS = 8192
PAD_CODE = N_TOKENS * TOP_K
MOE_BLOCKS = -(-(N_TOKENS * TOP_K + N_EXPERTS * (MOE_ROWS - 1)) // MOE_ROWS)
VMEM_LIMIT = 56 * 1024 * 1024
MOE_UP_VMEM_LIMIT = 62 * 1024 * 1024

bf16 = jnp.bfloat16
f32 = jnp.float32


def _params(*sem):
    return pltpu.CompilerParams(dimension_semantics=sem, vmem_limit_bytes=VMEM_LIMIT)


MATMUL_COLS = 1024


def _matmul_kernel(a_ref, b_ref, o_ref):
    a = a_ref[...].astype(bf16)
    for j in range(o_ref.shape[1] // MATMUL_COLS):
        cols = slice(j * MATMUL_COLS, (j + 1) * MATMUL_COLS)
        o_ref[:, cols] = jnp.dot(a, b_ref[:, cols], preferred_element_type=f32).astype(o_ref.dtype)


def _matmul(a, b, out_dtype, tm=256):
    m, k = a.shape
    _, n = b.shape
    return pl.pallas_call(
        _matmul_kernel,
        grid=(m // tm,),
        in_specs=[pl.BlockSpec((tm, k), lambda i: (i, 0)),
                  pl.BlockSpec((k, n), lambda i: (0, 0), pipeline_mode=pl.Buffered(1))],
        out_specs=pl.BlockSpec((tm, n), lambda i: (i, 0)),
        out_shape=jax.ShapeDtypeStruct((m, n), out_dtype),
        compiler_params=_params("parallel"),
        name="matmul",
    )(a, b)


def _s5_tables(log_dt, a_re, a_im, b_re, b_im, c_re, c_im, d_skip):
    t = S5_CHUNK
    gpb = S5_GROUPS_PER_BLOCK
    nblk = SSM_GROUPS // gpb
    hp = lax.Precision.HIGHEST
    lr = jnp.minimum(a_re.astype(f32), -1e-4)
    li = a_im.astype(f32)
    dt = jnp.exp(log_dt.astype(f32))
    kk = jnp.arange(t + 1, dtype=f32)[:, None, None]
    mag = jnp.exp(kk * (lr * dt))
    pr = mag * jnp.cos(kk * (li * dt))
    pi = mag * jnp.sin(kk * (li * dt))
    xr, xi = pr[1] - 1.0, pi[1]
    den = lr * lr + li * li
    cr = (xr * lr + xi * li) / den
    ci = (xi * lr - xr * li) / den
    bbr = cr[..., None] * b_re - ci[..., None] * b_im
    bbi = cr[..., None] * b_im + ci[..., None] * b_re
    wr = pr[:t, :, :, None] * bbr - pi[:t, :, :, None] * bbi
    wi = pr[:t, :, :, None] * bbi + pi[:t, :, :, None] * bbr
    taps = (jnp.einsum('gop,tgpc->tgco', c_re, wr, precision=hp)
            - jnp.einsum('gop,tgpc->tgco', c_im, wi, precision=hp))
    ktab = taps.reshape(t, nblk, gpb, SSM_CH, SSM_CH).transpose(1, 0, 3, 2, 4)
    ktab = ktab.reshape(nblk, t, SSM_CH, gpb * SSM_CH)

    rev = jnp.arange(t - 1, -1, -1)
    sr = pr[rev][..., None] * bbr - pi[rev][..., None] * bbi
    si = pr[rev][..., None] * bbi + pi[rev][..., None] * bbr
    sb = jnp.stack([sr, si], axis=0).reshape(2, t, nblk, gpb, SSM_STATE, SSM_CH)
    bsrc = sb.transpose(2, 1, 0, 3, 5, 4).reshape(nblk, t, 2, gpb * SSM_CH, SSM_STATE)
    bsrc = jnp.concatenate([bsrc, bsrc], axis=-1)

    er = c_re[None] * pr[1:, :, None, :] - c_im[None] * pi[1:, :, None, :]
    ei = c_re[None] * pi[1:, :, None, :] + c_im[None] * pr[1:, :, None, :]
    eb = jnp.stack([er, -ei], axis=0).reshape(2, t, nblk, gpb, SSM_CH, SSM_STATE)
    csrc = eb.transpose(2, 1, 0, 5, 3, 4).reshape(nblk, t, 2, SSM_STATE, gpb * SSM_CH)

    a_chunk = jnp.stack([pr[t], pi[t]], axis=0).reshape(2, nblk, 1, gpb * SSM_STATE)
    a_chunk = a_chunk.transpose(1, 0, 2, 3).reshape(nblk, 2, gpb * SSM_STATE)
    dvec = jnp.tile(d_skip.astype(f32).reshape(nblk, 1, gpb * SSM_CH), (1, 1, t))
    return ktab.astype(bf16), bsrc.astype(bf16), csrc.astype(bf16), a_chunk, dvec


def _s5_kernel(u_ref, ktab_ref, bsrc_ref, csrc_ref, a_ref, d_ref, y_ref,
               toep_ref, bpow_ref, cpow_ref, s_ref, h_ref, yt_ref, *, n_batch, n_chunk):
    t = S5_CHUNK
    gpb = S5_GROUPS_PER_BLOCK
    w = gpb * SSM_CH
    ns = gpb * SSM_STATE
    zero = jnp.zeros((), bf16)

    def same_group(shape, row_size, col_size):
        r = lax.broadcasted_iota(jnp.int32, shape, 0) // row_size
        c = lax.broadcasted_iota(jnp.int32, shape, 1) // col_size
        return r == c

    tap_mask = same_group((w, w), SSM_CH, SSM_CH)
    taps = [jnp.where(tap_mask, jnp.tile(ktab_ref[0, tau], (gpb, 1)), zero) for tau in range(t)]
    for tt in range(t):
        for ss in range(tt + 1):
            toep_ref[ss * w:(ss + 1) * w, tt * w:(tt + 1) * w] = taps[tt - ss]
        if tt % 2 == 0:
            toep_ref[(tt + 1) * w:(tt + 2) * w, tt * w:(tt + 1) * w] = jnp.zeros((w, w), bf16)
    b_mask = same_group((w, ns), SSM_CH, SSM_STATE)
    c_mask = same_group((ns, w), SSM_STATE, SSM_CH)
    for ss in range(t):
        for z in range(2):
            bpow_ref[ss * w:(ss + 1) * w, z * ns:(z + 1) * ns] = jnp.where(
                b_mask, jnp.tile(bsrc_ref[0, ss, z], (1, ns // w)), zero)
            cpow_ref[z * ns:(z + 1) * ns, ss * w:(ss + 1) * w] = jnp.where(
                c_mask, jnp.tile(csrc_ref[0, ss, z], (gpb, 1)), zero)

    rows = n_batch * n_chunk
    uf3 = pltpu.einshape("rsl->srl", u_ref[...].reshape(rows, t, w))
    uf = [uf3[s] for s in range(t)]
    u = jnp.concatenate([p.astype(bf16) for p in uf], axis=1)
    s_ref[...] = jnp.dot(u, bpow_ref[...], preferred_element_type=f32)

    ar = a_ref[0, 0:1, :]
    ai = a_ref[0, 1:2, :]

    def step(j, carry):
        new = []
        for b in range(n_batch):
            hr, hi = carry[2 * b], carry[2 * b + 1]
            row = b * n_chunk + j
            h_ref[pl.ds(row, 1), 0:ns] = hr
            h_ref[pl.ds(row, 1), ns:2 * ns] = hi
            sr = s_ref[pl.ds(row, 1), 0:ns]
            si = s_ref[pl.ds(row, 1), ns:2 * ns]
            new.append(ar * hr - ai * hi + sr)
            new.append(ar * hi + ai * hr + si)
        return tuple(new)

    zero = jnp.zeros((1, ns), f32)
    lax.fori_loop(0, n_chunk, step, (zero,) * (2 * n_batch))

    hprev = h_ref[...].astype(bf16)
    for tp in range(t // 2):
        c0, c1 = 2 * tp * w, (2 * tp + 2) * w
        y = jnp.dot(u[:, :c1], toep_ref[0:c1, c0:c1], preferred_element_type=f32)
        y = y + jnp.dot(hprev, cpow_ref[:, c0:c1], preferred_element_type=f32)
        for k, tt in enumerate((2 * tp, 2 * tp + 1)):
            yk = y[:, k * w:(k + 1) * w] + d_ref[0, :, tt * w:(tt + 1) * w] * uf[tt]
            yt_ref[tt] = jax.nn.gelu(yk, approximate=True)
    y_ref[...] = pltpu.einshape("srl->rsl", yt_ref[...]).reshape(rows * t, w)


def _s5_mixer(proj, tables, n_batch, n_chunk):
    ktab, bsrc, csrc, a_chunk, dvec = tables
    nblk = ktab.shape[0]
    w = S5_GROUPS_PER_BLOCK * SSM_CH
    cols = S5_CHUNK * w
    rows = n_batch * n_chunk
    n_tok = rows * S5_CHUNK
    ns = S5_GROUPS_PER_BLOCK * SSM_STATE
    kern = functools.partial(_s5_kernel, n_batch=n_batch, n_chunk=n_chunk)
    return pl.pallas_call(
        kern,
        grid=(nblk,),
        in_specs=[pl.BlockSpec((n_tok, w), lambda g: (0, g)),
                  pl.BlockSpec((1,) + ktab.shape[1:], lambda g: (g, 0, 0, 0)),
                  pl.BlockSpec((1,) + bsrc.shape[1:], lambda g: (g, 0, 0, 0, 0)),
                  pl.BlockSpec((1,) + csrc.shape[1:], lambda g: (g, 0, 0, 0, 0)),
                  pl.BlockSpec((1, 2, ns), lambda g: (g, 0, 0)),
                  pl.BlockSpec((1, 1, cols), lambda g: (g, 0, 0))],
        out_specs=pl.BlockSpec((n_tok, w), lambda g: (0, g)),
        out_shape=jax.ShapeDtypeStruct((n_tok, D_SSM), f32),
        scratch_shapes=[pltpu.VMEM((cols, cols), bf16),
                        pltpu.VMEM((cols, 2 * ns), bf16),
                        pltpu.VMEM((2 * ns, cols), bf16),
                        pltpu.VMEM((rows, 2 * ns), f32),
                        pltpu.VMEM((rows, 2 * ns), f32),
                        pltpu.VMEM((S5_CHUNK, rows, w), f32)],
        compiler_params=_params("parallel"),
        name="s5_mixer",
    )(proj, ktab, bsrc, csrc, a_chunk, dvec)


def _attn_bias_table():
    blk = ATTN_BLOCK
    slopes = 2.0 ** (-8.0 * jnp.arange(1, N_HEADS + 1, dtype=f32) / N_HEADS)
    delta = np.arange(blk)[:, None] - (np.arange(2 * blk)[None, :] - blk)
    tabs = []
    for window, dil in PATTERNS:
        assert window // dil == blk
        valid = (delta >= 0) & (delta <= window // dil)
        dist = jnp.asarray(delta * dil, dtype=f32)
        bias = jnp.where(valid[None], -slopes[:, None, None] * dist[None], MASK_VALUE)
        tabs.append(bias.reshape(N_HEADS // 2, 2 * blk, 2 * blk))
    return jnp.stack(tabs, axis=1)


def _attn_kernel(q_ref, k_ref, v_ref, bias_ref, o_ref, *scr):
    blk = ATTN_BLOCK
    seq = q_ref.shape[0]
    first_head = lax.broadcasted_iota(jnp.int32, (blk, 2 * HEAD_DIM), 1) < HEAD_DIM
    dims = (((1,), (1,)), ((), ()))
    for pi, (_, dil) in enumerate(PATTERNS):
        o_scr, l_scr = scr[2 * pi], scr[2 * pi + 1]
        sub = seq // dil
        for r in range(dil):
            rows = (lambda st, n: pl.ds(st, n)) if dil == 1 else (lambda st, n: pl.ds(st, n, stride=dil))
            qd = (q_ref[rows(r, sub), :] * HEAD_DIM ** -0.5).astype(bf16)
            kd = k_ref[rows(r, sub), :].astype(bf16)
            vd = v_ref[rows(r, sub), :].astype(bf16)
            for i in range(sub // blk):
                qb = qd[i * blk:(i + 1) * blk]
                zero = jnp.zeros_like(qb)
                q2 = jnp.concatenate([jnp.where(first_head, qb, zero), jnp.where(first_head, zero, qb)], axis=0)
                k0 = max(i - 1, 0) * blk
                nk = (i + 1) * blk - k0
                s = lax.dot_general(q2, kd[k0:k0 + nk], dims, preferred_element_type=f32)
                s = s + bias_ref[0, pi, :, 2 * blk - nk:]
                m = jnp.max(s, axis=-1, keepdims=True)
                p = jnp.exp(s - m)
                l = jnp.sum(p, axis=-1, keepdims=True)
                o = jnp.dot(p.astype(bf16), vd[k0:k0 + nk], preferred_element_type=f32) / l
                lse = m + jnp.log(l)
                dst = rows(r + dil * blk * i, blk)
                o_scr[dst, :] = jnp.where(first_head, o[:blk], o[blk:])
                l_scr[dst, :] = jnp.where(first_head, lse[:blk], lse[blk:])
    l1, l2, l3 = scr[1][...], scr[3][...], scr[5][...]
    m = jnp.maximum(jnp.maximum(l1, l2), l3)
    e1, e2, e3 = jnp.exp(l1 - m), jnp.exp(l2 - m), jnp.exp(l3 - m)
    o_ref[...] = (e1 * scr[0][...] + e2 * scr[2][...] + e3 * scr[4][...]) / (e1 + e2 + e3)


def _dilated_attention(qkv, first, bsz, seq):
    pairs = N_HEADS // 2
    width = 2 * HEAD_DIM
    bias = _attn_bias_table()
    col = lambda off: pl.BlockSpec((seq, width), lambda hp, b: (b, first + off + hp))
    return pl.pallas_call(
        _attn_kernel,
        grid=(pairs, bsz),
        in_specs=[col(0), col(pairs), col(2 * pairs),
                  pl.BlockSpec((1,) + bias.shape[1:], lambda hp, b: (hp, 0, 0, 0))],
        out_specs=pl.BlockSpec((seq, width), lambda hp, b: (b, hp)),
        out_shape=jax.ShapeDtypeStruct((bsz * seq, D_ATTN), f32),
        scratch_shapes=[pltpu.VMEM((seq, width), f32)] * (2 * len(PATTERNS)),
        compiler_params=_params("parallel", "parallel"),
        name="dilated_attention",
    )(qkv, qkv, qkv, bias)


def _layer_norm_rows(x, g, b):
    mu = jnp.mean(x, axis=-1, keepdims=True)
    xc = x - mu
    var = jnp.mean(xc * xc, axis=-1, keepdims=True)
    return xc * lax.rsqrt(var + NORM_EPS) * g + b


def _rms_rows(x, g):
    return x * lax.rsqrt(jnp.mean(x * x, axis=-1, keepdims=True) + NORM_EPS) * g


def _mix_out_kernel(z_ref, ya_ref, x_ref, w_ref, gs_ref, ga_ref, lg_ref, lb_ref, h_ref, hb_ref, hp_ref):
    z = z_ref[...]
    y_ssm = z[:, :D_SSM] * jax.nn.sigmoid(z[:, D_SSM:])
    ns = _rms_rows(y_ssm, gs_ref[...]).astype(bf16)
    na = _rms_rows(ya_ref[...], ga_ref[...]).astype(bf16)
    proj = jnp.dot(ns, w_ref[0:D_SSM, :], preferred_element_type=f32)
    proj = proj + jnp.dot(na, w_ref[D_SSM:, :], preferred_element_type=f32)
    h = _layer_norm_rows(DEEPNORM_ALPHA * x_ref[...] + proj, lg_ref[...], lb_ref[...])
    h_ref[...] = h
    hb_ref[...] = h.astype(bf16)
    half = D_MODEL // 2
    word = pltpu.pack_elementwise([h[:, :half], h[:, half:]], packed_dtype=bf16)
    tm = word.shape[0]
    chunks = jnp.stack([word[:, s * 128:(s + 1) * 128] for s in range(PACK_ROWS)], axis=0)
    hp_ref[...] = pltpu.einshape("srl->rsl", chunks).reshape(tm * PACK_ROWS, 128)


def _mix_out(z, y_attn, x, w_out, g_ssm, g_attn, ln_g, ln_b, tm=256):
    n = x.shape[0]
    row = lambda c: pl.BlockSpec((tm, c), lambda i: (i, 0))
    full = lambda a: pl.BlockSpec(a.shape, lambda i: (0,) * a.ndim)
    return pl.pallas_call(
        _mix_out_kernel,
        grid=(n // tm,),
        in_specs=[row(2 * D_SSM), row(D_ATTN), row(D_MODEL), full(w_out), full(g_ssm), full(g_attn),
                  full(ln_g), full(ln_b)],
        out_specs=[row(D_MODEL), row(D_MODEL), pl.BlockSpec((tm * PACK_ROWS, 128), lambda i: (i, 0))],
        out_shape=[jax.ShapeDtypeStruct((n, D_MODEL), f32), jax.ShapeDtypeStruct((n, D_MODEL), bf16),
                   jax.ShapeDtypeStruct((n * PACK_ROWS, 128), jnp.int32)],
        compiler_params=_params("parallel"),
        name="mix_out",
    )(z, y_attn, x, w_out, g_ssm, g_attn, ln_g, ln_b)


def _router_kernel(h_ref, wrt_ref, bias_ref, tri_ref, trie_ref, e8_ref, pos8_ref, wtok_ref, cnt_ref):
    gsz = N_EXPERTS // N_EXPERT_GROUPS
    tm = h_ref.shape[0]
    ninf = -jnp.inf

    @pl.when(pl.program_id(0) == 0)
    def _():
        cnt_ref[...] = jnp.zeros_like(cnt_ref)

    logits = lax.dot_general(wrt_ref[...], h_ref[...], (((1,), (1,)), ((), ())),
                             preferred_element_type=f32, precision=lax.Precision.HIGHEST)
    scores = jax.nn.sigmoid(logits)
    sel = scores + bias_ref[...]
    io = lax.broadcasted_iota(jnp.int32, (gsz, tm), 0)

    blks, gs_rows = [], []
    for g in range(N_EXPERT_GROUPS):
        blk = sel[g * gsz:(g + 1) * gsz, :]
        m1 = jnp.max(blk, axis=0, keepdims=True)
        first = jnp.min(jnp.where(blk == m1, io, gsz), axis=0, keepdims=True)
        m2 = jnp.max(jnp.where(io == first, ninf, blk), axis=0, keepdims=True)
        blks.append(blk)
        gs_rows.append(m1 + m2)
    gs = jnp.concatenate(gs_rows, axis=0)

    iog = lax.broadcasted_iota(jnp.int32, (N_EXPERT_GROUPS, tm), 0)
    beaten = jnp.zeros((N_EXPERT_GROUPS, tm), f32)
    for gp in range(N_EXPERT_GROUPS):
        row = gs_rows[gp]
        tie = jnp.where(iog > gp, 1.0, 0.0)
        beaten = beaten + jnp.where(row > gs, 1.0, jnp.where(row == gs, tie, 0.0))
    keep = beaten < TOPK_GROUPS
    masked = [jnp.where(keep[g:g + 1, :], blks[g], ninf) for g in range(N_EXPERT_GROUPS)]

    ranks = [jnp.zeros((gsz, tm), f32) for _ in range(N_EXPERT_GROUPS)]
    tie_in = [jnp.where(io > j, 1.0, 0.0) for j in range(gsz)]
    for gp in range(N_EXPERT_GROUPS):
        for j in range(gsz):
            row = masked[gp][j:j + 1, :]
            for g in range(N_EXPERT_GROUPS):
                if gp < g:
                    inc = jnp.where(row >= masked[g], 1.0, 0.0)
                elif gp > g:
                    inc = jnp.where(row > masked[g], 1.0, 0.0)
                else:
                    inc = jnp.where(row > masked[g], 1.0, jnp.where(row == masked[g], tie_in[j], 0.0))
                ranks[g] = ranks[g] + inc
    selb = jnp.concatenate([jnp.where(r < TOP_K, 1.0, 0.0) for r in ranks], axis=0)
    wsel = selb * scores
    wn = wsel / jnp.sum(wsel, axis=0, keepdims=True) * ROUTED_SCALE

    maskb = selb.astype(bf16)
    pos = jnp.dot(maskb, tri_ref[...], preferred_element_type=f32) + cnt_ref[:, 0:1]
    cnt_ref[...] = cnt_ref[...] + jnp.sum(selb, axis=1, keepdims=True)
    slot = jnp.dot(trie_ref[...], maskb, preferred_element_type=f32)
    ioe = lax.broadcasted_iota(jnp.int32, (N_EXPERTS, tm), 0).astype(f32)
    e_rows, p_rows = [], []
    for k in range(TOP_K):
        hit = jnp.where(slot == k, selb, 0.0)
        e_rows.append(jnp.sum(hit * ioe, axis=0, keepdims=True))
        p_rows.append(jnp.sum(hit * pos, axis=0, keepdims=True))
    e8_ref[...] = jnp.concatenate(e_rows, axis=0).astype(jnp.int32)
    pos8_ref[...] = jnp.concatenate(p_rows, axis=0).astype(jnp.int32)
    wtok_ref[...] = jnp.concatenate([wn.T, jnp.zeros((tm, 128 - N_EXPERTS), f32)], axis=1)


def _router(h, w_router, router_bias, tm=512):
    n = h.shape[0]
    wrt = w_router.astype(f32).T
    bias = router_bias.astype(f32).reshape(N_EXPERTS, 1)
    tri = (jnp.arange(tm)[:, None] < jnp.arange(tm)[None, :]).astype(bf16)
    trie = (jnp.arange(N_EXPERTS)[None, :] < jnp.arange(N_EXPERTS)[:, None]).astype(bf16)
    full = lambda a: pl.BlockSpec(a.shape, lambda i: (0,) * a.ndim)
    tok = lambda: pl.BlockSpec((TOP_K, tm), lambda i: (0, i))
    return pl.pallas_call(
        _router_kernel,
        grid=(n // tm,),
        in_specs=[pl.BlockSpec((tm, D_MODEL), lambda i: (i, 0)), full(wrt), full(bias), full(tri), full(trie)],
        out_specs=[tok(), tok(), pl.BlockSpec((tm, 128), lambda i: (i, 0)),
                   pl.BlockSpec((N_EXPERTS, 128), lambda i: (0, 0))],
        out_shape=[jax.ShapeDtypeStruct((TOP_K, n), jnp.int32), jax.ShapeDtypeStruct((TOP_K, n), jnp.int32),
                   jax.ShapeDtypeStruct((n, 128), f32), jax.ShapeDtypeStruct((N_EXPERTS, 128), f32)],
        compiler_params=_params("arbitrary"),
        name="router",
    )(h, wrt, bias, tri, trie)


def _invert_kernel(dest_ref, seg_end_ref, pad_end_ref, code_ref):
    def fill(lo, hi):
        def body(r, c):
            code_ref[r] = PAD_CODE
            return c
        lax.fori_loop(lo, hi, body, 0)

    def per_expert(e, c):
        fill(seg_end_ref[e], pad_end_ref[e])
        return c
    lax.fori_loop(0, N_EXPERTS, per_expert, 0)
    fill(pad_end_ref[N_EXPERTS - 1], code_ref.shape[0])

    def body(p, c):
        code_ref[dest_ref[p]] = p
        return c
    lax.fori_loop(0, dest_ref.shape[0], body, 0, unroll=8)


def _invert(dest, seg_end, pad_end, n_rows):
    smem = lambda: pl.BlockSpec(memory_space=pltpu.SMEM)
    return pl.pallas_call(
        _invert_kernel,
        in_specs=[smem(), smem(), smem()],
        out_specs=smem(),
        out_shape=jax.ShapeDtypeStruct((n_rows,), jnp.int32),
        name="invert_dispatch",
    )(dest, seg_end, pad_end)


def _expert_changed(be_ref, i):
    return jnp.logical_or(i == 0, be_ref[i] != be_ref[jnp.maximum(i - 1, 0)])


SCATTER_UNROLL = 8


def _moe_up_kernel(be_ref, code_ref, nb_ref, hp_ref, wtok_ref, wg_ref, wu_ref, mid_ref,
                   xg_scr, xa_scr, xb_scr, wra_scr, wrb_scr, wgu_scr):
    i = pl.program_id(0)
    rows = MOE_ROWS

    def gather_block(blk, x_dst, wrow_dst):
        base = blk * rows
        for r in range(rows):
            tok = (code_ref[base + r] >> 3) & (N_TOKENS - 1)
            xg_scr[r * PACK_ROWS:(r + 1) * PACK_ROWS, :] = (
                hp_ref[pl.ds(pl.multiple_of(tok * PACK_ROWS, PACK_ROWS), PACK_ROWS), :])
            wrow_dst[r:r + 1, :] = wtok_ref[pl.ds(tok, 1), :]
        half = D_MODEL // 2
        chunks = pltpu.einshape("rsl->srl", xg_scr[...].reshape(rows, PACK_ROWS, 128))
        for s in range(PACK_ROWS):
            wds = chunks[s]
            lo = pltpu.unpack_elementwise(wds, index=0, packed_dtype=bf16, unpacked_dtype=f32)
            hi = pltpu.unpack_elementwise(wds, index=1, packed_dtype=bf16, unpacked_dtype=f32)
            x_dst[:, s * 128:(s + 1) * 128] = lo.astype(bf16)
            x_dst[:, half + s * 128:half + (s + 1) * 128] = hi.astype(bf16)

    def expert_block(x_cur, wrow_cur, x_nxt, wrow_nxt):
        gather_block(jnp.minimum(i + 1, MOE_BLOCKS - 1), x_nxt, wrow_nxt)
        gu = jnp.dot(x_cur[...], wgu_scr[...], preferred_element_type=f32)
        g, u = gu[:, :D_EXPERT], gu[:, D_EXPERT:]
        lane = lax.broadcasted_iota(jnp.int32, (rows, 128), 1)
        w = jnp.sum(jnp.where(lane == be_ref[i], wrow_cur[...], 0.0), axis=1, keepdims=True)
        real = i * rows + lax.broadcasted_iota(jnp.int32, (rows, 1), 0) < nb_ref[1 + be_ref[i]]
        mid_ref[...] = (g * jax.nn.sigmoid(g) * u * jnp.where(real, w, 0.0)).astype(bf16)

    @pl.when(i == 0)
    def _():
        gather_block(0, xa_scr, wra_scr)

    @pl.when(i < nb_ref[0])
    def _():
        @pl.when(_expert_changed(be_ref, i))
        def _():
            wgu_scr[:, :D_EXPERT] = wg_ref[0].astype(bf16)
            wgu_scr[:, D_EXPERT:] = wu_ref[0].astype(bf16)

        @pl.when(i % 2 == 0)
        def _():
            expert_block(xa_scr, wra_scr, xb_scr, wrb_scr)

        @pl.when(i % 2 == 1)
        def _():
            expert_block(xb_scr, wrb_scr, xa_scr, wra_scr)

    @pl.when(i >= nb_ref[0])
    def _():
        mid_ref[...] = jnp.zeros_like(mid_ref)


def _moe_up(block_e, code, nb, hp, wtok, w_gate, w_up):
    wmap = lambda i, be, cd, nb: (be[i], 0, 0)
    grid_spec = pltpu.PrefetchScalarGridSpec(
        num_scalar_prefetch=3,
        grid=(MOE_BLOCKS,),
        in_specs=[pl.BlockSpec(memory_space=pltpu.VMEM),
                  pl.BlockSpec(memory_space=pltpu.VMEM),
                  pl.BlockSpec((1, D_MODEL, D_EXPERT), wmap),
                  pl.BlockSpec((1, D_MODEL, D_EXPERT), wmap)],
        out_specs=pl.BlockSpec((MOE_ROWS, D_EXPERT), lambda i, be, cd, nb: (i, 0)),
        scratch_shapes=[pltpu.VMEM((PACK_ROWS * MOE_ROWS, 128), jnp.int32),
                        pltpu.VMEM((MOE_ROWS, D_MODEL), bf16), pltpu.VMEM((MOE_ROWS, D_MODEL), bf16),
                        pltpu.VMEM((MOE_ROWS, 128), f32), pltpu.VMEM((MOE_ROWS, 128), f32),
                        pltpu.VMEM((D_MODEL, 2 * D_EXPERT), bf16)])
    return pl.pallas_call(
        _moe_up_kernel,
        grid_spec=grid_spec,
        out_shape=jax.ShapeDtypeStruct((MOE_BLOCKS * MOE_ROWS, D_EXPERT), bf16),
        compiler_params=pltpu.CompilerParams(dimension_semantics=("arbitrary",),
                                             vmem_limit_bytes=MOE_UP_VMEM_LIMIT),
        name="moe_up",
    )(block_e, code, nb, hp, wtok, w_gate, w_up)


def _moe_down_kernel(be_ref, code_ref, nb_ref, mid_ref, wd_ref, acc_ref, ya_scr, yb_scr, wd_scr):
    i = pl.program_id(1)
    rows = MOE_ROWS

    @pl.when(i == 0)
    def _():
        acc_ref[...] = jnp.zeros_like(acc_ref)

    def scatter_block(blk, ybuf):
        for g in range(rows // SCATTER_UNROLL):
            sums, addrs = [], []
            for j in range(SCATTER_UNROLL):
                r = g * SCATTER_UNROLL + j
                a = pl.multiple_of(code_ref[blk * rows + r] & -PACK_ROWS, PACK_ROWS)
                v = ybuf[r * PACK_ROWS:(r + 1) * PACK_ROWS, :]
                sums.append(acc_ref[pl.ds(a, PACK_ROWS), :] + v)
                addrs.append(a)
            for j in range(SCATTER_UNROLL):
                acc_ref[pl.ds(addrs[j], PACK_ROWS), :] = sums[j]

    def down_block(ybuf):
        y = jnp.dot(mid_ref[...], wd_scr[...], preferred_element_type=f32)
        y3 = jnp.stack([y[:, c * 128:(c + 1) * 128] for c in range(PACK_ROWS)], axis=0)
        ybuf[...] = pltpu.einshape("crl->rcl", y3).reshape(rows * PACK_ROWS, 128)

    @pl.when(jnp.logical_and(i < nb_ref[0], _expert_changed(be_ref, jnp.minimum(i, MOE_BLOCKS - 1))))
    def _():
        wd_scr[...] = wd_ref[0].astype(bf16)

    even = i % 2 == 0
    for parity, y_cur, y_prev in ((True, ya_scr, yb_scr), (False, yb_scr, ya_scr)):
        mine = even if parity else jnp.logical_not(even)

        @pl.when(jnp.logical_and(mine, jnp.logical_and(i > 0, i < nb_ref[0])))
        def _():
            down_block(y_cur)
            scatter_block(i - 1, y_prev)

        @pl.when(jnp.logical_and(mine, i == nb_ref[0]))
        def _():
            scatter_block(i - 1, y_prev)

    @pl.when(i == 0)
    def _():
        down_block(ya_scr)


def _moe_down(block_e, code, nb, mid, w_down):
    half = D_MODEL // 2
    acc_rows = (N_TOKENS + 1) * PACK_ROWS
    last = MOE_BLOCKS - 1
    grid_spec = pltpu.PrefetchScalarGridSpec(
        num_scalar_prefetch=3,
        grid=(2, MOE_BLOCKS + 1),
        in_specs=[pl.BlockSpec((MOE_ROWS, D_EXPERT), lambda p, i, be, cd, nb: (jnp.minimum(i, last), 0)),
                  pl.BlockSpec((1, D_EXPERT, half), lambda p, i, be, cd, nb: (be[jnp.minimum(i, last)], 0, p))],
        out_specs=pl.BlockSpec((None, acc_rows, 128), lambda p, i, be, cd, nb: (p, 0, 0),
                               pipeline_mode=pl.Buffered(1)),
        scratch_shapes=[pltpu.VMEM((PACK_ROWS * MOE_ROWS, 128), f32),
                        pltpu.VMEM((PACK_ROWS * MOE_ROWS, 128), f32),
                        pltpu.VMEM((D_EXPERT, half), bf16)])
    return pl.pallas_call(
        _moe_down_kernel,
        grid_spec=grid_spec,
        out_shape=jax.ShapeDtypeStruct((2, acc_rows, 128), f32),
        compiler_params=_params("arbitrary", "arbitrary"),
        name="moe_down",
    )(block_e, code, nb, mid, w_down)


def _final_kernel(hb_ref, h_ref, r0_ref, r1_ref, wgu_ref, wd_ref, lg_ref, lb_ref, o_ref):
    tm = h_ref.shape[0]
    gu = jnp.dot(hb_ref[...], wgu_ref[...], preferred_element_type=f32)
    g, u = gu[:, :D_EXPERT], gu[:, D_EXPERT:]
    mid = (g * jax.nn.sigmoid(g) * u).astype(bf16)
    shared = jnp.dot(mid, wd_ref[...], preferred_element_type=f32)
    halves = [pltpu.einshape("rcl->crl", r[...].reshape(tm, PACK_ROWS, 128)) for r in (r0_ref, r1_ref)]
    routed = jnp.concatenate([hv[c] for hv in halves for c in range(PACK_ROWS)], axis=1)
    o_ref[...] = _layer_norm_rows(DEEPNORM_ALPHA * h_ref[...] + routed + shared,
                                  lg_ref[...], lb_ref[...])


def _final(hb, h, racc, wgu, wd, ln_g, ln_b, tm=256):
    n = h.shape[0]
    row = lambda: pl.BlockSpec((tm, D_MODEL), lambda i: (i, 0))
    full = lambda a: pl.BlockSpec(a.shape, lambda i: (0,) * a.ndim)
    acc = lambda p: pl.BlockSpec((None, tm * PACK_ROWS, 128), lambda i: (p, i, 0))
    return pl.pallas_call(
        _final_kernel,
        grid=(n // tm,),
        in_specs=[row(), row(), acc(0), acc(1), full(wgu), full(wd), full(ln_g), full(ln_b)],
        out_specs=row(),
        out_shape=jax.ShapeDtypeStruct((n, D_MODEL), f32),
        compiler_params=_params("parallel"),
        name="shared_final",
    )(hb, h, racc, racc, wgu, wd, ln_g, ln_b)


def _dispatch_plan(e8, pos8, cnt):
    counts = cnt[:, 0].astype(jnp.int32)
    padded = (counts + MOE_ROWS - 1) // MOE_ROWS * MOE_ROWS
    pad_end = jnp.cumsum(padded).astype(jnp.int32)
    pad_start = pad_end - padded
    seg_end = pad_start + counts
    ids = jnp.arange(N_EXPERTS, dtype=jnp.int32)
    start8 = jnp.sum(jnp.where(e8[..., None] == ids, pad_start, 0), axis=-1)
    dest = (start8 + pos8).T.reshape(-1)
    block_start = jnp.arange(MOE_BLOCKS, dtype=jnp.int32) * MOE_ROWS
    block_e = jnp.minimum(jnp.sum((pad_end[None, :] <= block_start[:, None]).astype(jnp.int32), axis=1),
                          N_EXPERTS - 1)
    nb = jnp.concatenate([pad_end[-1:] // MOE_ROWS, seg_end]).astype(jnp.int32)
    return dest, seg_end, pad_end, block_e, nb


def kernel(x, w_in, ssm_log_dt, ssm_a_re, ssm_a_im, ssm_b_re, ssm_b_im, ssm_c_re, ssm_c_im, ssm_d,
           w_glu, g_ssm_out, g_attn_out, w_out, ln1_g, ln1_b, w_router, router_bias, w_gate, w_up,
           w_down, ws_gate, ws_up, ws_down, ln2_g, ln2_b):
    bsz, seq, d = x.shape
    n_tok = bsz * seq
    h = x.reshape(n_tok, d)
    for layer in range(DEPTH):
        proj = _matmul(h, w_in[layer].astype(bf16), f32)

        tables = _s5_tables(ssm_log_dt[layer], ssm_a_re[layer], ssm_a_im[layer], ssm_b_re[layer],
                            ssm_b_im[layer], ssm_c_re[layer], ssm_c_im[layer], ssm_d[layer])
        y = _s5_mixer(proj, tables, bsz, seq // S5_CHUNK)
        z = _matmul(y, w_glu[layer].astype(bf16), f32)

        y_attn = _dilated_attention(proj, D_SSM // (2 * HEAD_DIM), bsz, seq)

        row2 = lambda a: a.astype(f32).reshape(1, -1)
        h, hb, hp = _mix_out(z, y_attn, h, w_out[layer].astype(bf16), row2(g_ssm_out[layer]),
                             row2(g_attn_out[layer]), row2(ln1_g[layer]), row2(ln1_b[layer]))

        assert n_tok == N_TOKENS
        e8, pos8, wtok, cnt = _router(h, w_router[layer], router_bias[layer])
        dest, seg_end, pad_end, block_e, nb = _dispatch_plan(e8, pos8, cnt)
        code = _invert(dest, seg_end, pad_end, MOE_BLOCKS * MOE_ROWS)
        mid = _moe_up(block_e, code, nb, hp, wtok, w_gate[layer], w_up[layer])
        racc = _moe_down(block_e, code, nb, mid, w_down[layer])
        wgu = jnp.concatenate([ws_gate[layer], ws_up[layer]], axis=1).astype(bf16)
        h = _final(hb, h, racc, wgu, ws_down[layer].astype(bf16), row2(ln2_g[layer]), row2(ln2_b[layer]))
    return h.reshape(bsz, seq, d)
```

```python
import functools

import jax
import jax.numpy as jnp
import numpy as np
from jax import lax
from jax.experimental import pallas as pl
from jax.experimental.pallas import tpu as pltpu

D_MODEL = 2048
D_SSM = 1024
D_ATTN = 1024
SSM_CH = 16
SSM_GROUPS = 64
SSM_STATE = 64
HEAD_DIM = 64
N_HEADS = 16
PATTERNS = ((128, 1), (512, 4), (2048, 16))
ATTN_BLOCK = 128
N_EXPERTS = 64
TOP_K = 8
N_EXPERT_GROUPS = 8
TOPK_GROUPS = 4
D_EXPERT = 512
ROUTED_SCALE = 2.5
NORM_EPS = 1e-5
DEPTH = 1
DEEPNORM_ALPHA = (2.0 * DEPTH) ** 0.25

S5_CHUNK = 16
S5_GROUPS_PER_BLOCK = 8
MOE_ROWS = 256
MASK_VALUE = -1e30
PACK_ROWS = D_MODEL // 2 // 128
N_TOKENS = 8192
PAD_CODE = N_TOKENS * TOP_K
MOE_BLOCKS = -(-(N_TOKENS * TOP_K + N_EXPERTS * (MOE_ROWS - 1)) // MOE_ROWS)
VMEM_LIMIT = 56 * 1024 * 1024
MOE_UP_VMEM_LIMIT = 62 * 1024 * 1024

bf16 = jnp.bfloat16
f32 = jnp.float32


def _params(*sem):
    return pltpu.CompilerParams(dimension_semantics=sem, vmem_limit_bytes=VMEM_LIMIT)


MATMUL_COLS = 1024


def _matmul_kernel(a_ref, b_ref, o_ref):
    a = a_ref[...].astype(bf16)
    for j in range(o_ref.shape[1] // MATMUL_COLS):
        cols = slice(j * MATMUL_COLS, (j + 1) * MATMUL_COLS)
        o_ref[:, cols] = jnp.dot(a, b_ref[:, cols], preferred_element_type=f32).astype(o_ref.dtype)


def _matmul(a, b, out_dtype, tm=256):
    m, k = a.shape
    _, n = b.shape
    return pl.pallas_call(
        _matmul_kernel,
        grid=(m // tm,),
        in_specs=[pl.BlockSpec((tm, k), lambda i: (i, 0)),
                  pl.BlockSpec((k, n), lambda i: (0, 0), pipeline_mode=pl.Buffered(1))],
        out_specs=pl.BlockSpec((tm, n), lambda i: (i, 0)),
        out_shape=jax.ShapeDtypeStruct((m, n), out_dtype),
        compiler_params=_params("parallel"),
        name="matmul",
    )(a, b)


def _s5_tables(log_dt, a_re, a_im, b_re, b_im, c_re, c_im, d_skip):
    t = S5_CHUNK
    gpb = S5_GROUPS_PER_BLOCK
    nblk = SSM_GROUPS // gpb
    hp = lax.Precision.HIGHEST
    lr = jnp.minimum(a_re.astype(f32), -1e-4)
    li = a_im.astype(f32)
    dt = jnp.exp(log_dt.astype(f32))
    kk = jnp.arange(t + 1, dtype=f32)[:, None, None]
    mag = jnp.exp(kk * (lr * dt))
    pr = mag * jnp.cos(kk * (li * dt))
    pi = mag * jnp.sin(kk * (li * dt))
    xr, xi = pr[1] - 1.0, pi[1]
    den = lr * lr + li * li
    cr = (xr * lr + xi * li) / den
    ci = (xi * lr - xr * li) / den
    bbr = cr[..., None] * b_re - ci[..., None] * b_im
    bbi = cr[..., None] * b_im + ci[..., None] * b_re
    wr = pr[:t, :, :, None] * bbr - pi[:t, :, :, None] * bbi
    wi = pr[:t, :, :, None] * bbi + pi[:t, :, :, None] * bbr
    taps = (jnp.einsum('gop,tgpc->tgco', c_re, wr, precision=hp)
            - jnp.einsum('gop,tgpc->tgco', c_im, wi, precision=hp))
    ktab = taps.reshape(t, nblk, gpb, SSM_CH, SSM_CH).transpose(1, 0, 3, 2, 4)
    ktab = ktab.reshape(nblk, t, SSM_CH, gpb * SSM_CH)

    rev = jnp.arange(t - 1, -1, -1)
    sr = pr[rev][..., None] * bbr - pi[rev][..., None] * bbi
    si = pr[rev][..., None] * bbi + pi[rev][..., None] * bbr
    sb = jnp.stack([sr, si], axis=0).reshape(2, t, nblk, gpb, SSM_STATE, SSM_CH)
    bsrc = sb.transpose(2, 1, 0, 3, 5, 4).reshape(nblk, t, 2, gpb * SSM_CH, SSM_STATE)
    bsrc = jnp.concatenate([bsrc, bsrc], axis=-1)

    er = c_re[None] * pr[1:, :, None, :] - c_im[None] * pi[1:, :, None, :]
    ei = c_re[None] * pi[1:, :, None, :] + c_im[None] * pr[1:, :, None, :]
    eb = jnp.stack([er, -ei], axis=0).reshape(2, t, nblk, gpb, SSM_CH, SSM_STATE)
    csrc = eb.transpose(2, 1, 0, 5, 3, 4).reshape(nblk, t, 2, SSM_STATE, gpb * SSM_CH)

    a_chunk = jnp.stack([pr[t], pi[t]], axis=0).reshape(2, nblk, 1, gpb * SSM_STATE)
    a_chunk = a_chunk.transpose(1, 0, 2, 3).reshape(nblk, 2, gpb * SSM_STATE)
    dvec = jnp.tile(d_skip.astype(f32).reshape(nblk, 1, gpb * SSM_CH), (1, 1, t))
    return ktab.astype(bf16), bsrc.astype(bf16), csrc.astype(bf16), a_chunk, dvec


def _s5_kernel(u_ref, ktab_ref, bsrc_ref, csrc_ref, a_ref, d_ref, y_ref,
               toep_ref, bpow_ref, cpow_ref, s_ref, h_ref, yt_ref, *, n_batch, n_chunk):
    t = S5_CHUNK
    gpb = S5_GROUPS_PER_BLOCK
    w = gpb * SSM_CH
    ns = gpb * SSM_STATE
    zero = jnp.zeros((), bf16)

    def same_group(shape, row_size, col_size):
        r = lax.broadcasted_iota(jnp.int32, shape, 0) // row_size
        c = lax.broadcasted_iota(jnp.int32, shape, 1) // col_size
        return r == c

    tap_mask = same_group((w, w), SSM_CH, SSM_CH)
    taps = [jnp.where(tap_mask, jnp.tile(ktab_ref[0, tau], (gpb, 1)), zero) for tau in range(t)]
    for tt in range(t):
        for ss in range(tt + 1):
            toep_ref[ss * w:(ss + 1) * w, tt * w:(tt + 1) * w] = taps[tt - ss]
        if tt % 2 == 0:
            toep_ref[(tt + 1) * w:(tt + 2) * w, tt * w:(tt + 1) * w] = jnp.zeros((w, w), bf16)
    b_mask = same_group((w, ns), SSM_CH, SSM_STATE)
    c_mask = same_group((ns, w), SSM_STATE, SSM_CH)
    for ss in range(t):
        for z in range(2):
            bpow_ref[ss * w:(ss + 1) * w, z * ns:(z + 1) * ns] = jnp.where(
                b_mask, jnp.tile(bsrc_ref[0, ss, z], (1, ns // w)), zero)
            cpow_ref[z * ns:(z + 1) * ns, ss * w:(ss + 1) * w] = jnp.where(
                c_mask, jnp.tile(csrc_ref[0, ss, z], (gpb, 1)), zero)

    rows = n_batch * n_chunk
    uf3 = pltpu.einshape("rsl->srl", u_ref[...].reshape(rows, t, w))
    uf = [uf3[s] for s in range(t)]
    u = jnp.concatenate([p.astype(bf16) for p in uf], axis=1)
    s_ref[...] = jnp.dot(u, bpow_ref[...], preferred_element_type=f32)

    ar = a_ref[0, 0:1, :]
    ai = a_ref[0, 1:2, :]

    def step(j, carry):
        new = []
        for b in range(n_batch):
            hr, hi = carry[2 * b], carry[2 * b + 1]
            row = b * n_chunk + j
            h_ref[pl.ds(row, 1), 0:ns] = hr
            h_ref[pl.ds(row, 1), ns:2 * ns] = hi
            sr = s_ref[pl.ds(row, 1), 0:ns]
            si = s_ref[pl.ds(row, 1), ns:2 * ns]
            new.append(ar * hr - ai * hi + sr)
            new.append(ar * hi + ai * hr + si)
        return tuple(new)

    zero = jnp.zeros((1, ns), f32)
    lax.fori_loop(0, n_chunk, step, (zero,) * (2 * n_batch))

    hprev = h_ref[...].astype(bf16)
    for tp in range(t // 2):
        c0, c1 = 2 * tp * w, (2 * tp + 2) * w
        y = jnp.dot(u[:, :c1], toep_ref[0:c1, c0:c1], preferred_element_type=f32)
        y = y + jnp.dot(hprev, cpow_ref[:, c0:c1], preferred_element_type=f32)
        for k, tt in enumerate((2 * tp, 2 * tp + 1)):
            yk = y[:, k * w:(k + 1) * w] + d_ref[0, :, tt * w:(tt + 1) * w] * uf[tt]
            yt_ref[tt] = jax.nn.gelu(yk, approximate=True)
    y_ref[...] = pltpu.einshape("srl->rsl", yt_ref[...]).reshape(rows * t, w)


def _s5_mixer(proj, tables, n_batch, n_chunk):
    ktab, bsrc, csrc, a_chunk, dvec = tables
    nblk = ktab.shape[0]
    w = S5_GROUPS_PER_BLOCK * SSM_CH
    cols = S5_CHUNK * w
    rows = n_batch * n_chunk
    n_tok = rows * S5_CHUNK
    ns = S5_GROUPS_PER_BLOCK * SSM_STATE
    kern = functools.partial(_s5_kernel, n_batch=n_batch, n_chunk=n_chunk)
    return pl.pallas_call(
        kern,
        grid=(nblk,),
        in_specs=[pl.BlockSpec((n_tok, w), lambda g: (0, g)),
                  pl.BlockSpec((1,) + ktab.shape[1:], lambda g: (g, 0, 0, 0)),
                  pl.BlockSpec((1,) + bsrc.shape[1:], lambda g: (g, 0, 0, 0, 0)),
                  pl.BlockSpec((1,) + csrc.shape[1:], lambda g: (g, 0, 0, 0, 0)),
                  pl.BlockSpec((1, 2, ns), lambda g: (g, 0, 0)),
                  pl.BlockSpec((1, 1, cols), lambda g: (g, 0, 0))],
        out_specs=pl.BlockSpec((n_tok, w), lambda g: (0, g)),
        out_shape=jax.ShapeDtypeStruct((n_tok, D_SSM), f32),
        scratch_shapes=[pltpu.VMEM((cols, cols), bf16),
                        pltpu.VMEM((cols, 2 * ns), bf16),
                        pltpu.VMEM((2 * ns, cols), bf16),
                        pltpu.VMEM((rows, 2 * ns), f32),
                        pltpu.VMEM((rows, 2 * ns), f32),
                        pltpu.VMEM((S5_CHUNK, rows, w), f32)],
        compiler_params=_params("parallel"),
        name="s5_mixer",
    )(proj, ktab, bsrc, csrc, a_chunk, dvec)


def _attn_bias_table():
    blk = ATTN_BLOCK
    slopes = 2.0 ** (-8.0 * jnp.arange(1, N_HEADS + 1, dtype=f32) / N_HEADS)
    delta = np.arange(blk)[:, None] - (np.arange(2 * blk)[None, :] - blk)
    tabs = []
    for window, dil in PATTERNS:
        assert window // dil == blk
        valid = (delta >= 0) & (delta <= window // dil)
        dist = jnp.asarray(delta * dil, dtype=f32)
        bias = jnp.where(valid[None], -slopes[:, None, None] * dist[None], MASK_VALUE)
        tabs.append(bias.reshape(N_HEADS // 2, 2 * blk, 2 * blk))
    return jnp.stack(tabs, axis=1)


def _attn_kernel(q_ref, k_ref, v_ref, bias_ref, o_ref, *scr):
    blk = ATTN_BLOCK
    seq = q_ref.shape[0]
    first_head = lax.broadcasted_iota(jnp.int32, (blk, 2 * HEAD_DIM), 1) < HEAD_DIM
    dims = (((1,), (1,)), ((), ()))
    for pi, (_, dil) in enumerate(PATTERNS):
        o_scr, l_scr = scr[2 * pi], scr[2 * pi + 1]
        sub = seq // dil
        for r in range(dil):
            rows = (lambda st, n: pl.ds(st, n)) if dil == 1 else (lambda st, n: pl.ds(st, n, stride=dil))
            qd = (q_ref[rows(r, sub), :] * HEAD_DIM ** -0.5).astype(bf16)
            kd = k_ref[rows(r, sub), :].astype(bf16)
            vd = v_ref[rows(r, sub), :].astype(bf16)
            for i in range(sub // blk):
                qb = qd[i * blk:(i + 1) * blk]
                zero = jnp.zeros_like(qb)
                q2 = jnp.concatenate([jnp.where(first_head, qb, zero), jnp.where(first_head, zero, qb)], axis=0)
                k0 = max(i - 1, 0) * blk
                nk = (i + 1) * blk - k0
                s = lax.dot_general(q2, kd[k0:k0 + nk], dims, preferred_element_type=f32)
                s = s + bias_ref[0, pi, :, 2 * blk - nk:]
                m = jnp.max(s, axis=-1, keepdims=True)
                p = jnp.exp(s - m)
                l = jnp.sum(p, axis=-1, keepdims=True)
                o = jnp.dot(p.astype(bf16), vd[k0:k0 + nk], preferred_element_type=f32) / l
                lse = m + jnp.log(l)
                dst = rows(r + dil * blk * i, blk)
                o_scr[dst, :] = jnp.where(first_head, o[:blk], o[blk:])
                l_scr[dst, :] = jnp.where(first_head, lse[:blk], lse[blk:])
    l1, l2, l3 = scr[1][...], scr[3][...], scr[5][...]
    m = jnp.maximum(jnp.maximum(l1, l2), l3)
    e1, e2, e3 = jnp.exp(l1 - m), jnp.exp(l2 - m), jnp.exp(l3 - m)
    o_ref[...] = (e1 * scr[0][...] + e2 * scr[2][...] + e3 * scr[4][...]) / (e1 + e2 + e3)


def _dilated_attention(qkv, first, bsz, seq):
    pairs = N_HEADS // 2
    width = 2 * HEAD_DIM
    bias = _attn_bias_table()
    col = lambda off: pl.BlockSpec((seq, width), lambda hp, b: (b, first + off + hp))
    return pl.pallas_call(
        _attn_kernel,
        grid=(pairs, bsz),
        in_specs=[col(0), col(pairs), col(2 * pairs),
                  pl.BlockSpec((1,) + bias.shape[1:], lambda hp, b: (hp, 0, 0, 0))],
        out_specs=pl.BlockSpec((seq, width), lambda hp, b: (b, hp)),
        out_shape=jax.ShapeDtypeStruct((bsz * seq, D_ATTN), f32),
        scratch_shapes=[pltpu.VMEM((seq, width), f32)] * (2 * len(PATTERNS)),
        compiler_params=_params("parallel", "parallel"),
        name="dilated_attention",
    )(qkv, qkv, qkv, bias)


def _layer_norm_rows(x, g, b):
    mu = jnp.mean(x, axis=-1, keepdims=True)
    xc = x - mu
    var = jnp.mean(xc * xc, axis=-1, keepdims=True)
    return xc * lax.rsqrt(var + NORM_EPS) * g + b


def _rms_rows(x, g):
    return x * lax.rsqrt(jnp.mean(x * x, axis=-1, keepdims=True) + NORM_EPS) * g


def _mix_out_kernel(z_ref, ya_ref, x_ref, w_ref, gs_ref, ga_ref, lg_ref, lb_ref, h_ref, hb_ref, hp_ref):
    z = z_ref[...]
    y_ssm = z[:, :D_SSM] * jax.nn.sigmoid(z[:, D_SSM:])
    ns = _rms_rows(y_ssm, gs_ref[...]).astype(bf16)
    na = _rms_rows(ya_ref[...], ga_ref[...]).astype(bf16)
    proj = jnp.dot(ns, w_ref[0:D_SSM, :], preferred_element_type=f32)
    proj = proj + jnp.dot(na, w_ref[D_SSM:, :], preferred_element_type=f32)
    h = _layer_norm_rows(DEEPNORM_ALPHA * x_ref[...] + proj, lg_ref[...], lb_ref[...])
    h_ref[...] = h
    hb_ref[...] = h.astype(bf16)
    half = D_MODEL // 2
    word = pltpu.pack_elementwise([h[:, :half], h[:, half:]], packed_dtype=bf16)
    tm = word.shape[0]
    chunks = jnp.stack([word[:, s * 128:(s + 1) * 128] for s in range(PACK_ROWS)], axis=0)
    hp_ref[...] = pltpu.einshape("srl->rsl", chunks).reshape(tm * PACK_ROWS, 128)


def _mix_out(z, y_attn, x, w_out, g_ssm, g_attn, ln_g, ln_b, tm=256):
    n = x.shape[0]
    row = lambda c: pl.BlockSpec((tm, c), lambda i: (i, 0))
    full = lambda a: pl.BlockSpec(a.shape, lambda i: (0,) * a.ndim)
    return pl.pallas_call(
        _mix_out_kernel,
        grid=(n // tm,),
        in_specs=[row(2 * D_SSM), row(D_ATTN), row(D_MODEL), full(w_out), full(g_ssm), full(g_attn),
                  full(ln_g), full(ln_b)],
        out_specs=[row(D_MODEL), row(D_MODEL), pl.BlockSpec((tm * PACK_ROWS, 128), lambda i: (i, 0))],
        out_shape=[jax.ShapeDtypeStruct((n, D_MODEL), f32), jax.ShapeDtypeStruct((n, D_MODEL), bf16),
                   jax.ShapeDtypeStruct((n * PACK_ROWS, 128), jnp.int32)],
        compiler_params=_params("parallel"),
        name="mix_out",
    )(z, y_attn, x, w_out, g_ssm, g_attn, ln_g, ln_b)


def _router_kernel(h_ref, wrt_ref, bias_ref, tri_ref, trie_ref, e8_ref, pos8_ref, wtok_ref, cnt_ref):
    gsz = N_EXPERTS // N_EXPERT_GROUPS
    tm = h_ref.shape[0]
    ninf = -jnp.inf

    @pl.when(pl.program_id(0) == 0)
    def _():
        cnt_ref[...] = jnp.zeros_like(cnt_ref)

    logits = lax.dot_general(wrt_ref[...], h_ref[...], (((1,), (1,)), ((), ())),
                             preferred_element_type=f32, precision=lax.Precision.HIGHEST)
    scores = jax.nn.sigmoid(logits)
    sel = scores + bias_ref[...]
    io = lax.broadcasted_iota(jnp.int32, (gsz, tm), 0)

    blks, gs_rows = [], []
    for g in range(N_EXPERT_GROUPS):
        blk = sel[g * gsz:(g + 1) * gsz, :]
        m1 = jnp.max(blk, axis=0, keepdims=True)
        first = jnp.min(jnp.where(blk == m1, io, gsz), axis=0, keepdims=True)
        m2 = jnp.max(jnp.where(io == first, ninf, blk), axis=0, keepdims=True)
        blks.append(blk)
        gs_rows.append(m1 + m2)
    gs = jnp.concatenate(gs_rows, axis=0)

    iog = lax.broadcasted_iota(jnp.int32, (N_EXPERT_GROUPS, tm), 0)
    beaten = jnp.zeros((N_EXPERT_GROUPS, tm), f32)
    for gp in range(N_EXPERT_GROUPS):
        row = gs_rows[gp]
        tie = jnp.where(iog > gp, 1.0, 0.0)
        beaten = beaten + jnp.where(row > gs, 1.0, jnp.where(row == gs, tie, 0.0))
    keep = beaten < TOPK_GROUPS
    masked = [jnp.where(keep[g:g + 1, :], blks[g], ninf) for g in range(N_EXPERT_GROUPS)]

    ranks = [jnp.zeros((gsz, tm), f32) for _ in range(N_EXPERT_GROUPS)]
    tie_in = [jnp.where(io > j, 1.0, 0.0) for j in range(gsz)]
    for gp in range(N_EXPERT_GROUPS):
        for j in range(gsz):
            row = masked[gp][j:j + 1, :]
            for g in range(N_EXPERT_GROUPS):
                if gp < g:
                    inc = jnp.where(row >= masked[g], 1.0, 0.0)
                elif gp > g:
                    inc = jnp.where(row > masked[g], 1.0, 0.0)
                else:
                    inc = jnp.where(row > masked[g], 1.0, jnp.where(row == masked[g], tie_in[j], 0.0))
                ranks[g] = ranks[g] + inc
    selb = jnp.concatenate([jnp.where(r < TOP_K, 1.0, 0.0) for r in ranks], axis=0)
    wsel = selb * scores
    wn = wsel / jnp.sum(wsel, axis=0, keepdims=True) * ROUTED_SCALE

    maskb = selb.astype(bf16)
    pos = jnp.dot(maskb, tri_ref[...], preferred_element_type=f32) + cnt_ref[:, 0:1]
    cnt_ref[...] = cnt_ref[...] + jnp.sum(selb, axis=1, keepdims=True)
    slot = jnp.dot(trie_ref[...], maskb, preferred_element_type=f32)
    ioe = lax.broadcasted_iota(jnp.int32, (N_EXPERTS, tm), 0).astype(f32)
    e_rows, p_rows = [], []
    for k in range(TOP_K):
        hit = jnp.where(slot == k, selb, 0.0)
        e_rows.append(jnp.sum(hit * ioe, axis=0, keepdims=True))
        p_rows.append(jnp.sum(hit * pos, axis=0, keepdims=True))
    e8_ref[...] = jnp.concatenate(e_rows, axis=0).astype(jnp.int32)
    pos8_ref[...] = jnp.concatenate(p_rows, axis=0).astype(jnp.int32)
    wtok_ref[...] = jnp.concatenate([wn.T, jnp.zeros((tm, 128 - N_EXPERTS), f32)], axis=1)


def _router(h, w_router, router_bias, tm=512):
    n = h.shape[0]
    wrt = w_router.astype(f32).T
    bias = router_bias.astype(f32).reshape(N_EXPERTS, 1)
    tri = (jnp.arange(tm)[:, None] < jnp.arange(tm)[None, :]).astype(bf16)
    trie = (jnp.arange(N_EXPERTS)[None, :] < jnp.arange(N_EXPERTS)[:, None]).astype(bf16)
    full = lambda a: pl.BlockSpec(a.shape, lambda i: (0,) * a.ndim)
    tok = lambda: pl.BlockSpec((TOP_K, tm), lambda i: (0, i))
    return pl.pallas_call(
        _router_kernel,
        grid=(n // tm,),
        in_specs=[pl.BlockSpec((tm, D_MODEL), lambda i: (i, 0)), full(wrt), full(bias), full(tri), full(trie)],
        out_specs=[tok(), tok(), pl.BlockSpec((tm, 128), lambda i: (i, 0)),
                   pl.BlockSpec((N_EXPERTS, 128), lambda i: (0, 0))],
        out_shape=[jax.ShapeDtypeStruct((TOP_K, n), jnp.int32), jax.ShapeDtypeStruct((TOP_K, n), jnp.int32),
                   jax.ShapeDtypeStruct((n, 128), f32), jax.ShapeDtypeStruct((N_EXPERTS, 128), f32)],
        compiler_params=_params("arbitrary"),
        name="router",
    )(h, wrt, bias, tri, trie)


def _invert_kernel(dest_ref, init_ref, code_ref):
    del init_ref

    def body(p, c):
        code_ref[dest_ref[p]] = p
        return c
    lax.fori_loop(0, dest_ref.shape[0], body, 0, unroll=16)


def _invert(dest, n_rows):
    smem = lambda: pl.BlockSpec(memory_space=pltpu.SMEM)
    return pl.pallas_call(
        _invert_kernel,
        in_specs=[smem(), smem()],
        out_specs=smem(),
        out_shape=jax.ShapeDtypeStruct((n_rows,), jnp.int32),
        input_output_aliases={1: 0},
        name="invert_dispatch",
    )(dest, jnp.full((n_rows,), PAD_CODE, jnp.int32))


def _expert_changed(be_ref, i):
    return jnp.logical_or(i == 0, be_ref[i] != be_ref[jnp.maximum(i - 1, 0)])


SCATTER_UNROLL = 8


def _moe_up_kernel(be_ref, code_ref, nb_ref, hp_ref, wtok_ref, wg_ref, wu_ref, mid_ref,
                   xg_scr, xa_scr, xb_scr, wra_scr, wrb_scr, wgu_scr):
    i = pl.program_id(0)
    rows = MOE_ROWS

    def gather_block(blk, x_dst, wrow_dst):
        base = blk * rows
        for r in range(rows):
            tok = (code_ref[base + r] >> 3) & (N_TOKENS - 1)
            xg_scr[r * PACK_ROWS:(r + 1) * PACK_ROWS, :] = (
                hp_ref[pl.ds(pl.multiple_of(tok * PACK_ROWS, PACK_ROWS), PACK_ROWS), :])
            wrow_dst[r:r + 1, :] = wtok_ref[pl.ds(tok, 1), :]
        half = D_MODEL // 2
        chunks = pltpu.einshape("rsl->srl", xg_scr[...].reshape(rows, PACK_ROWS, 128))
        for s in range(PACK_ROWS):
            wds = chunks[s]
            lo = pltpu.unpack_elementwise(wds, index=0, packed_dtype=bf16, unpacked_dtype=f32)
            hi = pltpu.unpack_elementwise(wds, index=1, packed_dtype=bf16, unpacked_dtype=f32)
            x_dst[:, s * 128:(s + 1) * 128] = lo.astype(bf16)
            x_dst[:, half + s * 128:half + (s + 1) * 128] = hi.astype(bf16)

    def expert_block(x_cur, wrow_cur, x_nxt, wrow_nxt):
        gather_block(jnp.minimum(i + 1, MOE_BLOCKS - 1), x_nxt, wrow_nxt)
        gu = jnp.dot(x_cur[...], wgu_scr[...], preferred_element_type=f32)
        g, u = gu[:, :D_EXPERT], gu[:, D_EXPERT:]
        lane = lax.broadcasted_iota(jnp.int32, (rows, 128), 1)
        w = jnp.sum(jnp.where(lane == be_ref[i], wrow_cur[...], 0.0), axis=1, keepdims=True)
        real = i * rows + lax.broadcasted_iota(jnp.int32, (rows, 1), 0) < nb_ref[1 + be_ref[i]]
        mid_ref[...] = (g * jax.nn.sigmoid(g) * u * jnp.where(real, w, 0.0)).astype(bf16)

    @pl.when(i == 0)
    def _():
        gather_block(0, xa_scr, wra_scr)

    @pl.when(i < nb_ref[0])
    def _():
        @pl.when(_expert_changed(be_ref, i))
        def _():
            wgu_scr[:, :D_EXPERT] = wg_ref[0].astype(bf16)
            wgu_scr[:, D_EXPERT:] = wu_ref[0].astype(bf16)

        @pl.when(i % 2 == 0)
        def _():
            expert_block(xa_scr, wra_scr, xb_scr, wrb_scr)

        @pl.when(i % 2 == 1)
        def _():
            expert_block(xb_scr, wrb_scr, xa_scr, wra_scr)

    @pl.when(i >= nb_ref[0])
    def _():
        mid_ref[...] = jnp.zeros_like(mid_ref)


def _moe_up(block_e, code, nb, hp, wtok, w_gate, w_up):
    wmap = lambda i, be, cd, nb: (be[i], 0, 0)
    grid_spec = pltpu.PrefetchScalarGridSpec(
        num_scalar_prefetch=3,
        grid=(MOE_BLOCKS,),
        in_specs=[pl.BlockSpec(memory_space=pltpu.VMEM),
                  pl.BlockSpec(memory_space=pltpu.VMEM),
                  pl.BlockSpec((1, D_MODEL, D_EXPERT), wmap),
                  pl.BlockSpec((1, D_MODEL, D_EXPERT), wmap)],
        out_specs=pl.BlockSpec((MOE_ROWS, D_EXPERT), lambda i, be, cd, nb: (i, 0)),
        scratch_shapes=[pltpu.VMEM((PACK_ROWS * MOE_ROWS, 128), jnp.int32),
                        pltpu.VMEM((MOE_ROWS, D_MODEL), bf16), pltpu.VMEM((MOE_ROWS, D_MODEL), bf16),
                        pltpu.VMEM((MOE_ROWS, 128), f32), pltpu.VMEM((MOE_ROWS, 128), f32),
                        pltpu.VMEM((D_MODEL, 2 * D_EXPERT), bf16)])
    return pl.pallas_call(
        _moe_up_kernel,
        grid_spec=grid_spec,
        out_shape=jax.ShapeDtypeStruct((MOE_BLOCKS * MOE_ROWS, D_EXPERT), bf16),
        compiler_params=pltpu.CompilerParams(dimension_semantics=("arbitrary",),
                                             vmem_limit_bytes=MOE_UP_VMEM_LIMIT),
        name="moe_up",
    )(block_e, code, nb, hp, wtok, w_gate, w_up)


def _moe_down_kernel(be_ref, code_ref, nb_ref, mida_ref, midb_ref, wda_ref, wdb_ref, acc_ref,
                     ya_scr, yb_scr, wd_scr):
    j = pl.program_id(1)
    rows = MOE_ROWS
    nb = nb_ref[0]
    blk_a, blk_b = 2 * j, 2 * j + 1

    @pl.when(j == 0)
    def _():
        acc_ref[...] = jnp.zeros_like(acc_ref)

    def scatter_block(blk, ybuf):
        for g in range(rows // SCATTER_UNROLL):
            sums, addrs = [], []
            for k in range(SCATTER_UNROLL):
                r = g * SCATTER_UNROLL + k
                a = pl.multiple_of(code_ref[blk * rows + r] & -PACK_ROWS, PACK_ROWS)
                v = ybuf[r * PACK_ROWS:(r + 1) * PACK_ROWS, :]
                sums.append(acc_ref[pl.ds(a, PACK_ROWS), :] + v)
                addrs.append(a)
            for k in range(SCATTER_UNROLL):
                acc_ref[pl.ds(addrs[k], PACK_ROWS), :] = sums[k]

    def down_block(mid_ref, ybuf):
        y = jnp.dot(mid_ref[...], wd_scr[...], preferred_element_type=f32)
        y3 = jnp.stack([y[:, c * 128:(c + 1) * 128] for c in range(PACK_ROWS)], axis=0)
        ybuf[...] = pltpu.einshape("crl->rcl", y3).reshape(rows * PACK_ROWS, 128)

    def refresh_weights(blk, wd_ref):
        @pl.when(jnp.logical_and(blk < nb, _expert_changed(be_ref, jnp.minimum(blk, MOE_BLOCKS - 1))))
        def _():
            wd_scr[...] = wd_ref[0].astype(bf16)

    refresh_weights(blk_a, wda_ref)

    @pl.when(jnp.logical_and(j == 0, blk_a < nb))
    def _():
        down_block(mida_ref, ya_scr)

    @pl.when(jnp.logical_and(j > 0, blk_a < nb))
    def _():
        down_block(mida_ref, ya_scr)
        scatter_block(blk_a - 1, yb_scr)

    @pl.when(jnp.logical_and(j > 0, blk_a == nb))
    def _():
        scatter_block(blk_a - 1, yb_scr)

    refresh_weights(blk_b, wdb_ref)

    @pl.when(blk_b < nb)
    def _():
        down_block(midb_ref, yb_scr)
        scatter_block(blk_a, ya_scr)

    @pl.when(blk_b == nb)
    def _():
        scatter_block(blk_a, ya_scr)


def _moe_down(block_e, code, nb, mid, w_down):
    half = D_MODEL // 2
    acc_rows = (N_TOKENS + 1) * PACK_ROWS
    last = MOE_BLOCKS - 1
    assert MOE_BLOCKS % 2 == 0
    blk = lambda off: (lambda p, j, be, cd, nb: (jnp.minimum(2 * j + off, last), 0))
    wd = lambda off: (lambda p, j, be, cd, nb: (be[jnp.minimum(2 * j + off, last)], 0, p))
    grid_spec = pltpu.PrefetchScalarGridSpec(
        num_scalar_prefetch=3,
        grid=(2, MOE_BLOCKS // 2 + 1),
        in_specs=[pl.BlockSpec((MOE_ROWS, D_EXPERT), blk(0)), pl.BlockSpec((MOE_ROWS, D_EXPERT), blk(1)),
                  pl.BlockSpec((1, D_EXPERT, half), wd(0)), pl.BlockSpec((1, D_EXPERT, half), wd(1))],
        out_specs=pl.BlockSpec((None, acc_rows, 128), lambda p, j, be, cd, nb: (p, 0, 0),
                               pipeline_mode=pl.Buffered(1)),
        scratch_shapes=[pltpu.VMEM((PACK_ROWS * MOE_ROWS, 128), f32),
                        pltpu.VMEM((PACK_ROWS * MOE_ROWS, 128), f32),
                        pltpu.VMEM((D_EXPERT, half), bf16)])
    return pl.pallas_call(
        _moe_down_kernel,
        grid_spec=grid_spec,
        out_shape=jax.ShapeDtypeStruct((2, acc_rows, 128), f32),
        compiler_params=_params("arbitrary", "arbitrary"),
        name="moe_down",
    )(block_e, code, nb, mid, mid, w_down, w_down)


def _final_kernel(hb_ref, h_ref, r0_ref, r1_ref, wgu_ref, wd_ref, lg_ref, lb_ref, o_ref):
    tm = h_ref.shape[0]
    gu = jnp.dot(hb_ref[...], wgu_ref[...], preferred_element_type=f32)
    g, u = gu[:, :D_EXPERT], gu[:, D_EXPERT:]
    mid = (g * jax.nn.sigmoid(g) * u).astype(bf16)
    shared = jnp.dot(mid, wd_ref[...], preferred_element_type=f32)
    halves = [pltpu.einshape("rcl->crl", r[...].reshape(tm, PACK_ROWS, 128)) for r in (r0_ref, r1_ref)]
    routed = jnp.concatenate([hv[c] for hv in halves for c in range(PACK_ROWS)], axis=1)
    o_ref[...] = _layer_norm_rows(DEEPNORM_ALPHA * h_ref[...] + routed + shared,
                                  lg_ref[...], lb_ref[...])


def _final(hb, h, racc, wgu, wd, ln_g, ln_b, tm=256):
    n = h.shape[0]
    row = lambda: pl.BlockSpec((tm, D_MODEL), lambda i: (i, 0))
    full = lambda a: pl.BlockSpec(a.shape, lambda i: (0,) * a.ndim)
    acc = lambda p: pl.BlockSpec((None, tm * PACK_ROWS, 128), lambda i: (p, i, 0))
    return pl.pallas_call(
        _final_kernel,
        grid=(n // tm,),
        in_specs=[row(), row(), acc(0), acc(1), full(wgu), full(wd), full(ln_g), full(ln_b)],
        out_specs=row(),
        out_shape=jax.ShapeDtypeStruct((n, D_MODEL), f32),
        compiler_params=_params("parallel"),
        name="shared_final",
    )(hb, h, racc, racc, wgu, wd, ln_g, ln_b)


def _dispatch_plan(e8, pos8, cnt):
    counts = cnt[:, 0].astype(jnp.int32)
    padded = (counts + MOE_ROWS - 1) // MOE_ROWS * MOE_ROWS
    pad_end = jnp.cumsum(padded).astype(jnp.int32)
    pad_start = pad_end - padded
    seg_end = pad_start + counts
    ids = jnp.arange(N_EXPERTS, dtype=jnp.int32)
    start8 = jnp.sum(jnp.where(e8[..., None] == ids, pad_start, 0), axis=-1)
    dest = (start8 + pos8).T.reshape(-1)
    block_start = jnp.arange(MOE_BLOCKS, dtype=jnp.int32) * MOE_ROWS
    block_e = jnp.minimum(jnp.sum((pad_end[None, :] <= block_start[:, None]).astype(jnp.int32), axis=1),
                          N_EXPERTS - 1)
    nb = jnp.concatenate([pad_end[-1:] // MOE_ROWS, seg_end]).astype(jnp.int32)
    return dest, block_e, nb


def kernel(x, w_in, ssm_log_dt, ssm_a_re, ssm_a_im, ssm_b_re, ssm_b_im, ssm_c_re, ssm_c_im, ssm_d,
           w_glu, g_ssm_out, g_attn_out, w_out, ln1_g, ln1_b, w_router, router_bias, w_gate, w_up,
           w_down, ws_gate, ws_up, ws_down, ln2_g, ln2_b):
    bsz, seq, d = x.shape
    n_tok = bsz * seq
    h = x.reshape(n_tok, d)
    for layer in range(DEPTH):
        proj = _matmul(h, w_in[layer].astype(bf16), f32)

        tables = _s5_tables(ssm_log_dt[layer], ssm_a_re[layer], ssm_a_im[layer], ssm_b_re[layer],
                            ssm_b_im[layer], ssm_c_re[layer], ssm_c_im[layer], ssm_d[layer])
        y = _s5_mixer(proj, tables, bsz, seq // S5_CHUNK)
        z = _matmul(y, w_glu[layer].astype(bf16), f32)

        y_attn = _dilated_attention(proj, D_SSM // (2 * HEAD_DIM), bsz, seq)

        row2 = lambda a: a.astype(f32).reshape(1, -1)
        h, hb, hp = _mix_out(z, y_attn, h, w_out[layer].astype(bf16), row2(g_ssm_out[layer]),
                             row2(g_attn_out[layer]), row2(ln1_g[layer]), row2(ln1_b[layer]))

        assert n_tok == N_TOKENS
        e8, pos8, wtok, cnt = _router(h, w_router[layer], router_bias[layer])
        dest, block_e, nb = _dispatch_plan(e8, pos8, cnt)
        code = _invert(dest, MOE_BLOCKS * MOE_ROWS)
        mid = _moe_up(block_e, code, nb, hp, wtok, w_gate[layer], w_up[layer])
        racc = _moe_down(block_e, code, nb, mid, w_down[layer])
        wgu = jnp.concatenate([ws_gate[layer], ws_up[layer]], axis=1).astype(bf16)
        h = _final(hb, h, racc, wgu, ws_down[layer].astype(bf16), row2(ln2_g[layer]), row2(ln2_b[layer]))
    return h.reshape(bsz, seq, d)
```

```python
import functools

import jax
import jax.numpy as jnp
import numpy as np
from jax import lax
from jax.experimental import pallas as pl
from jax.experimental.pallas import tpu as pltpu

D_MODEL = 2048
D_SSM = 1024
D_ATTN = 1024
SSM_CH = 16
SSM_GROUPS = 64
SSM_STATE = 64
HEAD_DIM = 64
N_HEADS = 16
PATTERNS = ((128, 1), (512, 4), (2048, 16))
ATTN_BLOCK = 128
N_EXPERTS = 64
TOP_K = 8
N_EXPERT_GROUPS = 8
TOPK_GROUPS = 4
D_EXPERT = 512
ROUTED_SCALE = 2.5
NORM_EPS = 1e-5
DEPTH = 1
DEEPNORM_ALPHA = (2.0 * DEPTH) ** 0.25

S5_CHUNK = 16
S5_GROUPS_PER_BLOCK = 8
MOE_ROWS = 256
MASK_VALUE = -1e30
PACK_ROWS = D_MODEL // 2 // 128
N_TOKENS = 8192
PAD_CODE = N_TOKENS * TOP_K
MOE_BLOCKS = -(-(N_TOKENS * TOP_K + N_EXPERTS * (MOE_ROWS - 1)) // MOE_ROWS)
VMEM_LIMIT = 56 * 1024 * 1024
MOE_UP_VMEM_LIMIT = 62 * 1024 * 1024

bf16 = jnp.bfloat16
f32 = jnp.float32


def _params(*sem):
    return pltpu.CompilerParams(dimension_semantics=sem, vmem_limit_bytes=VMEM_LIMIT)


MATMUL_COLS = 1024


def _matmul_kernel(a_ref, b_ref, o_ref):
    a = a_ref[...].astype(bf16)
    for j in range(o_ref.shape[1] // MATMUL_COLS):
        cols = slice(j * MATMUL_COLS, (j + 1) * MATMUL_COLS)
        o_ref[:, cols] = jnp.dot(a, b_ref[:, cols], preferred_element_type=f32).astype(o_ref.dtype)


def _matmul(a, b, out_dtype, tm=256):
    m, k = a.shape
    _, n = b.shape
    return pl.pallas_call(
        _matmul_kernel,
        grid=(m // tm,),
        in_specs=[pl.BlockSpec((tm, k), lambda i: (i, 0)),
                  pl.BlockSpec((k, n), lambda i: (0, 0), pipeline_mode=pl.Buffered(1))],
        out_specs=pl.BlockSpec((tm, n), lambda i: (i, 0)),
        out_shape=jax.ShapeDtypeStruct((m, n), out_dtype),
        compiler_params=_params("parallel"),
        name="matmul",
    )(a, b)


def _s5_tables(log_dt, a_re, a_im, b_re, b_im, c_re, c_im, d_skip):
    t = S5_CHUNK
    gpb = S5_GROUPS_PER_BLOCK
    nblk = SSM_GROUPS // gpb
    hp = lax.Precision.HIGHEST
    lr = jnp.minimum(a_re.astype(f32), -1e-4)
    li = a_im.astype(f32)
    dt = jnp.exp(log_dt.astype(f32))
    kk = jnp.arange(t + 1, dtype=f32)[:, None, None]
    mag = jnp.exp(kk * (lr * dt))
    pr = mag * jnp.cos(kk * (li * dt))
    pi = mag * jnp.sin(kk * (li * dt))
    xr, xi = pr[1] - 1.0, pi[1]
    den = lr * lr + li * li
    cr = (xr * lr + xi * li) / den
    ci = (xi * lr - xr * li) / den
    bbr = cr[..., None] * b_re - ci[..., None] * b_im
    bbi = cr[..., None] * b_im + ci[..., None] * b_re
    wr = pr[:t, :, :, None] * bbr - pi[:t, :, :, None] * bbi
    wi = pr[:t, :, :, None] * bbi + pi[:t, :, :, None] * bbr
    taps = (jnp.einsum('gop,tgpc->tgco', c_re, wr, precision=hp)
            - jnp.einsum('gop,tgpc->tgco', c_im, wi, precision=hp))
    ktab = taps.reshape(t, nblk, gpb, SSM_CH, SSM_CH).transpose(1, 0, 3, 2, 4)
    ktab = ktab.reshape(nblk, t, SSM_CH, gpb * SSM_CH)

    rev = jnp.arange(t - 1, -1, -1)
    sr = pr[rev][..., None] * bbr - pi[rev][..., None] * bbi
    si = pr[rev][..., None] * bbi + pi[rev][..., None] * bbr
    sb = jnp.stack([sr, si], axis=0).reshape(2, t, nblk, gpb, SSM_STATE, SSM_CH)
    bsrc = sb.transpose(2, 1, 0, 3, 5, 4).reshape(nblk, t, 2, gpb * SSM_CH, SSM_STATE)
    bsrc = jnp.concatenate([bsrc, bsrc], axis=-1)

    er = c_re[None] * pr[1:, :, None, :] - c_im[None] * pi[1:, :, None, :]
    ei = c_re[None] * pi[1:, :, None, :] + c_im[None] * pr[1:, :, None, :]
    eb = jnp.stack([er, -ei], axis=0).reshape(2, t, nblk, gpb, SSM_CH, SSM_STATE)
    csrc = eb.transpose(2, 1, 0, 5, 3, 4).reshape(nblk, t, 2, SSM_STATE, gpb * SSM_CH)

    a_chunk = jnp.stack([pr[t], pi[t]], axis=0).reshape(2, nblk, 1, gpb * SSM_STATE)
    a_chunk = a_chunk.transpose(1, 0, 2, 3).reshape(nblk, 2, gpb * SSM_STATE)
    dvec = jnp.tile(d_skip.astype(f32).reshape(nblk, 1, gpb * SSM_CH), (1, 1, t))
    return ktab.astype(bf16), bsrc.astype(bf16), csrc.astype(bf16), a_chunk, dvec


def _s5_kernel(u_ref, ktab_ref, bsrc_ref, csrc_ref, a_ref, d_ref, y_ref,
               toep_ref, bpow_ref, cpow_ref, s_ref, h_ref, yt_ref, *, n_batch, n_chunk):
    t = S5_CHUNK
    gpb = S5_GROUPS_PER_BLOCK
    w = gpb * SSM_CH
    ns = gpb * SSM_STATE
    zero = jnp.zeros((), bf16)

    def same_group(shape, row_size, col_size):
        r = lax.broadcasted_iota(jnp.int32, shape, 0) // row_size
        c = lax.broadcasted_iota(jnp.int32, shape, 1) // col_size
        return r == c

    tap_mask = same_group((w, w), SSM_CH, SSM_CH)
    taps = [jnp.where(tap_mask, jnp.tile(ktab_ref[0, tau], (gpb, 1)), zero) for tau in range(t)]
    for tt in range(t):
        for ss in range(tt + 1):
            toep_ref[ss * w:(ss + 1) * w, tt * w:(tt + 1) * w] = taps[tt - ss]
        if tt % 2 == 0:
            toep_ref[(tt + 1) * w:(tt + 2) * w, tt * w:(tt + 1) * w] = jnp.zeros((w, w), bf16)
    b_mask = same_group((w, ns), SSM_CH, SSM_STATE)
    c_mask = same_group((ns, w), SSM_STATE, SSM_CH)
    for ss in range(t):
        for z in range(2):
            bpow_ref[ss * w:(ss + 1) * w, z * ns:(z + 1) * ns] = jnp.where(
                b_mask, jnp.tile(bsrc_ref[0, ss, z], (1, ns // w)), zero)
            cpow_ref[z * ns:(z + 1) * ns, ss * w:(ss + 1) * w] = jnp.where(
                c_mask, jnp.tile(csrc_ref[0, ss, z], (gpb, 1)), zero)

    rows = n_batch * n_chunk
    uf3 = pltpu.einshape("rsl->srl", u_ref[...].reshape(rows, t, w))
    uf = [uf3[s] for s in range(t)]
    u = jnp.concatenate([p.astype(bf16) for p in uf], axis=1)
    s_ref[...] = jnp.dot(u, bpow_ref[...], preferred_element_type=f32)

    ar = a_ref[0, 0:1, :]
    ai = a_ref[0, 1:2, :]

    def step(j, carry):
        new = []
        for b in range(n_batch):
            hr, hi = carry[2 * b], carry[2 * b + 1]
            row = b * n_chunk + j
            h_ref[pl.ds(row, 1), 0:ns] = hr
            h_ref[pl.ds(row, 1), ns:2 * ns] = hi
            sr = s_ref[pl.ds(row, 1), 0:ns]
            si = s_ref[pl.ds(row, 1), ns:2 * ns]
            new.append(ar * hr - ai * hi + sr)
            new.append(ar * hi + ai * hr + si)
        return tuple(new)

    zero = jnp.zeros((1, ns), f32)
    lax.fori_loop(0, n_chunk, step, (zero,) * (2 * n_batch))

    hprev = h_ref[...].astype(bf16)
    for tp in range(t // 2):
        c0, c1 = 2 * tp * w, (2 * tp + 2) * w
        y = jnp.dot(u[:, :c1], toep_ref[0:c1, c0:c1], preferred_element_type=f32)
        y = y + jnp.dot(hprev, cpow_ref[:, c0:c1], preferred_element_type=f32)
        for k, tt in enumerate((2 * tp, 2 * tp + 1)):
            yk = y[:, k * w:(k + 1) * w] + d_ref[0, :, tt * w:(tt + 1) * w] * uf[tt]
            yt_ref[tt] = jax.nn.gelu(yk, approximate=True)
    y_ref[...] = pltpu.einshape("srl->rsl", yt_ref[...]).reshape(rows * t, w)


def _s5_mixer(proj, tables, n_batch, n_chunk):
    ktab, bsrc, csrc, a_chunk, dvec = tables
    nblk = ktab.shape[0]
    w = S5_GROUPS_PER_BLOCK * SSM_CH
    cols = S5_CHUNK * w
    rows = n_batch * n_chunk
    n_tok = rows * S5_CHUNK
    ns = S5_GROUPS_PER_BLOCK * SSM_STATE
    kern = functools.partial(_s5_kernel, n_batch=n_batch, n_chunk=n_chunk)
    return pl.pallas_call(
        kern,
        grid=(nblk,),
        in_specs=[pl.BlockSpec((n_tok, w), lambda g: (0, g)),
                  pl.BlockSpec((1,) + ktab.shape[1:], lambda g: (g, 0, 0, 0)),
                  pl.BlockSpec((1,) + bsrc.shape[1:], lambda g: (g, 0, 0, 0, 0)),
                  pl.BlockSpec((1,) + csrc.shape[1:], lambda g: (g, 0, 0, 0, 0)),
                  pl.BlockSpec((1, 2, ns), lambda g: (g, 0, 0)),
                  pl.BlockSpec((1, 1, cols), lambda g: (g, 0, 0))],
        out_specs=pl.BlockSpec((n_tok, w), lambda g: (0, g)),
        out_shape=jax.ShapeDtypeStruct((n_tok, D_SSM), f32),
        scratch_shapes=[pltpu.VMEM((cols, cols), bf16),
                        pltpu.VMEM((cols, 2 * ns), bf16),
                        pltpu.VMEM((2 * ns, cols), bf16),
                        pltpu.VMEM((rows, 2 * ns), f32),
                        pltpu.VMEM((rows, 2 * ns), f32),
                        pltpu.VMEM((S5_CHUNK, rows, w), f32)],
        compiler_params=_params("parallel"),
        name="s5_mixer",
    )(proj, ktab, bsrc, csrc, a_chunk, dvec)


def _attn_bias_table():
    blk = ATTN_BLOCK
    slopes = 2.0 ** (-8.0 * jnp.arange(1, N_HEADS + 1, dtype=f32) / N_HEADS)
    delta = np.arange(blk)[:, None] - (np.arange(2 * blk)[None, :] - blk)
    tabs = []
    for window, dil in PATTERNS:
        assert window // dil == blk
        valid = (delta >= 0) & (delta <= window // dil)
        dist = jnp.asarray(delta * dil, dtype=f32)
        bias = jnp.where(valid[None], -slopes[:, None, None] * dist[None], MASK_VALUE)
        tabs.append(bias.reshape(N_HEADS // 2, 2 * blk, 2 * blk))
    return jnp.stack(tabs, axis=1)


def _attn_kernel(q_ref, k_ref, v_ref, bias_ref, o_ref, *scr):
    blk = ATTN_BLOCK
    seq = q_ref.shape[0]
    first_head = lax.broadcasted_iota(jnp.int32, (blk, 2 * HEAD_DIM), 1) < HEAD_DIM
    dims = (((1,), (1,)), ((), ()))
    for pi, (_, dil) in enumerate(PATTERNS):
        o_scr, l_scr = scr[2 * pi], scr[2 * pi + 1]
        sub = seq // dil
        for r in range(dil):
            rows = (lambda st, n: pl.ds(st, n)) if dil == 1 else (lambda st, n: pl.ds(st, n, stride=dil))
            qd = (q_ref[rows(r, sub), :] * HEAD_DIM ** -0.5).astype(bf16)
            kd = k_ref[rows(r, sub), :].astype(bf16)
            vd = v_ref[rows(r, sub), :].astype(bf16)
            for i in range(sub // blk):
                qb = qd[i * blk:(i + 1) * blk]
                zero = jnp.zeros_like(qb)
                q2 = jnp.concatenate([jnp.where(first_head, qb, zero), jnp.where(first_head, zero, qb)], axis=0)
                k0 = max(i - 1, 0) * blk
                nk = (i + 1) * blk - k0
                s = lax.dot_general(q2, kd[k0:k0 + nk], dims, preferred_element_type=f32)
                s = s + bias_ref[0, pi, :, 2 * blk - nk:]
                m = jnp.max(s, axis=-1, keepdims=True)
                p = jnp.exp(s - m)
                l = jnp.sum(p, axis=-1, keepdims=True)
                o = jnp.dot(p.astype(bf16), vd[k0:k0 + nk], preferred_element_type=f32) / l
                lse = m + jnp.log(l)
                dst = rows(r + dil * blk * i, blk)
                o_scr[dst, :] = jnp.where(first_head, o[:blk], o[blk:])
                l_scr[dst, :] = jnp.where(first_head, lse[:blk], lse[blk:])
    l1, l2, l3 = scr[1][...], scr[3][...], scr[5][...]
    m = jnp.maximum(jnp.maximum(l1, l2), l3)
    e1, e2, e3 = jnp.exp(l1 - m), jnp.exp(l2 - m), jnp.exp(l3 - m)
    o_ref[...] = (e1 * scr[0][...] + e2 * scr[2][...] + e3 * scr[4][...]) / (e1 + e2 + e3)


def _dilated_attention(qkv, first, bsz, seq):
    pairs = N_HEADS // 2
    width = 2 * HEAD_DIM
    bias = _attn_bias_table()
    col = lambda off: pl.BlockSpec((seq, width), lambda hp, b: (b, first + off + hp))
    return pl.pallas_call(
        _attn_kernel,
        grid=(pairs, bsz),
        in_specs=[col(0), col(pairs), col(2 * pairs),
                  pl.BlockSpec((1,) + bias.shape[1:], lambda hp, b: (hp, 0, 0, 0))],
        out_specs=pl.BlockSpec((seq, width), lambda hp, b: (b, hp)),
        out_shape=jax.ShapeDtypeStruct((bsz * seq, D_ATTN), f32),
        scratch_shapes=[pltpu.VMEM((seq, width), f32)] * (2 * len(PATTERNS)),
        compiler_params=_params("parallel", "parallel"),
        name="dilated_attention",
    )(qkv, qkv, qkv, bias)


def _layer_norm_rows(x, g, b):
    mu = jnp.mean(x, axis=-1, keepdims=True)
    xc = x - mu
    var = jnp.mean(xc * xc, axis=-1, keepdims=True)
    return xc * lax.rsqrt(var + NORM_EPS) * g + b


def _rms_rows(x, g):
    return x * lax.rsqrt(jnp.mean(x * x, axis=-1, keepdims=True) + NORM_EPS) * g


def _mix_out_kernel(y_ref, ya_ref, x_ref, wglu_ref, w_ref, gs_ref, ga_ref, lg_ref, lb_ref, h_ref, hp_ref):
    z = jnp.dot(y_ref[...].astype(bf16), wglu_ref[...], preferred_element_type=f32)
    y_ssm = z[:, :D_SSM] * jax.nn.sigmoid(z[:, D_SSM:])
    ns = _rms_rows(y_ssm, gs_ref[...]).astype(bf16)
    na = _rms_rows(ya_ref[...], ga_ref[...]).astype(bf16)
    proj = jnp.dot(ns, w_ref[0:D_SSM, :], preferred_element_type=f32)
    proj = proj + jnp.dot(na, w_ref[D_SSM:, :], preferred_element_type=f32)
    h = _layer_norm_rows(DEEPNORM_ALPHA * x_ref[...] + proj, lg_ref[...], lb_ref[...])
    h_ref[...] = h
    half = D_MODEL // 2
    word = pltpu.pack_elementwise([h[:, :half], h[:, half:]], packed_dtype=bf16)
    tm = word.shape[0]
    chunks = jnp.stack([word[:, s * 128:(s + 1) * 128] for s in range(PACK_ROWS)], axis=0)
    hp_ref[...] = pltpu.einshape("srl->rsl", chunks).reshape(tm * PACK_ROWS, 128)


def _mix_out(y, y_attn, x, w_glu, w_out, g_ssm, g_attn, ln_g, ln_b, tm=256):
    n = x.shape[0]
    row = lambda c: pl.BlockSpec((tm, c), lambda i: (i, 0))
    full = lambda a: pl.BlockSpec(a.shape, lambda i: (0,) * a.ndim)
    return pl.pallas_call(
        _mix_out_kernel,
        grid=(n // tm,),
        in_specs=[row(D_SSM), row(D_ATTN), row(D_MODEL), full(w_glu), full(w_out), full(g_ssm), full(g_attn),
                  full(ln_g), full(ln_b)],
        out_specs=[row(D_MODEL), pl.BlockSpec((tm * PACK_ROWS, 128), lambda i: (i, 0))],
        out_shape=[jax.ShapeDtypeStruct((n, D_MODEL), f32),
                   jax.ShapeDtypeStruct((n * PACK_ROWS, 128), jnp.int32)],
        compiler_params=_params("parallel"),
        name="mix_out",
    )(y, y_attn, x, w_glu, w_out, g_ssm, g_attn, ln_g, ln_b)


def _router_kernel(h_ref, wrt_ref, bias_ref, tri_ref, trie_ref, e8_ref, pos8_ref, wtok_ref, cnt_ref):
    gsz = N_EXPERTS // N_EXPERT_GROUPS
    tm = h_ref.shape[0]
    ninf = -jnp.inf

    @pl.when(pl.program_id(0) == 0)
    def _():
        cnt_ref[...] = jnp.zeros_like(cnt_ref)

    logits = lax.dot_general(wrt_ref[...], h_ref[...], (((1,), (1,)), ((), ())),
                             preferred_element_type=f32, precision=lax.Precision.HIGHEST)
    scores = jax.nn.sigmoid(logits)
    sel = scores + bias_ref[...]
    io = lax.broadcasted_iota(jnp.int32, (gsz, tm), 0)

    blks, gs_rows = [], []
    for g in range(N_EXPERT_GROUPS):
        blk = sel[g * gsz:(g + 1) * gsz, :]
        m1 = jnp.max(blk, axis=0, keepdims=True)
        first = jnp.min(jnp.where(blk == m1, io, gsz), axis=0, keepdims=True)
        m2 = jnp.max(jnp.where(io == first, ninf, blk), axis=0, keepdims=True)
        blks.append(blk)
        gs_rows.append(m1 + m2)
    gs = jnp.concatenate(gs_rows, axis=0)

    iog = lax.broadcasted_iota(jnp.int32, (N_EXPERT_GROUPS, tm), 0)
    beaten = jnp.zeros((N_EXPERT_GROUPS, tm), f32)
    for gp in range(N_EXPERT_GROUPS):
        row = gs_rows[gp]
        tie = jnp.where(iog > gp, 1.0, 0.0)
        beaten = beaten + jnp.where(row > gs, 1.0, jnp.where(row == gs, tie, 0.0))
    keep = beaten < TOPK_GROUPS
    masked = [jnp.where(keep[g:g + 1, :], blks[g], ninf) for g in range(N_EXPERT_GROUPS)]

    ranks = [jnp.zeros((gsz, tm), f32) for _ in range(N_EXPERT_GROUPS)]
    tie_in = [jnp.where(io > j, 1.0, 0.0) for j in range(gsz)]
    for gp in range(N_EXPERT_GROUPS):
        for j in range(gsz):
            row = masked[gp][j:j + 1, :]
            for g in range(N_EXPERT_GROUPS):
                if gp < g:
                    inc = jnp.where(row >= masked[g], 1.0, 0.0)
                elif gp > g:
                    inc = jnp.where(row > masked[g], 1.0, 0.0)
                else:
                    inc = jnp.where(row > masked[g], 1.0, jnp.where(row == masked[g], tie_in[j], 0.0))
                ranks[g] = ranks[g] + inc
    selb = jnp.concatenate([jnp.where(r < TOP_K, 1.0, 0.0) for r in ranks], axis=0)
    wsel = selb * scores
    wn = wsel / jnp.sum(wsel, axis=0, keepdims=True) * ROUTED_SCALE

    maskb = selb.astype(bf16)
    pos = jnp.dot(maskb, tri_ref[...], preferred_element_type=f32) + cnt_ref[:, 0:1]
    cnt_ref[...] = cnt_ref[...] + jnp.sum(selb, axis=1, keepdims=True)
    slot = jnp.dot(trie_ref[...], maskb, preferred_element_type=f32)
    ioe = lax.broadcasted_iota(jnp.int32, (N_EXPERTS, tm), 0).astype(f32)
    e_rows, p_rows = [], []
    for k in range(TOP_K):
        hit = jnp.where(slot == k, selb, 0.0)
        e_rows.append(jnp.sum(hit * ioe, axis=0, keepdims=True))
        p_rows.append(jnp.sum(hit * pos, axis=0, keepdims=True))
    e8_ref[...] = jnp.concatenate(e_rows, axis=0).astype(jnp.int32)
    pos8_ref[...] = jnp.concatenate(p_rows, axis=0).astype(jnp.int32)
    wtok_ref[...] = jnp.concatenate([wn.T, jnp.zeros((tm, 128 - N_EXPERTS), f32)], axis=1)


def _router(h, w_router, router_bias, tm=512):
    n = h.shape[0]
    wrt = w_router.astype(f32).T
    bias = router_bias.astype(f32).reshape(N_EXPERTS, 1)
    tri = (jnp.arange(tm)[:, None] < jnp.arange(tm)[None, :]).astype(bf16)
    trie = (jnp.arange(N_EXPERTS)[None, :] < jnp.arange(N_EXPERTS)[:, None]).astype(bf16)
    full = lambda a: pl.BlockSpec(a.shape, lambda i: (0,) * a.ndim)
    tok = lambda: pl.BlockSpec((TOP_K, tm), lambda i: (0, i))
    return pl.pallas_call(
        _router_kernel,
        grid=(n // tm,),
        in_specs=[pl.BlockSpec((tm, D_MODEL), lambda i: (i, 0)), full(wrt), full(bias), full(tri), full(trie)],
        out_specs=[tok(), tok(), pl.BlockSpec((tm, 128), lambda i: (i, 0)),
                   pl.BlockSpec((N_EXPERTS, 128), lambda i: (0, 0))],
        out_shape=[jax.ShapeDtypeStruct((TOP_K, n), jnp.int32), jax.ShapeDtypeStruct((TOP_K, n), jnp.int32),
                   jax.ShapeDtypeStruct((n, 128), f32), jax.ShapeDtypeStruct((N_EXPERTS, 128), f32)],
        compiler_params=_params("arbitrary"),
        name="router",
    )(h, wrt, bias, tri, trie)


def _invert_kernel(dest_ref, init_ref, code_ref):
    del init_ref

    def body(p, c):
        code_ref[dest_ref[p]] = p
        return c
    lax.fori_loop(0, dest_ref.shape[0], body, 0, unroll=16)


def _invert(dest, n_rows):
    smem = lambda: pl.BlockSpec(memory_space=pltpu.SMEM)
    return pl.pallas_call(
        _invert_kernel,
        in_specs=[smem(), smem()],
        out_specs=smem(),
        out_shape=jax.ShapeDtypeStruct((n_rows,), jnp.int32),
        input_output_aliases={1: 0},
        name="invert_dispatch",
    )(dest, jnp.full((n_rows,), PAD_CODE, jnp.int32))


def _expert_changed(be_ref, i):
    return jnp.logical_or(i == 0, be_ref[i] != be_ref[jnp.maximum(i - 1, 0)])


SCATTER_UNROLL = 8


def _moe_up_kernel(be_ref, code_ref, nb_ref, hp_ref, wtok_ref, wg_ref, wu_ref, mid_ref,
                   xg_scr, xa_scr, xb_scr, wra_scr, wrb_scr, wgu_scr):
    i = pl.program_id(0)
    rows = MOE_ROWS

    def gather_block(blk, x_dst, wrow_dst):
        base = blk * rows
        for r in range(rows):
            tok = (code_ref[base + r] >> 3) & (N_TOKENS - 1)
            xg_scr[r * PACK_ROWS:(r + 1) * PACK_ROWS, :] = (
                hp_ref[pl.ds(pl.multiple_of(tok * PACK_ROWS, PACK_ROWS), PACK_ROWS), :])
            wrow_dst[r:r + 1, :] = wtok_ref[pl.ds(tok, 1), :]
        half = D_MODEL // 2
        chunks = pltpu.einshape("rsl->srl", xg_scr[...].reshape(rows, PACK_ROWS, 128))
        for s in range(PACK_ROWS):
            wds = chunks[s]
            lo = pltpu.unpack_elementwise(wds, index=0, packed_dtype=bf16, unpacked_dtype=f32)
            hi = pltpu.unpack_elementwise(wds, index=1, packed_dtype=bf16, unpacked_dtype=f32)
            x_dst[:, s * 128:(s + 1) * 128] = lo.astype(bf16)
            x_dst[:, half + s * 128:half + (s + 1) * 128] = hi.astype(bf16)

    def expert_block(x_cur, wrow_cur, x_nxt, wrow_nxt):
        gather_block(jnp.minimum(i + 1, MOE_BLOCKS - 1), x_nxt, wrow_nxt)
        gu = jnp.dot(x_cur[...], wgu_scr[...], preferred_element_type=f32)
        g, u = gu[:, :D_EXPERT], gu[:, D_EXPERT:]
        lane = lax.broadcasted_iota(jnp.int32, (rows, 128), 1)
        w = jnp.sum(jnp.where(lane == be_ref[i], wrow_cur[...], 0.0), axis=1, keepdims=True)
        real = i * rows + lax.broadcasted_iota(jnp.int32, (rows, 1), 0) < nb_ref[1 + be_ref[i]]
        mid_ref[...] = (g * jax.nn.sigmoid(g) * u * jnp.where(real, w, 0.0)).astype(bf16)

    @pl.when(i == 0)
    def _():
        gather_block(0, xa_scr, wra_scr)

    @pl.when(i < nb_ref[0])
    def _():
        @pl.when(_expert_changed(be_ref, i))
        def _():
            wgu_scr[:, :D_EXPERT] = wg_ref[0].astype(bf16)
            wgu_scr[:, D_EXPERT:] = wu_ref[0].astype(bf16)

        @pl.when(i % 2 == 0)
        def _():
            expert_block(xa_scr, wra_scr, xb_scr, wrb_scr)

        @pl.when(i % 2 == 1)
        def _():
            expert_block(xb_scr, wrb_scr, xa_scr, wra_scr)

    @pl.when(i >= nb_ref[0])
    def _():
        mid_ref[...] = jnp.zeros_like(mid_ref)


def _moe_up(block_e, code, nb, hp, wtok, w_gate, w_up):
    wmap = lambda i, be, cd, nb: (be[i], 0, 0)
    grid_spec = pltpu.PrefetchScalarGridSpec(
        num_scalar_prefetch=3,
        grid=(MOE_BLOCKS,),
        in_specs=[pl.BlockSpec(memory_space=pltpu.VMEM),
                  pl.BlockSpec(memory_space=pltpu.VMEM),
                  pl.BlockSpec((1, D_MODEL, D_EXPERT), wmap),
                  pl.BlockSpec((1, D_MODEL, D_EXPERT), wmap)],
        out_specs=pl.BlockSpec((MOE_ROWS, D_EXPERT), lambda i, be, cd, nb: (i, 0)),
        scratch_shapes=[pltpu.VMEM((PACK_ROWS * MOE_ROWS, 128), jnp.int32),
                        pltpu.VMEM((MOE_ROWS, D_MODEL), bf16), pltpu.VMEM((MOE_ROWS, D_MODEL), bf16),
                        pltpu.VMEM((MOE_ROWS, 128), f32), pltpu.VMEM((MOE_ROWS, 128), f32),
                        pltpu.VMEM((D_MODEL, 2 * D_EXPERT), bf16)])
    return pl.pallas_call(
        _moe_up_kernel,
        grid_spec=grid_spec,
        out_shape=jax.ShapeDtypeStruct((MOE_BLOCKS * MOE_ROWS, D_EXPERT), bf16),
        compiler_params=pltpu.CompilerParams(dimension_semantics=("arbitrary",),
                                             vmem_limit_bytes=MOE_UP_VMEM_LIMIT),
        name="moe_up",
    )(block_e, code, nb, hp, wtok, w_gate, w_up)


def _moe_down_kernel(be_ref, code_ref, nb_ref, mida_ref, midb_ref, wda_ref, wdb_ref, acc_ref,
                     ya_scr, yb_scr, wd_scr):
    j = pl.program_id(1)
    rows = MOE_ROWS
    nb = nb_ref[0]
    blk_a, blk_b = 2 * j, 2 * j + 1

    @pl.when(j == 0)
    def _():
        acc_ref[...] = jnp.zeros_like(acc_ref)

    def scatter_block(blk, ybuf):
        for g in range(rows // SCATTER_UNROLL):
            sums, addrs = [], []
            for k in range(SCATTER_UNROLL):
                r = g * SCATTER_UNROLL + k
                a = pl.multiple_of(code_ref[blk * rows + r] & -PACK_ROWS, PACK_ROWS)
                v = ybuf[r * PACK_ROWS:(r + 1) * PACK_ROWS, :]
                sums.append(acc_ref[pl.ds(a, PACK_ROWS), :] + v)
                addrs.append(a)
            for k in range(SCATTER_UNROLL):
                acc_ref[pl.ds(addrs[k], PACK_ROWS), :] = sums[k]

    def down_block(mid_ref, ybuf):
        y = jnp.dot(mid_ref[...], wd_scr[...], preferred_element_type=f32)
        y3 = jnp.stack([y[:, c * 128:(c + 1) * 128] for c in range(PACK_ROWS)], axis=0)
        ybuf[...] = pltpu.einshape("crl->rcl", y3).reshape(rows * PACK_ROWS, 128)

    def refresh_weights(blk, wd_ref):
        @pl.when(jnp.logical_and(blk < nb, _expert_changed(be_ref, jnp.minimum(blk, MOE_BLOCKS - 1))))
        def _():
            wd_scr[...] = wd_ref[0].astype(bf16)

    refresh_weights(blk_a, wda_ref)

    @pl.when(jnp.logical_and(j == 0, blk_a < nb))
    def _():
        down_block(mida_ref, ya_scr)

    @pl.when(jnp.logical_and(j > 0, blk_a < nb))
    def _():
        down_block(mida_ref, ya_scr)
        scatter_block(blk_a - 1, yb_scr)

    @pl.when(jnp.logical_and(j > 0, blk_a == nb))
    def _():
        scatter_block(blk_a - 1, yb_scr)

    refresh_weights(blk_b, wdb_ref)

    @pl.when(blk_b < nb)
    def _():
        down_block(midb_ref, yb_scr)
        scatter_block(blk_a, ya_scr)

    @pl.when(blk_b == nb)
    def _():
        scatter_block(blk_a, ya_scr)


def _moe_down(block_e, code, nb, mid, w_down):
    half = D_MODEL // 2
    acc_rows = (N_TOKENS + 1) * PACK_ROWS
    last = MOE_BLOCKS - 1
    assert MOE_BLOCKS % 2 == 0
    blk = lambda off: (lambda p, j, be, cd, nb: (jnp.minimum(2 * j + off, last), 0))
    wd = lambda off: (lambda p, j, be, cd, nb: (be[jnp.minimum(2 * j + off, last)], 0, p))
    grid_spec = pltpu.PrefetchScalarGridSpec(
        num_scalar_prefetch=3,
        grid=(2, MOE_BLOCKS // 2 + 1),
        in_specs=[pl.BlockSpec((MOE_ROWS, D_EXPERT), blk(0)), pl.BlockSpec((MOE_ROWS, D_EXPERT), blk(1)),
                  pl.BlockSpec((1, D_EXPERT, half), wd(0)), pl.BlockSpec((1, D_EXPERT, half), wd(1))],
        out_specs=pl.BlockSpec((None, acc_rows, 128), lambda p, j, be, cd, nb: (p, 0, 0),
                               pipeline_mode=pl.Buffered(1)),
        scratch_shapes=[pltpu.VMEM((PACK_ROWS * MOE_ROWS, 128), f32),
                        pltpu.VMEM((PACK_ROWS * MOE_ROWS, 128), f32),
                        pltpu.VMEM((D_EXPERT, half), bf16)])
    return pl.pallas_call(
        _moe_down_kernel,
        grid_spec=grid_spec,
        out_shape=jax.ShapeDtypeStruct((2, acc_rows, 128), f32),
        compiler_params=_params("arbitrary", "arbitrary"),
        name="moe_down",
    )(block_e, code, nb, mid, mid, w_down, w_down)


def _final_kernel(h_ref, r0_ref, r1_ref, wgu_ref, wd_ref, lg_ref, lb_ref, o_ref):
    tm = h_ref.shape[0]
    gu = jnp.dot(h_ref[...].astype(bf16), wgu_ref[...], preferred_element_type=f32)
    g, u = gu[:, :D_EXPERT], gu[:, D_EXPERT:]
    mid = (g * jax.nn.sigmoid(g) * u).astype(bf16)
    shared = jnp.dot(mid, wd_ref[...], preferred_element_type=f32)
    halves = [pltpu.einshape("rcl->crl", r[...].reshape(tm, PACK_ROWS, 128)) for r in (r0_ref, r1_ref)]
    routed = jnp.concatenate([hv[c] for hv in halves for c in range(PACK_ROWS)], axis=1)
    o_ref[...] = _layer_norm_rows(DEEPNORM_ALPHA * h_ref[...] + routed + shared,
                                  lg_ref[...], lb_ref[...])


def _final(h, racc, wgu, wd, ln_g, ln_b, tm=256):
    n = h.shape[0]
    row = lambda: pl.BlockSpec((tm, D_MODEL), lambda i: (i, 0))
    full = lambda a: pl.BlockSpec(a.shape, lambda i: (0,) * a.ndim)
    acc = lambda p: pl.BlockSpec((None, tm * PACK_ROWS, 128), lambda i: (p, i, 0))
    return pl.pallas_call(
        _final_kernel,
        grid=(n // tm,),
        in_specs=[row(), acc(0), acc(1), full(wgu), full(wd), full(ln_g), full(ln_b)],
        out_specs=row(),
        out_shape=jax.ShapeDtypeStruct((n, D_MODEL), f32),
        compiler_params=_params("parallel"),
        name="shared_final",
    )(h, racc, racc, wgu, wd, ln_g, ln_b)


def _dispatch_plan(e8, pos8, cnt):
    counts = cnt[:, 0].astype(jnp.int32)
    padded = (counts + MOE_ROWS - 1) // MOE_ROWS * MOE_ROWS
    pad_end = jnp.cumsum(padded).astype(jnp.int32)
    pad_start = pad_end - padded
    seg_end = pad_start + counts
    ids = jnp.arange(N_EXPERTS, dtype=jnp.int32)
    start8 = jnp.sum(jnp.where(e8[..., None] == ids, pad_start, 0), axis=-1)
    dest = (start8 + pos8).T.reshape(-1)
    block_start = jnp.arange(MOE_BLOCKS, dtype=jnp.int32) * MOE_ROWS
    block_e = jnp.minimum(jnp.sum((pad_end[None, :] <= block_start[:, None]).astype(jnp.int32), axis=1),
                          N_EXPERTS - 1)
    nb = jnp.concatenate([pad_end[-1:] // MOE_ROWS, seg_end]).astype(jnp.int32)
    return dest, block_e, nb


def kernel(x, w_in, ssm_log_dt, ssm_a_re, ssm_a_im, ssm_b_re, ssm_b_im, ssm_c_re, ssm_c_im, ssm_d,
           w_glu, g_ssm_out, g_attn_out, w_out, ln1_g, ln1_b, w_router, router_bias, w_gate, w_up,
           w_down, ws_gate, ws_up, ws_down, ln2_g, ln2_b):
    bsz, seq, d = x.shape
    n_tok = bsz * seq
    h = x.reshape(n_tok, d)
    for layer in range(DEPTH):
        proj = _matmul(h, w_in[layer].astype(bf16), f32)

        tables = _s5_tables(ssm_log_dt[layer], ssm_a_re[layer], ssm_a_im[layer], ssm_b_re[layer],
                            ssm_b_im[layer], ssm_c_re[layer], ssm_c_im[layer], ssm_d[layer])
        y = _s5_mixer(proj, tables, bsz, seq // S5_CHUNK)

        y_attn = _dilated_attention(proj, D_SSM // (2 * HEAD_DIM), bsz, seq)

        row2 = lambda a: a.astype(f32).reshape(1, -1)
        h, hp = _mix_out(y, y_attn, h, w_glu[layer].astype(bf16), w_out[layer].astype(bf16),
                         row2(g_ssm_out[layer]), row2(g_attn_out[layer]), row2(ln1_g[layer]), row2(ln1_b[layer]))

        assert n_tok == N_TOKENS
        e8, pos8, wtok, cnt = _router(h, w_router[layer], router_bias[layer])
        dest, block_e, nb = _dispatch_plan(e8, pos8, cnt)
        code = _invert(dest, MOE_BLOCKS * MOE_ROWS)
        mid = _moe_up(block_e, code, nb, hp, wtok, w_gate[layer], w_up[layer])
        racc = _moe_down(block_e, code, nb, mid, w_down[layer])
        wgu = jnp.concatenate([ws_gate[layer], ws_up[layer]], axis=1).astype(bf16)
        h = _final(h, racc, wgu, ws_down[layer].astype(bf16), row2(ln2_g[layer]), row2(ln2_b[layer]))
    return h.reshape(bsz, seq, d)
```

```python
import functools

import jax
import jax.numpy as jnp
import numpy as np
from jax import lax
from jax.experimental import pallas as pl
from jax.experimental.pallas import tpu as pltpu

D_MODEL = 2048
D_SSM = 1024
D_ATTN = 1024
SSM_CH = 16
SSM_GROUPS = 64
SSM_STATE = 64
HEAD_DIM = 64
N_HEADS = 16
PATTERNS = ((128, 1), (512, 4), (2048, 16))
ATTN_BLOCK = 128
N_EXPERTS = 64
TOP_K = 8
N_EXPERT_GROUPS = 8
TOPK_GROUPS = 4
D_EXPERT = 512
ROUTED_SCALE = 2.5
NORM_EPS = 1e-5
DEPTH = 1
DEEPNORM_ALPHA = (2.0 * DEPTH) ** 0.25

S5_CHUNK = 16
S5_GROUPS_PER_BLOCK = 8
MOE_ROWS = 256
MASK_VALUE = -1e30
PACK_ROWS = D_MODEL // 2 // 128
N_TOKENS = 8192
PAD_CODE = N_TOKENS * TOP_K
NB_SLOT = 1 + N_EXPERTS
NB_NEXT = 1 + 2 * N_EXPERTS
MOE_BLOCKS = -(-(N_TOKENS * TOP_K + N_EXPERTS * (MOE_ROWS - 1)) // MOE_ROWS)
VMEM_LIMIT = 56 * 1024 * 1024
MOE_UP_VMEM_LIMIT = 62 * 1024 * 1024

bf16 = jnp.bfloat16
f32 = jnp.float32


def _params(*sem):
    return pltpu.CompilerParams(dimension_semantics=sem, vmem_limit_bytes=VMEM_LIMIT)


MATMUL_COLS = 1024


def _matmul_kernel(a_ref, b_ref, o_ref):
    a = a_ref[...].astype(bf16)
    for j in range(o_ref.shape[1] // MATMUL_COLS):
        cols = slice(j * MATMUL_COLS, (j + 1) * MATMUL_COLS)
        o_ref[:, cols] = jnp.dot(a, b_ref[:, cols], preferred_element_type=f32).astype(o_ref.dtype)


def _matmul(a, b, out_dtype, tm=256):
    m, k = a.shape
    _, n = b.shape
    return pl.pallas_call(
        _matmul_kernel,
        grid=(m // tm,),
        in_specs=[pl.BlockSpec((tm, k), lambda i: (i, 0)),
                  pl.BlockSpec((k, n), lambda i: (0, 0), pipeline_mode=pl.Buffered(1))],
        out_specs=pl.BlockSpec((tm, n), lambda i: (i, 0)),
        out_shape=jax.ShapeDtypeStruct((m, n), out_dtype),
        compiler_params=_params("parallel"),
        name="matmul",
    )(a, b)


def _s5_tables(log_dt, a_re, a_im, b_re, b_im, c_re, c_im, d_skip):
    t = S5_CHUNK
    gpb = S5_GROUPS_PER_BLOCK
    nblk = SSM_GROUPS // gpb
    hp = lax.Precision.HIGHEST
    lr = jnp.minimum(a_re.astype(f32), -1e-4)
    li = a_im.astype(f32)
    dt = jnp.exp(log_dt.astype(f32))
    kk = jnp.arange(t + 1, dtype=f32)[:, None, None]
    mag = jnp.exp(kk * (lr * dt))
    pr = mag * jnp.cos(kk * (li * dt))
    pi = mag * jnp.sin(kk * (li * dt))
    xr, xi = pr[1] - 1.0, pi[1]
    den = lr * lr + li * li
    cr = (xr * lr + xi * li) / den
    ci = (xi * lr - xr * li) / den
    bbr = cr[..., None] * b_re - ci[..., None] * b_im
    bbi = cr[..., None] * b_im + ci[..., None] * b_re
    wr = pr[:t, :, :, None] * bbr - pi[:t, :, :, None] * bbi
    wi = pr[:t, :, :, None] * bbi + pi[:t, :, :, None] * bbr
    taps = (jnp.einsum('gop,tgpc->tgco', c_re, wr, precision=hp)
            - jnp.einsum('gop,tgpc->tgco', c_im, wi, precision=hp))
    ktab = taps.reshape(t, nblk, gpb, SSM_CH, SSM_CH).transpose(1, 0, 3, 2, 4)
    ktab = ktab.reshape(nblk, t, SSM_CH, gpb * SSM_CH)

    rev = jnp.arange(t - 1, -1, -1)
    sr = pr[rev][..., None] * bbr - pi[rev][..., None] * bbi
    si = pr[rev][..., None] * bbi + pi[rev][..., None] * bbr
    sb = jnp.stack([sr, si], axis=0).reshape(2, t, nblk, gpb, SSM_STATE, SSM_CH)
    bsrc = sb.transpose(2, 1, 0, 3, 5, 4).reshape(nblk, t, 2, gpb * SSM_CH, SSM_STATE)
    bsrc = jnp.concatenate([bsrc, bsrc], axis=-1)

    er = c_re[None] * pr[1:, :, None, :] - c_im[None] * pi[1:, :, None, :]
    ei = c_re[None] * pi[1:, :, None, :] + c_im[None] * pr[1:, :, None, :]
    eb = jnp.stack([er, -ei], axis=0).reshape(2, t, nblk, gpb, SSM_CH, SSM_STATE)
    csrc = eb.transpose(2, 1, 0, 5, 3, 4).reshape(nblk, t, 2, SSM_STATE, gpb * SSM_CH)

    a_chunk = jnp.stack([pr[t], pi[t]], axis=0).reshape(2, nblk, 1, gpb * SSM_STATE)
    a_chunk = a_chunk.transpose(1, 0, 2, 3).reshape(nblk, 2, gpb * SSM_STATE)
    dvec = jnp.tile(d_skip.astype(f32).reshape(nblk, 1, gpb * SSM_CH), (1, 1, t))
    return ktab.astype(bf16), bsrc.astype(bf16), csrc.astype(bf16), a_chunk, dvec


def _s5_kernel(u_ref, ktab_ref, bsrc_ref, csrc_ref, a_ref, d_ref, y_ref,
               toep_ref, bpow_ref, cpow_ref, s_ref, h_ref, yt_ref, *, n_batch, n_chunk):
    t = S5_CHUNK
    gpb = S5_GROUPS_PER_BLOCK
    w = gpb * SSM_CH
    ns = gpb * SSM_STATE
    zero = jnp.zeros((), bf16)

    def same_group(shape, row_size, col_size):
        r = lax.broadcasted_iota(jnp.int32, shape, 0) // row_size
        c = lax.broadcasted_iota(jnp.int32, shape, 1) // col_size
        return r == c

    tap_mask = same_group((w, w), SSM_CH, SSM_CH)
    taps = [jnp.where(tap_mask, jnp.tile(ktab_ref[0, tau], (gpb, 1)), zero) for tau in range(t)]
    for tt in range(t):
        for ss in range(tt + 1):
            toep_ref[ss * w:(ss + 1) * w, tt * w:(tt + 1) * w] = taps[tt - ss]
        if tt % 2 == 0:
            toep_ref[(tt + 1) * w:(tt + 2) * w, tt * w:(tt + 1) * w] = jnp.zeros((w, w), bf16)
    b_mask = same_group((w, ns), SSM_CH, SSM_STATE)
    c_mask = same_group((ns, w), SSM_STATE, SSM_CH)
    for ss in range(t):
        for z in range(2):
            bpow_ref[ss * w:(ss + 1) * w, z * ns:(z + 1) * ns] = jnp.where(
                b_mask, jnp.tile(bsrc_ref[0, ss, z], (1, ns // w)), zero)
            cpow_ref[z * ns:(z + 1) * ns, ss * w:(ss + 1) * w] = jnp.where(
                c_mask, jnp.tile(csrc_ref[0, ss, z], (gpb, 1)), zero)

    rows = n_batch * n_chunk
    uf3 = pltpu.einshape("rsl->srl", u_ref[...].reshape(rows, t, w))
    uf = [uf3[s] for s in range(t)]
    u = jnp.concatenate([p.astype(bf16) for p in uf], axis=1)
    s_ref[...] = jnp.dot(u, bpow_ref[...], preferred_element_type=f32)

    ar = a_ref[0, 0:1, :]
    ai = a_ref[0, 1:2, :]

    def step(j, carry):
        new = []
        for b in range(n_batch):
            hr, hi = carry[2 * b], carry[2 * b + 1]
            row = b * n_chunk + j
            h_ref[pl.ds(row, 1), 0:ns] = hr
            h_ref[pl.ds(row, 1), ns:2 * ns] = hi
            sr = s_ref[pl.ds(row, 1), 0:ns]
            si = s_ref[pl.ds(row, 1), ns:2 * ns]
            new.append(ar * hr - ai * hi + sr)
            new.append(ar * hi + ai * hr + si)
        return tuple(new)

    zero = jnp.zeros((1, ns), f32)
    lax.fori_loop(0, n_chunk, step, (zero,) * (2 * n_batch))

    hprev = h_ref[...].astype(bf16)
    for tp in range(t // 2):
        c0, c1 = 2 * tp * w, (2 * tp + 2) * w
        y = jnp.dot(u[:, :c1], toep_ref[0:c1, c0:c1], preferred_element_type=f32)
        y = y + jnp.dot(hprev, cpow_ref[:, c0:c1], preferred_element_type=f32)
        for k, tt in enumerate((2 * tp, 2 * tp + 1)):
            yk = y[:, k * w:(k + 1) * w] + d_ref[0, :, tt * w:(tt + 1) * w] * uf[tt]
            yt_ref[tt] = jax.nn.gelu(yk, approximate=True)
    y_ref[...] = pltpu.einshape("srl->rsl", yt_ref[...]).reshape(rows * t, w)


def _s5_mixer(proj, tables, n_batch, n_chunk):
    ktab, bsrc, csrc, a_chunk, dvec = tables
    nblk = ktab.shape[0]
    w = S5_GROUPS_PER_BLOCK * SSM_CH
    cols = S5_CHUNK * w
    rows = n_batch * n_chunk
    n_tok = rows * S5_CHUNK
    ns = S5_GROUPS_PER_BLOCK * SSM_STATE
    kern = functools.partial(_s5_kernel, n_batch=n_batch, n_chunk=n_chunk)
    return pl.pallas_call(
        kern,
        grid=(nblk,),
        in_specs=[pl.BlockSpec((n_tok, w), lambda g: (0, g)),
                  pl.BlockSpec((1,) + ktab.shape[1:], lambda g: (g, 0, 0, 0)),
                  pl.BlockSpec((1,) + bsrc.shape[1:], lambda g: (g, 0, 0, 0, 0)),
                  pl.BlockSpec((1,) + csrc.shape[1:], lambda g: (g, 0, 0, 0, 0)),
                  pl.BlockSpec((1, 2, ns), lambda g: (g, 0, 0)),
                  pl.BlockSpec((1, 1, cols), lambda g: (g, 0, 0))],
        out_specs=pl.BlockSpec((n_tok, w), lambda g: (0, g)),
        out_shape=jax.ShapeDtypeStruct((n_tok, D_SSM), f32),
        scratch_shapes=[pltpu.VMEM((cols, cols), bf16),
                        pltpu.VMEM((cols, 2 * ns), bf16),
                        pltpu.VMEM((2 * ns, cols), bf16),
                        pltpu.VMEM((rows, 2 * ns), f32),
                        pltpu.VMEM((rows, 2 * ns), f32),
                        pltpu.VMEM((S5_CHUNK, rows, w), f32)],
        compiler_params=_params("parallel"),
        name="s5_mixer",
    )(proj, ktab, bsrc, csrc, a_chunk, dvec)


def _attn_bias_table():
    blk = ATTN_BLOCK
    slopes = 2.0 ** (-8.0 * jnp.arange(1, N_HEADS + 1, dtype=f32) / N_HEADS)
    delta = np.arange(blk)[:, None] - (np.arange(2 * blk)[None, :] - blk)
    tabs = []
    for window, dil in PATTERNS:
        assert window // dil == blk
        valid = (delta >= 0) & (delta <= window // dil)
        dist = jnp.asarray(delta * dil, dtype=f32)
        bias = jnp.where(valid[None], -slopes[:, None, None] * dist[None], MASK_VALUE)
        tabs.append(bias.reshape(N_HEADS // 2, 2 * blk, 2 * blk))
    return jnp.stack(tabs, axis=1)


def _attn_kernel(q_ref, k_ref, v_ref, bias_ref, o_ref, *scr):
    blk = ATTN_BLOCK
    seq = q_ref.shape[0]
    first_head = lax.broadcasted_iota(jnp.int32, (blk, 2 * HEAD_DIM), 1) < HEAD_DIM
    dims = (((1,), (1,)), ((), ()))
    for pi, (_, dil) in enumerate(PATTERNS):
        o_scr, l_scr = scr[2 * pi], scr[2 * pi + 1]
        sub = seq // dil
        for r in range(dil):
            rows = (lambda st, n: pl.ds(st, n)) if dil == 1 else (lambda st, n: pl.ds(st, n, stride=dil))
            qd = (q_ref[rows(r, sub), :] * HEAD_DIM ** -0.5).astype(bf16)
            kd = k_ref[rows(r, sub), :].astype(bf16)
            vd = v_ref[rows(r, sub), :].astype(bf16)
            for i in range(sub // blk):
                qb = qd[i * blk:(i + 1) * blk]
                zero = jnp.zeros_like(qb)
                q2 = jnp.concatenate([jnp.where(first_head, qb, zero), jnp.where(first_head, zero, qb)], axis=0)
                k0 = max(i - 1, 0) * blk
                nk = (i + 1) * blk - k0
                s = lax.dot_general(q2, kd[k0:k0 + nk], dims, preferred_element_type=f32)
                s = s + bias_ref[0, pi, :, 2 * blk - nk:]
                m = jnp.max(s, axis=-1, keepdims=True)
                p = jnp.exp(s - m)
                l = jnp.sum(p, axis=-1, keepdims=True)
                o = jnp.dot(p.astype(bf16), vd[k0:k0 + nk], preferred_element_type=f32) / l
                lse = m + jnp.log(l)
                dst = rows(r + dil * blk * i, blk)
                o_scr[dst, :] = jnp.where(first_head, o[:blk], o[blk:])
                l_scr[dst, :] = jnp.where(first_head, lse[:blk], lse[blk:])
    l1, l2, l3 = scr[1][...], scr[3][...], scr[5][...]
    m = jnp.maximum(jnp.maximum(l1, l2), l3)
    e1, e2, e3 = jnp.exp(l1 - m), jnp.exp(l2 - m), jnp.exp(l3 - m)
    o_ref[...] = (e1 * scr[0][...] + e2 * scr[2][...] + e3 * scr[4][...]) / (e1 + e2 + e3)


def _dilated_attention(qkv, first, bsz, seq):
    pairs = N_HEADS // 2
    width = 2 * HEAD_DIM
    bias = _attn_bias_table()
    col = lambda off: pl.BlockSpec((seq, width), lambda hp, b: (b, first + off + hp))
    return pl.pallas_call(
        _attn_kernel,
        grid=(pairs, bsz),
        in_specs=[col(0), col(pairs), col(2 * pairs),
                  pl.BlockSpec((1,) + bias.shape[1:], lambda hp, b: (hp, 0, 0, 0))],
        out_specs=pl.BlockSpec((seq, width), lambda hp, b: (b, hp)),
        out_shape=jax.ShapeDtypeStruct((bsz * seq, D_ATTN), f32),
        scratch_shapes=[pltpu.VMEM((seq, width), f32)] * (2 * len(PATTERNS)),
        compiler_params=_params("parallel", "parallel"),
        name="dilated_attention",
    )(qkv, qkv, qkv, bias)


def _layer_norm_rows(x, g, b):
    mu = jnp.mean(x, axis=-1, keepdims=True)
    xc = x - mu
    var = jnp.mean(xc * xc, axis=-1, keepdims=True)
    return xc * lax.rsqrt(var + NORM_EPS) * g + b


def _rms_rows(x, g):
    return x * lax.rsqrt(jnp.mean(x * x, axis=-1, keepdims=True) + NORM_EPS) * g


def _mix_out_kernel(y_ref, ya_ref, x_ref, wglu_ref, w_ref, gs_ref, ga_ref, lg_ref, lb_ref, h_ref, hp_ref):
    z = jnp.dot(y_ref[...].astype(bf16), wglu_ref[...], preferred_element_type=f32)
    y_ssm = z[:, :D_SSM] * jax.nn.sigmoid(z[:, D_SSM:])
    ns = _rms_rows(y_ssm, gs_ref[...]).astype(bf16)
    na = _rms_rows(ya_ref[...], ga_ref[...]).astype(bf16)
    proj = jnp.dot(ns, w_ref[0:D_SSM, :], preferred_element_type=f32)
    proj = proj + jnp.dot(na, w_ref[D_SSM:, :], preferred_element_type=f32)
    h = _layer_norm_rows(DEEPNORM_ALPHA * x_ref[...] + proj, lg_ref[...], lb_ref[...])
    h_ref[...] = h
    half = D_MODEL // 2
    word = pltpu.pack_elementwise([h[:, :half], h[:, half:]], packed_dtype=bf16)
    tm = word.shape[0]
    chunks = jnp.stack([word[:, s * 128:(s + 1) * 128] for s in range(PACK_ROWS)], axis=0)
    hp_ref[...] = pltpu.einshape("srl->rsl", chunks).reshape(tm * PACK_ROWS, 128)


def _mix_out(y, y_attn, x, w_glu, w_out, g_ssm, g_attn, ln_g, ln_b, tm=256):
    n = x.shape[0]
    row = lambda c: pl.BlockSpec((tm, c), lambda i: (i, 0))
    full = lambda a: pl.BlockSpec(a.shape, lambda i: (0,) * a.ndim)
    return pl.pallas_call(
        _mix_out_kernel,
        grid=(n // tm,),
        in_specs=[row(D_SSM), row(D_ATTN), row(D_MODEL), full(w_glu), full(w_out), full(g_ssm), full(g_attn),
                  full(ln_g), full(ln_b)],
        out_specs=[row(D_MODEL), pl.BlockSpec((tm * PACK_ROWS, 128), lambda i: (i, 0))],
        out_shape=[jax.ShapeDtypeStruct((n, D_MODEL), f32),
                   jax.ShapeDtypeStruct((n * PACK_ROWS, 128), jnp.int32)],
        compiler_params=_params("parallel"),
        name="mix_out",
    )(y, y_attn, x, w_glu, w_out, g_ssm, g_attn, ln_g, ln_b)


def _router_kernel(h_ref, wrt_ref, bias_ref, tri_ref, trie_ref, e8_ref, pos8_ref, wtok_ref, cnt_ref):
    gsz = N_EXPERTS // N_EXPERT_GROUPS
    tm = h_ref.shape[0]
    ninf = -jnp.inf

    @pl.when(pl.program_id(0) == 0)
    def _():
        cnt_ref[...] = jnp.zeros_like(cnt_ref)

    logits = lax.dot_general(wrt_ref[...], h_ref[...], (((1,), (1,)), ((), ())),
                             preferred_element_type=f32, precision=lax.Precision.HIGHEST)
    scores = jax.nn.sigmoid(logits)
    sel = scores + bias_ref[...]
    io = lax.broadcasted_iota(jnp.int32, (gsz, tm), 0)

    blks, gs_rows = [], []
    for g in range(N_EXPERT_GROUPS):
        blk = sel[g * gsz:(g + 1) * gsz, :]
        m1 = jnp.max(blk, axis=0, keepdims=True)
        first = jnp.min(jnp.where(blk == m1, io, gsz), axis=0, keepdims=True)
        m2 = jnp.max(jnp.where(io == first, ninf, blk), axis=0, keepdims=True)
        blks.append(blk)
        gs_rows.append(m1 + m2)
    gs = jnp.concatenate(gs_rows, axis=0)

    iog = lax.broadcasted_iota(jnp.int32, (N_EXPERT_GROUPS, tm), 0)
    beaten = jnp.zeros((N_EXPERT_GROUPS, tm), f32)
    for gp in range(N_EXPERT_GROUPS):
        row = gs_rows[gp]
        tie = jnp.where(iog > gp, 1.0, 0.0)
        beaten = beaten + jnp.where(row > gs, 1.0, jnp.where(row == gs, tie, 0.0))
    keep = beaten < TOPK_GROUPS
    masked = [jnp.where(keep[g:g + 1, :], blks[g], ninf) for g in range(N_EXPERT_GROUPS)]

    ranks = [jnp.zeros((gsz, tm), f32) for _ in range(N_EXPERT_GROUPS)]
    tie_in = [jnp.where(io > j, 1.0, 0.0) for j in range(gsz)]
    for gp in range(N_EXPERT_GROUPS):
        for j in range(gsz):
            row = masked[gp][j:j + 1, :]
            for g in range(N_EXPERT_GROUPS):
                if gp < g:
                    inc = jnp.where(row >= masked[g], 1.0, 0.0)
                elif gp > g:
                    inc = jnp.where(row > masked[g], 1.0, 0.0)
                else:
                    inc = jnp.where(row > masked[g], 1.0, jnp.where(row == masked[g], tie_in[j], 0.0))
                ranks[g] = ranks[g] + inc
    selb = jnp.concatenate([jnp.where(r < TOP_K, 1.0, 0.0) for r in ranks], axis=0)
    wsel = selb * scores
    wn = wsel / jnp.sum(wsel, axis=0, keepdims=True) * ROUTED_SCALE

    maskb = selb.astype(bf16)
    pos = jnp.dot(maskb, tri_ref[...], preferred_element_type=f32) + cnt_ref[:, 0:1]
    cnt_ref[...] = cnt_ref[...] + jnp.sum(selb, axis=1, keepdims=True)
    slot = jnp.dot(trie_ref[...], maskb, preferred_element_type=f32)
    ioe = lax.broadcasted_iota(jnp.int32, (N_EXPERTS, tm), 0).astype(f32)
    e_rows, p_rows = [], []
    for k in range(TOP_K):
        hit = jnp.where(slot == k, selb, 0.0)
        e_rows.append(jnp.sum(hit * ioe, axis=0, keepdims=True))
        p_rows.append(jnp.sum(hit * pos, axis=0, keepdims=True))
    e8_ref[...] = jnp.concatenate(e_rows, axis=0).astype(jnp.int32)
    pos8_ref[...] = jnp.concatenate(p_rows, axis=0).astype(jnp.int32)
    wtok_ref[...] = jnp.concatenate([wn.T, jnp.zeros((tm, 128 - N_EXPERTS), f32)], axis=1)


def _router(h, w_router, router_bias, tm=512):
    n = h.shape[0]
    wrt = w_router.astype(f32).T
    bias = router_bias.astype(f32).reshape(N_EXPERTS, 1)
    tri = (jnp.arange(tm)[:, None] < jnp.arange(tm)[None, :]).astype(bf16)
    trie = (jnp.arange(N_EXPERTS)[None, :] < jnp.arange(N_EXPERTS)[:, None]).astype(bf16)
    full = lambda a: pl.BlockSpec(a.shape, lambda i: (0,) * a.ndim)
    tok = lambda: pl.BlockSpec((TOP_K, tm), lambda i: (0, i))
    return pl.pallas_call(
        _router_kernel,
        grid=(n // tm,),
        in_specs=[pl.BlockSpec((tm, D_MODEL), lambda i: (i, 0)), full(wrt), full(bias), full(tri), full(trie)],
        out_specs=[tok(), tok(), pl.BlockSpec((tm, 128), lambda i: (i, 0)),
                   pl.BlockSpec((N_EXPERTS, 128), lambda i: (0, 0))],
        out_shape=[jax.ShapeDtypeStruct((TOP_K, n), jnp.int32), jax.ShapeDtypeStruct((TOP_K, n), jnp.int32),
                   jax.ShapeDtypeStruct((n, 128), f32), jax.ShapeDtypeStruct((N_EXPERTS, 128), f32)],
        compiler_params=_params("arbitrary"),
        name="router",
    )(h, wrt, bias, tri, trie)


INVERT_UNROLL = 16


def _invert_kernel(dest_ref, nb_ref, code_ref):
    n_rows = code_ref.shape[0]

    def fill(first_group, last_group):
        def body(k, c):
            for u in range(INVERT_UNROLL):
                code_ref[k * INVERT_UNROLL + u] = PAD_CODE
            return c
        lax.fori_loop(first_group, last_group, body, 0)

    def per_expert(e, c):
        groups = MOE_ROWS // INVERT_UNROLL + 1
        start = jnp.minimum(nb_ref[1 + e], n_rows - groups * INVERT_UNROLL) // INVERT_UNROLL
        fill(start, start + groups)
        return c
    lax.fori_loop(0, N_EXPERTS, per_expert, 0)
    fill(nb_ref[0] * (MOE_ROWS // INVERT_UNROLL), n_rows // INVERT_UNROLL)

    def body(p, c):
        code_ref[dest_ref[p]] = p
        return c
    lax.fori_loop(0, dest_ref.shape[0], body, 0, unroll=INVERT_UNROLL)


def _invert(dest, nb, n_rows):
    smem = lambda: pl.BlockSpec(memory_space=pltpu.SMEM)
    return pl.pallas_call(
        _invert_kernel,
        in_specs=[smem(), smem()],
        out_specs=smem(),
        out_shape=jax.ShapeDtypeStruct((n_rows,), jnp.int32),
        name="invert_dispatch",
    )(dest, nb)


def _expert_changed(be_ref, i):
    return jnp.logical_or(i == 0, be_ref[i] != be_ref[jnp.maximum(i - 1, 0)])


SCATTER_UNROLL = 8


def _moe_up_kernel(be_ref, code_ref, nb_ref, hp_ref, wtok_ref, wg_ref, wu_ref, mid_ref,
                   xg_scr, xa_scr, xb_scr, wra_scr, wrb_scr, wgu_scr, wstage_scr, wsem):
    i = pl.program_id(0)
    rows = MOE_ROWS

    def gather_block(blk, x_dst, wrow_dst):
        base = blk * rows
        for r in range(rows):
            tok = (code_ref[base + r] >> 3) & (N_TOKENS - 1)
            xg_scr[r * PACK_ROWS:(r + 1) * PACK_ROWS, :] = (
                hp_ref[pl.ds(pl.multiple_of(tok * PACK_ROWS, PACK_ROWS), PACK_ROWS), :])
            wrow_dst[r:r + 1, :] = wtok_ref[pl.ds(tok, 1), :]
        half = D_MODEL // 2
        chunks = pltpu.einshape("rsl->srl", xg_scr[...].reshape(rows, PACK_ROWS, 128))
        for s in range(PACK_ROWS):
            wds = chunks[s]
            lo = pltpu.unpack_elementwise(wds, index=0, packed_dtype=bf16, unpacked_dtype=f32)
            hi = pltpu.unpack_elementwise(wds, index=1, packed_dtype=bf16, unpacked_dtype=f32)
            x_dst[:, s * 128:(s + 1) * 128] = lo.astype(bf16)
            x_dst[:, half + s * 128:half + (s + 1) * 128] = hi.astype(bf16)

    def expert_block(x_cur, wrow_cur, x_nxt, wrow_nxt):
        gather_block(jnp.minimum(i + 1, MOE_BLOCKS - 1), x_nxt, wrow_nxt)
        gu = jnp.dot(x_cur[...], wgu_scr[...], preferred_element_type=f32)
        g, u = gu[:, :D_EXPERT], gu[:, D_EXPERT:]
        lane = lax.broadcasted_iota(jnp.int32, (rows, 128), 1)
        w = jnp.sum(jnp.where(lane == be_ref[i], wrow_cur[...], 0.0), axis=1, keepdims=True)
        real = i * rows + lax.broadcasted_iota(jnp.int32, (rows, 1), 0) < nb_ref[1 + be_ref[i]]
        mid_ref[...] = (g * jax.nn.sigmoid(g) * u * jnp.where(real, w, 0.0)).astype(bf16)

    @pl.when(i == 0)
    def _():
        gather_block(0, xa_scr, wra_scr)

    expert = be_ref[i]
    slot = nb_ref[NB_SLOT + expert]
    upcoming = nb_ref[NB_NEXT + expert]

    def weight_copies(e, s):
        return (pltpu.make_async_copy(wg_ref.at[e], wstage_scr.at[s, 0], wsem.at[s, 0]),
                pltpu.make_async_copy(wu_ref.at[e], wstage_scr.at[s, 1], wsem.at[s, 1]))

    @pl.when(jnp.logical_and(i == 0, nb_ref[0] > 0))
    def _():
        for cp in weight_copies(expert, slot):
            cp.start()

    @pl.when(i < nb_ref[0])
    def _():
        @pl.when(_expert_changed(be_ref, i))
        def _():
            for cp in weight_copies(expert, slot):
                cp.wait()

            @pl.when(upcoming >= 0)
            def _():
                for cp in weight_copies(upcoming, 1 - slot):
                    cp.start()

            wgu_scr[:, :D_EXPERT] = wstage_scr[slot, 0].astype(bf16)
            wgu_scr[:, D_EXPERT:] = wstage_scr[slot, 1].astype(bf16)

        @pl.when(i % 2 == 0)
        def _():
            expert_block(xa_scr, wra_scr, xb_scr, wrb_scr)

        @pl.when(i % 2 == 1)
        def _():
            expert_block(xb_scr, wrb_scr, xa_scr, wra_scr)

    @pl.when(i >= nb_ref[0])
    def _():
        mid_ref[...] = jnp.zeros_like(mid_ref)


def _moe_up(block_e, code, nb, hp, wtok, w_gate, w_up):
    grid_spec = pltpu.PrefetchScalarGridSpec(
        num_scalar_prefetch=3,
        grid=(MOE_BLOCKS,),
        in_specs=[pl.BlockSpec(memory_space=pltpu.VMEM),
                  pl.BlockSpec(memory_space=pltpu.VMEM),
                  pl.BlockSpec(memory_space=pl.ANY),
                  pl.BlockSpec(memory_space=pl.ANY)],
        out_specs=pl.BlockSpec((MOE_ROWS, D_EXPERT), lambda i, be, cd, nb: (i, 0)),
        scratch_shapes=[pltpu.VMEM((PACK_ROWS * MOE_ROWS, 128), jnp.int32),
                        pltpu.VMEM((MOE_ROWS, D_MODEL), bf16), pltpu.VMEM((MOE_ROWS, D_MODEL), bf16),
                        pltpu.VMEM((MOE_ROWS, 128), f32), pltpu.VMEM((MOE_ROWS, 128), f32),
                        pltpu.VMEM((D_MODEL, 2 * D_EXPERT), bf16),
                        pltpu.VMEM((2, 2, D_MODEL, D_EXPERT), f32),
                        pltpu.SemaphoreType.DMA((2, 2))])
    return pl.pallas_call(
        _moe_up_kernel,
        grid_spec=grid_spec,
        out_shape=jax.ShapeDtypeStruct((MOE_BLOCKS * MOE_ROWS, D_EXPERT), bf16),
        compiler_params=pltpu.CompilerParams(dimension_semantics=("arbitrary",),
                                             vmem_limit_bytes=MOE_UP_VMEM_LIMIT),
        name="moe_up",
    )(block_e, code, nb, hp, wtok, w_gate, w_up)


def _moe_down_kernel(be_ref, code_ref, nb_ref, mida_ref, midb_ref, wda_ref, wdb_ref, acc_ref,
                     ya_scr, yb_scr, wd_scr):
    j = pl.program_id(1)
    rows = MOE_ROWS
    nb = nb_ref[0]
    blk_a, blk_b = 2 * j, 2 * j + 1

    @pl.when(j == 0)
    def _():
        acc_ref[...] = jnp.zeros_like(acc_ref)

    def scatter_block(blk, ybuf):
        for g in range(rows // SCATTER_UNROLL):
            sums, addrs = [], []
            for k in range(SCATTER_UNROLL):
                r = g * SCATTER_UNROLL + k
                a = pl.multiple_of(code_ref[blk * rows + r] & -PACK_ROWS, PACK_ROWS)
                v = ybuf[r * PACK_ROWS:(r + 1) * PACK_ROWS, :]
                sums.append(acc_ref[pl.ds(a, PACK_ROWS), :] + v)
                addrs.append(a)
            for k in range(SCATTER_UNROLL):
                acc_ref[pl.ds(addrs[k], PACK_ROWS), :] = sums[k]

    def down_block(mid_ref, ybuf):
        y = jnp.dot(mid_ref[...], wd_scr[...], preferred_element_type=f32)
        y3 = jnp.stack([y[:, c * 128:(c + 1) * 128] for c in range(PACK_ROWS)], axis=0)
        ybuf[...] = pltpu.einshape("crl->rcl", y3).reshape(rows * PACK_ROWS, 128)

    def refresh_weights(blk, wd_ref):
        @pl.when(jnp.logical_and(blk < nb, _expert_changed(be_ref, jnp.minimum(blk, MOE_BLOCKS - 1))))
        def _():
            wd_scr[...] = wd_ref[0].astype(bf16)

    refresh_weights(blk_a, wda_ref)

    @pl.when(jnp.logical_and(j == 0, blk_a < nb))
    def _():
        down_block(mida_ref, ya_scr)

    @pl.when(jnp.logical_and(j > 0, blk_a < nb))
    def _():
        down_block(mida_ref, ya_scr)
        scatter_block(blk_a - 1, yb_scr)

    @pl.when(jnp.logical_and(j > 0, blk_a == nb))
    def _():
        scatter_block(blk_a - 1, yb_scr)

    refresh_weights(blk_b, wdb_ref)

    @pl.when(blk_b < nb)
    def _():
        down_block(midb_ref, yb_scr)
        scatter_block(blk_a, ya_scr)

    @pl.when(blk_b == nb)
    def _():
        scatter_block(blk_a, ya_scr)


def _moe_down(block_e, code, nb, mid, w_down):
    half = D_MODEL // 2
    acc_rows = (N_TOKENS + 1) * PACK_ROWS
    last = MOE_BLOCKS - 1
    assert MOE_BLOCKS % 2 == 0
    blk = lambda off: (lambda p, j, be, cd, nb: (jnp.minimum(2 * j + off, last), 0))
    wd = lambda off: (lambda p, j, be, cd, nb: (be[jnp.minimum(2 * j + off, last)], 0, p))
    grid_spec = pltpu.PrefetchScalarGridSpec(
        num_scalar_prefetch=3,
        grid=(2, MOE_BLOCKS // 2 + 1),
        in_specs=[pl.BlockSpec((MOE_ROWS, D_EXPERT), blk(0)), pl.BlockSpec((MOE_ROWS, D_EXPERT), blk(1)),
                  pl.BlockSpec((1, D_EXPERT, half), wd(0)), pl.BlockSpec((1, D_EXPERT, half), wd(1))],
        out_specs=pl.BlockSpec((None, acc_rows, 128), lambda p, j, be, cd, nb: (p, 0, 0),
                               pipeline_mode=pl.Buffered(1)),
        scratch_shapes=[pltpu.VMEM((PACK_ROWS * MOE_ROWS, 128), f32),
                        pltpu.VMEM((PACK_ROWS * MOE_ROWS, 128), f32),
                        pltpu.VMEM((D_EXPERT, half), bf16)])
    return pl.pallas_call(
        _moe_down_kernel,
        grid_spec=grid_spec,
        out_shape=jax.ShapeDtypeStruct((2, acc_rows, 128), f32),
        compiler_params=_params("arbitrary", "arbitrary"),
        name="moe_down",
    )(block_e, code, nb, mid, mid, w_down, w_down)


def _final_kernel(h_ref, r0_ref, r1_ref, wgu_ref, wd_ref, lg_ref, lb_ref, o_ref):
    tm = h_ref.shape[0]
    gu = jnp.dot(h_ref[...].astype(bf16), wgu_ref[...], preferred_element_type=f32)
    g, u = gu[:, :D_EXPERT], gu[:, D_EXPERT:]
    mid = (g * jax.nn.sigmoid(g) * u).astype(bf16)
    shared = jnp.dot(mid, wd_ref[...], preferred_element_type=f32)
    halves = [pltpu.einshape("rcl->crl", r[...].reshape(tm, PACK_ROWS, 128)) for r in (r0_ref, r1_ref)]
    routed = jnp.concatenate([hv[c] for hv in halves for c in range(PACK_ROWS)], axis=1)
    o_ref[...] = _layer_norm_rows(DEEPNORM_ALPHA * h_ref[...] + routed + shared,
                                  lg_ref[...], lb_ref[...])


def _final(h, racc, wgu, wd, ln_g, ln_b, tm=256):
    n = h.shape[0]
    row = lambda: pl.BlockSpec((tm, D_MODEL), lambda i: (i, 0))
    full = lambda a: pl.BlockSpec(a.shape, lambda i: (0,) * a.ndim)
    acc = lambda p: pl.BlockSpec((None, tm * PACK_ROWS, 128), lambda i: (p, i, 0))
    return pl.pallas_call(
        _final_kernel,
        grid=(n // tm,),
        in_specs=[row(), acc(0), acc(1), full(wgu), full(wd), full(ln_g), full(ln_b)],
        out_specs=row(),
        out_shape=jax.ShapeDtypeStruct((n, D_MODEL), f32),
        compiler_params=_params("parallel"),
        name="shared_final",
    )(h, racc, racc, wgu, wd, ln_g, ln_b)


def _dispatch_plan(e8, pos8, cnt):
    counts = cnt[:, 0].astype(jnp.int32)
    padded = (counts + MOE_ROWS - 1) // MOE_ROWS * MOE_ROWS
    pad_end = jnp.cumsum(padded).astype(jnp.int32)
    pad_start = pad_end - padded
    seg_end = pad_start + counts
    ids = jnp.arange(N_EXPERTS, dtype=jnp.int32)
    start8 = jnp.sum(jnp.where(e8[..., None] == ids, pad_start, 0), axis=-1)
    dest = (start8 + pos8).T.reshape(-1)
    block_start = jnp.arange(MOE_BLOCKS, dtype=jnp.int32) * MOE_ROWS
    block_e = jnp.minimum(jnp.sum((pad_end[None, :] <= block_start[:, None]).astype(jnp.int32), axis=1),
                          N_EXPERTS - 1)
    used = counts > 0
    slot = (jnp.cumsum(used.astype(jnp.int32)) - 1) % 2
    later = jnp.where(used, ids, N_EXPERTS)
    nxt = jnp.concatenate([lax.cummin(later[::-1])[::-1][1:], jnp.full((1,), N_EXPERTS, jnp.int32)])
    nxt = jnp.where(nxt == N_EXPERTS, -1, nxt)
    nb = jnp.concatenate([pad_end[-1:] // MOE_ROWS, seg_end, slot, nxt]).astype(jnp.int32)
    return dest, block_e, nb


def kernel(x, w_in, ssm_log_dt, ssm_a_re, ssm_a_im, ssm_b_re, ssm_b_im, ssm_c_re, ssm_c_im, ssm_d,
           w_glu, g_ssm_out, g_attn_out, w_out, ln1_g, ln1_b, w_router, router_bias, w_gate, w_up,
           w_down, ws_gate, ws_up, ws_down, ln2_g, ln2_b):
    bsz, seq, d = x.shape
    n_tok = bsz * seq
    h = x.reshape(n_tok, d)
    for layer in range(DEPTH):
        proj = _matmul(h, w_in[layer].astype(bf16), f32)

        tables = _s5_tables(ssm_log_dt[layer], ssm_a_re[layer], ssm_a_im[layer], ssm_b_re[layer],
                            ssm_b_im[layer], ssm_c_re[layer], ssm_c_im[layer], ssm_d[layer])
        y = _s5_mixer(proj, tables, bsz, seq // S5_CHUNK)

        y_attn = _dilated_attention(proj, D_SSM // (2 * HEAD_DIM), bsz, seq)

        row2 = lambda a: a.astype(f32).reshape(1, -1)
        h, hp = _mix_out(y, y_attn, h, w_glu[layer].astype(bf16), w_out[layer].astype(bf16),
                         row2(g_ssm_out[layer]), row2(g_attn_out[layer]), row2(ln1_g[layer]), row2(ln1_b[layer]))

        assert n_tok == N_TOKENS
        e8, pos8, wtok, cnt = _router(h, w_router[layer], router_bias[layer])
        dest, block_e, nb = _dispatch_plan(e8, pos8, cnt)
        code = _invert(dest, nb, MOE_BLOCKS * MOE_ROWS)
        mid = _moe_up(block_e, code, nb, hp, wtok, w_gate[layer], w_up[layer])
        racc = _moe_down(block_e, code, nb, mid, w_down[layer])
        wgu = jnp.concatenate([ws_gate[layer], ws_up[layer]], axis=1).astype(bf16)
        h = _final(h, racc, wgu, ws_down[layer].astype(bf16), row2(ln2_g[layer]), row2(ln2_b[layer]))
    return h.reshape(bsz, seq, d)
```

```python
import functools

import jax
import jax.numpy as jnp
import numpy as np
from jax import lax
from jax.experimental import pallas as pl
from jax.experimental.pallas import tpu as pltpu

D_MODEL = 2048
D_SSM = 1024
D_ATTN = 1024
SSM_CH = 16
SSM_GROUPS = 64
SSM_STATE = 64
HEAD_DIM = 64
N_HEADS = 16
PATTERNS = ((128, 1), (512, 4), (2048, 16))
ATTN_BLOCK = 128
N_EXPERTS = 64
TOP_K = 8
N_EXPERT_GROUPS = 8
TOPK_GROUPS = 4
D_EXPERT = 512
ROUTED_SCALE = 2.5
NORM_EPS = 1e-5
DEPTH = 1
DEEPNORM_ALPHA = (2.0 * DEPTH) ** 0.25

S5_CHUNK = 16
S5_GROUPS_PER_BLOCK = 8
MOE_ROWS = 256
MASK_VALUE = -1e30
PACK_ROWS = D_MODEL // 2 // 128
N_TOKENS = 8192
PAD_CODE = N_TOKENS * TOP_K
NB_SLOT = 1 + N_EXPERTS
NB_NEXT = 1 + 2 * N_EXPERTS
MOE_BLOCKS = -(-(N_TOKENS * TOP_K + N_EXPERTS * (MOE_ROWS - 1)) // MOE_ROWS)
VMEM_LIMIT = 56 * 1024 * 1024
MOE_UP_VMEM_LIMIT = 62 * 1024 * 1024

bf16 = jnp.bfloat16
f32 = jnp.float32


def _params(*sem):
    return pltpu.CompilerParams(dimension_semantics=sem, vmem_limit_bytes=VMEM_LIMIT)


MATMUL_COLS = 1024


def _matmul_kernel(a_ref, b_ref, o_ref):
    a = a_ref[...].astype(bf16)
    for j in range(o_ref.shape[1] // MATMUL_COLS):
        cols = slice(j * MATMUL_COLS, (j + 1) * MATMUL_COLS)
        o_ref[:, cols] = jnp.dot(a, b_ref[:, cols], preferred_element_type=f32).astype(o_ref.dtype)


def _matmul(a, b, out_dtype, tm=256):
    m, k = a.shape
    _, n = b.shape
    return pl.pallas_call(
        _matmul_kernel,
        grid=(m // tm,),
        in_specs=[pl.BlockSpec((tm, k), lambda i: (i, 0)),
                  pl.BlockSpec((k, n), lambda i: (0, 0), pipeline_mode=pl.Buffered(1))],
        out_specs=pl.BlockSpec((tm, n), lambda i: (i, 0)),
        out_shape=jax.ShapeDtypeStruct((m, n), out_dtype),
        compiler_params=_params("parallel"),
        name="matmul",
    )(a, b)


def _s5_tables(log_dt, a_re, a_im, b_re, b_im, c_re, c_im, d_skip):
    t = S5_CHUNK
    gpb = S5_GROUPS_PER_BLOCK
    nblk = SSM_GROUPS // gpb
    hp = lax.Precision.HIGHEST
    lr = jnp.minimum(a_re.astype(f32), -1e-4)
    li = a_im.astype(f32)
    dt = jnp.exp(log_dt.astype(f32))
    kk = jnp.arange(t + 1, dtype=f32)[:, None, None]
    mag = jnp.exp(kk * (lr * dt))
    pr = mag * jnp.cos(kk * (li * dt))
    pi = mag * jnp.sin(kk * (li * dt))
    xr, xi = pr[1] - 1.0, pi[1]
    den = lr * lr + li * li
    cr = (xr * lr + xi * li) / den
    ci = (xi * lr - xr * li) / den
    bbr = cr[..., None] * b_re - ci[..., None] * b_im
    bbi = cr[..., None] * b_im + ci[..., None] * b_re
    wr = pr[:t, :, :, None] * bbr - pi[:t, :, :, None] * bbi
    wi = pr[:t, :, :, None] * bbi + pi[:t, :, :, None] * bbr
    taps = (jnp.einsum('gop,tgpc->tgco', c_re, wr, precision=hp)
            - jnp.einsum('gop,tgpc->tgco', c_im, wi, precision=hp))
    ktab = taps.reshape(t, nblk, gpb, SSM_CH, SSM_CH).transpose(1, 0, 3, 2, 4)
    ktab = ktab.reshape(nblk, t, SSM_CH, gpb * SSM_CH)

    rev = jnp.arange(t - 1, -1, -1)
    sr = pr[rev][..., None] * bbr - pi[rev][..., None] * bbi
    si = pr[rev][..., None] * bbi + pi[rev][..., None] * bbr
    sb = jnp.stack([sr, si], axis=0).reshape(2, t, nblk, gpb, SSM_STATE, SSM_CH)
    bsrc = sb.transpose(2, 1, 0, 3, 5, 4).reshape(nblk, t, 2, gpb * SSM_CH, SSM_STATE)
    bsrc = jnp.concatenate([bsrc, bsrc], axis=-1)

    er = c_re[None] * pr[1:, :, None, :] - c_im[None] * pi[1:, :, None, :]
    ei = c_re[None] * pi[1:, :, None, :] + c_im[None] * pr[1:, :, None, :]
    eb = jnp.stack([er, -ei], axis=0).reshape(2, t, nblk, gpb, SSM_CH, SSM_STATE)
    csrc = eb.transpose(2, 1, 0, 5, 3, 4).reshape(nblk, t, 2, SSM_STATE, gpb * SSM_CH)

    a_chunk = jnp.stack([pr[t], pi[t]], axis=0).reshape(2, nblk, 1, gpb * SSM_STATE)
    a_chunk = a_chunk.transpose(1, 0, 2, 3).reshape(nblk, 2, gpb * SSM_STATE)
    dvec = jnp.tile(d_skip.astype(f32).reshape(nblk, 1, gpb * SSM_CH), (1, 1, t))
    return ktab.astype(bf16), bsrc.astype(bf16), csrc.astype(bf16), a_chunk, dvec


def _s5_kernel(u_ref, ktab_ref, bsrc_ref, csrc_ref, a_ref, d_ref, y_ref,
               toep_ref, bpow_ref, cpow_ref, s_ref, h_ref, yt_ref, *, n_batch, n_chunk):
    t = S5_CHUNK
    gpb = S5_GROUPS_PER_BLOCK
    w = gpb * SSM_CH
    ns = gpb * SSM_STATE
    zero = jnp.zeros((), bf16)

    def same_group(shape, row_size, col_size):
        r = lax.broadcasted_iota(jnp.int32, shape, 0) // row_size
        c = lax.broadcasted_iota(jnp.int32, shape, 1) // col_size
        return r == c

    tap_mask = same_group((w, w), SSM_CH, SSM_CH)
    taps = [jnp.where(tap_mask, jnp.tile(ktab_ref[0, tau], (gpb, 1)), zero) for tau in range(t)]
    for tt in range(t):
        for ss in range(tt + 1):
            toep_ref[ss * w:(ss + 1) * w, tt * w:(tt + 1) * w] = taps[tt - ss]
        if tt % 2 == 0:
            toep_ref[(tt + 1) * w:(tt + 2) * w, tt * w:(tt + 1) * w] = jnp.zeros((w, w), bf16)
    b_mask = same_group((w, ns), SSM_CH, SSM_STATE)
    c_mask = same_group((ns, w), SSM_STATE, SSM_CH)
    for ss in range(t):
        for z in range(2):
            bpow_ref[ss * w:(ss + 1) * w, z * ns:(z + 1) * ns] = jnp.where(
                b_mask, jnp.tile(bsrc_ref[0, ss, z], (1, ns // w)), zero)
            cpow_ref[z * ns:(z + 1) * ns, ss * w:(ss + 1) * w] = jnp.where(
                c_mask, jnp.tile(csrc_ref[0, ss, z], (gpb, 1)), zero)

    rows = n_batch * n_chunk
    uf3 = pltpu.einshape("rsl->srl", u_ref[...].reshape(rows, t, w))
    uf = [uf3[s] for s in range(t)]
    u = jnp.concatenate([p.astype(bf16) for p in uf], axis=1)
    s_ref[...] = jnp.dot(u, bpow_ref[...], preferred_element_type=f32)

    ar = a_ref[0, 0:1, :]
    ai = a_ref[0, 1:2, :]

    def step(j, carry):
        new = []
        for b in range(n_batch):
            hr, hi = carry[2 * b], carry[2 * b + 1]
            row = b * n_chunk + j
            h_ref[pl.ds(row, 1), 0:ns] = hr
            h_ref[pl.ds(row, 1), ns:2 * ns] = hi
            sr = s_ref[pl.ds(row, 1), 0:ns]
            si = s_ref[pl.ds(row, 1), ns:2 * ns]
            new.append(ar * hr - ai * hi + sr)
            new.append(ar * hi + ai * hr + si)
        return tuple(new)

    zero = jnp.zeros((1, ns), f32)
    lax.fori_loop(0, n_chunk, step, (zero,) * (2 * n_batch))

    hprev = h_ref[...].astype(bf16)
    for tp in range(t // 2):
        c0, c1 = 2 * tp * w, (2 * tp + 2) * w
        y = jnp.dot(u[:, :c1], toep_ref[0:c1, c0:c1], preferred_element_type=f32)
        y = y + jnp.dot(hprev, cpow_ref[:, c0:c1], preferred_element_type=f32)
        for k, tt in enumerate((2 * tp, 2 * tp + 1)):
            yk = y[:, k * w:(k + 1) * w] + d_ref[0, :, tt * w:(tt + 1) * w] * uf[tt]
            yt_ref[tt] = jax.nn.gelu(yk, approximate=True)
    y_ref[...] = pltpu.einshape("srl->rsl", yt_ref[...]).reshape(rows * t, w)


def _s5_mixer(proj, tables, n_batch, n_chunk):
    ktab, bsrc, csrc, a_chunk, dvec = tables
    nblk = ktab.shape[0]
    w = S5_GROUPS_PER_BLOCK * SSM_CH
    cols = S5_CHUNK * w
    rows = n_batch * n_chunk
    n_tok = rows * S5_CHUNK
    ns = S5_GROUPS_PER_BLOCK * SSM_STATE
    kern = functools.partial(_s5_kernel, n_batch=n_batch, n_chunk=n_chunk)
    return pl.pallas_call(
        kern,
        grid=(nblk,),
        in_specs=[pl.BlockSpec((n_tok, w), lambda g: (0, g)),
                  pl.BlockSpec((1,) + ktab.shape[1:], lambda g: (g, 0, 0, 0)),
                  pl.BlockSpec((1,) + bsrc.shape[1:], lambda g: (g, 0, 0, 0, 0)),
                  pl.BlockSpec((1,) + csrc.shape[1:], lambda g: (g, 0, 0, 0, 0)),
                  pl.BlockSpec((1, 2, ns), lambda g: (g, 0, 0)),
                  pl.BlockSpec((1, 1, cols), lambda g: (g, 0, 0))],
        out_specs=pl.BlockSpec((n_tok, w), lambda g: (0, g)),
        out_shape=jax.ShapeDtypeStruct((n_tok, D_SSM), f32),
        scratch_shapes=[pltpu.VMEM((cols, cols), bf16),
                        pltpu.VMEM((cols, 2 * ns), bf16),
                        pltpu.VMEM((2 * ns, cols), bf16),
                        pltpu.VMEM((rows, 2 * ns), f32),
                        pltpu.VMEM((rows, 2 * ns), f32),
                        pltpu.VMEM((S5_CHUNK, rows, w), f32)],
        compiler_params=_params("parallel"),
        name="s5_mixer",
    )(proj, ktab, bsrc, csrc, a_chunk, dvec)


def _attn_bias_table():
    blk = ATTN_BLOCK
    slopes = 2.0 ** (-8.0 * jnp.arange(1, N_HEADS + 1, dtype=f32) / N_HEADS)
    delta = np.arange(blk)[:, None] - (np.arange(2 * blk)[None, :] - blk)
    tabs = []
    for window, dil in PATTERNS:
        assert window // dil == blk
        valid = (delta >= 0) & (delta <= window // dil)
        dist = jnp.asarray(delta * dil, dtype=f32)
        bias = jnp.where(valid[None], -slopes[:, None, None] * dist[None], MASK_VALUE)
        tabs.append(bias.reshape(N_HEADS // 2, 2 * blk, 2 * blk))
    return jnp.stack(tabs, axis=1)


def _attn_kernel(q_ref, k_ref, v_ref, bias_ref, o_ref, *scr):
    blk = ATTN_BLOCK
    seq = q_ref.shape[0]
    first_head = lax.broadcasted_iota(jnp.int32, (blk, 2 * HEAD_DIM), 1) < HEAD_DIM
    dims = (((1,), (1,)), ((), ()))
    for pi, (_, dil) in enumerate(PATTERNS):
        o_scr, l_scr = scr[2 * pi], scr[2 * pi + 1]
        sub = seq // dil
        for r in range(dil):
            rows = (lambda st, n: pl.ds(st, n)) if dil == 1 else (lambda st, n: pl.ds(st, n, stride=dil))
            qd = (q_ref[rows(r, sub), :] * HEAD_DIM ** -0.5).astype(bf16)
            kd = k_ref[rows(r, sub), :].astype(bf16)
            vd = v_ref[rows(r, sub), :].astype(bf16)
            for i in range(sub // blk):
                qb = qd[i * blk:(i + 1) * blk]
                zero = jnp.zeros_like(qb)
                q2 = jnp.concatenate([jnp.where(first_head, qb, zero), jnp.where(first_head, zero, qb)], axis=0)
                k0 = max(i - 1, 0) * blk
                nk = (i + 1) * blk - k0
                s = lax.dot_general(q2, kd[k0:k0 + nk], dims, preferred_element_type=f32)
                s = s + bias_ref[0, pi, :, 2 * blk - nk:]
                m = jnp.max(s, axis=-1, keepdims=True)
                p = jnp.exp(s - m)
                l = jnp.sum(p, axis=-1, keepdims=True)
                o = jnp.dot(p.astype(bf16), vd[k0:k0 + nk], preferred_element_type=f32) / l
                lse = m + jnp.log(l)
                dst = rows(r + dil * blk * i, blk)
                o_scr[dst, :] = jnp.where(first_head, o[:blk], o[blk:])
                l_scr[dst, :] = jnp.where(first_head, lse[:blk], lse[blk:])
    l1, l2, l3 = scr[1][...], scr[3][...], scr[5][...]
    m = jnp.maximum(jnp.maximum(l1, l2), l3)
    e1, e2, e3 = jnp.exp(l1 - m), jnp.exp(l2 - m), jnp.exp(l3 - m)
    o_ref[...] = (e1 * scr[0][...] + e2 * scr[2][...] + e3 * scr[4][...]) / (e1 + e2 + e3)


def _dilated_attention(qkv, first, bsz, seq):
    pairs = N_HEADS // 2
    width = 2 * HEAD_DIM
    bias = _attn_bias_table()
    col = lambda off: pl.BlockSpec((seq, width), lambda hp, b: (b, first + off + hp))
    return pl.pallas_call(
        _attn_kernel,
        grid=(pairs, bsz),
        in_specs=[col(0), col(pairs), col(2 * pairs),
                  pl.BlockSpec((1,) + bias.shape[1:], lambda hp, b: (hp, 0, 0, 0))],
        out_specs=pl.BlockSpec((seq, width), lambda hp, b: (b, hp)),
        out_shape=jax.ShapeDtypeStruct((bsz * seq, D_ATTN), f32),
        scratch_shapes=[pltpu.VMEM((seq, width), f32)] * (2 * len(PATTERNS)),
        compiler_params=_params("parallel", "parallel"),
        name="dilated_attention",
    )(qkv, qkv, qkv, bias)


def _layer_norm_rows(x, g, b):
    mu = jnp.mean(x, axis=-1, keepdims=True)
    xc = x - mu
    var = jnp.mean(xc * xc, axis=-1, keepdims=True)
    return xc * lax.rsqrt(var + NORM_EPS) * g + b


def _rms_rows(x, g):
    return x * lax.rsqrt(jnp.mean(x * x, axis=-1, keepdims=True) + NORM_EPS) * g


def _mix_out_kernel(y_ref, ya_ref, x_ref, wglu_ref, w_ref, gs_ref, ga_ref, lg_ref, lb_ref, h_ref, hp_ref):
    z = jnp.dot(y_ref[...].astype(bf16), wglu_ref[...], preferred_element_type=f32)
    y_ssm = z[:, :D_SSM] * jax.nn.sigmoid(z[:, D_SSM:])
    ns = _rms_rows(y_ssm, gs_ref[...]).astype(bf16)
    na = _rms_rows(ya_ref[...], ga_ref[...]).astype(bf16)
    proj = jnp.dot(ns, w_ref[0:D_SSM, :], preferred_element_type=f32)
    proj = proj + jnp.dot(na, w_ref[D_SSM:, :], preferred_element_type=f32)
    h = _layer_norm_rows(DEEPNORM_ALPHA * x_ref[...] + proj, lg_ref[...], lb_ref[...])
    h_ref[...] = h
    half = D_MODEL // 2
    word = pltpu.pack_elementwise([h[:, :half], h[:, half:]], packed_dtype=bf16)
    tm = word.shape[0]
    chunks = jnp.stack([word[:, s * 128:(s + 1) * 128] for s in range(PACK_ROWS)], axis=0)
    hp_ref[...] = pltpu.einshape("srl->rsl", chunks).reshape(tm * PACK_ROWS, 128)


def _mix_out(y, y_attn, x, w_glu, w_out, g_ssm, g_attn, ln_g, ln_b, tm=256):
    n = x.shape[0]
    row = lambda c: pl.BlockSpec((tm, c), lambda i: (i, 0))
    full = lambda a: pl.BlockSpec(a.shape, lambda i: (0,) * a.ndim)
    return pl.pallas_call(
        _mix_out_kernel,
        grid=(n // tm,),
        in_specs=[row(D_SSM), row(D_ATTN), row(D_MODEL), full(w_glu), full(w_out), full(g_ssm), full(g_attn),
                  full(ln_g), full(ln_b)],
        out_specs=[row(D_MODEL), pl.BlockSpec((tm * PACK_ROWS, 128), lambda i: (i, 0))],
        out_shape=[jax.ShapeDtypeStruct((n, D_MODEL), f32),
                   jax.ShapeDtypeStruct((n * PACK_ROWS, 128), jnp.int32)],
        compiler_params=_params("parallel"),
        name="mix_out",
    )(y, y_attn, x, w_glu, w_out, g_ssm, g_attn, ln_g, ln_b)


def _router_kernel(h_ref, wrt_ref, bias_ref, tri_ref, trie_ref, e8_ref, pos8_ref, wtok_ref, cnt_ref):
    gsz = N_EXPERTS // N_EXPERT_GROUPS
    tm = h_ref.shape[0]
    ninf = -jnp.inf

    @pl.when(pl.program_id(0) == 0)
    def _():
        cnt_ref[...] = jnp.zeros_like(cnt_ref)

    logits = lax.dot_general(wrt_ref[...], h_ref[...], (((1,), (1,)), ((), ())),
                             preferred_element_type=f32, precision=lax.Precision.HIGHEST)
    scores = jax.nn.sigmoid(logits)
    sel = scores + bias_ref[...]
    io = lax.broadcasted_iota(jnp.int32, (gsz, tm), 0)

    blks, gs_rows = [], []
    for g in range(N_EXPERT_GROUPS):
        blk = sel[g * gsz:(g + 1) * gsz, :]
        m1 = jnp.max(blk, axis=0, keepdims=True)
        first = jnp.min(jnp.where(blk == m1, io, gsz), axis=0, keepdims=True)
        m2 = jnp.max(jnp.where(io == first, ninf, blk), axis=0, keepdims=True)
        blks.append(blk)
        gs_rows.append(m1 + m2)
    gs = jnp.concatenate(gs_rows, axis=0)

    iog = lax.broadcasted_iota(jnp.int32, (N_EXPERT_GROUPS, tm), 0)
    beaten = jnp.zeros((N_EXPERT_GROUPS, tm), f32)
    for gp in range(N_EXPERT_GROUPS):
        row = gs_rows[gp]
        tie = jnp.where(iog > gp, 1.0, 0.0)
        beaten = beaten + jnp.where(row > gs, 1.0, jnp.where(row == gs, tie, 0.0))
    keep = beaten < TOPK_GROUPS
    masked = [jnp.where(keep[g:g + 1, :], blks[g], ninf) for g in range(N_EXPERT_GROUPS)]

    ranks = [jnp.zeros((gsz, tm), f32) for _ in range(N_EXPERT_GROUPS)]
    tie_in = [jnp.where(io > j, 1.0, 0.0) for j in range(gsz)]
    for gp in range(N_EXPERT_GROUPS):
        for j in range(gsz):
            row = masked[gp][j:j + 1, :]
            for g in range(N_EXPERT_GROUPS):
                if gp < g:
                    inc = jnp.where(row >= masked[g], 1.0, 0.0)
                elif gp > g:
                    inc = jnp.where(row > masked[g], 1.0, 0.0)
                else:
                    inc = jnp.where(row > masked[g], 1.0, jnp.where(row == masked[g], tie_in[j], 0.0))
                ranks[g] = ranks[g] + inc
    selb = jnp.concatenate([jnp.where(r < TOP_K, 1.0, 0.0) for r in ranks], axis=0)
    wsel = selb * scores
    wn = wsel / jnp.sum(wsel, axis=0, keepdims=True) * ROUTED_SCALE

    maskb = selb.astype(bf16)
    pos = jnp.dot(maskb, tri_ref[...], preferred_element_type=f32) + cnt_ref[:, 0:1]
    cnt_ref[...] = cnt_ref[...] + jnp.sum(selb, axis=1, keepdims=True)
    slot = jnp.dot(trie_ref[...], maskb, preferred_element_type=f32)
    ioe = lax.broadcasted_iota(jnp.int32, (N_EXPERTS, tm), 0).astype(f32)
    e_rows, p_rows = [], []
    for k in range(TOP_K):
        hit = jnp.where(slot == k, selb, 0.0)
        e_rows.append(jnp.sum(hit * ioe, axis=0, keepdims=True))
        p_rows.append(jnp.sum(hit * pos, axis=0, keepdims=True))
    e8_ref[...] = jnp.concatenate(e_rows, axis=0).astype(jnp.int32)
    pos8_ref[...] = jnp.concatenate(p_rows, axis=0).astype(jnp.int32)
    wtok_ref[...] = jnp.concatenate([wn.T, jnp.zeros((tm, 128 - N_EXPERTS), f32)], axis=1)


def _router(h, w_router, router_bias, tm=512):
    n = h.shape[0]
    wrt = w_router.astype(f32).T
    bias = router_bias.astype(f32).reshape(N_EXPERTS, 1)
    tri = (jnp.arange(tm)[:, None] < jnp.arange(tm)[None, :]).astype(bf16)
    trie = (jnp.arange(N_EXPERTS)[None, :] < jnp.arange(N_EXPERTS)[:, None]).astype(bf16)
    full = lambda a: pl.BlockSpec(a.shape, lambda i: (0,) * a.ndim)
    tok = lambda: pl.BlockSpec((TOP_K, tm), lambda i: (0, i))
    return pl.pallas_call(
        _router_kernel,
        grid=(n // tm,),
        in_specs=[pl.BlockSpec((tm, D_MODEL), lambda i: (i, 0)), full(wrt), full(bias), full(tri), full(trie)],
        out_specs=[tok(), tok(), pl.BlockSpec((tm, 128), lambda i: (i, 0)),
                   pl.BlockSpec((N_EXPERTS, 128), lambda i: (0, 0))],
        out_shape=[jax.ShapeDtypeStruct((TOP_K, n), jnp.int32), jax.ShapeDtypeStruct((TOP_K, n), jnp.int32),
                   jax.ShapeDtypeStruct((n, 128), f32), jax.ShapeDtypeStruct((N_EXPERTS, 128), f32)],
        compiler_params=_params("arbitrary"),
        name="router",
    )(h, wrt, bias, tri, trie)


INVERT_UNROLL = 16


def _invert_kernel(dest_ref, nb_ref, code_ref):
    n_rows = code_ref.shape[0]

    def fill(first_group, last_group):
        def body(k, c):
            for u in range(INVERT_UNROLL):
                code_ref[k * INVERT_UNROLL + u] = PAD_CODE
            return c
        lax.fori_loop(first_group, last_group, body, 0)

    def per_expert(e, c):
        groups = MOE_ROWS // INVERT_UNROLL + 1
        start = jnp.minimum(nb_ref[1 + e], n_rows - groups * INVERT_UNROLL) // INVERT_UNROLL
        fill(start, start + groups)
        return c
    lax.fori_loop(0, N_EXPERTS, per_expert, 0)
    fill(nb_ref[0] * (MOE_ROWS // INVERT_UNROLL), n_rows // INVERT_UNROLL)

    def body(p, c):
        code_ref[dest_ref[p]] = p
        return c
    lax.fori_loop(0, dest_ref.shape[0], body, 0, unroll=INVERT_UNROLL)


def _invert(dest, nb, n_rows):
    smem = lambda: pl.BlockSpec(memory_space=pltpu.SMEM)
    return pl.pallas_call(
        _invert_kernel,
        in_specs=[smem(), smem()],
        out_specs=smem(),
        out_shape=jax.ShapeDtypeStruct((n_rows,), jnp.int32),
        name="invert_dispatch",
    )(dest, nb)


def _expert_changed(be_ref, i):
    return jnp.logical_or(i == 0, be_ref[i] != be_ref[jnp.maximum(i - 1, 0)])


SCATTER_UNROLL = 8


def _moe_up_kernel(be_ref, code_ref, nb_ref, hp_ref, wtok_ref, wg_ref, wu_ref, mid_ref,
                   xg_scr, xa_scr, xb_scr, wra_scr, wrb_scr, wgu_scr, wstage_scr, wsem):
    i = pl.program_id(0)
    rows = MOE_ROWS

    def gather_block(blk, x_dst, wrow_dst):
        base = blk * rows
        for r in range(rows):
            tok = (code_ref[base + r] >> 3) & (N_TOKENS - 1)
            xg_scr[r * PACK_ROWS:(r + 1) * PACK_ROWS, :] = (
                hp_ref[pl.ds(pl.multiple_of(tok * PACK_ROWS, PACK_ROWS), PACK_ROWS), :])
            wrow_dst[r:r + 1, :] = wtok_ref[pl.ds(tok, 1), :]
        half = D_MODEL // 2
        chunks = pltpu.einshape("rsl->srl", xg_scr[...].reshape(rows, PACK_ROWS, 128))
        for s in range(PACK_ROWS):
            wds = chunks[s]
            lo = pltpu.unpack_elementwise(wds, index=0, packed_dtype=bf16, unpacked_dtype=f32)
            hi = pltpu.unpack_elementwise(wds, index=1, packed_dtype=bf16, unpacked_dtype=f32)
            x_dst[:, s * 128:(s + 1) * 128] = lo.astype(bf16)
            x_dst[:, half + s * 128:half + (s + 1) * 128] = hi.astype(bf16)

    def expert_block(x_cur, wrow_cur, x_nxt, wrow_nxt):
        gather_block(jnp.minimum(i + 1, MOE_BLOCKS - 1), x_nxt, wrow_nxt)
        gu = jnp.dot(x_cur[...], wgu_scr[...], preferred_element_type=f32)
        g, u = gu[:, :D_EXPERT], gu[:, D_EXPERT:]
        lane = lax.broadcasted_iota(jnp.int32, (rows, 128), 1)
        w = jnp.sum(jnp.where(lane == be_ref[i], wrow_cur[...], 0.0), axis=1, keepdims=True)
        real = i * rows + lax.broadcasted_iota(jnp.int32, (rows, 1), 0) < nb_ref[1 + be_ref[i]]
        mid_ref[...] = (g * jax.nn.sigmoid(g) * u * jnp.where(real, w, 0.0)).astype(bf16)

    @pl.when(i == 0)
    def _():
        gather_block(0, xa_scr, wra_scr)

    expert = be_ref[i]
    slot = nb_ref[NB_SLOT + expert]
    upcoming = nb_ref[NB_NEXT + expert]

    def weight_copies(e, s):
        return (pltpu.make_async_copy(wg_ref.at[e], wstage_scr.at[s, 0], wsem.at[s, 0]),
                pltpu.make_async_copy(wu_ref.at[e], wstage_scr.at[s, 1], wsem.at[s, 1]))

    @pl.when(jnp.logical_and(i == 0, nb_ref[0] > 0))
    def _():
        for cp in weight_copies(expert, slot):
            cp.start()

    @pl.when(i < nb_ref[0])
    def _():
        @pl.when(_expert_changed(be_ref, i))
        def _():
            for cp in weight_copies(expert, slot):
                cp.wait()

            @pl.when(upcoming >= 0)
            def _():
                for cp in weight_copies(upcoming, 1 - slot):
                    cp.start()

            wgu_scr[:, :D_EXPERT] = wstage_scr[slot, 0].astype(bf16)
            wgu_scr[:, D_EXPERT:] = wstage_scr[slot, 1].astype(bf16)

        @pl.when(i % 2 == 0)
        def _():
            expert_block(xa_scr, wra_scr, xb_scr, wrb_scr)

        @pl.when(i % 2 == 1)
        def _():
            expert_block(xb_scr, wrb_scr, xa_scr, wra_scr)

    @pl.when(i >= nb_ref[0])
    def _():
        mid_ref[...] = jnp.zeros_like(mid_ref)


def _moe_up(block_e, code, nb, hp, wtok, w_gate, w_up):
    grid_spec = pltpu.PrefetchScalarGridSpec(
        num_scalar_prefetch=3,
        grid=(MOE_BLOCKS,),
        in_specs=[pl.BlockSpec(memory_space=pltpu.VMEM),
                  pl.BlockSpec(memory_space=pltpu.VMEM),
                  pl.BlockSpec(memory_space=pl.ANY),
                  pl.BlockSpec(memory_space=pl.ANY)],
        out_specs=pl.BlockSpec((MOE_ROWS, D_EXPERT), lambda i, be, cd, nb: (i, 0)),
        scratch_shapes=[pltpu.VMEM((PACK_ROWS * MOE_ROWS, 128), jnp.int32),
                        pltpu.VMEM((MOE_ROWS, D_MODEL), bf16), pltpu.VMEM((MOE_ROWS, D_MODEL), bf16),
                        pltpu.VMEM((MOE_ROWS, 128), f32), pltpu.VMEM((MOE_ROWS, 128), f32),
                        pltpu.VMEM((D_MODEL, 2 * D_EXPERT), bf16),
                        pltpu.VMEM((2, 2, D_MODEL, D_EXPERT), f32),
                        pltpu.SemaphoreType.DMA((2, 2))])
    return pl.pallas_call(
        _moe_up_kernel,
        grid_spec=grid_spec,
        out_shape=jax.ShapeDtypeStruct((MOE_BLOCKS * MOE_ROWS, D_EXPERT), bf16),
        compiler_params=pltpu.CompilerParams(dimension_semantics=("arbitrary",),
                                             vmem_limit_bytes=MOE_UP_VMEM_LIMIT),
        name="moe_up",
    )(block_e, code, nb, hp, wtok, w_gate, w_up)


def _moe_down_kernel(be_ref, code_ref, nb_ref, mida_ref, midb_ref, wda_ref, wdb_ref, acc_ref,
                     ya_scr, yb0_scr, yb1_scr, wa_scr, wb_scr):
    j = pl.program_id(1)
    rows = MOE_ROWS
    nb = nb_ref[0]
    blk_a, blk_b = 2 * j, 2 * j + 1
    last = MOE_BLOCKS - 1

    @pl.when(j == 0)
    def _():
        acc_ref[...] = jnp.zeros_like(acc_ref)

    def scatter_rows(base, ybuf, r0, n):
        sums, addrs = [], []
        for k in range(n):
            a = pl.multiple_of(code_ref[base + r0 + k] & -PACK_ROWS, PACK_ROWS)
            v = ybuf[pl.ds(pl.multiple_of((r0 + k) * PACK_ROWS, PACK_ROWS), PACK_ROWS), :]
            sums.append(acc_ref[pl.ds(a, PACK_ROWS), :] + v)
            addrs.append(a)
        for k in range(n):
            acc_ref[pl.ds(addrs[k], PACK_ROWS), :] = sums[k]

    def scatter_block(blk, ybuf):
        for g in range(rows // SCATTER_UNROLL):
            scatter_rows(blk * rows, ybuf, g * SCATTER_UNROLL, SCATTER_UNROLL)

    def scatter_block_compact(blk, ybuf):
        def body(g, c):
            scatter_rows(blk * rows, ybuf, g * SCATTER_UNROLL, SCATTER_UNROLL)
            return c
        lax.fori_loop(0, rows // SCATTER_UNROLL, body, 0)

    def down_block(mid_ref, w_scr, ybuf):
        y = jnp.dot(mid_ref[...], w_scr[...], preferred_element_type=f32)
        y3 = jnp.stack([y[:, c * 128:(c + 1) * 128] for c in range(PACK_ROWS)], axis=0)
        ybuf[...] = pltpu.einshape("crl->rcl", y3).reshape(rows * PACK_ROWS, 128)

    def refresh_weights(blk, wd_ref, w_scr):
        cur = be_ref[jnp.minimum(blk, last)]
        prev = be_ref[jnp.clip(blk - 2, 0, last)]

        @pl.when(jnp.logical_and(blk < nb, jnp.logical_or(j == 0, cur != prev)))
        def _():
            w_scr[...] = wd_ref[0].astype(bf16)

    refresh_weights(blk_a, wda_ref, wa_scr)
    refresh_weights(blk_b, wdb_ref, wb_scr)

    for parity, yb_this, yb_prev in ((0, yb0_scr, yb1_scr), (1, yb1_scr, yb0_scr)):
        mine = j % 2 == parity

        @pl.when(jnp.logical_and(mine, jnp.logical_and(j > 0, blk_b < nb)))
        def _():
            down_block(mida_ref, wa_scr, ya_scr)
            down_block(midb_ref, wb_scr, yb_this)
            scatter_block(blk_a - 1, yb_prev)
            scatter_block(blk_a, ya_scr)

        @pl.when(jnp.logical_and(mine, jnp.logical_and(j > 0, blk_b == nb)))
        def _():
            down_block(mida_ref, wa_scr, ya_scr)
            scatter_block_compact(blk_a - 1, yb_prev)
            scatter_block_compact(blk_a, ya_scr)

        @pl.when(jnp.logical_and(mine, jnp.logical_and(j > 0, blk_a == nb)))
        def _():
            scatter_block_compact(blk_a - 1, yb_prev)

    @pl.when(jnp.logical_and(j == 0, blk_a < nb))
    def _():
        down_block(mida_ref, wa_scr, ya_scr)
        scatter_block_compact(blk_a, ya_scr)

    @pl.when(jnp.logical_and(j == 0, blk_b < nb))
    def _():
        down_block(midb_ref, wb_scr, yb0_scr)


def _moe_down(block_e, code, nb, mid, w_down):
    half = D_MODEL // 2
    acc_rows = (N_TOKENS + 1) * PACK_ROWS
    last = MOE_BLOCKS - 1
    assert MOE_BLOCKS % 2 == 0
    blk = lambda off: (lambda p, j, be, cd, nb: (jnp.minimum(2 * j + off, last), 0))
    wd = lambda off: (lambda p, j, be, cd, nb: (be[jnp.minimum(2 * j + off, last)], 0, p))
    grid_spec = pltpu.PrefetchScalarGridSpec(
        num_scalar_prefetch=3,
        grid=(2, MOE_BLOCKS // 2 + 1),
        in_specs=[pl.BlockSpec((MOE_ROWS, D_EXPERT), blk(0)), pl.BlockSpec((MOE_ROWS, D_EXPERT), blk(1)),
                  pl.BlockSpec((1, D_EXPERT, half), wd(0)), pl.BlockSpec((1, D_EXPERT, half), wd(1))],
        out_specs=pl.BlockSpec((None, acc_rows, 128), lambda p, j, be, cd, nb: (p, 0, 0),
                               pipeline_mode=pl.Buffered(1)),
        scratch_shapes=[pltpu.VMEM((PACK_ROWS * MOE_ROWS, 128), f32)] * 3
        + [pltpu.VMEM((D_EXPERT, half), bf16)] * 2)
    return pl.pallas_call(
        _moe_down_kernel,
        grid_spec=grid_spec,
        out_shape=jax.ShapeDtypeStruct((2, acc_rows, 128), f32),
        compiler_params=_params("arbitrary", "arbitrary"),
        name="moe_down",
    )(block_e, code, nb, mid, mid, w_down, w_down)


def _final_kernel(h_ref, r0_ref, r1_ref, wgu_ref, wd_ref, lg_ref, lb_ref, o_ref):
    tm = h_ref.shape[0]
    gu = jnp.dot(h_ref[...].astype(bf16), wgu_ref[...], preferred_element_type=f32)
    g, u = gu[:, :D_EXPERT], gu[:, D_EXPERT:]
    mid = (g * jax.nn.sigmoid(g) * u).astype(bf16)
    shared = jnp.dot(mid, wd_ref[...], preferred_element_type=f32)
    halves = [pltpu.einshape("rcl->crl", r[...].reshape(tm, PACK_ROWS, 128)) for r in (r0_ref, r1_ref)]
    routed = jnp.concatenate([hv[c] for hv in halves for c in range(PACK_ROWS)], axis=1)
    o_ref[...] = _layer_norm_rows(DEEPNORM_ALPHA * h_ref[...] + routed + shared,
                                  lg_ref[...], lb_ref[...])


def _final(h, racc, wgu, wd, ln_g, ln_b, tm=256):
    n = h.shape[0]
    row = lambda: pl.BlockSpec((tm, D_MODEL), lambda i: (i, 0))
    full = lambda a: pl.BlockSpec(a.shape, lambda i: (0,) * a.ndim)
    acc = lambda p: pl.BlockSpec((None, tm * PACK_ROWS, 128), lambda i: (p, i, 0))
    return pl.pallas_call(
        _final_kernel,
        grid=(n // tm,),
        in_specs=[row(), acc(0), acc(1), full(wgu), full(wd), full(ln_g), full(ln_b)],
        out_specs=row(),
        out_shape=jax.ShapeDtypeStruct((n, D_MODEL), f32),
        compiler_params=_params("parallel"),
        name="shared_final",
    )(h, racc, racc, wgu, wd, ln_g, ln_b)


def _dispatch_plan(e8, pos8, cnt):
    counts = cnt[:, 0].astype(jnp.int32)
    padded = (counts + MOE_ROWS - 1) // MOE_ROWS * MOE_ROWS
    pad_end = jnp.cumsum(padded).astype(jnp.int32)
    pad_start = pad_end - padded
    seg_end = pad_start + counts
    ids = jnp.arange(N_EXPERTS, dtype=jnp.int32)
    start8 = jnp.sum(jnp.where(e8[..., None] == ids, pad_start, 0), axis=-1)
    dest = (start8 + pos8).T.reshape(-1)
    block_start = jnp.arange(MOE_BLOCKS, dtype=jnp.int32) * MOE_ROWS
    block_e = jnp.minimum(jnp.sum((pad_end[None, :] <= block_start[:, None]).astype(jnp.int32), axis=1),
                          N_EXPERTS - 1)
    used = counts > 0
    slot = (jnp.cumsum(used.astype(jnp.int32)) - 1) % 2
    later = jnp.where(used, ids, N_EXPERTS)
    nxt = jnp.concatenate([lax.cummin(later[::-1])[::-1][1:], jnp.full((1,), N_EXPERTS, jnp.int32)])
    nxt = jnp.where(nxt == N_EXPERTS, -1, nxt)
    nb = jnp.concatenate([pad_end[-1:] // MOE_ROWS, seg_end, slot, nxt]).astype(jnp.int32)
    return dest, block_e, nb


def kernel(x, w_in, ssm_log_dt, ssm_a_re, ssm_a_im, ssm_b_re, ssm_b_im, ssm_c_re, ssm_c_im, ssm_d,
           w_glu, g_ssm_out, g_attn_out, w_out, ln1_g, ln1_b, w_router, router_bias, w_gate, w_up,
           w_down, ws_gate, ws_up, ws_down, ln2_g, ln2_b):
    bsz, seq, d = x.shape
    n_tok = bsz * seq
    h = x.reshape(n_tok, d)
    for layer in range(DEPTH):
        proj = _matmul(h, w_in[layer].astype(bf16), f32)

        tables = _s5_tables(ssm_log_dt[layer], ssm_a_re[layer], ssm_a_im[layer], ssm_b_re[layer],
                            ssm_b_im[layer], ssm_c_re[layer], ssm_c_im[layer], ssm_d[layer])
        y = _s5_mixer(proj, tables, bsz, seq // S5_CHUNK)

        y_attn = _dilated_attention(proj, D_SSM // (2 * HEAD_DIM), bsz, seq)

        row2 = lambda a: a.astype(f32).reshape(1, -1)
        h, hp = _mix_out(y, y_attn, h, w_glu[layer].astype(bf16), w_out[layer].astype(bf16),
                         row2(g_ssm_out[layer]), row2(g_attn_out[layer]), row2(ln1_g[layer]), row2(ln1_b[layer]))

        assert n_tok == N_TOKENS
        e8, pos8, wtok, cnt = _router(h, w_router[layer], router_bias[layer])
        dest, block_e, nb = _dispatch_plan(e8, pos8, cnt)
        code = _invert(dest, nb, MOE_BLOCKS * MOE_ROWS)
        mid = _moe_up(block_e, code, nb, hp, wtok, w_gate[layer], w_up[layer])
        racc = _moe_down(block_e, code, nb, mid, w_down[layer])
        wgu = jnp.concatenate([ws_gate[layer], ws_up[layer]], axis=1).astype(bf16)
        h = _final(h, racc, wgu, ws_down[layer].astype(bf16), row2(ln2_g[layer]), row2(ln2_b[layer]))
    return h.reshape(bsz, seq, d)
```

```python
import functools

import jax
import jax.numpy as jnp
import numpy as np
from jax import lax
from jax.experimental import pallas as pl
from jax.experimental.pallas import tpu as pltpu

D_MODEL = 2048
D_SSM = 1024
D_ATTN = 1024
SSM_CH = 16
SSM_GROUPS = 64
SSM_STATE = 64
HEAD_DIM = 64
N_HEADS = 16
PATTERNS = ((128, 1), (512, 4), (2048, 16))
ATTN_BLOCK = 128
N_EXPERTS = 64
TOP_K = 8
N_EXPERT_GROUPS = 8
TOPK_GROUPS = 4
D_EXPERT = 512
ROUTED_SCALE = 2.5
NORM_EPS = 1e-5
DEPTH = 1
DEEPNORM_ALPHA = (2.0 * DEPTH) ** 0.25

S5_CHUNK = 16
S5_GROUPS_PER_BLOCK = 8
MOE_ROWS = 256
MASK_VALUE = -1e30
PACK_ROWS = D_MODEL // 2 // 128
N_TOKENS = 8192
PAD_CODE = N_TOKENS * TOP_K
NB_SLOT = 1 + N_EXPERTS
NB_NEXT = 1 + 2 * N_EXPERTS
MOE_BLOCKS = -(-(N_TOKENS * TOP_K + N_EXPERTS * (MOE_ROWS - 1)) // MOE_ROWS)
VMEM_LIMIT = 56 * 1024 * 1024
MOE_UP_VMEM_LIMIT = 62 * 1024 * 1024

bf16 = jnp.bfloat16
f32 = jnp.float32


def _params(*sem):
    return pltpu.CompilerParams(dimension_semantics=sem, vmem_limit_bytes=VMEM_LIMIT)


MATMUL_COLS = 1024


def _matmul_kernel(a_ref, b_ref, o_ref):
    a = a_ref[...].astype(bf16)
    for j in range(o_ref.shape[1] // MATMUL_COLS):
        cols = slice(j * MATMUL_COLS, (j + 1) * MATMUL_COLS)
        o_ref[:, cols] = jnp.dot(a, b_ref[:, cols], preferred_element_type=f32).astype(o_ref.dtype)


def _matmul(a, b, out_dtype, tm=256):
    m, k = a.shape
    _, n = b.shape
    return pl.pallas_call(
        _matmul_kernel,
        grid=(m // tm,),
        in_specs=[pl.BlockSpec((tm, k), lambda i: (i, 0)),
                  pl.BlockSpec((k, n), lambda i: (0, 0), pipeline_mode=pl.Buffered(1))],
        out_specs=pl.BlockSpec((tm, n), lambda i: (i, 0)),
        out_shape=jax.ShapeDtypeStruct((m, n), out_dtype),
        compiler_params=_params("parallel"),
        name="matmul",
    )(a, b)


def _s5_tables(log_dt, a_re, a_im, b_re, b_im, c_re, c_im, d_skip):
    t = S5_CHUNK
    gpb = S5_GROUPS_PER_BLOCK
    nblk = SSM_GROUPS // gpb
    hp = lax.Precision.HIGHEST
    lr = jnp.minimum(a_re.astype(f32), -1e-4)
    li = a_im.astype(f32)
    dt = jnp.exp(log_dt.astype(f32))
    kk = jnp.arange(t + 1, dtype=f32)[:, None, None]
    mag = jnp.exp(kk * (lr * dt))
    pr = mag * jnp.cos(kk * (li * dt))
    pi = mag * jnp.sin(kk * (li * dt))
    xr, xi = pr[1] - 1.0, pi[1]
    den = lr * lr + li * li
    cr = (xr * lr + xi * li) / den
    ci = (xi * lr - xr * li) / den
    bbr = cr[..., None] * b_re - ci[..., None] * b_im
    bbi = cr[..., None] * b_im + ci[..., None] * b_re
    wr = pr[:t, :, :, None] * bbr - pi[:t, :, :, None] * bbi
    wi = pr[:t, :, :, None] * bbi + pi[:t, :, :, None] * bbr
    taps = (jnp.einsum('gop,tgpc->tgco', c_re, wr, precision=hp)
            - jnp.einsum('gop,tgpc->tgco', c_im, wi, precision=hp))
    ktab = taps.reshape(t, nblk, gpb, SSM_CH, SSM_CH).transpose(1, 0, 3, 2, 4)
    ktab = ktab.reshape(nblk, t, SSM_CH, gpb * SSM_CH)

    rev = jnp.arange(t - 1, -1, -1)
    sr = pr[rev][..., None] * bbr - pi[rev][..., None] * bbi
    si = pr[rev][..., None] * bbi + pi[rev][..., None] * bbr
    sb = jnp.stack([sr, si], axis=0).reshape(2, t, nblk, gpb, SSM_STATE, SSM_CH)
    bsrc = sb.transpose(2, 1, 0, 3, 5, 4).reshape(nblk, t, 2, gpb * SSM_CH, SSM_STATE)
    bsrc = jnp.concatenate([bsrc, bsrc], axis=-1)

    er = c_re[None] * pr[1:, :, None, :] - c_im[None] * pi[1:, :, None, :]
    ei = c_re[None] * pi[1:, :, None, :] + c_im[None] * pr[1:, :, None, :]
    eb = jnp.stack([er, -ei], axis=0).reshape(2, t, nblk, gpb, SSM_CH, SSM_STATE)
    csrc = eb.transpose(2, 1, 0, 5, 3, 4).reshape(nblk, t, 2, SSM_STATE, gpb * SSM_CH)

    a_chunk = jnp.stack([pr[t], pi[t]], axis=0).reshape(2, nblk, 1, gpb * SSM_STATE)
    a_chunk = a_chunk.transpose(1, 0, 2, 3).reshape(nblk, 2, gpb * SSM_STATE)
    dvec = jnp.tile(d_skip.astype(f32).reshape(nblk, 1, gpb * SSM_CH), (1, 1, t))
    return ktab.astype(bf16), bsrc.astype(bf16), csrc.astype(bf16), a_chunk, dvec


def _s5_kernel(u_ref, ktab_ref, bsrc_ref, csrc_ref, a_ref, d_ref, y_ref,
               toep_ref, bpow_ref, cpow_ref, s_ref, h_ref, yt_ref, *, n_batch, n_chunk):
    t = S5_CHUNK
    gpb = S5_GROUPS_PER_BLOCK
    w = gpb * SSM_CH
    ns = gpb * SSM_STATE
    zero = jnp.zeros((), bf16)

    def same_group(shape, row_size, col_size):
        r = lax.broadcasted_iota(jnp.int32, shape, 0) // row_size
        c = lax.broadcasted_iota(jnp.int32, shape, 1) // col_size
        return r == c

    tap_mask = same_group((w, w), SSM_CH, SSM_CH)
    taps = [jnp.where(tap_mask, jnp.tile(ktab_ref[0, tau], (gpb, 1)), zero) for tau in range(t)]
    for tt in range(t):
        for ss in range(tt + 1):
            toep_ref[ss * w:(ss + 1) * w, tt * w:(tt + 1) * w] = taps[tt - ss]
        if tt % 2 == 0:
            toep_ref[(tt + 1) * w:(tt + 2) * w, tt * w:(tt + 1) * w] = jnp.zeros((w, w), bf16)
    b_mask = same_group((w, ns), SSM_CH, SSM_STATE)
    c_mask = same_group((ns, w), SSM_STATE, SSM_CH)
    for ss in range(t):
        for z in range(2):
            bpow_ref[ss * w:(ss + 1) * w, z * ns:(z + 1) * ns] = jnp.where(
                b_mask, jnp.tile(bsrc_ref[0, ss, z], (1, ns // w)), zero)
            cpow_ref[z * ns:(z + 1) * ns, ss * w:(ss + 1) * w] = jnp.where(
                c_mask, jnp.tile(csrc_ref[0, ss, z], (gpb, 1)), zero)

    rows = n_batch * n_chunk
    uf3 = pltpu.einshape("rsl->srl", u_ref[...].reshape(rows, t, w))
    uf = [uf3[s] for s in range(t)]
    u = jnp.concatenate([p.astype(bf16) for p in uf], axis=1)
    s_ref[...] = jnp.dot(u, bpow_ref[...], preferred_element_type=f32)

    ar = a_ref[0, 0:1, :]
    ai = a_ref[0, 1:2, :]

    def step(j, carry):
        new = []
        for b in range(n_batch):
            hr, hi = carry[2 * b], carry[2 * b + 1]
            row = b * n_chunk + j
            h_ref[pl.ds(row, 1), 0:ns] = hr
            h_ref[pl.ds(row, 1), ns:2 * ns] = hi
            sr = s_ref[pl.ds(row, 1), 0:ns]
            si = s_ref[pl.ds(row, 1), ns:2 * ns]
            new.append(ar * hr - ai * hi + sr)
            new.append(ar * hi + ai * hr + si)
        return tuple(new)

    zero = jnp.zeros((1, ns), f32)
    lax.fori_loop(0, n_chunk, step, (zero,) * (2 * n_batch))

    hprev = h_ref[...].astype(bf16)
    for tp in range(t // 2):
        c0, c1 = 2 * tp * w, (2 * tp + 2) * w
        y = jnp.dot(u[:, :c1], toep_ref[0:c1, c0:c1], preferred_element_type=f32)
        y = y + jnp.dot(hprev, cpow_ref[:, c0:c1], preferred_element_type=f32)
        for k, tt in enumerate((2 * tp, 2 * tp + 1)):
            yk = y[:, k * w:(k + 1) * w] + d_ref[0, :, tt * w:(tt + 1) * w] * uf[tt]
            yt_ref[tt] = jax.nn.gelu(yk, approximate=True)
    y_ref[...] = pltpu.einshape("srl->rsl", yt_ref[...]).reshape(rows * t, w)


def _s5_mixer(proj, tables, n_batch, n_chunk):
    ktab, bsrc, csrc, a_chunk, dvec = tables
    nblk = ktab.shape[0]
    w = S5_GROUPS_PER_BLOCK * SSM_CH
    cols = S5_CHUNK * w
    rows = n_batch * n_chunk
    n_tok = rows * S5_CHUNK
    ns = S5_GROUPS_PER_BLOCK * SSM_STATE
    kern = functools.partial(_s5_kernel, n_batch=n_batch, n_chunk=n_chunk)
    return pl.pallas_call(
        kern,
        grid=(nblk,),
        in_specs=[pl.BlockSpec((n_tok, w), lambda g: (0, g)),
                  pl.BlockSpec((1,) + ktab.shape[1:], lambda g: (g, 0, 0, 0)),
                  pl.BlockSpec((1,) + bsrc.shape[1:], lambda g: (g, 0, 0, 0, 0)),
                  pl.BlockSpec((1,) + csrc.shape[1:], lambda g: (g, 0, 0, 0, 0)),
                  pl.BlockSpec((1, 2, ns), lambda g: (g, 0, 0)),
                  pl.BlockSpec((1, 1, cols), lambda g: (g, 0, 0))],
        out_specs=pl.BlockSpec((n_tok, w), lambda g: (0, g)),
        out_shape=jax.ShapeDtypeStruct((n_tok, D_SSM), f32),
        scratch_shapes=[pltpu.VMEM((cols, cols), bf16),
                        pltpu.VMEM((cols, 2 * ns), bf16),
                        pltpu.VMEM((2 * ns, cols), bf16),
                        pltpu.VMEM((rows, 2 * ns), f32),
                        pltpu.VMEM((rows, 2 * ns), f32),
                        pltpu.VMEM((S5_CHUNK, rows, w), f32)],
        compiler_params=_params("parallel"),
        name="s5_mixer",
    )(proj, ktab, bsrc, csrc, a_chunk, dvec)


def _attn_bias_table():
    blk = ATTN_BLOCK
    slopes = 2.0 ** (-8.0 * jnp.arange(1, N_HEADS + 1, dtype=f32) / N_HEADS)
    delta = np.arange(blk)[:, None] - (np.arange(2 * blk)[None, :] - blk)
    tabs = []
    for window, dil in PATTERNS:
        assert window // dil == blk
        valid = (delta >= 0) & (delta <= window // dil)
        dist = jnp.asarray(delta * dil, dtype=f32)
        bias = jnp.where(valid[None], -slopes[:, None, None] * dist[None], MASK_VALUE)
        tabs.append(bias.reshape(N_HEADS // 2, 2 * blk, 2 * blk))
    return jnp.stack(tabs, axis=1)


def _attn_kernel(q_ref, k_ref, v_ref, bias_ref, o_ref, *scr):
    blk = ATTN_BLOCK
    seq = q_ref.shape[0]
    first_head = lax.broadcasted_iota(jnp.int32, (blk, 2 * HEAD_DIM), 1) < HEAD_DIM
    dims = (((1,), (1,)), ((), ()))
    for pi, (_, dil) in enumerate(PATTERNS):
        o_scr, l_scr = scr[2 * pi], scr[2 * pi + 1]
        sub = seq // dil
        for r in range(dil):
            rows = (lambda st, n: pl.ds(st, n)) if dil == 1 else (lambda st, n: pl.ds(st, n, stride=dil))
            qd = (q_ref[rows(r, sub), :] * HEAD_DIM ** -0.5).astype(bf16)
            kd = k_ref[rows(r, sub), :].astype(bf16)
            vd = v_ref[rows(r, sub), :].astype(bf16)
            for i in range(sub // blk):
                qb = qd[i * blk:(i + 1) * blk]
                zero = jnp.zeros_like(qb)
                q2 = jnp.concatenate([jnp.where(first_head, qb, zero), jnp.where(first_head, zero, qb)], axis=0)
                k0 = max(i - 1, 0) * blk
                nk = (i + 1) * blk - k0
                s = lax.dot_general(q2, kd[k0:k0 + nk], dims, preferred_element_type=f32)
                s = s + bias_ref[0, pi, :, 2 * blk - nk:]
                m = jnp.max(s, axis=-1, keepdims=True)
                p = jnp.exp(s - m)
                l = jnp.sum(p, axis=-1, keepdims=True)
                o = jnp.dot(p.astype(bf16), vd[k0:k0 + nk], preferred_element_type=f32) / l
                lse = m + jnp.log(l)
                dst = rows(r + dil * blk * i, blk)
                o_scr[dst, :] = jnp.where(first_head, o[:blk], o[blk:])
                l_scr[dst, :] = jnp.where(first_head, lse[:blk], lse[blk:])
    l1, l2, l3 = scr[1][...], scr[3][...], scr[5][...]
    m = jnp.maximum(jnp.maximum(l1, l2), l3)
    e1, e2, e3 = jnp.exp(l1 - m), jnp.exp(l2 - m), jnp.exp(l3 - m)
    o_ref[...] = (e1 * scr[0][...] + e2 * scr[2][...] + e3 * scr[4][...]) / (e1 + e2 + e3)


def _dilated_attention(qkv, first, bsz, seq):
    pairs = N_HEADS // 2
    width = 2 * HEAD_DIM
    bias = _attn_bias_table()
    col = lambda off: pl.BlockSpec((seq, width), lambda hp, b: (b, first + off + hp))
    return pl.pallas_call(
        _attn_kernel,
        grid=(pairs, bsz),
        in_specs=[col(0), col(pairs), col(2 * pairs),
                  pl.BlockSpec((1,) + bias.shape[1:], lambda hp, b: (hp, 0, 0, 0))],
        out_specs=pl.BlockSpec((seq, width), lambda hp, b: (b, hp)),
        out_shape=jax.ShapeDtypeStruct((bsz * seq, D_ATTN), f32),
        scratch_shapes=[pltpu.VMEM((seq, width), f32)] * (2 * len(PATTERNS)),
        compiler_params=_params("parallel", "parallel"),
        name="dilated_attention",
    )(qkv, qkv, qkv, bias)


def _layer_norm_rows(x, g, b):
    mu = jnp.mean(x, axis=-1, keepdims=True)
    xc = x - mu
    var = jnp.mean(xc * xc, axis=-1, keepdims=True)
    return xc * lax.rsqrt(var + NORM_EPS) * g + b


def _rms_rows(x, g):
    return x * lax.rsqrt(jnp.mean(x * x, axis=-1, keepdims=True) + NORM_EPS) * g


def _mix_out_kernel(y_ref, ya_ref, x_ref, wglu_ref, w_ref, gs_ref, ga_ref, lg_ref, lb_ref, h_ref, hp_ref):
    z = jnp.dot(y_ref[...].astype(bf16), wglu_ref[...], preferred_element_type=f32)
    y_ssm = z[:, :D_SSM] * jax.nn.sigmoid(z[:, D_SSM:])
    ns = _rms_rows(y_ssm, gs_ref[...]).astype(bf16)
    na = _rms_rows(ya_ref[...], ga_ref[...]).astype(bf16)
    proj = jnp.dot(ns, w_ref[0:D_SSM, :], preferred_element_type=f32)
    proj = proj + jnp.dot(na, w_ref[D_SSM:, :], preferred_element_type=f32)
    h = _layer_norm_rows(DEEPNORM_ALPHA * x_ref[...] + proj, lg_ref[...], lb_ref[...])
    h_ref[...] = h
    half = D_MODEL // 2
    word = pltpu.pack_elementwise([h[:, :half], h[:, half:]], packed_dtype=bf16)
    tm = word.shape[0]
    chunks = jnp.stack([word[:, s * 128:(s + 1) * 128] for s in range(PACK_ROWS)], axis=0)
    hp_ref[...] = pltpu.einshape("srl->rsl", chunks).reshape(tm * PACK_ROWS, 128)


def _mix_out(y, y_attn, x, w_glu, w_out, g_ssm, g_attn, ln_g, ln_b, tm=256):
    n = x.shape[0]
    row = lambda c: pl.BlockSpec((tm, c), lambda i: (i, 0))
    full = lambda a: pl.BlockSpec(a.shape, lambda i: (0,) * a.ndim)
    return pl.pallas_call(
        _mix_out_kernel,
        grid=(n // tm,),
        in_specs=[row(D_SSM), row(D_ATTN), row(D_MODEL), full(w_glu), full(w_out), full(g_ssm), full(g_attn),
                  full(ln_g), full(ln_b)],
        out_specs=[row(D_MODEL), pl.BlockSpec((tm * PACK_ROWS, 128), lambda i: (i, 0))],
        out_shape=[jax.ShapeDtypeStruct((n, D_MODEL), f32),
                   jax.ShapeDtypeStruct((n * PACK_ROWS, 128), jnp.int32)],
        compiler_params=_params("parallel"),
        name="mix_out",
    )(y, y_attn, x, w_glu, w_out, g_ssm, g_attn, ln_g, ln_b)


def _router_kernel(h_ref, wrt_ref, bias_ref, tri_ref, trie_ref, e8_ref, pos8_ref, wtok_ref, cnt_ref):
    gsz = N_EXPERTS // N_EXPERT_GROUPS
    tm = h_ref.shape[0]
    ninf = -jnp.inf

    @pl.when(pl.program_id(0) == 0)
    def _():
        cnt_ref[...] = jnp.zeros_like(cnt_ref)

    logits = lax.dot_general(wrt_ref[...], h_ref[...], (((1,), (1,)), ((), ())),
                             preferred_element_type=f32, precision=lax.Precision.HIGHEST)
    scores = jax.nn.sigmoid(logits)
    sel = scores + bias_ref[...]
    io = lax.broadcasted_iota(jnp.int32, (gsz, tm), 0)

    blks, gs_rows = [], []
    for g in range(N_EXPERT_GROUPS):
        blk = sel[g * gsz:(g + 1) * gsz, :]
        m1 = jnp.max(blk, axis=0, keepdims=True)
        first = jnp.min(jnp.where(blk == m1, io, gsz), axis=0, keepdims=True)
        m2 = jnp.max(jnp.where(io == first, ninf, blk), axis=0, keepdims=True)
        blks.append(blk)
        gs_rows.append(m1 + m2)
    gs = jnp.concatenate(gs_rows, axis=0)

    iog = lax.broadcasted_iota(jnp.int32, (N_EXPERT_GROUPS, tm), 0)
    beaten = jnp.zeros((N_EXPERT_GROUPS, tm), f32)
    for gp in range(N_EXPERT_GROUPS):
        row = gs_rows[gp]
        tie = jnp.where(iog > gp, 1.0, 0.0)
        beaten = beaten + jnp.where(row > gs, 1.0, jnp.where(row == gs, tie, 0.0))
    keep = beaten < TOPK_GROUPS
    masked = [jnp.where(keep[g:g + 1, :], blks[g], ninf) for g in range(N_EXPERT_GROUPS)]

    ranks = [jnp.zeros((gsz, tm), f32) for _ in range(N_EXPERT_GROUPS)]
    tie_in = [jnp.where(io > j, 1.0, 0.0) for j in range(gsz)]
    for gp in range(N_EXPERT_GROUPS):
        for j in range(gsz):
            row = masked[gp][j:j + 1, :]
            for g in range(N_EXPERT_GROUPS):
                if gp < g:
                    inc = jnp.where(row >= masked[g], 1.0, 0.0)
                elif gp > g:
                    inc = jnp.where(row > masked[g], 1.0, 0.0)
                else:
                    inc = jnp.where(row > masked[g], 1.0, jnp.where(row == masked[g], tie_in[j], 0.0))
                ranks[g] = ranks[g] + inc
    selb = jnp.concatenate([jnp.where(r < TOP_K, 1.0, 0.0) for r in ranks], axis=0)
    wsel = selb * scores
    wn = wsel / jnp.sum(wsel, axis=0, keepdims=True) * ROUTED_SCALE

    maskb = selb.astype(bf16)
    pos = jnp.dot(maskb, tri_ref[...], preferred_element_type=f32) + cnt_ref[:, 0:1]
    cnt_ref[...] = cnt_ref[...] + jnp.sum(selb, axis=1, keepdims=True)
    slot = jnp.dot(trie_ref[...], maskb, preferred_element_type=f32)
    ioe = lax.broadcasted_iota(jnp.int32, (N_EXPERTS, tm), 0).astype(f32)
    e_rows, p_rows = [], []
    for k in range(TOP_K):
        hit = jnp.where(slot == k, selb, 0.0)
        e_rows.append(jnp.sum(hit * ioe, axis=0, keepdims=True))
        p_rows.append(jnp.sum(hit * pos, axis=0, keepdims=True))
    e8_ref[...] = jnp.concatenate(e_rows, axis=0).astype(jnp.int32)
    pos8_ref[...] = jnp.concatenate(p_rows, axis=0).astype(jnp.int32)
    wtok_ref[...] = jnp.concatenate([wn.T, jnp.zeros((tm, 128 - N_EXPERTS), f32)], axis=1)


def _router(h, w_router, router_bias, tm=512):
    n = h.shape[0]
    wrt = w_router.astype(f32).T
    bias = router_bias.astype(f32).reshape(N_EXPERTS, 1)
    tri = (jnp.arange(tm)[:, None] < jnp.arange(tm)[None, :]).astype(bf16)
    trie = (jnp.arange(N_EXPERTS)[None, :] < jnp.arange(N_EXPERTS)[:, None]).astype(bf16)
    full = lambda a: pl.BlockSpec(a.shape, lambda i: (0,) * a.ndim)
    tok = lambda: pl.BlockSpec((TOP_K, tm), lambda i: (0, i))
    return pl.pallas_call(
        _router_kernel,
        grid=(n // tm,),
        in_specs=[pl.BlockSpec((tm, D_MODEL), lambda i: (i, 0)), full(wrt), full(bias), full(tri), full(trie)],
        out_specs=[tok(), tok(), pl.BlockSpec((tm, 128), lambda i: (i, 0)),
                   pl.BlockSpec((N_EXPERTS, 128), lambda i: (0, 0))],
        out_shape=[jax.ShapeDtypeStruct((TOP_K, n), jnp.int32), jax.ShapeDtypeStruct((TOP_K, n), jnp.int32),
                   jax.ShapeDtypeStruct((n, 128), f32), jax.ShapeDtypeStruct((N_EXPERTS, 128), f32)],
        compiler_params=_params("arbitrary"),
        name="router",
    )(h, wrt, bias, tri, trie)


INVERT_UNROLL = 16


def _invert_kernel(dest_ref, nb_ref, code_ref):
    n_rows = code_ref.shape[0]

    def fill(first_group, last_group):
        def body(k, c):
            for u in range(INVERT_UNROLL):
                code_ref[k * INVERT_UNROLL + u] = PAD_CODE
            return c
        lax.fori_loop(first_group, last_group, body, 0)

    def per_expert(e, c):
        groups = MOE_ROWS // INVERT_UNROLL + 1
        start = jnp.minimum(nb_ref[1 + e], n_rows - groups * INVERT_UNROLL) // INVERT_UNROLL
        fill(start, start + groups)
        return c
    lax.fori_loop(0, N_EXPERTS, per_expert, 0)
    fill(nb_ref[0] * (MOE_ROWS // INVERT_UNROLL), n_rows // INVERT_UNROLL)

    def body(p, c):
        code_ref[dest_ref[p]] = p
        return c
    lax.fori_loop(0, dest_ref.shape[0], body, 0, unroll=INVERT_UNROLL)


def _invert(dest, nb, n_rows):
    smem = lambda: pl.BlockSpec(memory_space=pltpu.SMEM)
    return pl.pallas_call(
        _invert_kernel,
        in_specs=[smem(), smem()],
        out_specs=smem(),
        out_shape=jax.ShapeDtypeStruct((n_rows,), jnp.int32),
        name="invert_dispatch",
    )(dest, nb)


def _expert_changed(be_ref, i):
    return jnp.logical_or(i == 0, be_ref[i] != be_ref[jnp.maximum(i - 1, 0)])


SCATTER_UNROLL = 8


def _moe_up_kernel(be_ref, code_ref, nb_ref, hp_ref, wtok_ref, wg_ref, wu_ref, mid_ref,
                   xg_scr, xa_scr, xb_scr, wra_scr, wrb_scr, wgu_scr, wstage_scr, wsem):
    i = pl.program_id(0)
    rows = MOE_ROWS

    def gather_block(blk, x_dst, wrow_dst):
        base = blk * rows
        for r in range(rows):
            tok = (code_ref[base + r] >> 3) & (N_TOKENS - 1)
            xg_scr[r * PACK_ROWS:(r + 1) * PACK_ROWS, :] = (
                hp_ref[pl.ds(pl.multiple_of(tok * PACK_ROWS, PACK_ROWS), PACK_ROWS), :])
            wrow_dst[r:r + 1, :] = wtok_ref[pl.ds(tok, 1), :]
        half = D_MODEL // 2
        chunks = pltpu.einshape("rsl->srl", xg_scr[...].reshape(rows, PACK_ROWS, 128))
        for s in range(PACK_ROWS):
            wds = chunks[s]
            lo = pltpu.unpack_elementwise(wds, index=0, packed_dtype=bf16, unpacked_dtype=f32)
            hi = pltpu.unpack_elementwise(wds, index=1, packed_dtype=bf16, unpacked_dtype=f32)
            x_dst[:, s * 128:(s + 1) * 128] = lo.astype(bf16)
            x_dst[:, half + s * 128:half + (s + 1) * 128] = hi.astype(bf16)

    def expert_block(x_cur, wrow_cur, x_nxt, wrow_nxt):
        gather_block(jnp.minimum(i + 1, MOE_BLOCKS - 1), x_nxt, wrow_nxt)
        gu = jnp.dot(x_cur[...], wgu_scr[...], preferred_element_type=f32)
        g, u = gu[:, :D_EXPERT], gu[:, D_EXPERT:]
        lane = lax.broadcasted_iota(jnp.int32, (rows, 128), 1)
        w = jnp.sum(jnp.where(lane == be_ref[i], wrow_cur[...], 0.0), axis=1, keepdims=True)
        real = i * rows + lax.broadcasted_iota(jnp.int32, (rows, 1), 0) < nb_ref[1 + be_ref[i]]
        mid_ref[...] = (g * jax.nn.sigmoid(g) * u * jnp.where(real, w, 0.0)).astype(bf16)

    @pl.when(i == 0)
    def _():
        gather_block(0, xa_scr, wra_scr)

    expert = be_ref[i]
    slot = nb_ref[NB_SLOT + expert]
    upcoming = nb_ref[NB_NEXT + expert]

    def weight_copies(e, s):
        return (pltpu.make_async_copy(wg_ref.at[e], wstage_scr.at[s, 0], wsem.at[s, 0]),
                pltpu.make_async_copy(wu_ref.at[e], wstage_scr.at[s, 1], wsem.at[s, 1]))

    @pl.when(jnp.logical_and(i == 0, nb_ref[0] > 0))
    def _():
        for cp in weight_copies(expert, slot):
            cp.start()

    @pl.when(i < nb_ref[0])
    def _():
        @pl.when(_expert_changed(be_ref, i))
        def _():
            for cp in weight_copies(expert, slot):
                cp.wait()

            @pl.when(upcoming >= 0)
            def _():
                for cp in weight_copies(upcoming, 1 - slot):
                    cp.start()

            wgu_scr[:, :D_EXPERT] = wstage_scr[slot, 0].astype(bf16)
            wgu_scr[:, D_EXPERT:] = wstage_scr[slot, 1].astype(bf16)

        @pl.when(i % 2 == 0)
        def _():
            expert_block(xa_scr, wra_scr, xb_scr, wrb_scr)

        @pl.when(i % 2 == 1)
        def _():
            expert_block(xb_scr, wrb_scr, xa_scr, wra_scr)

    @pl.when(i >= nb_ref[0])
    def _():
        mid_ref[...] = jnp.zeros_like(mid_ref)


def _moe_up(block_e, code, nb, hp, wtok, w_gate, w_up):
    grid_spec = pltpu.PrefetchScalarGridSpec(
        num_scalar_prefetch=3,
        grid=(MOE_BLOCKS,),
        in_specs=[pl.BlockSpec(memory_space=pltpu.VMEM),
                  pl.BlockSpec(memory_space=pltpu.VMEM),
                  pl.BlockSpec(memory_space=pl.ANY),
                  pl.BlockSpec(memory_space=pl.ANY)],
        out_specs=pl.BlockSpec((MOE_ROWS, D_EXPERT), lambda i, be, cd, nb: (i, 0)),
        scratch_shapes=[pltpu.VMEM((PACK_ROWS * MOE_ROWS, 128), jnp.int32),
                        pltpu.VMEM((MOE_ROWS, D_MODEL), bf16), pltpu.VMEM((MOE_ROWS, D_MODEL), bf16),
                        pltpu.VMEM((MOE_ROWS, 128), f32), pltpu.VMEM((MOE_ROWS, 128), f32),
                        pltpu.VMEM((D_MODEL, 2 * D_EXPERT), bf16),
                        pltpu.VMEM((2, 2, D_MODEL, D_EXPERT), f32),
                        pltpu.SemaphoreType.DMA((2, 2))])
    return pl.pallas_call(
        _moe_up_kernel,
        grid_spec=grid_spec,
        out_shape=jax.ShapeDtypeStruct((MOE_BLOCKS * MOE_ROWS, D_EXPERT), bf16),
        compiler_params=pltpu.CompilerParams(dimension_semantics=("arbitrary",),
                                             vmem_limit_bytes=MOE_UP_VMEM_LIMIT),
        name="moe_up",
    )(block_e, code, nb, hp, wtok, w_gate, w_up)


def _moe_down_kernel(be_ref, code_ref, nb_ref, mida_ref, midb_ref, wd_ref, acc_ref,
                     ya_scr, yb0_scr, yb1_scr, wa_scr, wb_scr, wstage_scr, wsem):
    j = pl.program_id(1)
    rows = MOE_ROWS
    nb = nb_ref[0]
    blk_a, blk_b = 2 * j, 2 * j + 1
    last = MOE_BLOCKS - 1

    @pl.when(j == 0)
    def _():
        acc_ref[...] = jnp.zeros_like(acc_ref)

    def scatter_rows(base, ybuf, r0, n):
        sums, addrs = [], []
        for k in range(n):
            a = pl.multiple_of(code_ref[base + r0 + k] & -PACK_ROWS, PACK_ROWS)
            v = ybuf[pl.ds(pl.multiple_of((r0 + k) * PACK_ROWS, PACK_ROWS), PACK_ROWS), :]
            sums.append(acc_ref[pl.ds(a, PACK_ROWS), :] + v)
            addrs.append(a)
        for k in range(n):
            acc_ref[pl.ds(addrs[k], PACK_ROWS), :] = sums[k]

    def scatter_block(blk, ybuf):
        for g in range(rows // SCATTER_UNROLL):
            scatter_rows(blk * rows, ybuf, g * SCATTER_UNROLL, SCATTER_UNROLL)

    def scatter_block_compact(blk, ybuf):
        def body(g, c):
            scatter_rows(blk * rows, ybuf, g * SCATTER_UNROLL, SCATTER_UNROLL)
            return c
        lax.fori_loop(0, rows // SCATTER_UNROLL, body, 0)

    def down_block(mid_ref, w_scr, ybuf):
        y = jnp.dot(mid_ref[...], w_scr[...], preferred_element_type=f32)
        y3 = jnp.stack([y[:, c * 128:(c + 1) * 128] for c in range(PACK_ROWS)], axis=0)
        ybuf[...] = pltpu.einshape("crl->rcl", y3).reshape(rows * PACK_ROWS, 128)

    half = D_MODEL // 2
    cols = pl.ds(pl.multiple_of(pl.program_id(0) * half, half), half)

    def weight_copy(e, s):
        return pltpu.make_async_copy(wd_ref.at[e, :, cols], wstage_scr.at[s], wsem.at[s])

    def refresh_weights(blk, w_scr):
        cur = be_ref[jnp.minimum(blk, last)]
        slot = nb_ref[NB_SLOT + cur]
        upcoming = nb_ref[NB_NEXT + cur]
        valid = blk < nb

        @pl.when(jnp.logical_and(valid, blk == 0))
        def _():
            weight_copy(cur, slot).start()

        @pl.when(jnp.logical_and(valid, _expert_changed(be_ref, jnp.minimum(blk, last))))
        def _():
            weight_copy(cur, slot).wait()

            @pl.when(upcoming >= 0)
            def _():
                weight_copy(upcoming, 1 - slot).start()

        prev = be_ref[jnp.clip(blk - 2, 0, last)]

        @pl.when(jnp.logical_and(valid, jnp.logical_or(j == 0, cur != prev)))
        def _():
            w_scr[...] = wstage_scr[slot].astype(bf16)

    refresh_weights(blk_a, wa_scr)
    refresh_weights(blk_b, wb_scr)

    for parity, yb_this, yb_prev in ((0, yb0_scr, yb1_scr), (1, yb1_scr, yb0_scr)):
        mine = j % 2 == parity

        @pl.when(jnp.logical_and(mine, jnp.logical_and(j > 0, blk_b < nb)))
        def _():
            down_block(mida_ref, wa_scr, ya_scr)
            down_block(midb_ref, wb_scr, yb_this)
            scatter_block(blk_a - 1, yb_prev)
            scatter_block(blk_a, ya_scr)

        @pl.when(jnp.logical_and(mine, jnp.logical_and(j > 0, blk_b == nb)))
        def _():
            down_block(mida_ref, wa_scr, ya_scr)
            scatter_block_compact(blk_a - 1, yb_prev)
            scatter_block_compact(blk_a, ya_scr)

        @pl.when(jnp.logical_and(mine, jnp.logical_and(j > 0, blk_a == nb)))
        def _():
            scatter_block_compact(blk_a - 1, yb_prev)

    @pl.when(jnp.logical_and(j == 0, blk_a < nb))
    def _():
        down_block(mida_ref, wa_scr, ya_scr)
        scatter_block_compact(blk_a, ya_scr)

    @pl.when(jnp.logical_and(j == 0, blk_b < nb))
    def _():
        down_block(midb_ref, wb_scr, yb0_scr)


def _moe_down(block_e, code, nb, mid, w_down):
    half = D_MODEL // 2
    acc_rows = (N_TOKENS + 1) * PACK_ROWS
    last = MOE_BLOCKS - 1
    assert MOE_BLOCKS % 2 == 0
    blk = lambda off: (lambda p, j, be, cd, nb: (jnp.minimum(2 * j + off, last), 0))
    grid_spec = pltpu.PrefetchScalarGridSpec(
        num_scalar_prefetch=3,
        grid=(2, MOE_BLOCKS // 2 + 1),
        in_specs=[pl.BlockSpec((MOE_ROWS, D_EXPERT), blk(0)), pl.BlockSpec((MOE_ROWS, D_EXPERT), blk(1)),
                  pl.BlockSpec(memory_space=pl.ANY)],
        out_specs=pl.BlockSpec((None, acc_rows, 128), lambda p, j, be, cd, nb: (p, 0, 0),
                               pipeline_mode=pl.Buffered(1)),
        scratch_shapes=[pltpu.VMEM((PACK_ROWS * MOE_ROWS, 128), f32)] * 3
        + [pltpu.VMEM((D_EXPERT, half), bf16)] * 2
        + [pltpu.VMEM((2, D_EXPERT, half), f32), pltpu.SemaphoreType.DMA((2,))])
    return pl.pallas_call(
        _moe_down_kernel,
        grid_spec=grid_spec,
        out_shape=jax.ShapeDtypeStruct((2, acc_rows, 128), f32),
        compiler_params=_params("arbitrary", "arbitrary"),
        name="moe_down",
    )(block_e, code, nb, mid, mid, w_down)


def _final_kernel(h_ref, r0_ref, r1_ref, wgu_ref, wd_ref, lg_ref, lb_ref, o_ref):
    tm = h_ref.shape[0]
    gu = jnp.dot(h_ref[...].astype(bf16), wgu_ref[...], preferred_element_type=f32)
    g, u = gu[:, :D_EXPERT], gu[:, D_EXPERT:]
    mid = (g * jax.nn.sigmoid(g) * u).astype(bf16)
    shared = jnp.dot(mid, wd_ref[...], preferred_element_type=f32)
    halves = [pltpu.einshape("rcl->crl", r[...].reshape(tm, PACK_ROWS, 128)) for r in (r0_ref, r1_ref)]
    routed = jnp.concatenate([hv[c] for hv in halves for c in range(PACK_ROWS)], axis=1)
    o_ref[...] = _layer_norm_rows(DEEPNORM_ALPHA * h_ref[...] + routed + shared,
                                  lg_ref[...], lb_ref[...])


def _final(h, racc, wgu, wd, ln_g, ln_b, tm=256):
    n = h.shape[0]
    row = lambda: pl.BlockSpec((tm, D_MODEL), lambda i: (i, 0))
    full = lambda a: pl.BlockSpec(a.shape, lambda i: (0,) * a.ndim)
    acc = lambda p: pl.BlockSpec((None, tm * PACK_ROWS, 128), lambda i: (p, i, 0))
    return pl.pallas_call(
        _final_kernel,
        grid=(n // tm,),
        in_specs=[row(), acc(0), acc(1), full(wgu), full(wd), full(ln_g), full(ln_b)],
        out_specs=row(),
        out_shape=jax.ShapeDtypeStruct((n, D_MODEL), f32),
        compiler_params=_params("parallel"),
        name="shared_final",
    )(h, racc, racc, wgu, wd, ln_g, ln_b)


def _dispatch_plan(e8, pos8, cnt):
    counts = cnt[:, 0].astype(jnp.int32)
    padded = (counts + MOE_ROWS - 1) // MOE_ROWS * MOE_ROWS
    pad_end = jnp.cumsum(padded).astype(jnp.int32)
    pad_start = pad_end - padded
    seg_end = pad_start + counts
    ids = jnp.arange(N_EXPERTS, dtype=jnp.int32)
    start8 = jnp.sum(jnp.where(e8[..., None] == ids, pad_start, 0), axis=-1)
    dest = (start8 + pos8).T.reshape(-1)
    block_start = jnp.arange(MOE_BLOCKS, dtype=jnp.int32) * MOE_ROWS
    block_e = jnp.minimum(jnp.sum((pad_end[None, :] <= block_start[:, None]).astype(jnp.int32), axis=1),
                          N_EXPERTS - 1)
    used = counts > 0
    slot = (jnp.cumsum(used.astype(jnp.int32)) - 1) % 2
    later = jnp.where(used, ids, N_EXPERTS)
    nxt = jnp.concatenate([lax.cummin(later[::-1])[::-1][1:], jnp.full((1,), N_EXPERTS, jnp.int32)])
    nxt = jnp.where(nxt == N_EXPERTS, -1, nxt)
    nb = jnp.concatenate([pad_end[-1:] // MOE_ROWS, seg_end, slot, nxt]).astype(jnp.int32)
    return dest, block_e, nb


def kernel(x, w_in, ssm_log_dt, ssm_a_re, ssm_a_im, ssm_b_re, ssm_b_im, ssm_c_re, ssm_c_im, ssm_d,
           w_glu, g_ssm_out, g_attn_out, w_out, ln1_g, ln1_b, w_router, router_bias, w_gate, w_up,
           w_down, ws_gate, ws_up, ws_down, ln2_g, ln2_b):
    bsz, seq, d = x.shape
    n_tok = bsz * seq
    h = x.reshape(n_tok, d)
    for layer in range(DEPTH):
        proj = _matmul(h, w_in[layer].astype(bf16), f32)

        tables = _s5_tables(ssm_log_dt[layer], ssm_a_re[layer], ssm_a_im[layer], ssm_b_re[layer],
                            ssm_b_im[layer], ssm_c_re[layer], ssm_c_im[layer], ssm_d[layer])
        y = _s5_mixer(proj, tables, bsz, seq // S5_CHUNK)

        y_attn = _dilated_attention(proj, D_SSM // (2 * HEAD_DIM), bsz, seq)

        row2 = lambda a: a.astype(f32).reshape(1, -1)
        h, hp = _mix_out(y, y_attn, h, w_glu[layer].astype(bf16), w_out[layer].astype(bf16),
                         row2(g_ssm_out[layer]), row2(g_attn_out[layer]), row2(ln1_g[layer]), row2(ln1_b[layer]))

        assert n_tok == N_TOKENS
        e8, pos8, wtok, cnt = _router(h, w_router[layer], router_bias[layer])
        dest, block_e, nb = _dispatch_plan(e8, pos8, cnt)
        code = _invert(dest, nb, MOE_BLOCKS * MOE_ROWS)
        mid = _moe_up(block_e, code, nb, hp, wtok, w_gate[layer], w_up[layer])
        racc = _moe_down(block_e, code, nb, mid, w_down[layer])
        wgu = jnp.concatenate([ws_gate[layer], ws_up[layer]], axis=1).astype(bf16)
        h = _final(h, racc, wgu, ws_down[layer].astype(bf16), row2(ln2_g[layer]), row2(ln2_b[layer]))
    return h.reshape(bsz, seq, d)
```

```python
import functools

import jax
import jax.numpy as jnp
import numpy as np
from jax import lax
from jax.experimental import pallas as pl
from jax.experimental.pallas import tpu as pltpu

D_MODEL = 2048
D_SSM = 1024
D_ATTN = 1024
SSM_CH = 16
SSM_GROUPS = 64
SSM_STATE = 64
HEAD_DIM = 64
N_HEADS = 16
PATTERNS = ((128, 1), (512, 4), (2048, 16))
ATTN_BLOCK = 128
N_EXPERTS = 64
TOP_K = 8
N_EXPERT_GROUPS = 8
TOPK_GROUPS = 4
D_EXPERT = 512
ROUTED_SCALE = 2.5
NORM_EPS = 1e-5
DEPTH = 1
DEEPNORM_ALPHA = (2.0 * DEPTH) ** 0.25

S5_CHUNK = 16
S5_GROUPS_PER_BLOCK = 8
MOE_ROWS = 256
MASK_VALUE = -1e30
PACK_ROWS = D_MODEL // 2 // 128
N_TOKENS = 8192
PAD_CODE = N_TOKENS * TOP_K
NB_SLOT = 1 + N_EXPERTS
NB_NEXT = 1 + 2 * N_EXPERTS
MOE_BLOCKS = -(-(N_TOKENS * TOP_K + N_EXPERTS * (MOE_ROWS - 1)) // MOE_ROWS)
VMEM_LIMIT = 56 * 1024 * 1024
MOE_UP_VMEM_LIMIT = 62 * 1024 * 1024

bf16 = jnp.bfloat16
f32 = jnp.float32


def _params(*sem):
    return pltpu.CompilerParams(dimension_semantics=sem, vmem_limit_bytes=VMEM_LIMIT)


MATMUL_COLS = 1024


def _matmul_kernel(a_ref, b_ref, o_ref):
    a = a_ref[...].astype(bf16)
    for j in range(o_ref.shape[1] // MATMUL_COLS):
        cols = slice(j * MATMUL_COLS, (j + 1) * MATMUL_COLS)
        o_ref[:, cols] = jnp.dot(a, b_ref[:, cols], preferred_element_type=f32).astype(o_ref.dtype)


def _matmul(a, b, out_dtype, tm=256):
    m, k = a.shape
    _, n = b.shape
    return pl.pallas_call(
        _matmul_kernel,
        grid=(m // tm,),
        in_specs=[pl.BlockSpec((tm, k), lambda i: (i, 0)),
                  pl.BlockSpec((k, n), lambda i: (0, 0), pipeline_mode=pl.Buffered(1))],
        out_specs=pl.BlockSpec((tm, n), lambda i: (i, 0)),
        out_shape=jax.ShapeDtypeStruct((m, n), out_dtype),
        compiler_params=_params("parallel"),
        name="matmul",
    )(a, b)


def _s5_tables(log_dt, a_re, a_im, b_re, b_im, c_re, c_im, d_skip):
    t = S5_CHUNK
    gpb = S5_GROUPS_PER_BLOCK
    nblk = SSM_GROUPS // gpb
    hp = lax.Precision.HIGHEST
    lr = jnp.minimum(a_re.astype(f32), -1e-4)
    li = a_im.astype(f32)
    dt = jnp.exp(log_dt.astype(f32))
    kk = jnp.arange(t + 1, dtype=f32)[:, None, None]
    mag = jnp.exp(kk * (lr * dt))
    pr = mag * jnp.cos(kk * (li * dt))
    pi = mag * jnp.sin(kk * (li * dt))
    xr, xi = pr[1] - 1.0, pi[1]
    den = lr * lr + li * li
    cr = (xr * lr + xi * li) / den
    ci = (xi * lr - xr * li) / den
    bbr = cr[..., None] * b_re - ci[..., None] * b_im
    bbi = cr[..., None] * b_im + ci[..., None] * b_re
    wr = pr[:t, :, :, None] * bbr - pi[:t, :, :, None] * bbi
    wi = pr[:t, :, :, None] * bbi + pi[:t, :, :, None] * bbr
    taps = (jnp.einsum('gop,tgpc->tgco', c_re, wr, precision=hp)
            - jnp.einsum('gop,tgpc->tgco', c_im, wi, precision=hp))
    ktab = taps.reshape(t, nblk, gpb, SSM_CH, SSM_CH).transpose(1, 0, 3, 2, 4)
    ktab = ktab.reshape(nblk, t, SSM_CH, gpb * SSM_CH)

    rev = jnp.arange(t - 1, -1, -1)
    sr = pr[rev][..., None] * bbr - pi[rev][..., None] * bbi
    si = pr[rev][..., None] * bbi + pi[rev][..., None] * bbr
    sb = jnp.stack([sr, si], axis=0).reshape(2, t, nblk, gpb, SSM_STATE, SSM_CH)
    bsrc = sb.transpose(2, 1, 0, 3, 5, 4).reshape(nblk, t, 2, gpb * SSM_CH, SSM_STATE)
    bsrc = jnp.concatenate([bsrc, bsrc], axis=-1)

    er = c_re[None] * pr[1:, :, None, :] - c_im[None] * pi[1:, :, None, :]
    ei = c_re[None] * pi[1:, :, None, :] + c_im[None] * pr[1:, :, None, :]
    eb = jnp.stack([er, -ei], axis=0).reshape(2, t, nblk, gpb, SSM_CH, SSM_STATE)
    csrc = eb.transpose(2, 1, 0, 5, 3, 4).reshape(nblk, t, 2, SSM_STATE, gpb * SSM_CH)

    a_chunk = jnp.stack([pr[t], pi[t]], axis=0).reshape(2, nblk, 1, gpb * SSM_STATE)
    a_chunk = a_chunk.transpose(1, 0, 2, 3).reshape(nblk, 2, gpb * SSM_STATE)
    dvec = jnp.tile(d_skip.astype(f32).reshape(nblk, 1, gpb * SSM_CH), (1, 1, t))
    return ktab.astype(bf16), bsrc.astype(bf16), csrc.astype(bf16), a_chunk, dvec


def _s5_kernel(u_ref, ktab_ref, bsrc_ref, csrc_ref, a_ref, d_ref, y_ref,
               toep_ref, bpow_ref, cpow_ref, s_ref, h_ref, yt_ref, *, n_batch, n_chunk):
    t = S5_CHUNK
    gpb = S5_GROUPS_PER_BLOCK
    w = gpb * SSM_CH
    ns = gpb * SSM_STATE
    zero = jnp.zeros((), bf16)

    def same_group(shape, row_size, col_size):
        r = lax.broadcasted_iota(jnp.int32, shape, 0) // row_size
        c = lax.broadcasted_iota(jnp.int32, shape, 1) // col_size
        return r == c

    tap_mask = same_group((w, w), SSM_CH, SSM_CH)
    taps = [jnp.where(tap_mask, jnp.tile(ktab_ref[0, tau], (gpb, 1)), zero) for tau in range(t)]
    for tt in range(t):
        for ss in range(tt + 1):
            toep_ref[ss * w:(ss + 1) * w, tt * w:(tt + 1) * w] = taps[tt - ss]
        if tt % 2 == 0:
            toep_ref[(tt + 1) * w:(tt + 2) * w, tt * w:(tt + 1) * w] = jnp.zeros((w, w), bf16)
    b_mask = same_group((w, ns), SSM_CH, SSM_STATE)
    c_mask = same_group((ns, w), SSM_STATE, SSM_CH)
    for ss in range(t):
        for z in range(2):
            bpow_ref[ss * w:(ss + 1) * w, z * ns:(z + 1) * ns] = jnp.where(
                b_mask, jnp.tile(bsrc_ref[0, ss, z], (1, ns // w)), zero)
            cpow_ref[z * ns:(z + 1) * ns, ss * w:(ss + 1) * w] = jnp.where(
                c_mask, jnp.tile(csrc_ref[0, ss, z], (gpb, 1)), zero)

    rows = n_batch * n_chunk
    uf3 = pltpu.einshape("rsl->srl", u_ref[...].reshape(rows, t, w))
    uf = [uf3[s] for s in range(t)]
    u = jnp.concatenate([p.astype(bf16) for p in uf], axis=1)
    s_ref[...] = jnp.dot(u, bpow_ref[...], preferred_element_type=f32)

    ar = a_ref[0, 0:1, :]
    ai = a_ref[0, 1:2, :]

    def step(j, carry):
        new = []
        for b in range(n_batch):
            hr, hi = carry[2 * b], carry[2 * b + 1]
            row = b * n_chunk + j
            h_ref[pl.ds(row, 1), 0:ns] = hr
            h_ref[pl.ds(row, 1), ns:2 * ns] = hi
            sr = s_ref[pl.ds(row, 1), 0:ns]
            si = s_ref[pl.ds(row, 1), ns:2 * ns]
            new.append(ar * hr - ai * hi + sr)
            new.append(ar * hi + ai * hr + si)
        return tuple(new)

    zero = jnp.zeros((1, ns), f32)
    lax.fori_loop(0, n_chunk, step, (zero,) * (2 * n_batch))

    hprev = h_ref[...].astype(bf16)
    for tp in range(t // 2):
        c0, c1 = 2 * tp * w, (2 * tp + 2) * w
        y = jnp.dot(u[:, :c1], toep_ref[0:c1, c0:c1], preferred_element_type=f32)
        y = y + jnp.dot(hprev, cpow_ref[:, c0:c1], preferred_element_type=f32)
        for k, tt in enumerate((2 * tp, 2 * tp + 1)):
            yk = y[:, k * w:(k + 1) * w] + d_ref[0, :, tt * w:(tt + 1) * w] * uf[tt]
            yt_ref[tt] = jax.nn.gelu(yk, approximate=True)
    y_ref[...] = pltpu.einshape("srl->rsl", yt_ref[...]).reshape(rows * t, w)


def _s5_mixer(proj, tables, n_batch, n_chunk):
    ktab, bsrc, csrc, a_chunk, dvec = tables
    nblk = ktab.shape[0]
    w = S5_GROUPS_PER_BLOCK * SSM_CH
    cols = S5_CHUNK * w
    rows = n_batch * n_chunk
    n_tok = rows * S5_CHUNK
    ns = S5_GROUPS_PER_BLOCK * SSM_STATE
    kern = functools.partial(_s5_kernel, n_batch=n_batch, n_chunk=n_chunk)
    return pl.pallas_call(
        kern,
        grid=(nblk,),
        in_specs=[pl.BlockSpec((n_tok, w), lambda g: (0, g)),
                  pl.BlockSpec((1,) + ktab.shape[1:], lambda g: (g, 0, 0, 0)),
                  pl.BlockSpec((1,) + bsrc.shape[1:], lambda g: (g, 0, 0, 0, 0)),
                  pl.BlockSpec((1,) + csrc.shape[1:], lambda g: (g, 0, 0, 0, 0)),
                  pl.BlockSpec((1, 2, ns), lambda g: (g, 0, 0)),
                  pl.BlockSpec((1, 1, cols), lambda g: (g, 0, 0))],
        out_specs=pl.BlockSpec((n_tok, w), lambda g: (0, g)),
        out_shape=jax.ShapeDtypeStruct((n_tok, D_SSM), f32),
        scratch_shapes=[pltpu.VMEM((cols, cols), bf16),
                        pltpu.VMEM((cols, 2 * ns), bf16),
                        pltpu.VMEM((2 * ns, cols), bf16),
                        pltpu.VMEM((rows, 2 * ns), f32),
                        pltpu.VMEM((rows, 2 * ns), f32),
                        pltpu.VMEM((S5_CHUNK, rows, w), f32)],
        compiler_params=_params("parallel"),
        name="s5_mixer",
    )(proj, ktab, bsrc, csrc, a_chunk, dvec)


def _attn_bias_table():
    blk = ATTN_BLOCK
    slopes = 2.0 ** (-8.0 * jnp.arange(1, N_HEADS + 1, dtype=f32) / N_HEADS)
    delta = np.arange(blk)[:, None] - (np.arange(2 * blk)[None, :] - blk)
    tabs = []
    for window, dil in PATTERNS:
        assert window // dil == blk
        valid = (delta >= 0) & (delta <= window // dil)
        dist = jnp.asarray(delta * dil, dtype=f32)
        bias = jnp.where(valid[None], -slopes[:, None, None] * dist[None], MASK_VALUE)
        tabs.append(bias.reshape(N_HEADS // 2, 2 * blk, 2 * blk))
    return jnp.stack(tabs, axis=1)


def _attn_kernel(q_ref, k_ref, v_ref, bias_ref, o_ref, *scr):
    blk = ATTN_BLOCK
    seq = q_ref.shape[0]
    first_head = lax.broadcasted_iota(jnp.int32, (blk, 2 * HEAD_DIM), 1) < HEAD_DIM
    dims = (((1,), (1,)), ((), ()))
    for pi, (_, dil) in enumerate(PATTERNS):
        o_scr, l_scr = scr[2 * pi], scr[2 * pi + 1]
        sub = seq // dil
        for r in range(dil):
            rows = (lambda st, n: pl.ds(st, n)) if dil == 1 else (lambda st, n: pl.ds(st, n, stride=dil))
            qd = (q_ref[rows(r, sub), :] * HEAD_DIM ** -0.5).astype(bf16)
            kd = k_ref[rows(r, sub), :].astype(bf16)
            vd = v_ref[rows(r, sub), :].astype(bf16)
            for i in range(sub // blk):
                qb = qd[i * blk:(i + 1) * blk]
                zero = jnp.zeros_like(qb)
                q2 = jnp.concatenate([jnp.where(first_head, qb, zero), jnp.where(first_head, zero, qb)], axis=0)
                k0 = max(i - 1, 0) * blk
                nk = (i + 1) * blk - k0
                s = lax.dot_general(q2, kd[k0:k0 + nk], dims, preferred_element_type=f32)
                s = s + bias_ref[0, pi, :, 2 * blk - nk:]
                m = jnp.max(s, axis=-1, keepdims=True)
                p = jnp.exp(s - m)
                l = jnp.sum(p, axis=-1, keepdims=True)
                o = jnp.dot(p.astype(bf16), vd[k0:k0 + nk], preferred_element_type=f32) / l
                lse = m + jnp.log(l)
                dst = rows(r + dil * blk * i, blk)
                o_scr[dst, :] = jnp.where(first_head, o[:blk], o[blk:])
                l_scr[dst, :] = jnp.where(first_head, lse[:blk], lse[blk:])
    l1, l2, l3 = scr[1][...], scr[3][...], scr[5][...]
    m = jnp.maximum(jnp.maximum(l1, l2), l3)
    e1, e2, e3 = jnp.exp(l1 - m), jnp.exp(l2 - m), jnp.exp(l3 - m)
    o_ref[...] = (e1 * scr[0][...] + e2 * scr[2][...] + e3 * scr[4][...]) / (e1 + e2 + e3)


def _dilated_attention(qkv, first, bsz, seq):
    pairs = N_HEADS // 2
    width = 2 * HEAD_DIM
    bias = _attn_bias_table()
    col = lambda off: pl.BlockSpec((seq, width), lambda hp, b: (b, first + off + hp))
    return pl.pallas_call(
        _attn_kernel,
        grid=(pairs, bsz),
        in_specs=[col(0), col(pairs), col(2 * pairs),
                  pl.BlockSpec((1,) + bias.shape[1:], lambda hp, b: (hp, 0, 0, 0))],
        out_specs=pl.BlockSpec((seq, width), lambda hp, b: (b, hp)),
        out_shape=jax.ShapeDtypeStruct((bsz * seq, D_ATTN), f32),
        scratch_shapes=[pltpu.VMEM((seq, width), f32)] * (2 * len(PATTERNS)),
        compiler_params=_params("parallel", "parallel"),
        name="dilated_attention",
    )(qkv, qkv, qkv, bias)


def _layer_norm_rows(x, g, b):
    mu = jnp.mean(x, axis=-1, keepdims=True)
    xc = x - mu
    var = jnp.mean(xc * xc, axis=-1, keepdims=True)
    return xc * lax.rsqrt(var + NORM_EPS) * g + b


def _rms_rows(x, g):
    return x * lax.rsqrt(jnp.mean(x * x, axis=-1, keepdims=True) + NORM_EPS) * g


def _mix_out_kernel(y_ref, ya_ref, x_ref, wglu_ref, w_ref, gs_ref, ga_ref, lg_ref, lb_ref, h_ref, hp_ref):
    z = jnp.dot(y_ref[...].astype(bf16), wglu_ref[...], preferred_element_type=f32)
    y_ssm = z[:, :D_SSM] * jax.nn.sigmoid(z[:, D_SSM:])
    ns = _rms_rows(y_ssm, gs_ref[...]).astype(bf16)
    na = _rms_rows(ya_ref[...], ga_ref[...]).astype(bf16)
    proj = jnp.dot(ns, w_ref[0:D_SSM, :], preferred_element_type=f32)
    proj = proj + jnp.dot(na, w_ref[D_SSM:, :], preferred_element_type=f32)
    h = _layer_norm_rows(DEEPNORM_ALPHA * x_ref[...] + proj, lg_ref[...], lb_ref[...])
    h_ref[...] = h
    half = D_MODEL // 2
    word = pltpu.pack_elementwise([h[:, :half], h[:, half:]], packed_dtype=bf16)
    tm = word.shape[0]
    chunks = jnp.stack([word[:, s * 128:(s + 1) * 128] for s in range(PACK_ROWS)], axis=0)
    hp_ref[...] = pltpu.einshape("srl->rsl", chunks).reshape(tm * PACK_ROWS, 128)


def _mix_out(y, y_attn, x, w_glu, w_out, g_ssm, g_attn, ln_g, ln_b, tm=256):
    n = x.shape[0]
    row = lambda c: pl.BlockSpec((tm, c), lambda i: (i, 0))
    full = lambda a: pl.BlockSpec(a.shape, lambda i: (0,) * a.ndim)
    return pl.pallas_call(
        _mix_out_kernel,
        grid=(n // tm,),
        in_specs=[row(D_SSM), row(D_ATTN), row(D_MODEL), full(w_glu), full(w_out), full(g_ssm), full(g_attn),
                  full(ln_g), full(ln_b)],
        out_specs=[row(D_MODEL), pl.BlockSpec((tm * PACK_ROWS, 128), lambda i: (i, 0))],
        out_shape=[jax.ShapeDtypeStruct((n, D_MODEL), f32),
                   jax.ShapeDtypeStruct((n * PACK_ROWS, 128), jnp.int32)],
        compiler_params=_params("parallel"),
        name="mix_out",
    )(y, y_attn, x, w_glu, w_out, g_ssm, g_attn, ln_g, ln_b)


def _router_kernel(h_ref, wrt_ref, bias_ref, tri_ref, trie_ref, e8_ref, pos8_ref, wtok_ref, cnt_ref):
    gsz = N_EXPERTS // N_EXPERT_GROUPS
    tm = h_ref.shape[0]
    ninf = -jnp.inf

    @pl.when(pl.program_id(0) == 0)
    def _():
        cnt_ref[...] = jnp.zeros_like(cnt_ref)

    logits = lax.dot_general(wrt_ref[...], h_ref[...], (((1,), (1,)), ((), ())),
                             preferred_element_type=f32, precision=lax.Precision.HIGHEST)
    scores = jax.nn.sigmoid(logits)
    sel = scores + bias_ref[...]
    io = lax.broadcasted_iota(jnp.int32, (gsz, tm), 0)

    blks, gs_rows = [], []
    for g in range(N_EXPERT_GROUPS):
        blk = sel[g * gsz:(g + 1) * gsz, :]
        m1 = jnp.max(blk, axis=0, keepdims=True)
        first = jnp.min(jnp.where(blk == m1, io, gsz), axis=0, keepdims=True)
        m2 = jnp.max(jnp.where(io == first, ninf, blk), axis=0, keepdims=True)
        blks.append(blk)
        gs_rows.append(m1 + m2)
    gs = jnp.concatenate(gs_rows, axis=0)

    iog = lax.broadcasted_iota(jnp.int32, (N_EXPERT_GROUPS, tm), 0)
    beaten = jnp.zeros((N_EXPERT_GROUPS, tm), f32)
    for gp in range(N_EXPERT_GROUPS):
        row = gs_rows[gp]
        tie = jnp.where(iog > gp, 1.0, 0.0)
        beaten = beaten + jnp.where(row > gs, 1.0, jnp.where(row == gs, tie, 0.0))
    keep = beaten < TOPK_GROUPS
    masked = [jnp.where(keep[g:g + 1, :], blks[g], ninf) for g in range(N_EXPERT_GROUPS)]

    ranks = [jnp.zeros((gsz, tm), f32) for _ in range(N_EXPERT_GROUPS)]
    tie_in = [jnp.where(io > j, 1.0, 0.0) for j in range(gsz)]
    for gp in range(N_EXPERT_GROUPS):
        for j in range(gsz):
            row = masked[gp][j:j + 1, :]
            for g in range(N_EXPERT_GROUPS):
                if gp < g:
                    inc = jnp.where(row >= masked[g], 1.0, 0.0)
                elif gp > g:
                    inc = jnp.where(row > masked[g], 1.0, 0.0)
                else:
                    inc = jnp.where(row > masked[g], 1.0, jnp.where(row == masked[g], tie_in[j], 0.0))
                ranks[g] = ranks[g] + inc
    selb = jnp.concatenate([jnp.where(r < TOP_K, 1.0, 0.0) for r in ranks], axis=0)
    wsel = selb * scores
    wn = wsel / jnp.sum(wsel, axis=0, keepdims=True) * ROUTED_SCALE

    maskb = selb.astype(bf16)
    pos = jnp.dot(maskb, tri_ref[...], preferred_element_type=f32) + cnt_ref[:, 0:1]
    cnt_ref[...] = cnt_ref[...] + jnp.sum(selb, axis=1, keepdims=True)
    slot = jnp.dot(trie_ref[...], maskb, preferred_element_type=f32)
    ioe = lax.broadcasted_iota(jnp.int32, (N_EXPERTS, tm), 0).astype(f32)
    e_rows, p_rows = [], []
    for k in range(TOP_K):
        hit = jnp.where(slot == k, selb, 0.0)
        e_rows.append(jnp.sum(hit * ioe, axis=0, keepdims=True))
        p_rows.append(jnp.sum(hit * pos, axis=0, keepdims=True))
    e8_ref[...] = jnp.concatenate(e_rows, axis=0).astype(jnp.int32)
    pos8_ref[...] = jnp.concatenate(p_rows, axis=0).astype(jnp.int32)
    wtok_ref[...] = jnp.concatenate([wn.T, jnp.zeros((tm, 128 - N_EXPERTS), f32)], axis=1)


def _router(h, w_router, router_bias, tm=512):
    n = h.shape[0]
    wrt = w_router.astype(f32).T
    bias = router_bias.astype(f32).reshape(N_EXPERTS, 1)
    tri = (jnp.arange(tm)[:, None] < jnp.arange(tm)[None, :]).astype(bf16)
    trie = (jnp.arange(N_EXPERTS)[None, :] < jnp.arange(N_EXPERTS)[:, None]).astype(bf16)
    full = lambda a: pl.BlockSpec(a.shape, lambda i: (0,) * a.ndim)
    tok = lambda: pl.BlockSpec((TOP_K, tm), lambda i: (0, i))
    return pl.pallas_call(
        _router_kernel,
        grid=(n // tm,),
        in_specs=[pl.BlockSpec((tm, D_MODEL), lambda i: (i, 0)), full(wrt), full(bias), full(tri), full(trie)],
        out_specs=[tok(), tok(), pl.BlockSpec((tm, 128), lambda i: (i, 0)),
                   pl.BlockSpec((N_EXPERTS, 128), lambda i: (0, 0))],
        out_shape=[jax.ShapeDtypeStruct((TOP_K, n), jnp.int32), jax.ShapeDtypeStruct((TOP_K, n), jnp.int32),
                   jax.ShapeDtypeStruct((n, 128), f32), jax.ShapeDtypeStruct((N_EXPERTS, 128), f32)],
        compiler_params=_params("arbitrary"),
        name="router",
    )(h, wrt, bias, tri, trie)


INVERT_UNROLL = 16


def _invert_kernel(dest_ref, nb_ref, code_ref):
    n_rows = code_ref.shape[0]

    def fill(first_group, last_group):
        def body(k, c):
            for u in range(INVERT_UNROLL):
                code_ref[k * INVERT_UNROLL + u] = PAD_CODE
            return c
        lax.fori_loop(first_group, last_group, body, 0)

    def per_expert(e, c):
        groups = MOE_ROWS // INVERT_UNROLL + 1
        start = jnp.minimum(nb_ref[1 + e], n_rows - groups * INVERT_UNROLL) // INVERT_UNROLL
        fill(start, start + groups)
        return c
    lax.fori_loop(0, N_EXPERTS, per_expert, 0)
    fill(nb_ref[0] * (MOE_ROWS // INVERT_UNROLL), n_rows // INVERT_UNROLL)

    def body(p, c):
        code_ref[dest_ref[p]] = p
        return c
    lax.fori_loop(0, dest_ref.shape[0], body, 0, unroll=INVERT_UNROLL)


def _invert(dest, nb, n_rows):
    smem = lambda: pl.BlockSpec(memory_space=pltpu.SMEM)
    return pl.pallas_call(
        _invert_kernel,
        in_specs=[smem(), smem()],
        out_specs=smem(),
        out_shape=jax.ShapeDtypeStruct((n_rows,), jnp.int32),
        name="invert_dispatch",
    )(dest, nb)


def _expert_changed(be_ref, i):
    return jnp.logical_or(i == 0, be_ref[i] != be_ref[jnp.maximum(i - 1, 0)])


SCATTER_UNROLL = 8


def _moe_up_kernel(be_ref, code_ref, nb_ref, hp_ref, wtok_ref, wg_ref, wu_ref, mid_ref,
                   xg_scr, xa_scr, xb_scr, wra_scr, wrb_scr, wgu_scr, wstage_scr, wsem):
    i = pl.program_id(0)
    rows = MOE_ROWS

    def gather_block(blk, x_dst, wrow_dst):
        base = blk * rows
        for r in range(rows):
            tok = (code_ref[base + r] >> 3) & (N_TOKENS - 1)
            xg_scr[r * PACK_ROWS:(r + 1) * PACK_ROWS, :] = (
                hp_ref[pl.ds(pl.multiple_of(tok * PACK_ROWS, PACK_ROWS), PACK_ROWS), :])
            wrow_dst[r:r + 1, :] = wtok_ref[pl.ds(tok, 1), :]
        half = D_MODEL // 2
        chunks = pltpu.einshape("rsl->srl", xg_scr[...].reshape(rows, PACK_ROWS, 128))
        for s in range(PACK_ROWS):
            wds = chunks[s]
            lo = pltpu.unpack_elementwise(wds, index=0, packed_dtype=bf16, unpacked_dtype=f32)
            hi = pltpu.unpack_elementwise(wds, index=1, packed_dtype=bf16, unpacked_dtype=f32)
            x_dst[:, s * 128:(s + 1) * 128] = lo.astype(bf16)
            x_dst[:, half + s * 128:half + (s + 1) * 128] = hi.astype(bf16)

    def weight_copies(e, s):
        return (pltpu.make_async_copy(wg_ref.at[e], wstage_scr.at[s, 0], wsem.at[s, 0]),
                pltpu.make_async_copy(wu_ref.at[e], wstage_scr.at[s, 1], wsem.at[s, 1]))

    def expert_block(blk, out_rows, x_cur, wrow_cur, x_nxt, wrow_nxt):
        last = MOE_BLOCKS - 1
        expert = be_ref[jnp.minimum(blk, last)]
        slot = nb_ref[NB_SLOT + expert]
        upcoming = nb_ref[NB_NEXT + expert]
        valid = blk < nb_ref[0]

        @pl.when(jnp.logical_and(valid, blk == 0))
        def _():
            for cp in weight_copies(expert, slot):
                cp.start()

        @pl.when(jnp.logical_and(valid, _expert_changed(be_ref, jnp.minimum(blk, last))))
        def _():
            for cp in weight_copies(expert, slot):
                cp.wait()

            @pl.when(upcoming >= 0)
            def _():
                for cp in weight_copies(upcoming, 1 - slot):
                    cp.start()

            wgu_scr[:, :D_EXPERT] = wstage_scr[slot, 0].astype(bf16)
            wgu_scr[:, D_EXPERT:] = wstage_scr[slot, 1].astype(bf16)

        @pl.when(valid)
        def _():
            gather_block(jnp.minimum(blk + 1, last), x_nxt, wrow_nxt)
            gu = jnp.dot(x_cur[...], wgu_scr[...], preferred_element_type=f32)
            g, u = gu[:, :D_EXPERT], gu[:, D_EXPERT:]
            lane = lax.broadcasted_iota(jnp.int32, (rows, 128), 1)
            w = jnp.sum(jnp.where(lane == expert, wrow_cur[...], 0.0), axis=1, keepdims=True)
            real = blk * rows + lax.broadcasted_iota(jnp.int32, (rows, 1), 0) < nb_ref[1 + expert]
            mid_ref[out_rows, :] = (g * jax.nn.sigmoid(g) * u * jnp.where(real, w, 0.0)).astype(bf16)

        @pl.when(jnp.logical_not(valid))
        def _():
            mid_ref[out_rows, :] = jnp.zeros((rows, D_EXPERT), bf16)

    @pl.when(i == 0)
    def _():
        gather_block(0, xa_scr, wra_scr)

    expert_block(2 * i, slice(0, rows), xa_scr, wra_scr, xb_scr, wrb_scr)
    expert_block(2 * i + 1, slice(rows, 2 * rows), xb_scr, wrb_scr, xa_scr, wra_scr)


def _moe_up(block_e, code, nb, hp, wtok, w_gate, w_up):
    grid_spec = pltpu.PrefetchScalarGridSpec(
        num_scalar_prefetch=3,
        grid=(MOE_BLOCKS // 2,),
        in_specs=[pl.BlockSpec(memory_space=pltpu.VMEM),
                  pl.BlockSpec(memory_space=pltpu.VMEM),
                  pl.BlockSpec(memory_space=pl.ANY),
                  pl.BlockSpec(memory_space=pl.ANY)],
        out_specs=pl.BlockSpec((2 * MOE_ROWS, D_EXPERT), lambda i, be, cd, nb: (i, 0)),
        scratch_shapes=[pltpu.VMEM((PACK_ROWS * MOE_ROWS, 128), jnp.int32),
                        pltpu.VMEM((MOE_ROWS, D_MODEL), bf16), pltpu.VMEM((MOE_ROWS, D_MODEL), bf16),
                        pltpu.VMEM((MOE_ROWS, 128), f32), pltpu.VMEM((MOE_ROWS, 128), f32),
                        pltpu.VMEM((D_MODEL, 2 * D_EXPERT), bf16),
                        pltpu.VMEM((2, 2, D_MODEL, D_EXPERT), f32),
                        pltpu.SemaphoreType.DMA((2, 2))])
    return pl.pallas_call(
        _moe_up_kernel,
        grid_spec=grid_spec,
        out_shape=jax.ShapeDtypeStruct((MOE_BLOCKS * MOE_ROWS, D_EXPERT), bf16),
        compiler_params=pltpu.CompilerParams(dimension_semantics=("arbitrary",),
                                             vmem_limit_bytes=MOE_UP_VMEM_LIMIT),
        name="moe_up",
    )(block_e, code, nb, hp, wtok, w_gate, w_up)


def _moe_down_kernel(be_ref, code_ref, nb_ref, mida_ref, midb_ref, wd_ref, acc_ref,
                     ya_scr, yb0_scr, yb1_scr, wa_scr, wb_scr, wstage_scr, wsem):
    j = pl.program_id(1)
    rows = MOE_ROWS
    nb = nb_ref[0]
    blk_a, blk_b = 2 * j, 2 * j + 1
    last = MOE_BLOCKS - 1

    @pl.when(j == 0)
    def _():
        acc_ref[...] = jnp.zeros_like(acc_ref)

    def scatter_rows(base, ybuf, r0, n):
        sums, addrs = [], []
        for k in range(n):
            a = pl.multiple_of(code_ref[base + r0 + k] & -PACK_ROWS, PACK_ROWS)
            v = ybuf[pl.ds(pl.multiple_of((r0 + k) * PACK_ROWS, PACK_ROWS), PACK_ROWS), :]
            sums.append(acc_ref[pl.ds(a, PACK_ROWS), :] + v)
            addrs.append(a)
        for k in range(n):
            acc_ref[pl.ds(addrs[k], PACK_ROWS), :] = sums[k]

    def scatter_block(blk, ybuf):
        for g in range(rows // SCATTER_UNROLL):
            scatter_rows(blk * rows, ybuf, g * SCATTER_UNROLL, SCATTER_UNROLL)

    def scatter_block_compact(blk, ybuf):
        def body(g, c):
            scatter_rows(blk * rows, ybuf, g * SCATTER_UNROLL, SCATTER_UNROLL)
            return c
        lax.fori_loop(0, rows // SCATTER_UNROLL, body, 0)

    def down_block(mid_ref, w_scr, ybuf):
        y = jnp.dot(mid_ref[...], w_scr[...], preferred_element_type=f32)
        y3 = jnp.stack([y[:, c * 128:(c + 1) * 128] for c in range(PACK_ROWS)], axis=0)
        ybuf[...] = pltpu.einshape("crl->rcl", y3).reshape(rows * PACK_ROWS, 128)

    half = D_MODEL // 2
    cols = pl.ds(pl.multiple_of(pl.program_id(0) * half, half), half)

    def weight_copy(e, s):
        return pltpu.make_async_copy(wd_ref.at[e, :, cols], wstage_scr.at[s], wsem.at[s])

    def refresh_weights(blk, w_scr):
        cur = be_ref[jnp.minimum(blk, last)]
        slot = nb_ref[NB_SLOT + cur]
        upcoming = nb_ref[NB_NEXT + cur]
        valid = blk < nb

        @pl.when(jnp.logical_and(valid, blk == 0))
        def _():
            weight_copy(cur, slot).start()

        @pl.when(jnp.logical_and(valid, _expert_changed(be_ref, jnp.minimum(blk, last))))
        def _():
            weight_copy(cur, slot).wait()

            @pl.when(upcoming >= 0)
            def _():
                weight_copy(upcoming, 1 - slot).start()

        prev = be_ref[jnp.clip(blk - 2, 0, last)]

        @pl.when(jnp.logical_and(valid, jnp.logical_or(j == 0, cur != prev)))
        def _():
            w_scr[...] = wstage_scr[slot].astype(bf16)

    refresh_weights(blk_a, wa_scr)
    refresh_weights(blk_b, wb_scr)

    for parity, yb_this, yb_prev in ((0, yb0_scr, yb1_scr), (1, yb1_scr, yb0_scr)):
        mine = j % 2 == parity

        @pl.when(jnp.logical_and(mine, jnp.logical_and(j > 0, blk_b < nb)))
        def _():
            down_block(mida_ref, wa_scr, ya_scr)
            down_block(midb_ref, wb_scr, yb_this)
            scatter_block(blk_a - 1, yb_prev)
            scatter_block(blk_a, ya_scr)

        @pl.when(jnp.logical_and(mine, jnp.logical_and(j > 0, blk_b == nb)))
        def _():
            down_block(mida_ref, wa_scr, ya_scr)
            scatter_block_compact(blk_a - 1, yb_prev)
            scatter_block_compact(blk_a, ya_scr)

        @pl.when(jnp.logical_and(mine, jnp.logical_and(j > 0, blk_a == nb)))
        def _():
            scatter_block_compact(blk_a - 1, yb_prev)

    @pl.when(jnp.logical_and(j == 0, blk_a < nb))
    def _():
        down_block(mida_ref, wa_scr, ya_scr)
        scatter_block_compact(blk_a, ya_scr)

    @pl.when(jnp.logical_and(j == 0, blk_b < nb))
    def _():
        down_block(midb_ref, wb_scr, yb0_scr)


def _moe_down(block_e, code, nb, mid, w_down):
    half = D_MODEL // 2
    acc_rows = (N_TOKENS + 1) * PACK_ROWS
    last = MOE_BLOCKS - 1
    assert MOE_BLOCKS % 2 == 0
    blk = lambda off: (lambda p, j, be, cd, nb: (jnp.minimum(2 * j + off, last), 0))
    grid_spec = pltpu.PrefetchScalarGridSpec(
        num_scalar_prefetch=3,
        grid=(2, MOE_BLOCKS // 2 + 1),
        in_specs=[pl.BlockSpec((MOE_ROWS, D_EXPERT), blk(0)), pl.BlockSpec((MOE_ROWS, D_EXPERT), blk(1)),
                  pl.BlockSpec(memory_space=pl.ANY)],
        out_specs=pl.BlockSpec((None, acc_rows, 128), lambda p, j, be, cd, nb: (p, 0, 0),
                               pipeline_mode=pl.Buffered(1)),
        scratch_shapes=[pltpu.VMEM((PACK_ROWS * MOE_ROWS, 128), f32)] * 3
        + [pltpu.VMEM((D_EXPERT, half), bf16)] * 2
        + [pltpu.VMEM((2, D_EXPERT, half), f32), pltpu.SemaphoreType.DMA((2,))])
    return pl.pallas_call(
        _moe_down_kernel,
        grid_spec=grid_spec,
        out_shape=jax.ShapeDtypeStruct((2, acc_rows, 128), f32),
        compiler_params=_params("arbitrary", "arbitrary"),
        name="moe_down",
    )(block_e, code, nb, mid, mid, w_down)


def _final_kernel(h_ref, r0_ref, r1_ref, wgu_ref, wd_ref, lg_ref, lb_ref, o_ref):
    tm = h_ref.shape[0]
    gu = jnp.dot(h_ref[...].astype(bf16), wgu_ref[...], preferred_element_type=f32)
    g, u = gu[:, :D_EXPERT], gu[:, D_EXPERT:]
    mid = (g * jax.nn.sigmoid(g) * u).astype(bf16)
    shared = jnp.dot(mid, wd_ref[...], preferred_element_type=f32)
    halves = [pltpu.einshape("rcl->crl", r[...].reshape(tm, PACK_ROWS, 128)) for r in (r0_ref, r1_ref)]
    routed = jnp.concatenate([hv[c] for hv in halves for c in range(PACK_ROWS)], axis=1)
    o_ref[...] = _layer_norm_rows(DEEPNORM_ALPHA * h_ref[...] + routed + shared,
                                  lg_ref[...], lb_ref[...])


def _final(h, racc, wgu, wd, ln_g, ln_b, tm=256):
    n = h.shape[0]
    row = lambda: pl.BlockSpec((tm, D_MODEL), lambda i: (i, 0))
    full = lambda a: pl.BlockSpec(a.shape, lambda i: (0,) * a.ndim)
    acc = lambda p: pl.BlockSpec((None, tm * PACK_ROWS, 128), lambda i: (p, i, 0))
    return pl.pallas_call(
        _final_kernel,
        grid=(n // tm,),
        in_specs=[row(), acc(0), acc(1), full(wgu), full(wd), full(ln_g), full(ln_b)],
        out_specs=row(),
        out_shape=jax.ShapeDtypeStruct((n, D_MODEL), f32),
        compiler_params=_params("parallel"),
        name="shared_final",
    )(h, racc, racc, wgu, wd, ln_g, ln_b)


def _dispatch_plan(e8, pos8, cnt):
    counts = cnt[:, 0].astype(jnp.int32)
    padded = (counts + MOE_ROWS - 1) // MOE_ROWS * MOE_ROWS
    pad_end = jnp.cumsum(padded).astype(jnp.int32)
    pad_start = pad_end - padded
    seg_end = pad_start + counts
    ids = jnp.arange(N_EXPERTS, dtype=jnp.int32)
    start8 = jnp.sum(jnp.where(e8[..., None] == ids, pad_start, 0), axis=-1)
    dest = (start8 + pos8).T.reshape(-1)
    block_start = jnp.arange(MOE_BLOCKS, dtype=jnp.int32) * MOE_ROWS
    block_e = jnp.minimum(jnp.sum((pad_end[None, :] <= block_start[:, None]).astype(jnp.int32), axis=1),
                          N_EXPERTS - 1)
    used = counts > 0
    slot = (jnp.cumsum(used.astype(jnp.int32)) - 1) % 2
    later = jnp.where(used, ids, N_EXPERTS)
    nxt = jnp.concatenate([lax.cummin(later[::-1])[::-1][1:], jnp.full((1,), N_EXPERTS, jnp.int32)])
    nxt = jnp.where(nxt == N_EXPERTS, -1, nxt)
    nb = jnp.concatenate([pad_end[-1:] // MOE_ROWS, seg_end, slot, nxt]).astype(jnp.int32)
    return dest, block_e, nb


def kernel(x, w_in, ssm_log_dt, ssm_a_re, ssm_a_im, ssm_b_re, ssm_b_im, ssm_c_re, ssm_c_im, ssm_d,
           w_glu, g_ssm_out, g_attn_out, w_out, ln1_g, ln1_b, w_router, router_bias, w_gate, w_up,
           w_down, ws_gate, ws_up, ws_down, ln2_g, ln2_b):
    bsz, seq, d = x.shape
    n_tok = bsz * seq
    h = x.reshape(n_tok, d)
    for layer in range(DEPTH):
        proj = _matmul(h, w_in[layer].astype(bf16), f32)

        tables = _s5_tables(ssm_log_dt[layer], ssm_a_re[layer], ssm_a_im[layer], ssm_b_re[layer],
                            ssm_b_im[layer], ssm_c_re[layer], ssm_c_im[layer], ssm_d[layer])
        y = _s5_mixer(proj, tables, bsz, seq // S5_CHUNK)

        y_attn = _dilated_attention(proj, D_SSM // (2 * HEAD_DIM), bsz, seq)

        row2 = lambda a: a.astype(f32).reshape(1, -1)
        h, hp = _mix_out(y, y_attn, h, w_glu[layer].astype(bf16), w_out[layer].astype(bf16),
                         row2(g_ssm_out[layer]), row2(g_attn_out[layer]), row2(ln1_g[layer]), row2(ln1_b[layer]))

        assert n_tok == N_TOKENS
        e8, pos8, wtok, cnt = _router(h, w_router[layer], router_bias[layer])
        dest, block_e, nb = _dispatch_plan(e8, pos8, cnt)
        code = _invert(dest, nb, MOE_BLOCKS * MOE_ROWS)
        mid = _moe_up(block_e, code, nb, hp, wtok, w_gate[layer], w_up[layer])
        racc = _moe_down(block_e, code, nb, mid, w_down[layer])
        wgu = jnp.concatenate([ws_gate[layer], ws_up[layer]], axis=1).astype(bf16)
        h = _final(h, racc, wgu, ws_down[layer].astype(bf16), row2(ln2_g[layer]), row2(ln2_b[layer]))
    return h.reshape(bsz, seq, d)
```

```python
import functools

import jax
import jax.numpy as jnp
import numpy as np
from jax import lax
from jax.experimental import pallas as pl
from jax.experimental.pallas import tpu as pltpu

D_MODEL = 2048
D_SSM = 1024
D_ATTN = 1024
SSM_CH = 16
SSM_GROUPS = 64
SSM_STATE = 64
HEAD_DIM = 64
N_HEADS = 16
PATTERNS = ((128, 1), (512, 4), (2048, 16))
ATTN_BLOCK = 128
N_EXPERTS = 64
TOP_K = 8
N_EXPERT_GROUPS = 8
TOPK_GROUPS = 4
D_EXPERT = 512
ROUTED_SCALE = 2.5
NORM_EPS = 1e-5
DEPTH = 1
DEEPNORM_ALPHA = (2.0 * DEPTH) ** 0.25

S5_CHUNK = 16
S5_GROUPS_PER_BLOCK = 8
MOE_ROWS = 256
MASK_VALUE = -1e30
PACK_ROWS = D_MODEL // 2 // 128
N_TOKENS = 8192
PAD_CODE = N_TOKENS * TOP_K
NB_SLOT = 1 + N_EXPERTS
NB_NEXT = 1 + 2 * N_EXPERTS
MOE_BLOCKS = -(-(N_TOKENS * TOP_K + N_EXPERTS * (MOE_ROWS - 1)) // MOE_ROWS)
VMEM_LIMIT = 56 * 1024 * 1024
MOE_UP_VMEM_LIMIT = 62 * 1024 * 1024

bf16 = jnp.bfloat16
f32 = jnp.float32


def _params(*sem):
    return pltpu.CompilerParams(dimension_semantics=sem, vmem_limit_bytes=VMEM_LIMIT)


MATMUL_COLS = 1024


def _matmul_kernel(a_ref, b_ref, o_ref):
    a = a_ref[...].astype(bf16)
    for j in range(o_ref.shape[1] // MATMUL_COLS):
        cols = slice(j * MATMUL_COLS, (j + 1) * MATMUL_COLS)
        o_ref[:, cols] = jnp.dot(a, b_ref[:, cols], preferred_element_type=f32).astype(o_ref.dtype)


def _matmul(a, b, out_dtype, tm=256):
    m, k = a.shape
    _, n = b.shape
    return pl.pallas_call(
        _matmul_kernel,
        grid=(m // tm,),
        in_specs=[pl.BlockSpec((tm, k), lambda i: (i, 0)),
                  pl.BlockSpec((k, n), lambda i: (0, 0), pipeline_mode=pl.Buffered(1))],
        out_specs=pl.BlockSpec((tm, n), lambda i: (i, 0)),
        out_shape=jax.ShapeDtypeStruct((m, n), out_dtype),
        compiler_params=_params("parallel"),
        name="matmul",
    )(a, b)


def _s5_tables(log_dt, a_re, a_im, b_re, b_im, c_re, c_im, d_skip):
    t = S5_CHUNK
    gpb = S5_GROUPS_PER_BLOCK
    nblk = SSM_GROUPS // gpb
    hp = lax.Precision.HIGHEST
    lr = jnp.minimum(a_re.astype(f32), -1e-4)
    li = a_im.astype(f32)
    dt = jnp.exp(log_dt.astype(f32))
    kk = jnp.arange(t + 1, dtype=f32)[:, None, None]
    mag = jnp.exp(kk * (lr * dt))
    pr = mag * jnp.cos(kk * (li * dt))
    pi = mag * jnp.sin(kk * (li * dt))
    xr, xi = pr[1] - 1.0, pi[1]
    den = lr * lr + li * li
    cr = (xr * lr + xi * li) / den
    ci = (xi * lr - xr * li) / den
    bbr = cr[..., None] * b_re - ci[..., None] * b_im
    bbi = cr[..., None] * b_im + ci[..., None] * b_re
    wr = pr[:t, :, :, None] * bbr - pi[:t, :, :, None] * bbi
    wi = pr[:t, :, :, None] * bbi + pi[:t, :, :, None] * bbr
    taps = (jnp.einsum('gop,tgpc->tgco', c_re, wr, precision=hp)
            - jnp.einsum('gop,tgpc->tgco', c_im, wi, precision=hp))
    ktab = taps.reshape(t, nblk, gpb, SSM_CH, SSM_CH).transpose(1, 0, 3, 2, 4)
    ktab = ktab.reshape(nblk, t, SSM_CH, gpb * SSM_CH)

    rev = jnp.arange(t - 1, -1, -1)
    sr = pr[rev][..., None] * bbr - pi[rev][..., None] * bbi
    si = pr[rev][..., None] * bbi + pi[rev][..., None] * bbr
    sb = jnp.stack([sr, si], axis=0).reshape(2, t, nblk, gpb, SSM_STATE, SSM_CH)
    bsrc = sb.transpose(2, 1, 0, 3, 5, 4).reshape(nblk, t, 2, gpb * SSM_CH, SSM_STATE)
    bsrc = jnp.concatenate([bsrc, bsrc], axis=-1)

    er = c_re[None] * pr[1:, :, None, :] - c_im[None] * pi[1:, :, None, :]
    ei = c_re[None] * pi[1:, :, None, :] + c_im[None] * pr[1:, :, None, :]
    eb = jnp.stack([er, -ei], axis=0).reshape(2, t, nblk, gpb, SSM_CH, SSM_STATE)
    csrc = eb.transpose(2, 1, 0, 5, 3, 4).reshape(nblk, t, 2, SSM_STATE, gpb * SSM_CH)

    a_chunk = jnp.stack([pr[t], pi[t]], axis=0).reshape(2, nblk, 1, gpb * SSM_STATE)
    a_chunk = a_chunk.transpose(1, 0, 2, 3).reshape(nblk, 2, gpb * SSM_STATE)
    dvec = jnp.tile(d_skip.astype(f32).reshape(nblk, 1, gpb * SSM_CH), (1, 1, t))
    return ktab.astype(bf16), bsrc.astype(bf16), csrc.astype(bf16), a_chunk, dvec


def _s5_kernel(u_ref, ktab_ref, bsrc_ref, csrc_ref, a_ref, d_ref, y_ref,
               toep_ref, bpow_ref, cpow_ref, s_ref, h_ref, yt_ref, *, n_batch, n_chunk):
    t = S5_CHUNK
    gpb = S5_GROUPS_PER_BLOCK
    w = gpb * SSM_CH
    ns = gpb * SSM_STATE
    zero = jnp.zeros((), bf16)

    def same_group(shape, row_size, col_size):
        r = lax.broadcasted_iota(jnp.int32, shape, 0) // row_size
        c = lax.broadcasted_iota(jnp.int32, shape, 1) // col_size
        return r == c

    tap_mask = same_group((w, w), SSM_CH, SSM_CH)
    taps = [jnp.where(tap_mask, jnp.tile(ktab_ref[0, tau], (gpb, 1)), zero) for tau in range(t)]
    for tt in range(t):
        for ss in range(tt + 1):
            toep_ref[ss * w:(ss + 1) * w, tt * w:(tt + 1) * w] = taps[tt - ss]
        if tt % 2 == 0:
            toep_ref[(tt + 1) * w:(tt + 2) * w, tt * w:(tt + 1) * w] = jnp.zeros((w, w), bf16)
    b_mask = same_group((w, ns), SSM_CH, SSM_STATE)
    c_mask = same_group((ns, w), SSM_STATE, SSM_CH)
    for ss in range(t):
        for z in range(2):
            bpow_ref[ss * w:(ss + 1) * w, z * ns:(z + 1) * ns] = jnp.where(
                b_mask, jnp.tile(bsrc_ref[0, ss, z], (1, ns // w)), zero)
            cpow_ref[z * ns:(z + 1) * ns, ss * w:(ss + 1) * w] = jnp.where(
                c_mask, jnp.tile(csrc_ref[0, ss, z], (gpb, 1)), zero)

    rows = n_batch * n_chunk
    uf3 = pltpu.einshape("rsl->srl", u_ref[...].reshape(rows, t, w))
    uf = [uf3[s] for s in range(t)]
    u = jnp.concatenate([p.astype(bf16) for p in uf], axis=1)
    s_ref[...] = jnp.dot(u, bpow_ref[...], preferred_element_type=f32)

    ar = a_ref[0, 0:1, :]
    ai = a_ref[0, 1:2, :]

    def step(j, carry):
        new = []
        for b in range(n_batch):
            hr, hi = carry[2 * b], carry[2 * b + 1]
            row = b * n_chunk + j
            h_ref[pl.ds(row, 1), 0:ns] = hr
            h_ref[pl.ds(row, 1), ns:2 * ns] = hi
            sr = s_ref[pl.ds(row, 1), 0:ns]
            si = s_ref[pl.ds(row, 1), ns:2 * ns]
            new.append(ar * hr - ai * hi + sr)
            new.append(ar * hi + ai * hr + si)
        return tuple(new)

    zero = jnp.zeros((1, ns), f32)
    lax.fori_loop(0, n_chunk, step, (zero,) * (2 * n_batch))

    hprev = h_ref[...].astype(bf16)
    for tp in range(t // 2):
        c0, c1 = 2 * tp * w, (2 * tp + 2) * w
        y = jnp.dot(u[:, :c1], toep_ref[0:c1, c0:c1], preferred_element_type=f32)
        y = y + jnp.dot(hprev, cpow_ref[:, c0:c1], preferred_element_type=f32)
        for k, tt in enumerate((2 * tp, 2 * tp + 1)):
            yk = y[:, k * w:(k + 1) * w] + d_ref[0, :, tt * w:(tt + 1) * w] * uf[tt]
            yt_ref[tt] = jax.nn.gelu(yk, approximate=True)
    y_ref[...] = pltpu.einshape("srl->rsl", yt_ref[...]).reshape(rows * t, w)


def _s5_mixer(proj, tables, n_batch, n_chunk):
    ktab, bsrc, csrc, a_chunk, dvec = tables
    nblk = ktab.shape[0]
    w = S5_GROUPS_PER_BLOCK * SSM_CH
    cols = S5_CHUNK * w
    rows = n_batch * n_chunk
    n_tok = rows * S5_CHUNK
    ns = S5_GROUPS_PER_BLOCK * SSM_STATE
    kern = functools.partial(_s5_kernel, n_batch=n_batch, n_chunk=n_chunk)
    return pl.pallas_call(
        kern,
        grid=(nblk,),
        in_specs=[pl.BlockSpec((n_tok, w), lambda g: (0, g)),
                  pl.BlockSpec((1,) + ktab.shape[1:], lambda g: (g, 0, 0, 0)),
                  pl.BlockSpec((1,) + bsrc.shape[1:], lambda g: (g, 0, 0, 0, 0)),
                  pl.BlockSpec((1,) + csrc.shape[1:], lambda g: (g, 0, 0, 0, 0)),
                  pl.BlockSpec((1, 2, ns), lambda g: (g, 0, 0)),
                  pl.BlockSpec((1, 1, cols), lambda g: (g, 0, 0))],
        out_specs=pl.BlockSpec((n_tok, w), lambda g: (0, g)),
        out_shape=jax.ShapeDtypeStruct((n_tok, D_SSM), f32),
        scratch_shapes=[pltpu.VMEM((cols, cols), bf16),
                        pltpu.VMEM((cols, 2 * ns), bf16),
                        pltpu.VMEM((2 * ns, cols), bf16),
                        pltpu.VMEM((rows, 2 * ns), f32),
                        pltpu.VMEM((rows, 2 * ns), f32),
                        pltpu.VMEM((S5_CHUNK, rows, w), f32)],
        compiler_params=_params("parallel"),
        name="s5_mixer",
    )(proj, ktab, bsrc, csrc, a_chunk, dvec)


def _attn_bias_table():
    blk = ATTN_BLOCK
    slopes = 2.0 ** (-8.0 * jnp.arange(1, N_HEADS + 1, dtype=f32) / N_HEADS)
    delta = np.arange(blk)[:, None] - (np.arange(2 * blk)[None, :] - blk)
    tabs = []
    for window, dil in PATTERNS:
        assert window // dil == blk
        valid = (delta >= 0) & (delta <= window // dil)
        dist = jnp.asarray(delta * dil, dtype=f32)
        bias = jnp.where(valid[None], -slopes[:, None, None] * dist[None], MASK_VALUE)
        tabs.append(bias.reshape(N_HEADS // 2, 2 * blk, 2 * blk))
    return jnp.stack(tabs, axis=1)


def _attn_kernel(q_ref, k_ref, v_ref, bias_ref, o_ref, *scr):
    blk = ATTN_BLOCK
    seq = q_ref.shape[0]
    first_head = lax.broadcasted_iota(jnp.int32, (blk, 2 * HEAD_DIM), 1) < HEAD_DIM
    dims = (((1,), (1,)), ((), ()))
    for pi, (_, dil) in enumerate(PATTERNS):
        o_scr, l_scr = scr[2 * pi], scr[2 * pi + 1]
        sub = seq // dil
        for r in range(dil):
            rows = (lambda st, n: pl.ds(st, n)) if dil == 1 else (lambda st, n: pl.ds(st, n, stride=dil))
            qd = (q_ref[rows(r, sub), :] * HEAD_DIM ** -0.5).astype(bf16)
            kd = k_ref[rows(r, sub), :].astype(bf16)
            vd = v_ref[rows(r, sub), :].astype(bf16)
            for i in range(sub // blk):
                qb = qd[i * blk:(i + 1) * blk]
                zero = jnp.zeros_like(qb)
                q2 = jnp.concatenate([jnp.where(first_head, qb, zero), jnp.where(first_head, zero, qb)], axis=0)
                k0 = max(i - 1, 0) * blk
                nk = (i + 1) * blk - k0
                s = lax.dot_general(q2, kd[k0:k0 + nk], dims, preferred_element_type=f32)
                s = s + bias_ref[0, pi, :, 2 * blk - nk:]
                m = jnp.max(s, axis=-1, keepdims=True)
                p = jnp.exp(s - m)
                l = jnp.sum(p, axis=-1, keepdims=True)
                o = jnp.dot(p.astype(bf16), vd[k0:k0 + nk], preferred_element_type=f32) / l
                lse = m + jnp.log(l)
                dst = rows(r + dil * blk * i, blk)
                o_scr[dst, :] = jnp.where(first_head, o[:blk], o[blk:])
                l_scr[dst, :] = jnp.where(first_head, lse[:blk], lse[blk:])
    l1, l2, l3 = scr[1][...], scr[3][...], scr[5][...]
    m = jnp.maximum(jnp.maximum(l1, l2), l3)
    e1, e2, e3 = jnp.exp(l1 - m), jnp.exp(l2 - m), jnp.exp(l3 - m)
    o_ref[...] = (e1 * scr[0][...] + e2 * scr[2][...] + e3 * scr[4][...]) / (e1 + e2 + e3)


def _dilated_attention(qkv, first, bsz, seq):
    pairs = N_HEADS // 2
    width = 2 * HEAD_DIM
    bias = _attn_bias_table()
    col = lambda off: pl.BlockSpec((seq, width), lambda hp, b: (b, first + off + hp))
    return pl.pallas_call(
        _attn_kernel,
        grid=(pairs, bsz),
        in_specs=[col(0), col(pairs), col(2 * pairs),
                  pl.BlockSpec((1,) + bias.shape[1:], lambda hp, b: (hp, 0, 0, 0))],
        out_specs=pl.BlockSpec((seq, width), lambda hp, b: (b, hp)),
        out_shape=jax.ShapeDtypeStruct((bsz * seq, D_ATTN), f32),
        scratch_shapes=[pltpu.VMEM((seq, width), f32)] * (2 * len(PATTERNS)),
        compiler_params=_params("parallel", "parallel"),
        name="dilated_attention",
    )(qkv, qkv, qkv, bias)


def _layer_norm_rows(x, g, b):
    mu = jnp.mean(x, axis=-1, keepdims=True)
    xc = x - mu
    var = jnp.mean(xc * xc, axis=-1, keepdims=True)
    return xc * lax.rsqrt(var + NORM_EPS) * g + b


def _rms_rows(x, g):
    return x * lax.rsqrt(jnp.mean(x * x, axis=-1, keepdims=True) + NORM_EPS) * g


def _mix_out_kernel(y_ref, ya_ref, x_ref, wglu_ref, w_ref, gs_ref, ga_ref, lg_ref, lb_ref, h_ref, hp_ref):
    z = jnp.dot(y_ref[...].astype(bf16), wglu_ref[...], preferred_element_type=f32)
    y_ssm = z[:, :D_SSM] * jax.nn.sigmoid(z[:, D_SSM:])
    ns = _rms_rows(y_ssm, gs_ref[...]).astype(bf16)
    na = _rms_rows(ya_ref[...], ga_ref[...]).astype(bf16)
    proj = jnp.dot(ns, w_ref[0:D_SSM, :], preferred_element_type=f32)
    proj = proj + jnp.dot(na, w_ref[D_SSM:, :], preferred_element_type=f32)
    h = _layer_norm_rows(DEEPNORM_ALPHA * x_ref[...] + proj, lg_ref[...], lb_ref[...])
    h_ref[...] = h
    half = D_MODEL // 2
    word = pltpu.pack_elementwise([h[:, :half], h[:, half:]], packed_dtype=bf16)
    tm = word.shape[0]
    chunks = jnp.stack([word[:, s * 128:(s + 1) * 128] for s in range(PACK_ROWS)], axis=0)
    hp_ref[...] = pltpu.einshape("srl->rsl", chunks).reshape(tm * PACK_ROWS, 128)


def _mix_out(y, y_attn, x, w_glu, w_out, g_ssm, g_attn, ln_g, ln_b, tm=256):
    n = x.shape[0]
    row = lambda c: pl.BlockSpec((tm, c), lambda i: (i, 0))
    full = lambda a: pl.BlockSpec(a.shape, lambda i: (0,) * a.ndim)
    return pl.pallas_call(
        _mix_out_kernel,
        grid=(n // tm,),
        in_specs=[row(D_SSM), row(D_ATTN), row(D_MODEL), full(w_glu), full(w_out), full(g_ssm), full(g_attn),
                  full(ln_g), full(ln_b)],
        out_specs=[row(D_MODEL), pl.BlockSpec((tm * PACK_ROWS, 128), lambda i: (i, 0))],
        out_shape=[jax.ShapeDtypeStruct((n, D_MODEL), f32),
                   jax.ShapeDtypeStruct((n * PACK_ROWS, 128), jnp.int32)],
        compiler_params=_params("parallel"),
        name="mix_out",
    )(y, y_attn, x, w_glu, w_out, g_ssm, g_attn, ln_g, ln_b)


def _router_kernel(h_ref, wrt_ref, bias_ref, tri_ref, trie_ref, e8_ref, pos8_ref, wtok_ref, cnt_ref):
    gsz = N_EXPERTS // N_EXPERT_GROUPS
    tm = h_ref.shape[0]
    ninf = -jnp.inf

    @pl.when(pl.program_id(0) == 0)
    def _():
        cnt_ref[...] = jnp.zeros_like(cnt_ref)

    h = h_ref[...]
    h_hi = h.astype(bf16)
    h_lo = (h - h_hi.astype(f32)).astype(bf16)
    dims = (((1,), (1,)), ((), ()))
    w_hi, w_lo = wrt_ref[0], wrt_ref[1]
    logits = (lax.dot_general(w_hi, h_hi, dims, preferred_element_type=f32)
              + lax.dot_general(w_hi, h_lo, dims, preferred_element_type=f32)
              + lax.dot_general(w_lo, h_hi, dims, preferred_element_type=f32))
    scores = jax.nn.sigmoid(logits)
    sel = scores + bias_ref[...]
    io = lax.broadcasted_iota(jnp.int32, (gsz, tm), 0)

    blks, gs_rows = [], []
    for g in range(N_EXPERT_GROUPS):
        blk = sel[g * gsz:(g + 1) * gsz, :]
        m1 = jnp.max(blk, axis=0, keepdims=True)
        first = jnp.min(jnp.where(blk == m1, io, gsz), axis=0, keepdims=True)
        m2 = jnp.max(jnp.where(io == first, ninf, blk), axis=0, keepdims=True)
        blks.append(blk)
        gs_rows.append(m1 + m2)
    gs = jnp.concatenate(gs_rows, axis=0)

    iog = lax.broadcasted_iota(jnp.int32, (N_EXPERT_GROUPS, tm), 0)
    beaten = jnp.zeros((N_EXPERT_GROUPS, tm), f32)
    for gp in range(N_EXPERT_GROUPS):
        row = gs_rows[gp]
        tie = jnp.where(iog > gp, 1.0, 0.0)
        beaten = beaten + jnp.where(row > gs, 1.0, jnp.where(row == gs, tie, 0.0))
    keep = beaten < TOPK_GROUPS
    masked = [jnp.where(keep[g:g + 1, :], blks[g], ninf) for g in range(N_EXPERT_GROUPS)]

    cand = jnp.concatenate(masked, axis=0)
    eid = lax.broadcasted_iota(jnp.int32, (N_EXPERTS, tm), 0)
    selb = jnp.zeros((N_EXPERTS, tm), f32)
    for _ in range(TOP_K):
        best = jnp.max(cand, axis=0, keepdims=True)
        pick = jnp.min(jnp.where(cand == best, eid, N_EXPERTS), axis=0, keepdims=True)
        hit = eid == pick
        selb = jnp.where(hit, 1.0, selb)
        cand = jnp.where(hit, ninf, cand)
    wsel = selb * scores
    wn = wsel / jnp.sum(wsel, axis=0, keepdims=True) * ROUTED_SCALE

    maskb = selb.astype(bf16)
    pos = jnp.dot(maskb, tri_ref[...], preferred_element_type=f32) + cnt_ref[:, 0:1]
    cnt_ref[...] = cnt_ref[...] + jnp.sum(selb, axis=1, keepdims=True)
    slot = jnp.dot(trie_ref[...], maskb, preferred_element_type=f32)
    ioe = lax.broadcasted_iota(jnp.int32, (N_EXPERTS, tm), 0).astype(f32)
    e_rows, p_rows = [], []
    for k in range(TOP_K):
        hit = jnp.where(slot == k, selb, 0.0)
        e_rows.append(jnp.sum(hit * ioe, axis=0, keepdims=True))
        p_rows.append(jnp.sum(hit * pos, axis=0, keepdims=True))
    e8_ref[...] = jnp.concatenate(e_rows, axis=0).astype(jnp.int32)
    pos8_ref[...] = jnp.concatenate(p_rows, axis=0).astype(jnp.int32)
    wtok_ref[...] = jnp.concatenate([wn.T, jnp.zeros((tm, 128 - N_EXPERTS), f32)], axis=1)


def _router(h, w_router, router_bias, tm=512):
    n = h.shape[0]
    wt = w_router.astype(f32).T
    wt_hi = wt.astype(bf16)
    wrt = jnp.stack([wt_hi, (wt - wt_hi.astype(f32)).astype(bf16)])
    bias = router_bias.astype(f32).reshape(N_EXPERTS, 1)
    tri = (jnp.arange(tm)[:, None] < jnp.arange(tm)[None, :]).astype(bf16)
    trie = (jnp.arange(N_EXPERTS)[None, :] < jnp.arange(N_EXPERTS)[:, None]).astype(bf16)
    full = lambda a: pl.BlockSpec(a.shape, lambda i: (0,) * a.ndim)
    tok = lambda: pl.BlockSpec((TOP_K, tm), lambda i: (0, i))
    return pl.pallas_call(
        _router_kernel,
        grid=(n // tm,),
        in_specs=[pl.BlockSpec((tm, D_MODEL), lambda i: (i, 0)), full(wrt), full(bias), full(tri), full(trie)],
        out_specs=[tok(), tok(), pl.BlockSpec((tm, 128), lambda i: (i, 0)),
                   pl.BlockSpec((N_EXPERTS, 128), lambda i: (0, 0))],
        out_shape=[jax.ShapeDtypeStruct((TOP_K, n), jnp.int32), jax.ShapeDtypeStruct((TOP_K, n), jnp.int32),
                   jax.ShapeDtypeStruct((n, 128), f32), jax.ShapeDtypeStruct((N_EXPERTS, 128), f32)],
        compiler_params=_params("arbitrary"),
        name="router",
    )(h, wrt, bias, tri, trie)


INVERT_UNROLL = 16


def _invert_kernel(dest_ref, nb_ref, code_ref):
    n_rows = code_ref.shape[0]

    def fill(first_group, last_group):
        def body(k, c):
            for u in range(INVERT_UNROLL):
                code_ref[k * INVERT_UNROLL + u] = PAD_CODE
            return c
        lax.fori_loop(first_group, last_group, body, 0)

    def per_expert(e, c):
        groups = MOE_ROWS // INVERT_UNROLL + 1
        start = jnp.minimum(nb_ref[1 + e], n_rows - groups * INVERT_UNROLL) // INVERT_UNROLL
        fill(start, start + groups)
        return c
    lax.fori_loop(0, N_EXPERTS, per_expert, 0)
    fill(nb_ref[0] * (MOE_ROWS // INVERT_UNROLL), n_rows // INVERT_UNROLL)

    def body(p, c):
        code_ref[dest_ref[p]] = p
        return c
    lax.fori_loop(0, dest_ref.shape[0], body, 0, unroll=INVERT_UNROLL)


def _invert(dest, nb, n_rows):
    smem = lambda: pl.BlockSpec(memory_space=pltpu.SMEM)
    return pl.pallas_call(
        _invert_kernel,
        in_specs=[smem(), smem()],
        out_specs=smem(),
        out_shape=jax.ShapeDtypeStruct((n_rows,), jnp.int32),
        name="invert_dispatch",
    )(dest, nb)


def _expert_changed(be_ref, i):
    return jnp.logical_or(i == 0, be_ref[i] != be_ref[jnp.maximum(i - 1, 0)])


SCATTER_UNROLL = 8


def _moe_up_kernel(be_ref, code_ref, nb_ref, hp_ref, wtok_ref, wg_ref, wu_ref, mid_ref,
                   xg_scr, xa_scr, xb_scr, wra_scr, wrb_scr, wgu_scr, wstage_scr, wsem):
    i = pl.program_id(0)
    rows = MOE_ROWS

    def gather_block(blk, x_dst, wrow_dst):
        base = blk * rows
        for r in range(rows):
            tok = (code_ref[base + r] >> 3) & (N_TOKENS - 1)
            xg_scr[r * PACK_ROWS:(r + 1) * PACK_ROWS, :] = (
                hp_ref[pl.ds(pl.multiple_of(tok * PACK_ROWS, PACK_ROWS), PACK_ROWS), :])
            wrow_dst[r:r + 1, :] = wtok_ref[pl.ds(tok, 1), :]
        half = D_MODEL // 2
        chunks = pltpu.einshape("rsl->srl", xg_scr[...].reshape(rows, PACK_ROWS, 128))
        for s in range(PACK_ROWS):
            wds = chunks[s]
            lo = pltpu.unpack_elementwise(wds, index=0, packed_dtype=bf16, unpacked_dtype=f32)
            hi = pltpu.unpack_elementwise(wds, index=1, packed_dtype=bf16, unpacked_dtype=f32)
            x_dst[:, s * 128:(s + 1) * 128] = lo.astype(bf16)
            x_dst[:, half + s * 128:half + (s + 1) * 128] = hi.astype(bf16)

    def weight_copies(e, s):
        return (pltpu.make_async_copy(wg_ref.at[e], wstage_scr.at[s, 0], wsem.at[s, 0]),
                pltpu.make_async_copy(wu_ref.at[e], wstage_scr.at[s, 1], wsem.at[s, 1]))

    def expert_block(blk, out_rows, x_cur, wrow_cur, x_nxt, wrow_nxt):
        last = MOE_BLOCKS - 1
        expert = be_ref[jnp.minimum(blk, last)]
        slot = nb_ref[NB_SLOT + expert]
        upcoming = nb_ref[NB_NEXT + expert]
        valid = blk < nb_ref[0]

        @pl.when(jnp.logical_and(valid, blk == 0))
        def _():
            for cp in weight_copies(expert, slot):
                cp.start()

        @pl.when(jnp.logical_and(valid, _expert_changed(be_ref, jnp.minimum(blk, last))))
        def _():
            for cp in weight_copies(expert, slot):
                cp.wait()

            @pl.when(upcoming >= 0)
            def _():
                for cp in weight_copies(upcoming, 1 - slot):
                    cp.start()

            wgu_scr[:, :D_EXPERT] = wstage_scr[slot, 0].astype(bf16)
            wgu_scr[:, D_EXPERT:] = wstage_scr[slot, 1].astype(bf16)

        @pl.when(valid)
        def _():
            gather_block(jnp.minimum(blk + 1, last), x_nxt, wrow_nxt)
            lane = lax.broadcasted_iota(jnp.int32, (rows, 128), 1)
            w = jnp.sum(jnp.where(lane == expert, wrow_cur[...], 0.0), axis=1, keepdims=True)
            real = blk * rows + lax.broadcasted_iota(jnp.int32, (rows, 1), 0) < nb_ref[1 + expert]
            gu = jnp.dot(x_cur[...], wgu_scr[...], preferred_element_type=f32)
            g, u = gu[:, :D_EXPERT], gu[:, D_EXPERT:]
            mid_ref[out_rows, :] = (g * jax.nn.sigmoid(g) * u * jnp.where(real, w, 0.0)).astype(bf16)

        @pl.when(jnp.logical_not(valid))
        def _():
            mid_ref[out_rows, :] = jnp.zeros((rows, D_EXPERT), bf16)

    @pl.when(i == 0)
    def _():
        gather_block(0, xa_scr, wra_scr)

    expert_block(2 * i, slice(0, rows), xa_scr, wra_scr, xb_scr, wrb_scr)
    expert_block(2 * i + 1, slice(rows, 2 * rows), xb_scr, wrb_scr, xa_scr, wra_scr)


def _moe_up(block_e, code, nb, hp, wtok, w_gate, w_up):
    grid_spec = pltpu.PrefetchScalarGridSpec(
        num_scalar_prefetch=3,
        grid=(MOE_BLOCKS // 2,),
        in_specs=[pl.BlockSpec(memory_space=pltpu.VMEM),
                  pl.BlockSpec(memory_space=pltpu.VMEM),
                  pl.BlockSpec(memory_space=pl.ANY),
                  pl.BlockSpec(memory_space=pl.ANY)],
        out_specs=pl.BlockSpec((2 * MOE_ROWS, D_EXPERT), lambda i, be, cd, nb: (i, 0)),
        scratch_shapes=[pltpu.VMEM((PACK_ROWS * MOE_ROWS, 128), jnp.int32),
                        pltpu.VMEM((MOE_ROWS, D_MODEL), bf16), pltpu.VMEM((MOE_ROWS, D_MODEL), bf16),
                        pltpu.VMEM((MOE_ROWS, 128), f32), pltpu.VMEM((MOE_ROWS, 128), f32),
                        pltpu.VMEM((D_MODEL, 2 * D_EXPERT), bf16),
                        pltpu.VMEM((2, 2, D_MODEL, D_EXPERT), f32),
                        pltpu.SemaphoreType.DMA((2, 2))])
    return pl.pallas_call(
        _moe_up_kernel,
        grid_spec=grid_spec,
        out_shape=jax.ShapeDtypeStruct((MOE_BLOCKS * MOE_ROWS, D_EXPERT), bf16),
        compiler_params=pltpu.CompilerParams(dimension_semantics=("arbitrary",),
                                             vmem_limit_bytes=MOE_UP_VMEM_LIMIT),
        name="moe_up",
    )(block_e, code, nb, hp, wtok, w_gate, w_up)


def _moe_down_kernel(be_ref, code_ref, nb_ref, mida_ref, midb_ref, wd_ref, acc_ref,
                     ya_scr, yb0_scr, yb1_scr, wa_scr, wb_scr, wstage_scr, wsem):
    j = pl.program_id(1)
    rows = MOE_ROWS
    nb = nb_ref[0]
    blk_a, blk_b = 2 * j, 2 * j + 1
    last = MOE_BLOCKS - 1

    @pl.when(j == 0)
    def _():
        acc_ref[...] = jnp.zeros_like(acc_ref)

    def scatter_rows(base, ybuf, r0, n):
        sums, addrs = [], []
        for k in range(n):
            a = pl.multiple_of(code_ref[base + r0 + k] & -PACK_ROWS, PACK_ROWS)
            v = ybuf[pl.ds(pl.multiple_of((r0 + k) * PACK_ROWS, PACK_ROWS), PACK_ROWS), :]
            sums.append(acc_ref[pl.ds(a, PACK_ROWS), :] + v)
            addrs.append(a)
        for k in range(n):
            acc_ref[pl.ds(addrs[k], PACK_ROWS), :] = sums[k]

    def scatter_block(blk, ybuf):
        for g in range(rows // SCATTER_UNROLL):
            scatter_rows(blk * rows, ybuf, g * SCATTER_UNROLL, SCATTER_UNROLL)

    def scatter_block_compact(blk, ybuf):
        def body(g, c):
            scatter_rows(blk * rows, ybuf, g * SCATTER_UNROLL, SCATTER_UNROLL)
            return c
        lax.fori_loop(0, rows // SCATTER_UNROLL, body, 0)

    def down_block(mid_ref, w_scr, ybuf):
        y = jnp.dot(mid_ref[...], w_scr[...], preferred_element_type=f32)
        y3 = jnp.stack([y[:, c * 128:(c + 1) * 128] for c in range(PACK_ROWS)], axis=0)
        ybuf[...] = pltpu.einshape("crl->rcl", y3).reshape(rows * PACK_ROWS, 128)

    half = D_MODEL // 2
    cols = pl.ds(pl.multiple_of(pl.program_id(0) * half, half), half)

    def weight_copy(e, s):
        return pltpu.make_async_copy(wd_ref.at[e, :, cols], wstage_scr.at[s], wsem.at[s])

    def refresh_weights(blk, w_scr):
        cur = be_ref[jnp.minimum(blk, last)]
        slot = nb_ref[NB_SLOT + cur]
        upcoming = nb_ref[NB_NEXT + cur]
        valid = blk < nb

        @pl.when(jnp.logical_and(valid, blk == 0))
        def _():
            weight_copy(cur, slot).start()

        @pl.when(jnp.logical_and(valid, _expert_changed(be_ref, jnp.minimum(blk, last))))
        def _():
            weight_copy(cur, slot).wait()

            @pl.when(upcoming >= 0)
            def _():
                weight_copy(upcoming, 1 - slot).start()

        prev = be_ref[jnp.clip(blk - 2, 0, last)]

        @pl.when(jnp.logical_and(valid, jnp.logical_or(j == 0, cur != prev)))
        def _():
            w_scr[...] = wstage_scr[slot].astype(bf16)

    refresh_weights(blk_a, wa_scr)
    refresh_weights(blk_b, wb_scr)

    for parity, yb_this, yb_prev in ((0, yb0_scr, yb1_scr), (1, yb1_scr, yb0_scr)):
        mine = j % 2 == parity

        @pl.when(jnp.logical_and(mine, jnp.logical_and(j > 0, blk_b < nb)))
        def _():
            down_block(mida_ref, wa_scr, ya_scr)
            down_block(midb_ref, wb_scr, yb_this)
            scatter_block(blk_a - 1, yb_prev)
            scatter_block(blk_a, ya_scr)

        @pl.when(jnp.logical_and(mine, jnp.logical_and(j > 0, blk_b == nb)))
        def _():
            down_block(mida_ref, wa_scr, ya_scr)
            scatter_block_compact(blk_a - 1, yb_prev)
            scatter_block_compact(blk_a, ya_scr)

        @pl.when(jnp.logical_and(mine, jnp.logical_and(j > 0, blk_a == nb)))
        def _():
            scatter_block_compact(blk_a - 1, yb_prev)

    @pl.when(jnp.logical_and(j == 0, blk_a < nb))
    def _():
        down_block(mida_ref, wa_scr, ya_scr)
        scatter_block_compact(blk_a, ya_scr)

    @pl.when(jnp.logical_and(j == 0, blk_b < nb))
    def _():
        down_block(midb_ref, wb_scr, yb0_scr)


def _moe_down(block_e, code, nb, mid, w_down):
    half = D_MODEL // 2
    acc_rows = (N_TOKENS + 1) * PACK_ROWS
    last = MOE_BLOCKS - 1
    assert MOE_BLOCKS % 2 == 0
    blk = lambda off: (lambda p, j, be, cd, nb: (jnp.minimum(2 * j + off, last), 0))
    grid_spec = pltpu.PrefetchScalarGridSpec(
        num_scalar_prefetch=3,
        grid=(2, MOE_BLOCKS // 2 + 1),
        in_specs=[pl.BlockSpec((MOE_ROWS, D_EXPERT), blk(0)), pl.BlockSpec((MOE_ROWS, D_EXPERT), blk(1)),
                  pl.BlockSpec(memory_space=pl.ANY)],
        out_specs=pl.BlockSpec((None, acc_rows, 128), lambda p, j, be, cd, nb: (p, 0, 0),
                               pipeline_mode=pl.Buffered(1)),
        scratch_shapes=[pltpu.VMEM((PACK_ROWS * MOE_ROWS, 128), f32)] * 3
        + [pltpu.VMEM((D_EXPERT, half), bf16)] * 2
        + [pltpu.VMEM((2, D_EXPERT, half), f32), pltpu.SemaphoreType.DMA((2,))])
    return pl.pallas_call(
        _moe_down_kernel,
        grid_spec=grid_spec,
        out_shape=jax.ShapeDtypeStruct((2, acc_rows, 128), f32),
        compiler_params=_params("arbitrary", "arbitrary"),
        name="moe_down",
    )(block_e, code, nb, mid, mid, w_down)


def _final_kernel(h_ref, r0_ref, r1_ref, wgu_ref, wd_ref, lg_ref, lb_ref, o_ref):
    tm = h_ref.shape[0]
    gu = jnp.dot(h_ref[...].astype(bf16), wgu_ref[...], preferred_element_type=f32)
    g, u = gu[:, :D_EXPERT], gu[:, D_EXPERT:]
    mid = (g * jax.nn.sigmoid(g) * u).astype(bf16)
    shared = jnp.dot(mid, wd_ref[...], preferred_element_type=f32)
    halves = [pltpu.einshape("rcl->crl", r[...].reshape(tm, PACK_ROWS, 128)) for r in (r0_ref, r1_ref)]
    routed = jnp.concatenate([hv[c] for hv in halves for c in range(PACK_ROWS)], axis=1)
    o_ref[...] = _layer_norm_rows(DEEPNORM_ALPHA * h_ref[...] + routed + shared,
                                  lg_ref[...], lb_ref[...])


def _final(h, racc, wgu, wd, ln_g, ln_b, tm=256):
    n = h.shape[0]
    row = lambda: pl.BlockSpec((tm, D_MODEL), lambda i: (i, 0))
    full = lambda a: pl.BlockSpec(a.shape, lambda i: (0,) * a.ndim)
    acc = lambda p: pl.BlockSpec((None, tm * PACK_ROWS, 128), lambda i: (p, i, 0))
    return pl.pallas_call(
        _final_kernel,
        grid=(n // tm,),
        in_specs=[row(), acc(0), acc(1), full(wgu), full(wd), full(ln_g), full(ln_b)],
        out_specs=row(),
        out_shape=jax.ShapeDtypeStruct((n, D_MODEL), f32),
        compiler_params=_params("parallel"),
        name="shared_final",
    )(h, racc, racc, wgu, wd, ln_g, ln_b)


def _dispatch_plan(e8, pos8, cnt):
    counts = cnt[:, 0].astype(jnp.int32)
    padded = (counts + MOE_ROWS - 1) // MOE_ROWS * MOE_ROWS
    pad_end = jnp.cumsum(padded).astype(jnp.int32)
    pad_start = pad_end - padded
    seg_end = pad_start + counts
    ids = jnp.arange(N_EXPERTS, dtype=jnp.int32)
    start8 = jnp.sum(jnp.where(e8[..., None] == ids, pad_start, 0), axis=-1)
    dest = (start8 + pos8).T.reshape(-1)
    block_start = jnp.arange(MOE_BLOCKS, dtype=jnp.int32) * MOE_ROWS
    block_e = jnp.minimum(jnp.sum((pad_end[None, :] <= block_start[:, None]).astype(jnp.int32), axis=1),
                          N_EXPERTS - 1)
    used = counts > 0
    slot = (jnp.cumsum(used.astype(jnp.int32)) - 1) % 2
    later = jnp.where(used, ids, N_EXPERTS)
    nxt = jnp.concatenate([lax.cummin(later[::-1])[::-1][1:], jnp.full((1,), N_EXPERTS, jnp.int32)])
    nxt = jnp.where(nxt == N_EXPERTS, -1, nxt)
    nb = jnp.concatenate([pad_end[-1:] // MOE_ROWS, seg_end, slot, nxt]).astype(jnp.int32)
    return dest, block_e, nb


def kernel(x, w_in, ssm_log_dt, ssm_a_re, ssm_a_im, ssm_b_re, ssm_b_im, ssm_c_re, ssm_c_im, ssm_d,
           w_glu, g_ssm_out, g_attn_out, w_out, ln1_g, ln1_b, w_router, router_bias, w_gate, w_up,
           w_down, ws_gate, ws_up, ws_down, ln2_g, ln2_b):
    bsz, seq, d = x.shape
    n_tok = bsz * seq
    h = x.reshape(n_tok, d)
    for layer in range(DEPTH):
        proj = _matmul(h, w_in[layer].astype(bf16), f32)

        tables = _s5_tables(ssm_log_dt[layer], ssm_a_re[layer], ssm_a_im[layer], ssm_b_re[layer],
                            ssm_b_im[layer], ssm_c_re[layer], ssm_c_im[layer], ssm_d[layer])
        y = _s5_mixer(proj, tables, bsz, seq // S5_CHUNK)

        y_attn = _dilated_attention(proj, D_SSM // (2 * HEAD_DIM), bsz, seq)

        row2 = lambda a: a.astype(f32).reshape(1, -1)
        h, hp = _mix_out(y, y_attn, h, w_glu[layer].astype(bf16), w_out[layer].astype(bf16),
                         row2(g_ssm_out[layer]), row2(g_attn_out[layer]), row2(ln1_g[layer]), row2(ln1_b[layer]))

        assert n_tok == N_TOKENS
        e8, pos8, wtok, cnt = _router(h, w_router[layer], router_bias[layer])
        dest, block_e, nb = _dispatch_plan(e8, pos8, cnt)
        code = _invert(dest, nb, MOE_BLOCKS * MOE_ROWS)
        mid = _moe_up(block_e, code, nb, hp, wtok, w_gate[layer], w_up[layer])
        racc = _moe_down(block_e, code, nb, mid, w_down[layer])
        wgu = jnp.concatenate([ws_gate[layer], ws_up[layer]], axis=1).astype(bf16)
        h = _final(h, racc, wgu, ws_down[layer].astype(bf16), row2(ln2_g[layer]), row2(ln2_b[layer]))
    return h.reshape(bsz, seq, d)
```

```python
import functools

import jax
import jax.numpy as jnp
import numpy as np
from jax import lax
from jax.experimental import pallas as pl
from jax.experimental.pallas import tpu as pltpu

D_MODEL = 2048
D_SSM = 1024
D_ATTN = 1024
SSM_CH = 16
SSM_GROUPS = 64
SSM_STATE = 64
HEAD_DIM = 64
N_HEADS = 16
PATTERNS = ((128, 1), (512, 4), (2048, 16))
ATTN_BLOCK = 128
N_EXPERTS = 64
TOP_K = 8
N_EXPERT_GROUPS = 8
TOPK_GROUPS = 4
D_EXPERT = 512
ROUTED_SCALE = 2.5
NORM_EPS = 1e-5
DEPTH = 1
DEEPNORM_ALPHA = (2.0 * DEPTH) ** 0.25

LANES = 128
SUBLANES = 8
V7X_VMEM_BYTES = 64 * 1024 * 1024
S5_CHUNK = 16
S5_GROUPS_PER_BLOCK = LANES // SSM_CH
MOE_ROWS = 256
MASK_VALUE = -1e30
PACK_ROWS = D_MODEL // 2 // LANES
assert PACK_ROWS == SUBLANES
N_TOKENS = 8192
PAD_CODE = N_TOKENS * TOP_K
NB_SLOT = 1 + N_EXPERTS
NB_NEXT = 1 + 2 * N_EXPERTS
MOE_BLOCKS = -(-(N_TOKENS * TOP_K + N_EXPERTS * (MOE_ROWS - 1)) // MOE_ROWS)
VMEM_LIMIT = V7X_VMEM_BYTES - 8 * 1024 * 1024
MOE_UP_VMEM_LIMIT = V7X_VMEM_BYTES - 2 * 1024 * 1024

bf16 = jnp.bfloat16
f32 = jnp.float32


def _params(*sem):
    return pltpu.CompilerParams(dimension_semantics=sem, vmem_limit_bytes=VMEM_LIMIT)


MATMUL_COLS = 1024


def _matmul_kernel(a_ref, b_ref, o_ref):
    a = a_ref[...].astype(bf16)
    for j in range(o_ref.shape[1] // MATMUL_COLS):
        cols = slice(j * MATMUL_COLS, (j + 1) * MATMUL_COLS)
        o_ref[:, cols] = jnp.dot(a, b_ref[:, cols], preferred_element_type=f32).astype(o_ref.dtype)


def _matmul(a, b, out_dtype, tm=256):
    m, k = a.shape
    _, n = b.shape
    return pl.pallas_call(
        _matmul_kernel,
        grid=(m // tm,),
        in_specs=[pl.BlockSpec((tm, k), lambda i: (i, 0)),
                  pl.BlockSpec((k, n), lambda i: (0, 0), pipeline_mode=pl.Buffered(1))],
        out_specs=pl.BlockSpec((tm, n), lambda i: (i, 0)),
        out_shape=jax.ShapeDtypeStruct((m, n), out_dtype),
        compiler_params=_params("parallel"),
        name="matmul",
    )(a, b)


def _s5_tables(log_dt, a_re, a_im, b_re, b_im, c_re, c_im, d_skip):
    t = S5_CHUNK
    gpb = S5_GROUPS_PER_BLOCK
    nblk = SSM_GROUPS // gpb
    hp = lax.Precision.HIGHEST
    lr = jnp.minimum(a_re.astype(f32), -1e-4)
    li = a_im.astype(f32)
    dt = jnp.exp(log_dt.astype(f32))
    kk = jnp.arange(t + 1, dtype=f32)[:, None, None]
    mag = jnp.exp(kk * (lr * dt))
    pr = mag * jnp.cos(kk * (li * dt))
    pi = mag * jnp.sin(kk * (li * dt))
    xr, xi = pr[1] - 1.0, pi[1]
    den = lr * lr + li * li
    cr = (xr * lr + xi * li) / den
    ci = (xi * lr - xr * li) / den
    bbr = cr[..., None] * b_re - ci[..., None] * b_im
    bbi = cr[..., None] * b_im + ci[..., None] * b_re
    wr = pr[:t, :, :, None] * bbr - pi[:t, :, :, None] * bbi
    wi = pr[:t, :, :, None] * bbi + pi[:t, :, :, None] * bbr
    taps = (jnp.einsum('gop,tgpc->tgco', c_re, wr, precision=hp)
            - jnp.einsum('gop,tgpc->tgco', c_im, wi, precision=hp))
    ktab = taps.reshape(t, nblk, gpb, SSM_CH, SSM_CH).transpose(1, 0, 3, 2, 4)
    ktab = ktab.reshape(nblk, t, SSM_CH, gpb * SSM_CH)

    rev = jnp.arange(t - 1, -1, -1)
    sr = pr[rev][..., None] * bbr - pi[rev][..., None] * bbi
    si = pr[rev][..., None] * bbi + pi[rev][..., None] * bbr
    sb = jnp.stack([sr, si], axis=0).reshape(2, t, nblk, gpb, SSM_STATE, SSM_CH)
    bsrc = sb.transpose(2, 1, 0, 3, 5, 4).reshape(nblk, t, 2, gpb * SSM_CH, SSM_STATE)
    bsrc = jnp.concatenate([bsrc, bsrc], axis=-1)

    er = c_re[None] * pr[1:, :, None, :] - c_im[None] * pi[1:, :, None, :]
    ei = c_re[None] * pi[1:, :, None, :] + c_im[None] * pr[1:, :, None, :]
    eb = jnp.stack([er, -ei], axis=0).reshape(2, t, nblk, gpb, SSM_CH, SSM_STATE)
    csrc = eb.transpose(2, 1, 0, 5, 3, 4).reshape(nblk, t, 2, SSM_STATE, gpb * SSM_CH)

    a_chunk = jnp.stack([pr[t], pi[t]], axis=0).reshape(2, nblk, 1, gpb * SSM_STATE)
    a_chunk = a_chunk.transpose(1, 0, 2, 3).reshape(nblk, 2, gpb * SSM_STATE)
    dvec = jnp.tile(d_skip.astype(f32).reshape(nblk, 1, gpb * SSM_CH), (1, 1, t))
    return ktab.astype(bf16), bsrc.astype(bf16), csrc.astype(bf16), a_chunk, dvec


def _s5_kernel(u_ref, ktab_ref, bsrc_ref, csrc_ref, a_ref, d_ref, y_ref,
               toep_ref, bpow_ref, cpow_ref, s_ref, h_ref, yt_ref, *, n_batch, n_chunk):
    t = S5_CHUNK
    gpb = S5_GROUPS_PER_BLOCK
    w = gpb * SSM_CH
    ns = gpb * SSM_STATE
    zero = jnp.zeros((), bf16)

    def same_group(shape, row_size, col_size):
        r = lax.broadcasted_iota(jnp.int32, shape, 0) // row_size
        c = lax.broadcasted_iota(jnp.int32, shape, 1) // col_size
        return r == c

    tap_mask = same_group((w, w), SSM_CH, SSM_CH)
    taps = [jnp.where(tap_mask, jnp.tile(ktab_ref[0, tau], (gpb, 1)), zero) for tau in range(t)]
    for tt in range(t):
        for ss in range(tt + 1):
            toep_ref[ss * w:(ss + 1) * w, tt * w:(tt + 1) * w] = taps[tt - ss]
        if tt % 2 == 0:
            toep_ref[(tt + 1) * w:(tt + 2) * w, tt * w:(tt + 1) * w] = jnp.zeros((w, w), bf16)
    b_mask = same_group((w, ns), SSM_CH, SSM_STATE)
    c_mask = same_group((ns, w), SSM_STATE, SSM_CH)
    for ss in range(t):
        for z in range(2):
            bpow_ref[ss * w:(ss + 1) * w, z * ns:(z + 1) * ns] = jnp.where(
                b_mask, jnp.tile(bsrc_ref[0, ss, z], (1, ns // w)), zero)
            cpow_ref[z * ns:(z + 1) * ns, ss * w:(ss + 1) * w] = jnp.where(
                c_mask, jnp.tile(csrc_ref[0, ss, z], (gpb, 1)), zero)

    rows = n_batch * n_chunk
    uf3 = pltpu.einshape("rsl->srl", u_ref[...].reshape(rows, t, w))
    uf = [uf3[s] for s in range(t)]
    u = jnp.concatenate([p.astype(bf16) for p in uf], axis=1)
    s_ref[...] = jnp.dot(u, bpow_ref[...], preferred_element_type=f32)

    ar = a_ref[0, 0:1, :]
    ai = a_ref[0, 1:2, :]

    def step(j, carry):
        new = []
        for b in range(n_batch):
            hr, hi = carry[2 * b], carry[2 * b + 1]
            row = b * n_chunk + j
            h_ref[pl.ds(row, 1), 0:ns] = hr
            h_ref[pl.ds(row, 1), ns:2 * ns] = hi
            sr = s_ref[pl.ds(row, 1), 0:ns]
            si = s_ref[pl.ds(row, 1), ns:2 * ns]
            new.append(ar * hr - ai * hi + sr)
            new.append(ar * hi + ai * hr + si)
        return tuple(new)

    zero = jnp.zeros((1, ns), f32)
    lax.fori_loop(0, n_chunk, step, (zero,) * (2 * n_batch))

    hprev = h_ref[...].astype(bf16)
    for tp in range(t // 2):
        c0, c1 = 2 * tp * w, (2 * tp + 2) * w
        y = jnp.dot(u[:, :c1], toep_ref[0:c1, c0:c1], preferred_element_type=f32)
        y = y + jnp.dot(hprev, cpow_ref[:, c0:c1], preferred_element_type=f32)
        for k, tt in enumerate((2 * tp, 2 * tp + 1)):
            yk = y[:, k * w:(k + 1) * w] + d_ref[0, :, tt * w:(tt + 1) * w] * uf[tt]
            yt_ref[tt] = jax.nn.gelu(yk, approximate=True)
    y_ref[...] = pltpu.einshape("srl->rsl", yt_ref[...]).reshape(rows * t, w)


def _s5_mixer(proj, tables, n_batch, n_chunk):
    ktab, bsrc, csrc, a_chunk, dvec = tables
    nblk = ktab.shape[0]
    w = S5_GROUPS_PER_BLOCK * SSM_CH
    cols = S5_CHUNK * w
    rows = n_batch * n_chunk
    n_tok = rows * S5_CHUNK
    ns = S5_GROUPS_PER_BLOCK * SSM_STATE
    kern = functools.partial(_s5_kernel, n_batch=n_batch, n_chunk=n_chunk)
    return pl.pallas_call(
        kern,
        grid=(nblk,),
        in_specs=[pl.BlockSpec((n_tok, w), lambda g: (0, g)),
                  pl.BlockSpec((1,) + ktab.shape[1:], lambda g: (g, 0, 0, 0)),
                  pl.BlockSpec((1,) + bsrc.shape[1:], lambda g: (g, 0, 0, 0, 0)),
                  pl.BlockSpec((1,) + csrc.shape[1:], lambda g: (g, 0, 0, 0, 0)),
                  pl.BlockSpec((1, 2, ns), lambda g: (g, 0, 0)),
                  pl.BlockSpec((1, 1, cols), lambda g: (g, 0, 0))],
        out_specs=pl.BlockSpec((n_tok, w), lambda g: (0, g)),
        out_shape=jax.ShapeDtypeStruct((n_tok, D_SSM), f32),
        scratch_shapes=[pltpu.VMEM((cols, cols), bf16),
                        pltpu.VMEM((cols, 2 * ns), bf16),
                        pltpu.VMEM((2 * ns, cols), bf16),
                        pltpu.VMEM((rows, 2 * ns), f32),
                        pltpu.VMEM((rows, 2 * ns), f32),
                        pltpu.VMEM((S5_CHUNK, rows, w), f32)],
        compiler_params=_params("parallel"),
        name="s5_mixer",
    )(proj, ktab, bsrc, csrc, a_chunk, dvec)


def _attn_bias_table():
    blk = ATTN_BLOCK
    slopes = 2.0 ** (-8.0 * jnp.arange(1, N_HEADS + 1, dtype=f32) / N_HEADS)
    delta = np.arange(blk)[:, None] - (np.arange(2 * blk)[None, :] - blk)
    tabs = []
    for window, dil in PATTERNS:
        assert window // dil == blk
        valid = (delta >= 0) & (delta <= window // dil)
        dist = jnp.asarray(delta * dil, dtype=f32)
        bias = jnp.where(valid[None], -slopes[:, None, None] * dist[None], MASK_VALUE)
        tabs.append(bias.reshape(N_HEADS // 2, 2 * blk, 2 * blk))
    return jnp.stack(tabs, axis=1)


def _attn_kernel(q_ref, k_ref, v_ref, bias_ref, o_ref, *scr):
    blk = ATTN_BLOCK
    seq = q_ref.shape[0]
    first_head = lax.broadcasted_iota(jnp.int32, (blk, 2 * HEAD_DIM), 1) < HEAD_DIM
    dims = (((1,), (1,)), ((), ()))
    for pi, (_, dil) in enumerate(PATTERNS):
        o_scr, l_scr = scr[2 * pi], scr[2 * pi + 1]
        sub = seq // dil
        for r in range(dil):
            rows = (lambda st, n: pl.ds(st, n)) if dil == 1 else (lambda st, n: pl.ds(st, n, stride=dil))
            qd = (q_ref[rows(r, sub), :] * HEAD_DIM ** -0.5).astype(bf16)
            kd = k_ref[rows(r, sub), :].astype(bf16)
            vd = v_ref[rows(r, sub), :].astype(bf16)
            for i in range(sub // blk):
                qb = qd[i * blk:(i + 1) * blk]
                zero = jnp.zeros_like(qb)
                q2 = jnp.concatenate([jnp.where(first_head, qb, zero), jnp.where(first_head, zero, qb)], axis=0)
                k0 = max(i - 1, 0) * blk
                nk = (i + 1) * blk - k0
                s = lax.dot_general(q2, kd[k0:k0 + nk], dims, preferred_element_type=f32)
                s = s + bias_ref[0, pi, :, 2 * blk - nk:]
                m = jnp.max(s, axis=-1, keepdims=True)
                p = jnp.exp(s - m)
                l = jnp.sum(p, axis=-1, keepdims=True)
                o = jnp.dot(p.astype(bf16), vd[k0:k0 + nk], preferred_element_type=f32) / l
                lse = m + jnp.log(l)
                dst = rows(r + dil * blk * i, blk)
                o_scr[dst, :] = jnp.where(first_head, o[:blk], o[blk:])
                l_scr[dst, :] = jnp.where(first_head, lse[:blk], lse[blk:])
    l1, l2, l3 = scr[1][...], scr[3][...], scr[5][...]
    m = jnp.maximum(jnp.maximum(l1, l2), l3)
    e1, e2, e3 = jnp.exp(l1 - m), jnp.exp(l2 - m), jnp.exp(l3 - m)
    o_ref[...] = (e1 * scr[0][...] + e2 * scr[2][...] + e3 * scr[4][...]) / (e1 + e2 + e3)


def _dilated_attention(qkv, first, bsz, seq):
    pairs = N_HEADS // 2
    width = 2 * HEAD_DIM
    bias = _attn_bias_table()
    col = lambda off: pl.BlockSpec((seq, width), lambda hp, b: (b, first + off + hp))
    return pl.pallas_call(
        _attn_kernel,
        grid=(pairs, bsz),
        in_specs=[col(0), col(pairs), col(2 * pairs),
                  pl.BlockSpec((1,) + bias.shape[1:], lambda hp, b: (hp, 0, 0, 0))],
        out_specs=pl.BlockSpec((seq, width), lambda hp, b: (b, hp)),
        out_shape=jax.ShapeDtypeStruct((bsz * seq, D_ATTN), f32),
        scratch_shapes=[pltpu.VMEM((seq, width), f32)] * (2 * len(PATTERNS)),
        compiler_params=_params("parallel", "parallel"),
        name="dilated_attention",
    )(qkv, qkv, qkv, bias)


def _layer_norm_rows(x, g, b):
    mu = jnp.mean(x, axis=-1, keepdims=True)
    xc = x - mu
    var = jnp.mean(xc * xc, axis=-1, keepdims=True)
    return xc * lax.rsqrt(var + NORM_EPS) * g + b


def _rms_rows(x, g):
    return x * lax.rsqrt(jnp.mean(x * x, axis=-1, keepdims=True) + NORM_EPS) * g


def _mix_out_kernel(y_ref, ya_ref, x_ref, wglu_ref, w_ref, gs_ref, ga_ref, lg_ref, lb_ref, h_ref, hp_ref):
    z = jnp.dot(y_ref[...].astype(bf16), wglu_ref[...], preferred_element_type=f32)
    y_ssm = z[:, :D_SSM] * jax.nn.sigmoid(z[:, D_SSM:])
    ns = _rms_rows(y_ssm, gs_ref[...]).astype(bf16)
    na = _rms_rows(ya_ref[...], ga_ref[...]).astype(bf16)
    proj = jnp.dot(ns, w_ref[0:D_SSM, :], preferred_element_type=f32)
    proj = proj + jnp.dot(na, w_ref[D_SSM:, :], preferred_element_type=f32)
    h = _layer_norm_rows(DEEPNORM_ALPHA * x_ref[...] + proj, lg_ref[...], lb_ref[...])
    h_ref[...] = h
    half = D_MODEL // 2
    word = pltpu.pack_elementwise([h[:, :half], h[:, half:]], packed_dtype=bf16)
    tm = word.shape[0]
    chunks = jnp.stack([word[:, s * LANES:(s + 1) * LANES] for s in range(PACK_ROWS)], axis=0)
    hp_ref[...] = pltpu.einshape("srl->rsl", chunks).reshape(tm * PACK_ROWS, LANES)


def _mix_out(y, y_attn, x, w_glu, w_out, g_ssm, g_attn, ln_g, ln_b, tm=256):
    n = x.shape[0]
    row = lambda c: pl.BlockSpec((tm, c), lambda i: (i, 0))
    full = lambda a: pl.BlockSpec(a.shape, lambda i: (0,) * a.ndim)
    return pl.pallas_call(
        _mix_out_kernel,
        grid=(n // tm,),
        in_specs=[row(D_SSM), row(D_ATTN), row(D_MODEL), full(w_glu), full(w_out), full(g_ssm), full(g_attn),
                  full(ln_g), full(ln_b)],
        out_specs=[row(D_MODEL), pl.BlockSpec((tm * PACK_ROWS, LANES), lambda i: (i, 0))],
        out_shape=[jax.ShapeDtypeStruct((n, D_MODEL), f32),
                   jax.ShapeDtypeStruct((n * PACK_ROWS, LANES), jnp.int32)],
        compiler_params=_params("parallel"),
        name="mix_out",
    )(y, y_attn, x, w_glu, w_out, g_ssm, g_attn, ln_g, ln_b)


def _router_kernel(h_ref, wrt_ref, bias_ref, tri_ref, trie_ref, e8_ref, pos8_ref, wtok_ref, cnt_ref):
    gsz = N_EXPERTS // N_EXPERT_GROUPS
    tm = h_ref.shape[0]
    ninf = -jnp.inf

    @pl.when(pl.program_id(0) == 0)
    def _():
        cnt_ref[...] = jnp.zeros_like(cnt_ref)

    h = h_ref[...]
    h_hi = h.astype(bf16)
    h_lo = (h - h_hi.astype(f32)).astype(bf16)
    dims = (((1,), (1,)), ((), ()))
    w_hi, w_lo = wrt_ref[0], wrt_ref[1]
    logits = (lax.dot_general(w_hi, h_hi, dims, preferred_element_type=f32)
              + lax.dot_general(w_hi, h_lo, dims, preferred_element_type=f32)
              + lax.dot_general(w_lo, h_hi, dims, preferred_element_type=f32))
    scores = jax.nn.sigmoid(logits)
    sel = scores + bias_ref[...]
    io = lax.broadcasted_iota(jnp.int32, (gsz, tm), 0)

    blks, gs_rows = [], []
    for g in range(N_EXPERT_GROUPS):
        blk = sel[g * gsz:(g + 1) * gsz, :]
        m1 = jnp.max(blk, axis=0, keepdims=True)
        first = jnp.min(jnp.where(blk == m1, io, gsz), axis=0, keepdims=True)
        m2 = jnp.max(jnp.where(io == first, ninf, blk), axis=0, keepdims=True)
        blks.append(blk)
        gs_rows.append(m1 + m2)
    gs = jnp.concatenate(gs_rows, axis=0)

    iog = lax.broadcasted_iota(jnp.int32, (N_EXPERT_GROUPS, tm), 0)
    beaten = jnp.zeros((N_EXPERT_GROUPS, tm), f32)
    for gp in range(N_EXPERT_GROUPS):
        row = gs_rows[gp]
        tie = jnp.where(iog > gp, 1.0, 0.0)
        beaten = beaten + jnp.where(row > gs, 1.0, jnp.where(row == gs, tie, 0.0))
    keep = beaten < TOPK_GROUPS
    masked = [jnp.where(keep[g:g + 1, :], blks[g], ninf) for g in range(N_EXPERT_GROUPS)]

    cand = jnp.concatenate(masked, axis=0)
    eid = lax.broadcasted_iota(jnp.int32, (N_EXPERTS, tm), 0)
    selb = jnp.zeros((N_EXPERTS, tm), f32)
    for _ in range(TOP_K):
        best = jnp.max(cand, axis=0, keepdims=True)
        pick = jnp.min(jnp.where(cand == best, eid, N_EXPERTS), axis=0, keepdims=True)
        hit = eid == pick
        selb = jnp.where(hit, 1.0, selb)
        cand = jnp.where(hit, ninf, cand)
    wsel = selb * scores
    wn = wsel / jnp.sum(wsel, axis=0, keepdims=True) * ROUTED_SCALE

    maskb = selb.astype(bf16)
    pos = jnp.dot(maskb, tri_ref[...], preferred_element_type=f32) + cnt_ref[:, 0:1]
    cnt_ref[...] = cnt_ref[...] + jnp.sum(selb, axis=1, keepdims=True)
    slot = jnp.dot(trie_ref[...], maskb, preferred_element_type=f32)
    ioe = lax.broadcasted_iota(jnp.int32, (N_EXPERTS, tm), 0).astype(f32)
    e_rows, p_rows = [], []
    for k in range(TOP_K):
        hit = jnp.where(slot == k, selb, 0.0)
        e_rows.append(jnp.sum(hit * ioe, axis=0, keepdims=True))
        p_rows.append(jnp.sum(hit * pos, axis=0, keepdims=True))
    e8_ref[...] = jnp.concatenate(e_rows, axis=0).astype(jnp.int32)
    pos8_ref[...] = jnp.concatenate(p_rows, axis=0).astype(jnp.int32)
    wtok_ref[...] = jnp.concatenate([wn.T, jnp.zeros((tm, LANES - N_EXPERTS), f32)], axis=1)


def _router(h, w_router, router_bias, tm=512):
    n = h.shape[0]
    wt = w_router.astype(f32).T
    wt_hi = wt.astype(bf16)
    wrt = jnp.stack([wt_hi, (wt - wt_hi.astype(f32)).astype(bf16)])
    bias = router_bias.astype(f32).reshape(N_EXPERTS, 1)
    tri = (jnp.arange(tm)[:, None] < jnp.arange(tm)[None, :]).astype(bf16)
    trie = (jnp.arange(N_EXPERTS)[None, :] < jnp.arange(N_EXPERTS)[:, None]).astype(bf16)
    full = lambda a: pl.BlockSpec(a.shape, lambda i: (0,) * a.ndim)
    tok = lambda: pl.BlockSpec((TOP_K, tm), lambda i: (0, i))
    return pl.pallas_call(
        _router_kernel,
        grid=(n // tm,),
        in_specs=[pl.BlockSpec((tm, D_MODEL), lambda i: (i, 0)), full(wrt), full(bias), full(tri), full(trie)],
        out_specs=[tok(), tok(), pl.BlockSpec((tm, LANES), lambda i: (i, 0)),
                   pl.BlockSpec((N_EXPERTS, LANES), lambda i: (0, 0))],
        out_shape=[jax.ShapeDtypeStruct((TOP_K, n), jnp.int32), jax.ShapeDtypeStruct((TOP_K, n), jnp.int32),
                   jax.ShapeDtypeStruct((n, LANES), f32), jax.ShapeDtypeStruct((N_EXPERTS, LANES), f32)],
        compiler_params=_params("arbitrary"),
        name="router",
    )(h, wrt, bias, tri, trie)


INVERT_UNROLL = 16


def _invert_kernel(dest_ref, nb_ref, code_ref):
    n_rows = code_ref.shape[0]

    def fill(first_group, last_group):
        def body(k, c):
            for u in range(INVERT_UNROLL):
                code_ref[k * INVERT_UNROLL + u] = PAD_CODE
            return c
        lax.fori_loop(first_group, last_group, body, 0)

    def per_expert(e, c):
        groups = MOE_ROWS // INVERT_UNROLL + 1
        start = jnp.minimum(nb_ref[1 + e], n_rows - groups * INVERT_UNROLL) // INVERT_UNROLL
        fill(start, start + groups)
        return c
    lax.fori_loop(0, N_EXPERTS, per_expert, 0)
    fill(nb_ref[0] * (MOE_ROWS // INVERT_UNROLL), n_rows // INVERT_UNROLL)

    def body(p, c):
        code_ref[dest_ref[p]] = p
        return c
    lax.fori_loop(0, dest_ref.shape[0], body, 0, unroll=INVERT_UNROLL)


def _invert(dest, nb, n_rows):
    smem = lambda: pl.BlockSpec(memory_space=pltpu.SMEM)
    return pl.pallas_call(
        _invert_kernel,
        in_specs=[smem(), smem()],
        out_specs=smem(),
        out_shape=jax.ShapeDtypeStruct((n_rows,), jnp.int32),
        name="invert_dispatch",
    )(dest, nb)


def _expert_changed(be_ref, i):
    return jnp.logical_or(i == 0, be_ref[i] != be_ref[jnp.maximum(i - 1, 0)])


SCATTER_UNROLL = 16


def _moe_up_kernel(be_ref, code_ref, nb_ref, hp_ref, wtok_ref, wg_ref, wu_ref, mid_ref,
                   xg_scr, xa_scr, xb_scr, wra_scr, wrb_scr, wgu_scr, wstage_scr, wsem):
    i = pl.program_id(0)
    rows = MOE_ROWS

    def gather_block(blk, x_dst, wrow_dst):
        base = blk * rows
        for r in range(rows):
            tok = (code_ref[base + r] >> 3) & (N_TOKENS - 1)
            xg_scr[r * PACK_ROWS:(r + 1) * PACK_ROWS, :] = (
                hp_ref[pl.ds(pl.multiple_of(tok * PACK_ROWS, PACK_ROWS), PACK_ROWS), :])
            wrow_dst[r:r + 1, :] = wtok_ref[pl.ds(tok, 1), :]
        half = D_MODEL // 2
        chunks = pltpu.einshape("rsl->srl", xg_scr[...].reshape(rows, PACK_ROWS, LANES))
        for s in range(PACK_ROWS):
            wds = chunks[s]
            lo = pltpu.unpack_elementwise(wds, index=0, packed_dtype=bf16, unpacked_dtype=f32)
            hi = pltpu.unpack_elementwise(wds, index=1, packed_dtype=bf16, unpacked_dtype=f32)
            x_dst[:, s * LANES:(s + 1) * LANES] = lo.astype(bf16)
            x_dst[:, half + s * LANES:half + (s + 1) * LANES] = hi.astype(bf16)

    def weight_copies(e, s):
        return (pltpu.make_async_copy(wg_ref.at[e], wstage_scr.at[s, 0], wsem.at[s, 0]),
                pltpu.make_async_copy(wu_ref.at[e], wstage_scr.at[s, 1], wsem.at[s, 1]))

    def expert_block(blk, out_rows, x_cur, wrow_cur, x_nxt, wrow_nxt):
        last = MOE_BLOCKS - 1
        expert = be_ref[jnp.minimum(blk, last)]
        slot = nb_ref[NB_SLOT + expert]
        upcoming = nb_ref[NB_NEXT + expert]
        valid = blk < nb_ref[0]

        @pl.when(jnp.logical_and(valid, blk == 0))
        def _():
            for cp in weight_copies(expert, slot):
                cp.start()

        @pl.when(jnp.logical_and(valid, _expert_changed(be_ref, jnp.minimum(blk, last))))
        def _():
            for cp in weight_copies(expert, slot):
                cp.wait()

            @pl.when(upcoming >= 0)
            def _():
                for cp in weight_copies(upcoming, 1 - slot):
                    cp.start()

            wgu_scr[:, :D_EXPERT] = wstage_scr[slot, 0].astype(bf16)
            wgu_scr[:, D_EXPERT:] = wstage_scr[slot, 1].astype(bf16)

        @pl.when(valid)
        def _():
            gather_block(jnp.minimum(blk + 1, last), x_nxt, wrow_nxt)
            lane = lax.broadcasted_iota(jnp.int32, (rows, LANES), 1)
            w = jnp.sum(jnp.where(lane == expert, wrow_cur[...], 0.0), axis=1, keepdims=True)
            real = blk * rows + lax.broadcasted_iota(jnp.int32, (rows, 1), 0) < nb_ref[1 + expert]
            gu = jnp.dot(x_cur[...], wgu_scr[...], preferred_element_type=f32)
            g, u = gu[:, :D_EXPERT], gu[:, D_EXPERT:]
            mid_ref[out_rows, :] = (g * jax.nn.sigmoid(g) * u * jnp.where(real, w, 0.0)).astype(bf16)

        @pl.when(jnp.logical_not(valid))
        def _():
            mid_ref[out_rows, :] = jnp.zeros((rows, D_EXPERT), bf16)

    @pl.when(i == 0)
    def _():
        gather_block(0, xa_scr, wra_scr)

    expert_block(2 * i, slice(0, rows), xa_scr, wra_scr, xb_scr, wrb_scr)
    expert_block(2 * i + 1, slice(rows, 2 * rows), xb_scr, wrb_scr, xa_scr, wra_scr)


def _moe_up(block_e, code, nb, hp, wtok, w_gate, w_up):
    grid_spec = pltpu.PrefetchScalarGridSpec(
        num_scalar_prefetch=3,
        grid=(MOE_BLOCKS // 2,),
        in_specs=[pl.BlockSpec(memory_space=pltpu.VMEM),
                  pl.BlockSpec(memory_space=pltpu.VMEM),
                  pl.BlockSpec(memory_space=pl.ANY),
                  pl.BlockSpec(memory_space=pl.ANY)],
        out_specs=pl.BlockSpec((2 * MOE_ROWS, D_EXPERT), lambda i, be, cd, nb: (i, 0)),
        scratch_shapes=[pltpu.VMEM((PACK_ROWS * MOE_ROWS, LANES), jnp.int32),
                        pltpu.VMEM((MOE_ROWS, D_MODEL), bf16), pltpu.VMEM((MOE_ROWS, D_MODEL), bf16),
                        pltpu.VMEM((MOE_ROWS, LANES), f32), pltpu.VMEM((MOE_ROWS, LANES), f32),
                        pltpu.VMEM((D_MODEL, 2 * D_EXPERT), bf16),
                        pltpu.VMEM((2, 2, D_MODEL, D_EXPERT), f32),
                        pltpu.SemaphoreType.DMA((2, 2))])
    return pl.pallas_call(
        _moe_up_kernel,
        grid_spec=grid_spec,
        out_shape=jax.ShapeDtypeStruct((MOE_BLOCKS * MOE_ROWS, D_EXPERT), bf16),
        compiler_params=pltpu.CompilerParams(dimension_semantics=("arbitrary",),
                                             vmem_limit_bytes=MOE_UP_VMEM_LIMIT),
        name="moe_up",
    )(block_e, code, nb, hp, wtok, w_gate, w_up)


def _moe_down_kernel(be_ref, code_ref, nb_ref, mida_ref, midb_ref, wd_ref, acc_ref,
                     ya_scr, yb0_scr, yb1_scr, wa_scr, wb_scr, wstage_scr, wsem):
    j = pl.program_id(1)
    rows = MOE_ROWS
    nb = nb_ref[0]
    blk_a, blk_b = 2 * j, 2 * j + 1
    last = MOE_BLOCKS - 1

    @pl.when(j == 0)
    def _():
        acc_ref[...] = jnp.zeros_like(acc_ref)

    def scatter_rows(base, ybuf, r0, n):
        sums, addrs = [], []
        for k in range(n):
            a = pl.multiple_of(code_ref[base + r0 + k] & -PACK_ROWS, PACK_ROWS)
            v = ybuf[pl.ds(pl.multiple_of((r0 + k) * PACK_ROWS, PACK_ROWS), PACK_ROWS), :]
            sums.append(acc_ref[pl.ds(a, PACK_ROWS), :] + v)
            addrs.append(a)
        for k in range(n):
            acc_ref[pl.ds(addrs[k], PACK_ROWS), :] = sums[k]

    def scatter_block(blk, ybuf):
        for g in range(rows // SCATTER_UNROLL):
            scatter_rows(blk * rows, ybuf, g * SCATTER_UNROLL, SCATTER_UNROLL)

    def scatter_block_compact(blk, ybuf):
        def body(g, c):
            scatter_rows(blk * rows, ybuf, g * SCATTER_UNROLL, SCATTER_UNROLL)
            return c
        lax.fori_loop(0, rows // SCATTER_UNROLL, body, 0)

    def down_block(mid_ref, w_scr, ybuf):
        y = jnp.dot(mid_ref[...], w_scr[...], preferred_element_type=f32)
        y3 = jnp.stack([y[:, c * LANES:(c + 1) * LANES] for c in range(PACK_ROWS)], axis=0)
        ybuf[...] = pltpu.einshape("crl->rcl", y3).reshape(rows * PACK_ROWS, LANES)

    half = D_MODEL // 2
    cols = pl.ds(pl.multiple_of(pl.program_id(0) * half, half), half)

    def weight_copy(e, s):
        return pltpu.make_async_copy(wd_ref.at[e, :, cols], wstage_scr.at[s], wsem.at[s])

    def refresh_weights(blk, w_scr):
        cur = be_ref[jnp.minimum(blk, last)]
        slot = nb_ref[NB_SLOT + cur]
        upcoming = nb_ref[NB_NEXT + cur]
        valid = blk < nb

        @pl.when(jnp.logical_and(valid, blk == 0))
        def _():
            weight_copy(cur, slot).start()

        @pl.when(jnp.logical_and(valid, _expert_changed(be_ref, jnp.minimum(blk, last))))
        def _():
            weight_copy(cur, slot).wait()

            @pl.when(upcoming >= 0)
            def _():
                weight_copy(upcoming, 1 - slot).start()

        prev = be_ref[jnp.clip(blk - 2, 0, last)]

        @pl.when(jnp.logical_and(valid, jnp.logical_or(j == 0, cur != prev)))
        def _():
            w_scr[...] = wstage_scr[slot].astype(bf16)

    refresh_weights(blk_a, wa_scr)
    refresh_weights(blk_b, wb_scr)

    for parity, yb_this, yb_prev in ((0, yb0_scr, yb1_scr), (1, yb1_scr, yb0_scr)):
        mine = j % 2 == parity

        @pl.when(jnp.logical_and(mine, jnp.logical_and(j > 0, blk_b < nb)))
        def _():
            down_block(mida_ref, wa_scr, ya_scr)
            down_block(midb_ref, wb_scr, yb_this)
            scatter_block(blk_a - 1, yb_prev)
            scatter_block(blk_a, ya_scr)

        @pl.when(jnp.logical_and(mine, jnp.logical_and(j > 0, blk_b == nb)))
        def _():
            down_block(mida_ref, wa_scr, ya_scr)
            scatter_block_compact(blk_a - 1, yb_prev)
            scatter_block_compact(blk_a, ya_scr)

        @pl.when(jnp.logical_and(mine, jnp.logical_and(j > 0, blk_a == nb)))
        def _():
            scatter_block_compact(blk_a - 1, yb_prev)

    @pl.when(jnp.logical_and(j == 0, blk_a < nb))
    def _():
        down_block(mida_ref, wa_scr, ya_scr)
        scatter_block_compact(blk_a, ya_scr)

    @pl.when(jnp.logical_and(j == 0, blk_b < nb))
    def _():
        down_block(midb_ref, wb_scr, yb0_scr)


def _moe_down(block_e, code, nb, mid, w_down):
    half = D_MODEL // 2
    acc_rows = (N_TOKENS + 1) * PACK_ROWS
    last = MOE_BLOCKS - 1
    assert MOE_BLOCKS % 2 == 0
    blk = lambda off: (lambda p, j, be, cd, nb: (jnp.minimum(2 * j + off, last), 0))
    grid_spec = pltpu.PrefetchScalarGridSpec(
        num_scalar_prefetch=3,
        grid=(2, MOE_BLOCKS // 2 + 1),
        in_specs=[pl.BlockSpec((MOE_ROWS, D_EXPERT), blk(0)), pl.BlockSpec((MOE_ROWS, D_EXPERT), blk(1)),
                  pl.BlockSpec(memory_space=pl.ANY)],
        out_specs=pl.BlockSpec((None, acc_rows, LANES), lambda p, j, be, cd, nb: (p, 0, 0),
                               pipeline_mode=pl.Buffered(1)),
        scratch_shapes=[pltpu.VMEM((PACK_ROWS * MOE_ROWS, LANES), f32)] * 3
        + [pltpu.VMEM((D_EXPERT, half), bf16)] * 2
        + [pltpu.VMEM((2, D_EXPERT, half), f32), pltpu.SemaphoreType.DMA((2,))])
    return pl.pallas_call(
        _moe_down_kernel,
        grid_spec=grid_spec,
        out_shape=jax.ShapeDtypeStruct((2, acc_rows, LANES), f32),
        compiler_params=_params("arbitrary", "arbitrary"),
        name="moe_down",
    )(block_e, code, nb, mid, mid, w_down)


def _final_kernel(h_ref, r0_ref, r1_ref, wgu_ref, wd_ref, lg_ref, lb_ref, o_ref):
    tm = h_ref.shape[0]
    gu = jnp.dot(h_ref[...].astype(bf16), wgu_ref[...], preferred_element_type=f32)
    g, u = gu[:, :D_EXPERT], gu[:, D_EXPERT:]
    mid = (g * jax.nn.sigmoid(g) * u).astype(bf16)
    shared = jnp.dot(mid, wd_ref[...], preferred_element_type=f32)
    halves = [pltpu.einshape("rcl->crl", r[...].reshape(tm, PACK_ROWS, LANES)) for r in (r0_ref, r1_ref)]
    routed = jnp.concatenate([hv[c] for hv in halves for c in range(PACK_ROWS)], axis=1)
    o_ref[...] = _layer_norm_rows(DEEPNORM_ALPHA * h_ref[...] + routed + shared,
                                  lg_ref[...], lb_ref[...])


def _final(h, racc, wgu, wd, ln_g, ln_b, tm=256):
    n = h.shape[0]
    row = lambda: pl.BlockSpec((tm, D_MODEL), lambda i: (i, 0))
    full = lambda a: pl.BlockSpec(a.shape, lambda i: (0,) * a.ndim)
    acc = lambda p: pl.BlockSpec((None, tm * PACK_ROWS, LANES), lambda i: (p, i, 0))
    return pl.pallas_call(
        _final_kernel,
        grid=(n // tm,),
        in_specs=[row(), acc(0), acc(1), full(wgu), full(wd), full(ln_g), full(ln_b)],
        out_specs=row(),
        out_shape=jax.ShapeDtypeStruct((n, D_MODEL), f32),
        compiler_params=_params("parallel"),
        name="shared_final",
    )(h, racc, racc, wgu, wd, ln_g, ln_b)


def _dispatch_plan(e8, pos8, cnt):
    counts = cnt[:, 0].astype(jnp.int32)
    padded = (counts + MOE_ROWS - 1) // MOE_ROWS * MOE_ROWS
    pad_end = jnp.cumsum(padded).astype(jnp.int32)
    pad_start = pad_end - padded
    seg_end = pad_start + counts
    ids = jnp.arange(N_EXPERTS, dtype=jnp.int32)
    start8 = jnp.sum(jnp.where(e8[..., None] == ids, pad_start, 0), axis=-1)
    dest = (start8 + pos8).T.reshape(-1)
    block_start = jnp.arange(MOE_BLOCKS, dtype=jnp.int32) * MOE_ROWS
    block_e = jnp.minimum(jnp.sum((pad_end[None, :] <= block_start[:, None]).astype(jnp.int32), axis=1),
                          N_EXPERTS - 1)
    used = counts > 0
    slot = (jnp.cumsum(used.astype(jnp.int32)) - 1) % 2
    later = jnp.where(used, ids, N_EXPERTS)
    nxt = jnp.concatenate([lax.cummin(later[::-1])[::-1][1:], jnp.full((1,), N_EXPERTS, jnp.int32)])
    nxt = jnp.where(nxt == N_EXPERTS, -1, nxt)
    nb = jnp.concatenate([pad_end[-1:] // MOE_ROWS, seg_end, slot, nxt]).astype(jnp.int32)
    return dest, block_e, nb


def kernel(x, w_in, ssm_log_dt, ssm_a_re, ssm_a_im, ssm_b_re, ssm_b_im, ssm_c_re, ssm_c_im, ssm_d,
           w_glu, g_ssm_out, g_attn_out, w_out, ln1_g, ln1_b, w_router, router_bias, w_gate, w_up,
           w_down, ws_gate, ws_up, ws_down, ln2_g, ln2_b):
    bsz, seq, d = x.shape
    n_tok = bsz * seq
    h = x.reshape(n_tok, d)
    for layer in range(DEPTH):
        proj = _matmul(h, w_in[layer].astype(bf16), f32)

        tables = _s5_tables(ssm_log_dt[layer], ssm_a_re[layer], ssm_a_im[layer], ssm_b_re[layer],
                            ssm_b_im[layer], ssm_c_re[layer], ssm_c_im[layer], ssm_d[layer])
        y = _s5_mixer(proj, tables, bsz, seq // S5_CHUNK)

        y_attn = _dilated_attention(proj, D_SSM // (2 * HEAD_DIM), bsz, seq)

        row2 = lambda a: a.astype(f32).reshape(1, -1)
        h, hp = _mix_out(y, y_attn, h, w_glu[layer].astype(bf16), w_out[layer].astype(bf16),
                         row2(g_ssm_out[layer]), row2(g_attn_out[layer]), row2(ln1_g[layer]), row2(ln1_b[layer]))

        assert n_tok == N_TOKENS
        e8, pos8, wtok, cnt = _router(h, w_router[layer], router_bias[layer])
        dest, block_e, nb = _dispatch_plan(e8, pos8, cnt)
        code = _invert(dest, nb, MOE_BLOCKS * MOE_ROWS)
        mid = _moe_up(block_e, code, nb, hp, wtok, w_gate[layer], w_up[layer])
        racc = _moe_down(block_e, code, nb, mid, w_down[layer])
        wgu = jnp.concatenate([ws_gate[layer], ws_up[layer]], axis=1).astype(bf16)
        h = _final(h, racc, wgu, ws_down[layer].astype(bf16), row2(ln2_g[layer]), row2(ln2_b[layer]))
    return h.reshape(bsz, seq, d)
```

```python
import functools

import jax
import jax.numpy as jnp
import numpy as np
from jax import lax
from jax.experimental import pallas as pl
from jax.experimental.pallas import tpu as pltpu

D_MODEL = 2048
D_SSM = 1024
D_ATTN = 1024
SSM_CH = 16
SSM_GROUPS = 64
SSM_STATE = 64
HEAD_DIM = 64
N_HEADS = 16
PATTERNS = ((128, 1), (512, 4), (2048, 16))
ATTN_BLOCK = 128
N_EXPERTS = 64
TOP_K = 8
N_EXPERT_GROUPS = 8
TOPK_GROUPS = 4
D_EXPERT = 512
ROUTED_SCALE = 2.5
NORM_EPS = 1e-5
DEPTH = 1
DEEPNORM_ALPHA = (2.0 * DEPTH) ** 0.25

LANES = 128
SUBLANES = 8
V7X_VMEM_BYTES = 64 * 1024 * 1024
S5_CHUNK = 16
S5_GROUPS_PER_BLOCK = LANES // SSM_CH
MOE_ROWS = 256
MASK_VALUE = -1e30
PACK_ROWS = D_MODEL // 2 // LANES
assert PACK_ROWS == SUBLANES
N_TOKENS = 8192
PAD_CODE = N_TOKENS * TOP_K
NB_SLOT = 1 + N_EXPERTS
NB_NEXT = 1 + 2 * N_EXPERTS
MOE_BLOCKS = -(-(N_TOKENS * TOP_K + N_EXPERTS * (MOE_ROWS - 1)) // MOE_ROWS)
VMEM_LIMIT = V7X_VMEM_BYTES - 8 * 1024 * 1024
MOE_UP_VMEM_LIMIT = V7X_VMEM_BYTES - 2 * 1024 * 1024

bf16 = jnp.bfloat16
f32 = jnp.float32


def _params(*sem):
    return pltpu.CompilerParams(dimension_semantics=sem, vmem_limit_bytes=VMEM_LIMIT)


MATMUL_COLS = 1024


def _matmul_kernel(a_ref, b_ref, o_ref):
    a = a_ref[...].astype(bf16)
    for j in range(o_ref.shape[1] // MATMUL_COLS):
        cols = slice(j * MATMUL_COLS, (j + 1) * MATMUL_COLS)
        o_ref[:, cols] = jnp.dot(a, b_ref[:, cols], preferred_element_type=f32).astype(o_ref.dtype)


def _matmul(a, b, out_dtype, tm=256):
    m, k = a.shape
    _, n = b.shape
    return pl.pallas_call(
        _matmul_kernel,
        grid=(m // tm,),
        in_specs=[pl.BlockSpec((tm, k), lambda i: (i, 0)),
                  pl.BlockSpec((k, n), lambda i: (0, 0), pipeline_mode=pl.Buffered(1))],
        out_specs=pl.BlockSpec((tm, n), lambda i: (i, 0)),
        out_shape=jax.ShapeDtypeStruct((m, n), out_dtype),
        compiler_params=_params("parallel"),
        name="matmul",
    )(a, b)


def _s5_tables(log_dt, a_re, a_im, b_re, b_im, c_re, c_im, d_skip):
    t = S5_CHUNK
    gpb = S5_GROUPS_PER_BLOCK
    nblk = SSM_GROUPS // gpb
    hp = lax.Precision.HIGHEST
    lr = jnp.minimum(a_re.astype(f32), -1e-4)
    li = a_im.astype(f32)
    dt = jnp.exp(log_dt.astype(f32))
    kk = jnp.arange(t + 1, dtype=f32)[:, None, None]
    mag = jnp.exp(kk * (lr * dt))
    pr = mag * jnp.cos(kk * (li * dt))
    pi = mag * jnp.sin(kk * (li * dt))
    xr, xi = pr[1] - 1.0, pi[1]
    den = lr * lr + li * li
    cr = (xr * lr + xi * li) / den
    ci = (xi * lr - xr * li) / den
    bbr = cr[..., None] * b_re - ci[..., None] * b_im
    bbi = cr[..., None] * b_im + ci[..., None] * b_re
    wr = pr[:t, :, :, None] * bbr - pi[:t, :, :, None] * bbi
    wi = pr[:t, :, :, None] * bbi + pi[:t, :, :, None] * bbr
    taps = (jnp.einsum('gop,tgpc->tgco', c_re, wr, precision=hp)
            - jnp.einsum('gop,tgpc->tgco', c_im, wi, precision=hp))
    ktab = taps.reshape(t, nblk, gpb, SSM_CH, SSM_CH).transpose(1, 0, 3, 2, 4)
    ktab = ktab.reshape(nblk, t, SSM_CH, gpb * SSM_CH)

    rev = jnp.arange(t - 1, -1, -1)
    sr = pr[rev][..., None] * bbr - pi[rev][..., None] * bbi
    si = pr[rev][..., None] * bbi + pi[rev][..., None] * bbr
    sb = jnp.stack([sr, si], axis=0).reshape(2, t, nblk, gpb, SSM_STATE, SSM_CH)
    bsrc = sb.transpose(2, 1, 0, 3, 5, 4).reshape(nblk, t, 2, gpb * SSM_CH, SSM_STATE)
    bsrc = jnp.concatenate([bsrc, bsrc], axis=-1)

    er = c_re[None] * pr[1:, :, None, :] - c_im[None] * pi[1:, :, None, :]
    ei = c_re[None] * pi[1:, :, None, :] + c_im[None] * pr[1:, :, None, :]
    eb = jnp.stack([er, -ei], axis=0).reshape(2, t, nblk, gpb, SSM_CH, SSM_STATE)
    csrc = eb.transpose(2, 1, 0, 5, 3, 4).reshape(nblk, t, 2, SSM_STATE, gpb * SSM_CH)

    a_chunk = jnp.stack([pr[t], pi[t]], axis=0).reshape(2, nblk, 1, gpb * SSM_STATE)
    a_chunk = a_chunk.transpose(1, 0, 2, 3).reshape(nblk, 2, gpb * SSM_STATE)
    dvec = jnp.tile(d_skip.astype(f32).reshape(nblk, 1, gpb * SSM_CH), (1, 1, t))
    return ktab.astype(bf16), bsrc.astype(bf16), csrc.astype(bf16), a_chunk, dvec


def _s5_kernel(u_ref, ktab_ref, bsrc_ref, csrc_ref, a_ref, d_ref, y_ref,
               toep_ref, bpow_ref, cpow_ref, s_ref, h_ref, yt_ref, *, n_batch, n_chunk):
    t = S5_CHUNK
    gpb = S5_GROUPS_PER_BLOCK
    w = gpb * SSM_CH
    ns = gpb * SSM_STATE
    zero = jnp.zeros((), bf16)

    def same_group(shape, row_size, col_size):
        r = lax.broadcasted_iota(jnp.int32, shape, 0) // row_size
        c = lax.broadcasted_iota(jnp.int32, shape, 1) // col_size
        return r == c

    tap_mask = same_group((w, w), SSM_CH, SSM_CH)
    taps = [jnp.where(tap_mask, jnp.tile(ktab_ref[0, tau], (gpb, 1)), zero) for tau in range(t)]
    for tt in range(t):
        for ss in range(tt + 1):
            toep_ref[ss * w:(ss + 1) * w, tt * w:(tt + 1) * w] = taps[tt - ss]
        if tt % 2 == 0:
            toep_ref[(tt + 1) * w:(tt + 2) * w, tt * w:(tt + 1) * w] = jnp.zeros((w, w), bf16)
    b_mask = same_group((w, ns), SSM_CH, SSM_STATE)
    c_mask = same_group((ns, w), SSM_STATE, SSM_CH)
    for ss in range(t):
        for z in range(2):
            bpow_ref[ss * w:(ss + 1) * w, z * ns:(z + 1) * ns] = jnp.where(
                b_mask, jnp.tile(bsrc_ref[0, ss, z], (1, ns // w)), zero)
            cpow_ref[z * ns:(z + 1) * ns, ss * w:(ss + 1) * w] = jnp.where(
                c_mask, jnp.tile(csrc_ref[0, ss, z], (gpb, 1)), zero)

    rows = n_batch * n_chunk
    uf3 = pltpu.einshape("rsl->srl", u_ref[...].reshape(rows, t, w))
    uf = [uf3[s] for s in range(t)]
    u = jnp.concatenate([p.astype(bf16) for p in uf], axis=1)
    s_ref[...] = jnp.dot(u, bpow_ref[...], preferred_element_type=f32)

    ar = a_ref[0, 0:1, :]
    ai = a_ref[0, 1:2, :]

    def step(j, carry):
        new = []
        for b in range(n_batch):
            hr, hi = carry[2 * b], carry[2 * b + 1]
            row = b * n_chunk + j
            h_ref[pl.ds(row, 1), 0:ns] = hr
            h_ref[pl.ds(row, 1), ns:2 * ns] = hi
            sr = s_ref[pl.ds(row, 1), 0:ns]
            si = s_ref[pl.ds(row, 1), ns:2 * ns]
            new.append(ar * hr - ai * hi + sr)
            new.append(ar * hi + ai * hr + si)
        return tuple(new)

    zero = jnp.zeros((1, ns), f32)
    lax.fori_loop(0, n_chunk, step, (zero,) * (2 * n_batch))

    hprev = h_ref[...].astype(bf16)
    for tp in range(t // 2):
        c0, c1 = 2 * tp * w, (2 * tp + 2) * w
        y = jnp.dot(u[:, :c1], toep_ref[0:c1, c0:c1], preferred_element_type=f32)
        y = y + jnp.dot(hprev, cpow_ref[:, c0:c1], preferred_element_type=f32)
        for k, tt in enumerate((2 * tp, 2 * tp + 1)):
            yk = y[:, k * w:(k + 1) * w] + d_ref[0, :, tt * w:(tt + 1) * w] * uf[tt]
            yt_ref[tt] = jax.nn.gelu(yk, approximate=True)
    y_ref[...] = pltpu.einshape("srl->rsl", yt_ref[...]).reshape(rows * t, w)


def _s5_mixer(proj, tables, n_batch, n_chunk):
    ktab, bsrc, csrc, a_chunk, dvec = tables
    nblk = ktab.shape[0]
    w = S5_GROUPS_PER_BLOCK * SSM_CH
    cols = S5_CHUNK * w
    rows = n_batch * n_chunk
    n_tok = rows * S5_CHUNK
    ns = S5_GROUPS_PER_BLOCK * SSM_STATE
    kern = functools.partial(_s5_kernel, n_batch=n_batch, n_chunk=n_chunk)
    return pl.pallas_call(
        kern,
        grid=(nblk,),
        in_specs=[pl.BlockSpec((n_tok, w), lambda g: (0, g)),
                  pl.BlockSpec((1,) + ktab.shape[1:], lambda g: (g, 0, 0, 0)),
                  pl.BlockSpec((1,) + bsrc.shape[1:], lambda g: (g, 0, 0, 0, 0)),
                  pl.BlockSpec((1,) + csrc.shape[1:], lambda g: (g, 0, 0, 0, 0)),
                  pl.BlockSpec((1, 2, ns), lambda g: (g, 0, 0)),
                  pl.BlockSpec((1, 1, cols), lambda g: (g, 0, 0))],
        out_specs=pl.BlockSpec((n_tok, w), lambda g: (0, g)),
        out_shape=jax.ShapeDtypeStruct((n_tok, D_SSM), f32),
        scratch_shapes=[pltpu.VMEM((cols, cols), bf16),
                        pltpu.VMEM((cols, 2 * ns), bf16),
                        pltpu.VMEM((2 * ns, cols), bf16),
                        pltpu.VMEM((rows, 2 * ns), f32),
                        pltpu.VMEM((rows, 2 * ns), f32),
                        pltpu.VMEM((S5_CHUNK, rows, w), f32)],
        compiler_params=_params("parallel"),
        name="s5_mixer",
    )(proj, ktab, bsrc, csrc, a_chunk, dvec)


def _attn_bias_table():
    blk = ATTN_BLOCK
    slopes = 2.0 ** (-8.0 * jnp.arange(1, N_HEADS + 1, dtype=f32) / N_HEADS)
    delta = np.arange(blk)[:, None] - (np.arange(2 * blk)[None, :] - blk)
    tabs = []
    for window, dil in PATTERNS:
        assert window // dil == blk
        valid = (delta >= 0) & (delta <= window // dil)
        dist = jnp.asarray(delta * dil, dtype=f32)
        bias = jnp.where(valid[None], -slopes[:, None, None] * dist[None], MASK_VALUE)
        tabs.append(bias.reshape(N_HEADS // 2, 2 * blk, 2 * blk))
    return jnp.stack(tabs, axis=1)


def _attn_kernel(q_ref, k_ref, v_ref, bias_ref, o_ref, *scr):
    blk = ATTN_BLOCK
    seq = q_ref.shape[0]
    first_head = lax.broadcasted_iota(jnp.int32, (blk, 2 * HEAD_DIM), 1) < HEAD_DIM
    dims = (((1,), (1,)), ((), ()))
    for pi, (_, dil) in enumerate(PATTERNS):
        o_scr, l_scr = scr[2 * pi], scr[2 * pi + 1]
        sub = seq // dil
        for r in range(dil):
            rows = (lambda st, n: pl.ds(st, n)) if dil == 1 else (lambda st, n: pl.ds(st, n, stride=dil))
            qd = (q_ref[rows(r, sub), :] * HEAD_DIM ** -0.5).astype(bf16)
            kd = k_ref[rows(r, sub), :].astype(bf16)
            vd = v_ref[rows(r, sub), :].astype(bf16)
            for i in range(sub // blk):
                qb = qd[i * blk:(i + 1) * blk]
                zero = jnp.zeros_like(qb)
                q2 = jnp.concatenate([jnp.where(first_head, qb, zero), jnp.where(first_head, zero, qb)], axis=0)
                k0 = max(i - 1, 0) * blk
                nk = (i + 1) * blk - k0
                s = lax.dot_general(q2, kd[k0:k0 + nk], dims, preferred_element_type=f32)
                s = s + bias_ref[0, pi, :, 2 * blk - nk:]
                m = jnp.max(s, axis=-1, keepdims=True)
                p = jnp.exp(s - m)
                l = jnp.sum(p, axis=-1, keepdims=True)
                o = jnp.dot(p.astype(bf16), vd[k0:k0 + nk], preferred_element_type=f32) / l
                lse = m + jnp.log(l)
                dst = rows(r + dil * blk * i, blk)
                o_scr[dst, :] = jnp.where(first_head, o[:blk], o[blk:])
                l_scr[dst, :] = jnp.where(first_head, lse[:blk], lse[blk:])
    l1, l2, l3 = scr[1][...], scr[3][...], scr[5][...]
    m = jnp.maximum(jnp.maximum(l1, l2), l3)
    e1, e2, e3 = jnp.exp(l1 - m), jnp.exp(l2 - m), jnp.exp(l3 - m)
    o_ref[...] = (e1 * scr[0][...] + e2 * scr[2][...] + e3 * scr[4][...]) / (e1 + e2 + e3)


def _dilated_attention(qkv, first, bsz, seq):
    pairs = N_HEADS // 2
    width = 2 * HEAD_DIM
    bias = _attn_bias_table()
    col = lambda off: pl.BlockSpec((seq, width), lambda hp, b: (b, first + off + hp))
    return pl.pallas_call(
        _attn_kernel,
        grid=(pairs, bsz),
        in_specs=[col(0), col(pairs), col(2 * pairs),
                  pl.BlockSpec((1,) + bias.shape[1:], lambda hp, b: (hp, 0, 0, 0))],
        out_specs=pl.BlockSpec((seq, width), lambda hp, b: (b, hp)),
        out_shape=jax.ShapeDtypeStruct((bsz * seq, D_ATTN), f32),
        scratch_shapes=[pltpu.VMEM((seq, width), f32)] * (2 * len(PATTERNS)),
        compiler_params=_params("parallel", "parallel"),
        name="dilated_attention",
    )(qkv, qkv, qkv, bias)


def _layer_norm_rows(x, g, b):
    mu = jnp.mean(x, axis=-1, keepdims=True)
    xc = x - mu
    var = jnp.mean(xc * xc, axis=-1, keepdims=True)
    return xc * lax.rsqrt(var + NORM_EPS) * g + b


def _rms_rows(x, g):
    return x * lax.rsqrt(jnp.mean(x * x, axis=-1, keepdims=True) + NORM_EPS) * g


def _mix_out_kernel(y_ref, ya_ref, x_ref, wglu_ref, w_ref, gs_ref, ga_ref, lg_ref, lb_ref, h_ref, hp_ref):
    z = jnp.dot(y_ref[...].astype(bf16), wglu_ref[...], preferred_element_type=f32)
    y_ssm = z[:, :D_SSM] * jax.nn.sigmoid(z[:, D_SSM:])
    ns = _rms_rows(y_ssm, gs_ref[...]).astype(bf16)
    na = _rms_rows(ya_ref[...], ga_ref[...]).astype(bf16)
    proj = jnp.dot(ns, w_ref[0:D_SSM, :], preferred_element_type=f32)
    proj = proj + jnp.dot(na, w_ref[D_SSM:, :], preferred_element_type=f32)
    h = _layer_norm_rows(DEEPNORM_ALPHA * x_ref[...] + proj, lg_ref[...], lb_ref[...])
    h_ref[...] = h
    half = D_MODEL // 2
    word = pltpu.pack_elementwise([h[:, :half], h[:, half:]], packed_dtype=bf16)
    tm = word.shape[0]
    chunks = jnp.stack([word[:, s * LANES:(s + 1) * LANES] for s in range(PACK_ROWS)], axis=0)
    hp_ref[...] = pltpu.einshape("srl->rsl", chunks).reshape(tm * PACK_ROWS, LANES)


def _mix_out(y, y_attn, x, w_glu, w_out, g_ssm, g_attn, ln_g, ln_b, tm=256):
    n = x.shape[0]
    row = lambda c: pl.BlockSpec((tm, c), lambda i: (i, 0))
    full = lambda a: pl.BlockSpec(a.shape, lambda i: (0,) * a.ndim)
    return pl.pallas_call(
        _mix_out_kernel,
        grid=(n // tm,),
        in_specs=[row(D_SSM), row(D_ATTN), row(D_MODEL), full(w_glu), full(w_out), full(g_ssm), full(g_attn),
                  full(ln_g), full(ln_b)],
        out_specs=[row(D_MODEL), pl.BlockSpec((tm * PACK_ROWS, LANES), lambda i: (i, 0))],
        out_shape=[jax.ShapeDtypeStruct((n, D_MODEL), f32),
                   jax.ShapeDtypeStruct((n * PACK_ROWS, LANES), jnp.int32)],
        compiler_params=_params("parallel"),
        name="mix_out",
    )(y, y_attn, x, w_glu, w_out, g_ssm, g_attn, ln_g, ln_b)


def _router_kernel(h_ref, wrt_ref, bias_ref, tri_ref, trie_ref, e8_ref, pos8_ref, w8_ref, cnt_ref):
    gsz = N_EXPERTS // N_EXPERT_GROUPS
    tm = h_ref.shape[0]
    ninf = -jnp.inf

    @pl.when(pl.program_id(0) == 0)
    def _():
        cnt_ref[...] = jnp.zeros_like(cnt_ref)

    h = h_ref[...]
    h_hi = h.astype(bf16)
    h_lo = (h - h_hi.astype(f32)).astype(bf16)
    dims = (((1,), (1,)), ((), ()))
    w_hi, w_lo = wrt_ref[0], wrt_ref[1]
    logits = (lax.dot_general(w_hi, h_hi, dims, preferred_element_type=f32)
              + lax.dot_general(w_hi, h_lo, dims, preferred_element_type=f32)
              + lax.dot_general(w_lo, h_hi, dims, preferred_element_type=f32))
    scores = jax.nn.sigmoid(logits)
    sel = scores + bias_ref[...]
    io = lax.broadcasted_iota(jnp.int32, (gsz, tm), 0)

    blks, gs_rows = [], []
    for g in range(N_EXPERT_GROUPS):
        blk = sel[g * gsz:(g + 1) * gsz, :]
        m1 = jnp.max(blk, axis=0, keepdims=True)
        first = jnp.min(jnp.where(blk == m1, io, gsz), axis=0, keepdims=True)
        m2 = jnp.max(jnp.where(io == first, ninf, blk), axis=0, keepdims=True)
        blks.append(blk)
        gs_rows.append(m1 + m2)
    gs = jnp.concatenate(gs_rows, axis=0)

    iog = lax.broadcasted_iota(jnp.int32, (N_EXPERT_GROUPS, tm), 0)
    beaten = jnp.zeros((N_EXPERT_GROUPS, tm), f32)
    for gp in range(N_EXPERT_GROUPS):
        row = gs_rows[gp]
        tie = jnp.where(iog > gp, 1.0, 0.0)
        beaten = beaten + jnp.where(row > gs, 1.0, jnp.where(row == gs, tie, 0.0))
    keep = beaten < TOPK_GROUPS
    masked = [jnp.where(keep[g:g + 1, :], blks[g], ninf) for g in range(N_EXPERT_GROUPS)]

    cand = jnp.concatenate(masked, axis=0)
    eid = lax.broadcasted_iota(jnp.int32, (N_EXPERTS, tm), 0)
    selb = jnp.zeros((N_EXPERTS, tm), f32)
    for _ in range(TOP_K):
        best = jnp.max(cand, axis=0, keepdims=True)
        pick = jnp.min(jnp.where(cand == best, eid, N_EXPERTS), axis=0, keepdims=True)
        hit = eid == pick
        selb = jnp.where(hit, 1.0, selb)
        cand = jnp.where(hit, ninf, cand)
    wsel = selb * scores
    wn = wsel / jnp.sum(wsel, axis=0, keepdims=True) * ROUTED_SCALE

    maskb = selb.astype(bf16)
    pos = jnp.dot(maskb, tri_ref[...], preferred_element_type=f32) + cnt_ref[:, 0:1]
    cnt_ref[...] = cnt_ref[...] + jnp.sum(selb, axis=1, keepdims=True)
    slot = jnp.dot(trie_ref[...], maskb, preferred_element_type=f32)
    ioe = lax.broadcasted_iota(jnp.int32, (N_EXPERTS, tm), 0).astype(f32)
    e_rows, p_rows, w_rows = [], [], []
    for k in range(TOP_K):
        hit = jnp.where(slot == k, selb, 0.0)
        e_rows.append(jnp.sum(hit * ioe, axis=0, keepdims=True))
        p_rows.append(jnp.sum(hit * pos, axis=0, keepdims=True))
        w_rows.append(jnp.sum(hit * wn, axis=0, keepdims=True))
    e8_ref[...] = jnp.concatenate(e_rows, axis=0).astype(jnp.int32)
    pos8_ref[...] = jnp.concatenate(p_rows, axis=0).astype(jnp.int32)
    w8_ref[...] = jnp.concatenate(w_rows, axis=0)


def _router(h, w_router, router_bias, tm=512):
    n = h.shape[0]
    wt = w_router.astype(f32).T
    wt_hi = wt.astype(bf16)
    wrt = jnp.stack([wt_hi, (wt - wt_hi.astype(f32)).astype(bf16)])
    bias = router_bias.astype(f32).reshape(N_EXPERTS, 1)
    tri = (jnp.arange(tm)[:, None] < jnp.arange(tm)[None, :]).astype(bf16)
    trie = (jnp.arange(N_EXPERTS)[None, :] < jnp.arange(N_EXPERTS)[:, None]).astype(bf16)
    full = lambda a: pl.BlockSpec(a.shape, lambda i: (0,) * a.ndim)
    tok = lambda: pl.BlockSpec((TOP_K, tm), lambda i: (0, i))
    return pl.pallas_call(
        _router_kernel,
        grid=(n // tm,),
        in_specs=[pl.BlockSpec((tm, D_MODEL), lambda i: (i, 0)), full(wrt), full(bias), full(tri), full(trie)],
        out_specs=[tok(), tok(), tok(), pl.BlockSpec((N_EXPERTS, LANES), lambda i: (0, 0))],
        out_shape=[jax.ShapeDtypeStruct((TOP_K, n), jnp.int32), jax.ShapeDtypeStruct((TOP_K, n), jnp.int32),
                   jax.ShapeDtypeStruct((TOP_K, n), f32), jax.ShapeDtypeStruct((N_EXPERTS, LANES), f32)],
        compiler_params=_params("arbitrary"),
        name="router",
    )(h, wrt, bias, tri, trie)


INVERT_UNROLL = 16


def _invert_kernel(dest_ref, nb_ref, code_ref):
    n_rows = code_ref.shape[0]

    def fill(first_group, last_group):
        def body(k, c):
            for u in range(INVERT_UNROLL):
                code_ref[k * INVERT_UNROLL + u] = PAD_CODE
            return c
        lax.fori_loop(first_group, last_group, body, 0)

    def per_expert(e, c):
        groups = MOE_ROWS // INVERT_UNROLL + 1
        start = jnp.minimum(nb_ref[1 + e], n_rows - groups * INVERT_UNROLL) // INVERT_UNROLL
        fill(start, start + groups)
        return c
    lax.fori_loop(0, N_EXPERTS, per_expert, 0)
    fill(nb_ref[0] * (MOE_ROWS // INVERT_UNROLL), n_rows // INVERT_UNROLL)

    def body(p, c):
        code_ref[dest_ref[p]] = p
        return c
    lax.fori_loop(0, dest_ref.shape[0], body, 0, unroll=INVERT_UNROLL)


def _invert(dest, nb, n_rows):
    smem = lambda: pl.BlockSpec(memory_space=pltpu.SMEM)
    return pl.pallas_call(
        _invert_kernel,
        in_specs=[smem(), smem()],
        out_specs=smem(),
        out_shape=jax.ShapeDtypeStruct((n_rows,), jnp.int32),
        name="invert_dispatch",
    )(dest, nb)


def _expert_changed(be_ref, i):
    return jnp.logical_or(i == 0, be_ref[i] != be_ref[jnp.maximum(i - 1, 0)])


SCATTER_UNROLL = 8


def _moe_up_kernel(be_ref, code_ref, nb_ref, w_ref, hp_ref, wg_ref, wu_ref, mid_ref,
                   xg_scr, xa_scr, xb_scr, wra_scr, wrb_scr, wgu_scr, wstage_scr, wsem):
    i = pl.program_id(0)
    rows = MOE_ROWS

    def gather_block(blk, x_dst, wrow_dst):
        base = blk * rows
        for r in range(rows):
            code = code_ref[base + r]
            tok = (code >> 3) & (N_TOKENS - 1)
            xg_scr[r * PACK_ROWS:(r + 1) * PACK_ROWS, :] = (
                hp_ref[pl.ds(pl.multiple_of(tok * PACK_ROWS, PACK_ROWS), PACK_ROWS), :])
            wrow_dst[r:r + 1, :] = jnp.full((1, LANES), w_ref[code], f32)
        half = D_MODEL // 2
        chunks = pltpu.einshape("rsl->srl", xg_scr[...].reshape(rows, PACK_ROWS, LANES))
        for s in range(PACK_ROWS):
            wds = chunks[s]
            lo = pltpu.unpack_elementwise(wds, index=0, packed_dtype=bf16, unpacked_dtype=f32)
            hi = pltpu.unpack_elementwise(wds, index=1, packed_dtype=bf16, unpacked_dtype=f32)
            x_dst[:, s * LANES:(s + 1) * LANES] = lo.astype(bf16)
            x_dst[:, half + s * LANES:half + (s + 1) * LANES] = hi.astype(bf16)

    def weight_copies(e, s):
        return (pltpu.make_async_copy(wg_ref.at[e], wstage_scr.at[s, 0], wsem.at[s, 0]),
                pltpu.make_async_copy(wu_ref.at[e], wstage_scr.at[s, 1], wsem.at[s, 1]))

    def expert_block(blk, out_rows, x_cur, wrow_cur, x_nxt, wrow_nxt):
        last = MOE_BLOCKS - 1
        expert = be_ref[jnp.minimum(blk, last)]
        slot = nb_ref[NB_SLOT + expert]
        upcoming = nb_ref[NB_NEXT + expert]
        valid = blk < nb_ref[0]

        @pl.when(jnp.logical_and(valid, blk == 0))
        def _():
            for cp in weight_copies(expert, slot):
                cp.start()

        @pl.when(jnp.logical_and(valid, _expert_changed(be_ref, jnp.minimum(blk, last))))
        def _():
            for cp in weight_copies(expert, slot):
                cp.wait()

            @pl.when(upcoming >= 0)
            def _():
                for cp in weight_copies(upcoming, 1 - slot):
                    cp.start()

            wgu_scr[:, :D_EXPERT] = wstage_scr[slot, 0].astype(bf16)
            wgu_scr[:, D_EXPERT:] = wstage_scr[slot, 1].astype(bf16)

        @pl.when(valid)
        def _():
            gather_block(jnp.minimum(blk + 1, last), x_nxt, wrow_nxt)
            gu = jnp.dot(x_cur[...], wgu_scr[...], preferred_element_type=f32)
            g, u = gu[:, :D_EXPERT], gu[:, D_EXPERT:]
            w = jnp.concatenate([wrow_cur[...]] * (D_EXPERT // LANES), axis=1)
            mid_ref[out_rows, :] = (g * jax.nn.sigmoid(g) * u * w).astype(bf16)

        @pl.when(jnp.logical_not(valid))
        def _():
            mid_ref[out_rows, :] = jnp.zeros((rows, D_EXPERT), bf16)

    @pl.when(i == 0)
    def _():
        gather_block(0, xa_scr, wra_scr)

    expert_block(2 * i, slice(0, rows), xa_scr, wra_scr, xb_scr, wrb_scr)
    expert_block(2 * i + 1, slice(rows, 2 * rows), xb_scr, wrb_scr, xa_scr, wra_scr)


def _moe_up(block_e, code, nb, w_pairs, hp, w_gate, w_up):
    grid_spec = pltpu.PrefetchScalarGridSpec(
        num_scalar_prefetch=4,
        grid=(MOE_BLOCKS // 2,),
        in_specs=[pl.BlockSpec(memory_space=pltpu.VMEM),
                  pl.BlockSpec(memory_space=pl.ANY),
                  pl.BlockSpec(memory_space=pl.ANY)],
        out_specs=pl.BlockSpec((2 * MOE_ROWS, D_EXPERT), lambda i, be, cd, nb, w: (i, 0)),
        scratch_shapes=[pltpu.VMEM((PACK_ROWS * MOE_ROWS, LANES), jnp.int32),
                        pltpu.VMEM((MOE_ROWS, D_MODEL), bf16), pltpu.VMEM((MOE_ROWS, D_MODEL), bf16),
                        pltpu.VMEM((MOE_ROWS, LANES), f32), pltpu.VMEM((MOE_ROWS, LANES), f32),
                        pltpu.VMEM((D_MODEL, 2 * D_EXPERT), bf16),
                        pltpu.VMEM((2, 2, D_MODEL, D_EXPERT), f32),
                        pltpu.SemaphoreType.DMA((2, 2))])
    return pl.pallas_call(
        _moe_up_kernel,
        grid_spec=grid_spec,
        out_shape=jax.ShapeDtypeStruct((MOE_BLOCKS * MOE_ROWS, D_EXPERT), bf16),
        compiler_params=pltpu.CompilerParams(dimension_semantics=("arbitrary",),
                                             vmem_limit_bytes=MOE_UP_VMEM_LIMIT),
        name="moe_up",
    )(block_e, code, nb, w_pairs, hp, w_gate, w_up)


def _moe_down_kernel(be_ref, code_ref, nb_ref, mida_ref, midb_ref, wd_ref, acc_ref,
                     ya_scr, yb0_scr, yb1_scr, wa_scr, wb_scr, wstage_scr, wsem):
    j = pl.program_id(1)
    rows = MOE_ROWS
    nb = nb_ref[0]
    blk_a, blk_b = 2 * j, 2 * j + 1
    last = MOE_BLOCKS - 1

    @pl.when(j == 0)
    def _():
        acc_ref[...] = jnp.zeros_like(acc_ref)

    def scatter_rows(base, ybuf, r0, n):
        sums, addrs = [], []
        for k in range(n):
            a = pl.multiple_of(code_ref[base + r0 + k] & -PACK_ROWS, PACK_ROWS)
            v = ybuf[pl.ds(pl.multiple_of((r0 + k) * PACK_ROWS, PACK_ROWS), PACK_ROWS), :]
            sums.append(acc_ref[pl.ds(a, PACK_ROWS), :] + v)
            addrs.append(a)
        for k in range(n):
            acc_ref[pl.ds(addrs[k], PACK_ROWS), :] = sums[k]

    def scatter_block(blk, ybuf):
        for g in range(rows // SCATTER_UNROLL):
            scatter_rows(blk * rows, ybuf, g * SCATTER_UNROLL, SCATTER_UNROLL)

    def scatter_block_compact(blk, ybuf):
        def body(g, c):
            scatter_rows(blk * rows, ybuf, g * SCATTER_UNROLL, SCATTER_UNROLL)
            return c
        lax.fori_loop(0, rows // SCATTER_UNROLL, body, 0)

    def down_block(mid_ref, w_scr, ybuf):
        y = jnp.dot(mid_ref[...], w_scr[...], preferred_element_type=f32)
        y3 = jnp.stack([y[:, c * LANES:(c + 1) * LANES] for c in range(PACK_ROWS)], axis=0)
        ybuf[...] = pltpu.einshape("crl->rcl", y3).reshape(rows * PACK_ROWS, LANES)

    half = D_MODEL // 2
    cols = pl.ds(pl.multiple_of(pl.program_id(0) * half, half), half)

    def weight_copy(e, s):
        return pltpu.make_async_copy(wd_ref.at[e, :, cols], wstage_scr.at[s], wsem.at[s])

    def refresh_weights(blk, w_scr):
        cur = be_ref[jnp.minimum(blk, last)]
        slot = nb_ref[NB_SLOT + cur]
        upcoming = nb_ref[NB_NEXT + cur]
        valid = blk < nb

        @pl.when(jnp.logical_and(valid, blk == 0))
        def _():
            weight_copy(cur, slot).start()

        @pl.when(jnp.logical_and(valid, _expert_changed(be_ref, jnp.minimum(blk, last))))
        def _():
            weight_copy(cur, slot).wait()

            @pl.when(upcoming >= 0)
            def _():
                weight_copy(upcoming, 1 - slot).start()

        prev = be_ref[jnp.clip(blk - 2, 0, last)]

        @pl.when(jnp.logical_and(valid, jnp.logical_or(j == 0, cur != prev)))
        def _():
            w_scr[...] = wstage_scr[slot].astype(bf16)

    refresh_weights(blk_a, wa_scr)
    refresh_weights(blk_b, wb_scr)

    for parity, yb_this, yb_prev in ((0, yb0_scr, yb1_scr), (1, yb1_scr, yb0_scr)):
        mine = j % 2 == parity

        @pl.when(jnp.logical_and(mine, jnp.logical_and(j > 0, blk_b < nb)))
        def _():
            down_block(mida_ref, wa_scr, ya_scr)
            down_block(midb_ref, wb_scr, yb_this)
            scatter_block(blk_a - 1, yb_prev)
            scatter_block(blk_a, ya_scr)

        @pl.when(jnp.logical_and(mine, jnp.logical_and(j > 0, blk_b == nb)))
        def _():
            down_block(mida_ref, wa_scr, ya_scr)
            scatter_block_compact(blk_a - 1, yb_prev)
            scatter_block_compact(blk_a, ya_scr)

        @pl.when(jnp.logical_and(mine, jnp.logical_and(j > 0, blk_a == nb)))
        def _():
            scatter_block_compact(blk_a - 1, yb_prev)

    @pl.when(jnp.logical_and(j == 0, blk_a < nb))
    def _():
        down_block(mida_ref, wa_scr, ya_scr)
        scatter_block_compact(blk_a, ya_scr)

    @pl.when(jnp.logical_and(j == 0, blk_b < nb))
    def _():
        down_block(midb_ref, wb_scr, yb0_scr)


def _moe_down(block_e, code, nb, mid, w_down):
    half = D_MODEL // 2
    acc_rows = (N_TOKENS + 1) * PACK_ROWS
    last = MOE_BLOCKS - 1
    assert MOE_BLOCKS % 2 == 0
    blk = lambda off: (lambda p, j, be, cd, nb: (jnp.minimum(2 * j + off, last), 0))
    grid_spec = pltpu.PrefetchScalarGridSpec(
        num_scalar_prefetch=3,
        grid=(2, MOE_BLOCKS // 2 + 1),
        in_specs=[pl.BlockSpec((MOE_ROWS, D_EXPERT), blk(0)), pl.BlockSpec((MOE_ROWS, D_EXPERT), blk(1)),
                  pl.BlockSpec(memory_space=pl.ANY)],
        out_specs=pl.BlockSpec((None, acc_rows, LANES), lambda p, j, be, cd, nb: (p, 0, 0),
                               pipeline_mode=pl.Buffered(1)),
        scratch_shapes=[pltpu.VMEM((PACK_ROWS * MOE_ROWS, LANES), f32)] * 3
        + [pltpu.VMEM((D_EXPERT, half), bf16)] * 2
        + [pltpu.VMEM((2, D_EXPERT, half), f32), pltpu.SemaphoreType.DMA((2,))])
    return pl.pallas_call(
        _moe_down_kernel,
        grid_spec=grid_spec,
        out_shape=jax.ShapeDtypeStruct((2, acc_rows, LANES), f32),
        compiler_params=_params("arbitrary", "arbitrary"),
        name="moe_down",
    )(block_e, code, nb, mid, mid, w_down)


def _final_kernel(h_ref, r0_ref, r1_ref, wgu_ref, wd_ref, lg_ref, lb_ref, o_ref):
    tm = h_ref.shape[0]
    gu = jnp.dot(h_ref[...].astype(bf16), wgu_ref[...], preferred_element_type=f32)
    g, u = gu[:, :D_EXPERT], gu[:, D_EXPERT:]
    mid = (g * jax.nn.sigmoid(g) * u).astype(bf16)
    shared = jnp.dot(mid, wd_ref[...], preferred_element_type=f32)
    halves = [pltpu.einshape("rcl->crl", r[...].reshape(tm, PACK_ROWS, LANES)) for r in (r0_ref, r1_ref)]
    routed = jnp.concatenate([hv[c] for hv in halves for c in range(PACK_ROWS)], axis=1)
    o_ref[...] = _layer_norm_rows(DEEPNORM_ALPHA * h_ref[...] + routed + shared,
                                  lg_ref[...], lb_ref[...])


def _final(h, racc, wgu, wd, ln_g, ln_b, tm=256):
    n = h.shape[0]
    row = lambda: pl.BlockSpec((tm, D_MODEL), lambda i: (i, 0))
    full = lambda a: pl.BlockSpec(a.shape, lambda i: (0,) * a.ndim)
    acc = lambda p: pl.BlockSpec((None, tm * PACK_ROWS, LANES), lambda i: (p, i, 0))
    return pl.pallas_call(
        _final_kernel,
        grid=(n // tm,),
        in_specs=[row(), acc(0), acc(1), full(wgu), full(wd), full(ln_g), full(ln_b)],
        out_specs=row(),
        out_shape=jax.ShapeDtypeStruct((n, D_MODEL), f32),
        compiler_params=_params("parallel"),
        name="shared_final",
    )(h, racc, racc, wgu, wd, ln_g, ln_b)


def _dispatch_plan(e8, pos8, cnt):
    counts = cnt[:, 0].astype(jnp.int32)
    padded = (counts + MOE_ROWS - 1) // MOE_ROWS * MOE_ROWS
    pad_end = jnp.cumsum(padded).astype(jnp.int32)
    pad_start = pad_end - padded
    seg_end = pad_start + counts
    ids = jnp.arange(N_EXPERTS, dtype=jnp.int32)
    start8 = jnp.sum(jnp.where(e8[..., None] == ids, pad_start, 0), axis=-1)
    dest = (start8 + pos8).T.reshape(-1)
    block_start = jnp.arange(MOE_BLOCKS, dtype=jnp.int32) * MOE_ROWS
    block_e = jnp.minimum(jnp.sum((pad_end[None, :] <= block_start[:, None]).astype(jnp.int32), axis=1),
                          N_EXPERTS - 1)
    used = counts > 0
    slot = (jnp.cumsum(used.astype(jnp.int32)) - 1) % 2
    later = jnp.where(used, ids, N_EXPERTS)
    nxt = jnp.concatenate([lax.cummin(later[::-1])[::-1][1:], jnp.full((1,), N_EXPERTS, jnp.int32)])
    nxt = jnp.where(nxt == N_EXPERTS, -1, nxt)
    nb = jnp.concatenate([pad_end[-1:] // MOE_ROWS, seg_end, slot, nxt]).astype(jnp.int32)
    return dest, block_e, nb


def kernel(x, w_in, ssm_log_dt, ssm_a_re, ssm_a_im, ssm_b_re, ssm_b_im, ssm_c_re, ssm_c_im, ssm_d,
           w_glu, g_ssm_out, g_attn_out, w_out, ln1_g, ln1_b, w_router, router_bias, w_gate, w_up,
           w_down, ws_gate, ws_up, ws_down, ln2_g, ln2_b):
    bsz, seq, d = x.shape
    n_tok = bsz * seq
    h = x.reshape(n_tok, d)
    for layer in range(DEPTH):
        proj = _matmul(h, w_in[layer].astype(bf16), f32)

        tables = _s5_tables(ssm_log_dt[layer], ssm_a_re[layer], ssm_a_im[layer], ssm_b_re[layer],
                            ssm_b_im[layer], ssm_c_re[layer], ssm_c_im[layer], ssm_d[layer])
        y = _s5_mixer(proj, tables, bsz, seq // S5_CHUNK)

        y_attn = _dilated_attention(proj, D_SSM // (2 * HEAD_DIM), bsz, seq)

        row2 = lambda a: a.astype(f32).reshape(1, -1)
        h, hp = _mix_out(y, y_attn, h, w_glu[layer].astype(bf16), w_out[layer].astype(bf16),
                         row2(g_ssm_out[layer]), row2(g_attn_out[layer]), row2(ln1_g[layer]), row2(ln1_b[layer]))

        assert n_tok == N_TOKENS
        e8, pos8, w8, cnt = _router(h, w_router[layer], router_bias[layer])
        dest, block_e, nb = _dispatch_plan(e8, pos8, cnt)
        code = _invert(dest, nb, MOE_BLOCKS * MOE_ROWS)
        w_pairs = jnp.concatenate([w8.T.reshape(-1), jnp.zeros((1,), f32)])
        mid = _moe_up(block_e, code, nb, w_pairs, hp, w_gate[layer], w_up[layer])
        racc = _moe_down(block_e, code, nb, mid, w_down[layer])
        wgu = jnp.concatenate([ws_gate[layer], ws_up[layer]], axis=1).astype(bf16)
        h = _final(h, racc, wgu, ws_down[layer].astype(bf16), row2(ln2_g[layer]), row2(ln2_b[layer]))
    return h.reshape(bsz, seq, d)
```

```python
import functools

import jax
import jax.numpy as jnp
import numpy as np
from jax import lax
from jax.experimental import pallas as pl
from jax.experimental.pallas import tpu as pltpu

D_MODEL = 2048
D_SSM = 1024
D_ATTN = 1024
SSM_CH = 16
SSM_GROUPS = 64
SSM_STATE = 64
HEAD_DIM = 64
N_HEADS = 16
PATTERNS = ((128, 1), (512, 4), (2048, 16))
ATTN_BLOCK = 128
N_EXPERTS = 64
TOP_K = 8
N_EXPERT_GROUPS = 8
TOPK_GROUPS = 4
D_EXPERT = 512
ROUTED_SCALE = 2.5
NORM_EPS = 1e-5
DEPTH = 1
DEEPNORM_ALPHA = (2.0 * DEPTH) ** 0.25

LANES = 128
SUBLANES = 8
V7X_VMEM_BYTES = 64 * 1024 * 1024
S5_CHUNK = 16
S5_GROUPS_PER_BLOCK = LANES // SSM_CH
MOE_ROWS = 256
MASK_VALUE = -1e30
PACK_ROWS = D_MODEL // 2 // LANES
assert PACK_ROWS == SUBLANES
N_TOKENS = 8192
PAD_CODE = N_TOKENS * TOP_K
NB_SLOT = 1 + N_EXPERTS
NB_NEXT = 1 + 2 * N_EXPERTS
MOE_BLOCKS = -(-(N_TOKENS * TOP_K + N_EXPERTS * (MOE_ROWS - 1)) // MOE_ROWS)
VMEM_LIMIT = V7X_VMEM_BYTES - 8 * 1024 * 1024
MOE_UP_VMEM_LIMIT = V7X_VMEM_BYTES - 2 * 1024 * 1024

bf16 = jnp.bfloat16
f32 = jnp.float32


def _params(*sem):
    return pltpu.CompilerParams(dimension_semantics=sem, vmem_limit_bytes=VMEM_LIMIT)


MATMUL_COLS = 1024


def _matmul_kernel(a_ref, b_ref, o_ref):
    a = a_ref[...].astype(bf16)
    for j in range(o_ref.shape[1] // MATMUL_COLS):
        cols = slice(j * MATMUL_COLS, (j + 1) * MATMUL_COLS)
        o_ref[:, cols] = jnp.dot(a, b_ref[:, cols], preferred_element_type=f32).astype(o_ref.dtype)


def _matmul(a, b, out_dtype, tm=256):
    m, k = a.shape
    _, n = b.shape
    return pl.pallas_call(
        _matmul_kernel,
        grid=(m // tm,),
        in_specs=[pl.BlockSpec((tm, k), lambda i: (i, 0)),
                  pl.BlockSpec((k, n), lambda i: (0, 0), pipeline_mode=pl.Buffered(1))],
        out_specs=pl.BlockSpec((tm, n), lambda i: (i, 0)),
        out_shape=jax.ShapeDtypeStruct((m, n), out_dtype),
        compiler_params=_params("parallel"),
        name="matmul",
    )(a, b)


def _s5_tables(log_dt, a_re, a_im, b_re, b_im, c_re, c_im, d_skip):
    t = S5_CHUNK
    gpb = S5_GROUPS_PER_BLOCK
    nblk = SSM_GROUPS // gpb
    hp = lax.Precision.HIGHEST
    lr = jnp.minimum(a_re.astype(f32), -1e-4)
    li = a_im.astype(f32)
    dt = jnp.exp(log_dt.astype(f32))
    kk = jnp.arange(t + 1, dtype=f32)[:, None, None]
    mag = jnp.exp(kk * (lr * dt))
    pr = mag * jnp.cos(kk * (li * dt))
    pi = mag * jnp.sin(kk * (li * dt))
    xr, xi = pr[1] - 1.0, pi[1]
    den = lr * lr + li * li
    cr = (xr * lr + xi * li) / den
    ci = (xi * lr - xr * li) / den
    bbr = cr[..., None] * b_re - ci[..., None] * b_im
    bbi = cr[..., None] * b_im + ci[..., None] * b_re
    wr = pr[:t, :, :, None] * bbr - pi[:t, :, :, None] * bbi
    wi = pr[:t, :, :, None] * bbi + pi[:t, :, :, None] * bbr
    taps = (jnp.einsum('gop,tgpc->tgco', c_re, wr, precision=hp)
            - jnp.einsum('gop,tgpc->tgco', c_im, wi, precision=hp))
    ktab = taps.reshape(t, nblk, gpb, SSM_CH, SSM_CH).transpose(1, 0, 3, 2, 4)
    ktab = ktab.reshape(nblk, t, SSM_CH, gpb * SSM_CH)

    rev = jnp.arange(t - 1, -1, -1)
    sr = pr[rev][..., None] * bbr - pi[rev][..., None] * bbi
    si = pr[rev][..., None] * bbi + pi[rev][..., None] * bbr
    sb = jnp.stack([sr, si], axis=0).reshape(2, t, nblk, gpb, SSM_STATE, SSM_CH)
    bsrc = sb.transpose(2, 1, 0, 3, 5, 4).reshape(nblk, t, 2, gpb * SSM_CH, SSM_STATE)
    bsrc = jnp.concatenate([bsrc, bsrc], axis=-1)

    er = c_re[None] * pr[1:, :, None, :] - c_im[None] * pi[1:, :, None, :]
    ei = c_re[None] * pi[1:, :, None, :] + c_im[None] * pr[1:, :, None, :]
    eb = jnp.stack([er, -ei], axis=0).reshape(2, t, nblk, gpb, SSM_CH, SSM_STATE)
    csrc = eb.transpose(2, 1, 0, 5, 3, 4).reshape(nblk, t, 2, SSM_STATE, gpb * SSM_CH)

    a_chunk = jnp.stack([pr[t], pi[t]], axis=0).reshape(2, nblk, 1, gpb * SSM_STATE)
    a_chunk = a_chunk.transpose(1, 0, 2, 3).reshape(nblk, 2, gpb * SSM_STATE)
    dvec = jnp.tile(d_skip.astype(f32).reshape(nblk, 1, gpb * SSM_CH), (1, 1, t))
    return ktab.astype(bf16), bsrc.astype(bf16), csrc.astype(bf16), a_chunk, dvec


def _s5_kernel(u_ref, ktab_ref, bsrc_ref, csrc_ref, a_ref, d_ref, y_ref,
               toep_ref, bpow_ref, cpow_ref, s_ref, h_ref, yt_ref, *, n_batch, n_chunk):
    t = S5_CHUNK
    gpb = S5_GROUPS_PER_BLOCK
    w = gpb * SSM_CH
    ns = gpb * SSM_STATE
    zero = jnp.zeros((), bf16)

    def same_group(shape, row_size, col_size):
        r = lax.broadcasted_iota(jnp.int32, shape, 0) // row_size
        c = lax.broadcasted_iota(jnp.int32, shape, 1) // col_size
        return r == c

    tap_mask = same_group((w, w), SSM_CH, SSM_CH)
    taps = [jnp.where(tap_mask, jnp.tile(ktab_ref[0, tau], (gpb, 1)), zero) for tau in range(t)]
    for tt in range(t):
        for ss in range(tt + 1):
            toep_ref[ss * w:(ss + 1) * w, tt * w:(tt + 1) * w] = taps[tt - ss]
        if tt % 2 == 0:
            toep_ref[(tt + 1) * w:(tt + 2) * w, tt * w:(tt + 1) * w] = jnp.zeros((w, w), bf16)
    b_mask = same_group((w, ns), SSM_CH, SSM_STATE)
    c_mask = same_group((ns, w), SSM_STATE, SSM_CH)
    for ss in range(t):
        for z in range(2):
            bpow_ref[ss * w:(ss + 1) * w, z * ns:(z + 1) * ns] = jnp.where(
                b_mask, jnp.tile(bsrc_ref[0, ss, z], (1, ns // w)), zero)
            cpow_ref[z * ns:(z + 1) * ns, ss * w:(ss + 1) * w] = jnp.where(
                c_mask, jnp.tile(csrc_ref[0, ss, z], (gpb, 1)), zero)

    rows = n_batch * n_chunk
    uf3 = pltpu.einshape("rsl->srl", u_ref[...].reshape(rows, t, w))
    uf = [uf3[s] for s in range(t)]
    u = jnp.concatenate([p.astype(bf16) for p in uf], axis=1)
    s_ref[...] = jnp.dot(u, bpow_ref[...], preferred_element_type=f32)

    ar = a_ref[0, 0:1, :]
    ai = a_ref[0, 1:2, :]

    def step(j, carry):
        new = []
        for b in range(n_batch):
            hr, hi = carry[2 * b], carry[2 * b + 1]
            row = b * n_chunk + j
            h_ref[pl.ds(row, 1), 0:ns] = hr
            h_ref[pl.ds(row, 1), ns:2 * ns] = hi
            sr = s_ref[pl.ds(row, 1), 0:ns]
            si = s_ref[pl.ds(row, 1), ns:2 * ns]
            new.append(ar * hr - ai * hi + sr)
            new.append(ar * hi + ai * hr + si)
        return tuple(new)

    zero = jnp.zeros((1, ns), f32)
    lax.fori_loop(0, n_chunk, step, (zero,) * (2 * n_batch))

    hprev = h_ref[...].astype(bf16)
    for tp in range(t // 2):
        c0, c1 = 2 * tp * w, (2 * tp + 2) * w
        y = jnp.dot(u[:, :c1], toep_ref[0:c1, c0:c1], preferred_element_type=f32)
        y = y + jnp.dot(hprev, cpow_ref[:, c0:c1], preferred_element_type=f32)
        for k, tt in enumerate((2 * tp, 2 * tp + 1)):
            yk = y[:, k * w:(k + 1) * w] + d_ref[0, :, tt * w:(tt + 1) * w] * uf[tt]
            yt_ref[tt] = jax.nn.gelu(yk, approximate=True)
    y_ref[...] = pltpu.einshape("srl->rsl", yt_ref[...]).reshape(rows * t, w)


def _s5_mixer(proj, tables, n_batch, n_chunk):
    ktab, bsrc, csrc, a_chunk, dvec = tables
    nblk = ktab.shape[0]
    w = S5_GROUPS_PER_BLOCK * SSM_CH
    cols = S5_CHUNK * w
    rows = n_batch * n_chunk
    n_tok = rows * S5_CHUNK
    ns = S5_GROUPS_PER_BLOCK * SSM_STATE
    kern = functools.partial(_s5_kernel, n_batch=n_batch, n_chunk=n_chunk)
    return pl.pallas_call(
        kern,
        grid=(nblk,),
        in_specs=[pl.BlockSpec((n_tok, w), lambda g: (0, g)),
                  pl.BlockSpec((1,) + ktab.shape[1:], lambda g: (g, 0, 0, 0)),
                  pl.BlockSpec((1,) + bsrc.shape[1:], lambda g: (g, 0, 0, 0, 0)),
                  pl.BlockSpec((1,) + csrc.shape[1:], lambda g: (g, 0, 0, 0, 0)),
                  pl.BlockSpec((1, 2, ns), lambda g: (g, 0, 0)),
                  pl.BlockSpec((1, 1, cols), lambda g: (g, 0, 0))],
        out_specs=pl.BlockSpec((n_tok, w), lambda g: (0, g)),
        out_shape=jax.ShapeDtypeStruct((n_tok, D_SSM), f32),
        scratch_shapes=[pltpu.VMEM((cols, cols), bf16),
                        pltpu.VMEM((cols, 2 * ns), bf16),
                        pltpu.VMEM((2 * ns, cols), bf16),
                        pltpu.VMEM((rows, 2 * ns), f32),
                        pltpu.VMEM((rows, 2 * ns), f32),
                        pltpu.VMEM((S5_CHUNK, rows, w), f32)],
        compiler_params=_params("parallel"),
        name="s5_mixer",
    )(proj, ktab, bsrc, csrc, a_chunk, dvec)


def _attn_bias_table():
    blk = ATTN_BLOCK
    slopes = 2.0 ** (-8.0 * jnp.arange(1, N_HEADS + 1, dtype=f32) / N_HEADS)
    delta = np.arange(blk)[:, None] - (np.arange(2 * blk)[None, :] - blk)
    tabs = []
    for window, dil in PATTERNS:
        assert window // dil == blk
        valid = (delta >= 0) & (delta <= window // dil)
        dist = jnp.asarray(delta * dil, dtype=f32)
        bias = jnp.where(valid[None], -slopes[:, None, None] * dist[None], MASK_VALUE)
        tabs.append(bias.reshape(N_HEADS // 2, 2 * blk, 2 * blk))
    return jnp.stack(tabs, axis=1)


def _attn_kernel(q_ref, k_ref, v_ref, bias_ref, o_ref, *scr):
    blk = ATTN_BLOCK
    seq = q_ref.shape[0]
    first_head = lax.broadcasted_iota(jnp.int32, (blk, 2 * HEAD_DIM), 1) < HEAD_DIM
    dims = (((1,), (1,)), ((), ()))
    for pi, (_, dil) in enumerate(PATTERNS):
        o_scr, l_scr = scr[2 * pi], scr[2 * pi + 1]
        sub = seq // dil
        for r in range(dil):
            rows = (lambda st, n: pl.ds(st, n)) if dil == 1 else (lambda st, n: pl.ds(st, n, stride=dil))
            qd = (q_ref[rows(r, sub), :] * HEAD_DIM ** -0.5).astype(bf16)
            kd = k_ref[rows(r, sub), :].astype(bf16)
            vd = v_ref[rows(r, sub), :].astype(bf16)
            for i in range(sub // blk):
                qb = qd[i * blk:(i + 1) * blk]
                zero = jnp.zeros_like(qb)
                q2 = jnp.concatenate([jnp.where(first_head, qb, zero), jnp.where(first_head, zero, qb)], axis=0)
                k0 = max(i - 1, 0) * blk
                nk = (i + 1) * blk - k0
                s = lax.dot_general(q2, kd[k0:k0 + nk], dims, preferred_element_type=f32)
                s = s + bias_ref[0, pi, :, 2 * blk - nk:]
                m = jnp.max(s, axis=-1, keepdims=True)
                p = jnp.exp(s - m)
                l = jnp.sum(p, axis=-1, keepdims=True)
                o = jnp.dot(p.astype(bf16), vd[k0:k0 + nk], preferred_element_type=f32) / l
                lse = m + jnp.log(l)
                dst = rows(r + dil * blk * i, blk)
                o_scr[dst, :] = jnp.where(first_head, o[:blk], o[blk:])
                l_scr[dst, :] = jnp.where(first_head, lse[:blk], lse[blk:])
    l1, l2, l3 = scr[1][...], scr[3][...], scr[5][...]
    m = jnp.maximum(jnp.maximum(l1, l2), l3)
    e1, e2, e3 = jnp.exp(l1 - m), jnp.exp(l2 - m), jnp.exp(l3 - m)
    o_ref[...] = (e1 * scr[0][...] + e2 * scr[2][...] + e3 * scr[4][...]) / (e1 + e2 + e3)


def _dilated_attention(qkv, first, bsz, seq):
    pairs = N_HEADS // 2
    width = 2 * HEAD_DIM
    bias = _attn_bias_table()
    col = lambda off: pl.BlockSpec((seq, width), lambda hp, b: (b, first + off + hp))
    return pl.pallas_call(
        _attn_kernel,
        grid=(pairs, bsz),
        in_specs=[col(0), col(pairs), col(2 * pairs),
                  pl.BlockSpec((1,) + bias.shape[1:], lambda hp, b: (hp, 0, 0, 0))],
        out_specs=pl.BlockSpec((seq, width), lambda hp, b: (b, hp)),
        out_shape=jax.ShapeDtypeStruct((bsz * seq, D_ATTN), f32),
        scratch_shapes=[pltpu.VMEM((seq, width), f32)] * (2 * len(PATTERNS)),
        compiler_params=_params("parallel", "parallel"),
        name="dilated_attention",
    )(qkv, qkv, qkv, bias)


def _layer_norm_rows(x, g, b):
    mu = jnp.mean(x, axis=-1, keepdims=True)
    xc = x - mu
    var = jnp.mean(xc * xc, axis=-1, keepdims=True)
    return xc * lax.rsqrt(var + NORM_EPS) * g + b


def _rms_rows(x, g):
    return x * lax.rsqrt(jnp.mean(x * x, axis=-1, keepdims=True) + NORM_EPS) * g


def _mix_out_kernel(y_ref, ya_ref, x_ref, wglu_ref, w_ref, gs_ref, ga_ref, lg_ref, lb_ref, h_ref, hp_ref):
    z = jnp.dot(y_ref[...].astype(bf16), wglu_ref[...], preferred_element_type=f32)
    y_ssm = z[:, :D_SSM] * jax.nn.sigmoid(z[:, D_SSM:])
    ns = _rms_rows(y_ssm, gs_ref[...]).astype(bf16)
    na = _rms_rows(ya_ref[...], ga_ref[...]).astype(bf16)
    proj = jnp.dot(ns, w_ref[0:D_SSM, :], preferred_element_type=f32)
    proj = proj + jnp.dot(na, w_ref[D_SSM:, :], preferred_element_type=f32)
    h = _layer_norm_rows(DEEPNORM_ALPHA * x_ref[...] + proj, lg_ref[...], lb_ref[...])
    h_ref[...] = h
    half = D_MODEL // 2
    word = pltpu.pack_elementwise([h[:, :half], h[:, half:]], packed_dtype=bf16)
    tm = word.shape[0]
    chunks = jnp.stack([word[:, s * LANES:(s + 1) * LANES] for s in range(PACK_ROWS)], axis=0)
    hp_ref[...] = pltpu.einshape("srl->rsl", chunks).reshape(tm * PACK_ROWS, LANES)


def _mix_out(y, y_attn, x, w_glu, w_out, g_ssm, g_attn, ln_g, ln_b, tm=256):
    n = x.shape[0]
    row = lambda c: pl.BlockSpec((tm, c), lambda i: (i, 0))
    full = lambda a: pl.BlockSpec(a.shape, lambda i: (0,) * a.ndim)
    return pl.pallas_call(
        _mix_out_kernel,
        grid=(n // tm,),
        in_specs=[row(D_SSM), row(D_ATTN), row(D_MODEL), full(w_glu), full(w_out), full(g_ssm), full(g_attn),
                  full(ln_g), full(ln_b)],
        out_specs=[row(D_MODEL), pl.BlockSpec((tm * PACK_ROWS, LANES), lambda i: (i, 0))],
        out_shape=[jax.ShapeDtypeStruct((n, D_MODEL), f32),
                   jax.ShapeDtypeStruct((n * PACK_ROWS, LANES), jnp.int32)],
        compiler_params=_params("parallel"),
        name="mix_out",
    )(y, y_attn, x, w_glu, w_out, g_ssm, g_attn, ln_g, ln_b)


def _router_kernel(h_ref, wrt_ref, bias_ref, tri_ref, trie_ref, e8_ref, pos8_ref, wtok_ref, cnt_ref):
    gsz = N_EXPERTS // N_EXPERT_GROUPS
    tm = h_ref.shape[0]
    ninf = -jnp.inf

    @pl.when(pl.program_id(0) == 0)
    def _():
        cnt_ref[...] = jnp.zeros_like(cnt_ref)

    h = h_ref[...]
    h_hi = h.astype(bf16)
    h_lo = (h - h_hi.astype(f32)).astype(bf16)
    dims = (((1,), (1,)), ((), ()))
    w_hi, w_lo = wrt_ref[0], wrt_ref[1]
    logits = (lax.dot_general(w_hi, h_hi, dims, preferred_element_type=f32)
              + lax.dot_general(w_hi, h_lo, dims, preferred_element_type=f32)
              + lax.dot_general(w_lo, h_hi, dims, preferred_element_type=f32))
    scores = jax.nn.sigmoid(logits)
    sel = scores + bias_ref[...]
    io = lax.broadcasted_iota(jnp.int32, (gsz, tm), 0)

    blks, gs_rows = [], []
    for g in range(N_EXPERT_GROUPS):
        blk = sel[g * gsz:(g + 1) * gsz, :]
        m1 = jnp.max(blk, axis=0, keepdims=True)
        first = jnp.min(jnp.where(blk == m1, io, gsz), axis=0, keepdims=True)
        m2 = jnp.max(jnp.where(io == first, ninf, blk), axis=0, keepdims=True)
        blks.append(blk)
        gs_rows.append(m1 + m2)
    gs = jnp.concatenate(gs_rows, axis=0)

    iog = lax.broadcasted_iota(jnp.int32, (N_EXPERT_GROUPS, tm), 0)
    beaten = jnp.zeros((N_EXPERT_GROUPS, tm), f32)
    for gp in range(N_EXPERT_GROUPS):
        row = gs_rows[gp]
        tie = jnp.where(iog > gp, 1.0, 0.0)
        beaten = beaten + jnp.where(row > gs, 1.0, jnp.where(row == gs, tie, 0.0))
    keep = beaten < TOPK_GROUPS
    masked = [jnp.where(keep[g:g + 1, :], blks[g], ninf) for g in range(N_EXPERT_GROUPS)]

    cand = jnp.concatenate(masked, axis=0)
    eid = lax.broadcasted_iota(jnp.int32, (N_EXPERTS, tm), 0)
    selb = jnp.zeros((N_EXPERTS, tm), f32)
    for _ in range(TOP_K):
        best = jnp.max(cand, axis=0, keepdims=True)
        pick = jnp.min(jnp.where(cand == best, eid, N_EXPERTS), axis=0, keepdims=True)
        hit = eid == pick
        selb = jnp.where(hit, 1.0, selb)
        cand = jnp.where(hit, ninf, cand)
    wsel = selb * scores
    wn = wsel / jnp.sum(wsel, axis=0, keepdims=True) * ROUTED_SCALE

    maskb = selb.astype(bf16)
    pos = jnp.dot(maskb, tri_ref[...], preferred_element_type=f32) + cnt_ref[:, 0:1]
    cnt_ref[...] = cnt_ref[...] + jnp.sum(selb, axis=1, keepdims=True)
    slot = jnp.dot(trie_ref[...], maskb, preferred_element_type=f32)
    ioe = lax.broadcasted_iota(jnp.int32, (N_EXPERTS, tm), 0).astype(f32)
    e_rows, p_rows = [], []
    for k in range(TOP_K):
        hit = jnp.where(slot == k, selb, 0.0)
        e_rows.append(jnp.sum(hit * ioe, axis=0, keepdims=True))
        p_rows.append(jnp.sum(hit * pos, axis=0, keepdims=True))
    e8_ref[...] = jnp.concatenate(e_rows, axis=0).astype(jnp.int32)
    pos8_ref[...] = jnp.concatenate(p_rows, axis=0).astype(jnp.int32)
    wtok_ref[...] = jnp.concatenate([wn.T, jnp.zeros((tm, LANES - N_EXPERTS), f32)], axis=1)


def _router(h, w_router, router_bias, tm=512):
    n = h.shape[0]
    wt = w_router.astype(f32).T
    wt_hi = wt.astype(bf16)
    wrt = jnp.stack([wt_hi, (wt - wt_hi.astype(f32)).astype(bf16)])
    bias = router_bias.astype(f32).reshape(N_EXPERTS, 1)
    tri = (jnp.arange(tm)[:, None] < jnp.arange(tm)[None, :]).astype(bf16)
    trie = (jnp.arange(N_EXPERTS)[None, :] < jnp.arange(N_EXPERTS)[:, None]).astype(bf16)
    full = lambda a: pl.BlockSpec(a.shape, lambda i: (0,) * a.ndim)
    tok = lambda: pl.BlockSpec((TOP_K, tm), lambda i: (0, i))
    return pl.pallas_call(
        _router_kernel,
        grid=(n // tm,),
        in_specs=[pl.BlockSpec((tm, D_MODEL), lambda i: (i, 0)), full(wrt), full(bias), full(tri), full(trie)],
        out_specs=[tok(), tok(), pl.BlockSpec((tm, LANES), lambda i: (i, 0)),
                   pl.BlockSpec((N_EXPERTS, LANES), lambda i: (0, 0))],
        out_shape=[jax.ShapeDtypeStruct((TOP_K, n), jnp.int32), jax.ShapeDtypeStruct((TOP_K, n), jnp.int32),
                   jax.ShapeDtypeStruct((n, LANES), f32), jax.ShapeDtypeStruct((N_EXPERTS, LANES), f32)],
        compiler_params=_params("arbitrary"),
        name="router",
    )(h, wrt, bias, tri, trie)


INVERT_UNROLL = 16


def _invert_kernel(dest_ref, nb_ref, code_ref):
    n_rows = code_ref.shape[0]

    def fill(first_group, last_group):
        def body(k, c):
            for u in range(INVERT_UNROLL):
                code_ref[k * INVERT_UNROLL + u] = PAD_CODE
            return c
        lax.fori_loop(first_group, last_group, body, 0)

    def per_expert(e, c):
        groups = MOE_ROWS // INVERT_UNROLL + 1
        start = jnp.minimum(nb_ref[1 + e], n_rows - groups * INVERT_UNROLL) // INVERT_UNROLL
        fill(start, start + groups)
        return c
    lax.fori_loop(0, N_EXPERTS, per_expert, 0)
    fill(nb_ref[0] * (MOE_ROWS // INVERT_UNROLL), n_rows // INVERT_UNROLL)

    def body(p, c):
        code_ref[dest_ref[p]] = p
        return c
    lax.fori_loop(0, dest_ref.shape[0], body, 0, unroll=INVERT_UNROLL)


def _invert(dest, nb, n_rows):
    smem = lambda: pl.BlockSpec(memory_space=pltpu.SMEM)
    return pl.pallas_call(
        _invert_kernel,
        in_specs=[smem(), smem()],
        out_specs=smem(),
        out_shape=jax.ShapeDtypeStruct((n_rows,), jnp.int32),
        name="invert_dispatch",
    )(dest, nb)


def _expert_changed(be_ref, i):
    return jnp.logical_or(i == 0, be_ref[i] != be_ref[jnp.maximum(i - 1, 0)])


SCATTER_UNROLL = 8


def _moe_up_kernel(be_ref, code_ref, nb_ref, hp_ref, wtok_ref, wg_ref, wu_ref, mid_ref,
                   xg_scr, xa_scr, xb_scr, wra_scr, wrb_scr, wgu_scr, wstage_scr, wsem):
    i = pl.program_id(0)
    rows = MOE_ROWS

    def gather_block(blk, x_dst, wrow_dst):
        base = blk * rows
        for r in range(rows):
            off = pl.multiple_of(code_ref[base + r], PACK_ROWS)
            xg_scr[r * PACK_ROWS:(r + 1) * PACK_ROWS, :] = hp_ref[pl.ds(off, PACK_ROWS), :]
            wrow_dst[r:r + 1, :] = wtok_ref[pl.ds(off // PACK_ROWS, 1), :]
        half = D_MODEL // 2
        chunks = pltpu.einshape("rsl->srl", xg_scr[...].reshape(rows, PACK_ROWS, LANES))
        for s in range(PACK_ROWS):
            wds = chunks[s]
            lo = pltpu.unpack_elementwise(wds, index=0, packed_dtype=bf16, unpacked_dtype=f32)
            hi = pltpu.unpack_elementwise(wds, index=1, packed_dtype=bf16, unpacked_dtype=f32)
            x_dst[:, s * LANES:(s + 1) * LANES] = lo.astype(bf16)
            x_dst[:, half + s * LANES:half + (s + 1) * LANES] = hi.astype(bf16)

    def weight_copies(e, s):
        return (pltpu.make_async_copy(wg_ref.at[e], wstage_scr.at[s, 0], wsem.at[s, 0]),
                pltpu.make_async_copy(wu_ref.at[e], wstage_scr.at[s, 1], wsem.at[s, 1]))

    def expert_block(blk, out_rows, x_cur, wrow_cur, x_nxt, wrow_nxt):
        last = MOE_BLOCKS - 1
        expert = be_ref[jnp.minimum(blk, last)]
        slot = nb_ref[NB_SLOT + expert]
        upcoming = nb_ref[NB_NEXT + expert]
        valid = blk < nb_ref[0]

        @pl.when(jnp.logical_and(valid, blk == 0))
        def _():
            for cp in weight_copies(expert, slot):
                cp.start()

        @pl.when(jnp.logical_and(valid, _expert_changed(be_ref, jnp.minimum(blk, last))))
        def _():
            for cp in weight_copies(expert, slot):
                cp.wait()

            @pl.when(upcoming >= 0)
            def _():
                for cp in weight_copies(upcoming, 1 - slot):
                    cp.start()

            wgu_scr[:, :D_EXPERT] = wstage_scr[slot, 0].astype(bf16)
            wgu_scr[:, D_EXPERT:] = wstage_scr[slot, 1].astype(bf16)

        @pl.when(valid)
        def _():
            gather_block(jnp.minimum(blk + 1, last), x_nxt, wrow_nxt)
            lane = lax.broadcasted_iota(jnp.int32, (rows, LANES), 1)
            w = jnp.sum(jnp.where(lane == expert, wrow_cur[...], 0.0), axis=1, keepdims=True)
            real = blk * rows + lax.broadcasted_iota(jnp.int32, (rows, 1), 0) < nb_ref[1 + expert]
            gu = jnp.dot(x_cur[...], wgu_scr[...], preferred_element_type=f32)
            g, u = gu[:, :D_EXPERT], gu[:, D_EXPERT:]
            mid_ref[out_rows, :] = (g * jax.nn.sigmoid(g) * u * jnp.where(real, w, 0.0)).astype(bf16)

        @pl.when(jnp.logical_not(valid))
        def _():
            mid_ref[out_rows, :] = jnp.zeros((rows, D_EXPERT), bf16)

    @pl.when(i == 0)
    def _():
        gather_block(0, xa_scr, wra_scr)

    expert_block(2 * i, slice(0, rows), xa_scr, wra_scr, xb_scr, wrb_scr)
    expert_block(2 * i + 1, slice(rows, 2 * rows), xb_scr, wrb_scr, xa_scr, wra_scr)


def _moe_up(block_e, code, nb, hp, wtok, w_gate, w_up):
    grid_spec = pltpu.PrefetchScalarGridSpec(
        num_scalar_prefetch=3,
        grid=(MOE_BLOCKS // 2,),
        in_specs=[pl.BlockSpec(memory_space=pltpu.VMEM),
                  pl.BlockSpec(memory_space=pltpu.VMEM),
                  pl.BlockSpec(memory_space=pl.ANY),
                  pl.BlockSpec(memory_space=pl.ANY)],
        out_specs=pl.BlockSpec((2 * MOE_ROWS, D_EXPERT), lambda i, be, cd, nb: (i, 0)),
        scratch_shapes=[pltpu.VMEM((PACK_ROWS * MOE_ROWS, LANES), jnp.int32),
                        pltpu.VMEM((MOE_ROWS, D_MODEL), bf16), pltpu.VMEM((MOE_ROWS, D_MODEL), bf16),
                        pltpu.VMEM((MOE_ROWS, LANES), f32), pltpu.VMEM((MOE_ROWS, LANES), f32),
                        pltpu.VMEM((D_MODEL, 2 * D_EXPERT), bf16),
                        pltpu.VMEM((2, 2, D_MODEL, D_EXPERT), f32),
                        pltpu.SemaphoreType.DMA((2, 2))])
    return pl.pallas_call(
        _moe_up_kernel,
        grid_spec=grid_spec,
        out_shape=jax.ShapeDtypeStruct((MOE_BLOCKS * MOE_ROWS, D_EXPERT), bf16),
        compiler_params=pltpu.CompilerParams(dimension_semantics=("arbitrary",),
                                             vmem_limit_bytes=MOE_UP_VMEM_LIMIT),
        name="moe_up",
    )(block_e, code, nb, hp, wtok, w_gate, w_up)


def _moe_down_kernel(be_ref, code_ref, nb_ref, mida_ref, midb_ref, wd_ref, acc_ref,
                     ya_scr, yb0_scr, yb1_scr, wa_scr, wb_scr, wstage_scr, wsem):
    j = pl.program_id(1)
    rows = MOE_ROWS
    nb = nb_ref[0]
    blk_a, blk_b = 2 * j, 2 * j + 1
    last = MOE_BLOCKS - 1

    @pl.when(j == 0)
    def _():
        acc_ref[...] = jnp.zeros_like(acc_ref)

    def scatter_rows(base, ybuf, r0, n):
        sums, addrs = [], []
        for k in range(n):
            a = pl.multiple_of(code_ref[base + r0 + k], PACK_ROWS)
            v = ybuf[pl.ds(pl.multiple_of((r0 + k) * PACK_ROWS, PACK_ROWS), PACK_ROWS), :]
            sums.append(acc_ref[pl.ds(a, PACK_ROWS), :] + v)
            addrs.append(a)
        for k in range(n):
            acc_ref[pl.ds(addrs[k], PACK_ROWS), :] = sums[k]

    def scatter_block(blk, ybuf):
        for g in range(rows // SCATTER_UNROLL):
            scatter_rows(blk * rows, ybuf, g * SCATTER_UNROLL, SCATTER_UNROLL)

    def scatter_block_compact(blk, ybuf):
        def body(g, c):
            scatter_rows(blk * rows, ybuf, g * SCATTER_UNROLL, SCATTER_UNROLL)
            return c
        lax.fori_loop(0, rows // SCATTER_UNROLL, body, 0)

    def down_block(mid_ref, w_scr, ybuf):
        y = jnp.dot(mid_ref[...], w_scr[...], preferred_element_type=f32)
        y3 = jnp.stack([y[:, c * LANES:(c + 1) * LANES] for c in range(PACK_ROWS)], axis=0)
        ybuf[...] = pltpu.einshape("crl->rcl", y3).reshape(rows * PACK_ROWS, LANES)

    half = D_MODEL // 2
    cols = pl.ds(pl.multiple_of(pl.program_id(0) * half, half), half)

    def weight_copy(e, s):
        return pltpu.make_async_copy(wd_ref.at[e, :, cols], wstage_scr.at[s], wsem.at[s])

    def refresh_weights(blk, w_scr):
        cur = be_ref[jnp.minimum(blk, last)]
        slot = nb_ref[NB_SLOT + cur]
        upcoming = nb_ref[NB_NEXT + cur]
        valid = blk < nb

        @pl.when(jnp.logical_and(valid, blk == 0))
        def _():
            weight_copy(cur, slot).start()

        @pl.when(jnp.logical_and(valid, _expert_changed(be_ref, jnp.minimum(blk, last))))
        def _():
            weight_copy(cur, slot).wait()

            @pl.when(upcoming >= 0)
            def _():
                weight_copy(upcoming, 1 - slot).start()

        prev = be_ref[jnp.clip(blk - 2, 0, last)]

        @pl.when(jnp.logical_and(valid, jnp.logical_or(j == 0, cur != prev)))
        def _():
            w_scr[...] = wstage_scr[slot].astype(bf16)

    refresh_weights(blk_a, wa_scr)
    refresh_weights(blk_b, wb_scr)

    for parity, yb_this, yb_prev in ((0, yb0_scr, yb1_scr), (1, yb1_scr, yb0_scr)):
        mine = j % 2 == parity

        @pl.when(jnp.logical_and(mine, jnp.logical_and(j > 0, blk_b < nb)))
        def _():
            down_block(mida_ref, wa_scr, ya_scr)
            down_block(midb_ref, wb_scr, yb_this)
            scatter_block(blk_a - 1, yb_prev)
            scatter_block(blk_a, ya_scr)

        @pl.when(jnp.logical_and(mine, jnp.logical_and(j > 0, blk_b == nb)))
        def _():
            down_block(mida_ref, wa_scr, ya_scr)
            scatter_block_compact(blk_a - 1, yb_prev)
            scatter_block_compact(blk_a, ya_scr)

        @pl.when(jnp.logical_and(mine, jnp.logical_and(j > 0, blk_a == nb)))
        def _():
            scatter_block_compact(blk_a - 1, yb_prev)

    @pl.when(jnp.logical_and(j == 0, blk_a < nb))
    def _():
        down_block(mida_ref, wa_scr, ya_scr)
        scatter_block_compact(blk_a, ya_scr)

    @pl.when(jnp.logical_and(j == 0, blk_b < nb))
    def _():
        down_block(midb_ref, wb_scr, yb0_scr)


def _moe_down(block_e, code, nb, mid, w_down):
    half = D_MODEL // 2
    acc_rows = (N_TOKENS + 1) * PACK_ROWS
    last = MOE_BLOCKS - 1
    assert MOE_BLOCKS % 2 == 0
    blk = lambda off: (lambda p, j, be, cd, nb: (jnp.minimum(2 * j + off, last), 0))
    grid_spec = pltpu.PrefetchScalarGridSpec(
        num_scalar_prefetch=3,
        grid=(2, MOE_BLOCKS // 2 + 1),
        in_specs=[pl.BlockSpec((MOE_ROWS, D_EXPERT), blk(0)), pl.BlockSpec((MOE_ROWS, D_EXPERT), blk(1)),
                  pl.BlockSpec(memory_space=pl.ANY)],
        out_specs=pl.BlockSpec((None, acc_rows, LANES), lambda p, j, be, cd, nb: (p, 0, 0),
                               pipeline_mode=pl.Buffered(1)),
        scratch_shapes=[pltpu.VMEM((PACK_ROWS * MOE_ROWS, LANES), f32)] * 3
        + [pltpu.VMEM((D_EXPERT, half), bf16)] * 2
        + [pltpu.VMEM((2, D_EXPERT, half), f32), pltpu.SemaphoreType.DMA((2,))])
    return pl.pallas_call(
        _moe_down_kernel,
        grid_spec=grid_spec,
        out_shape=jax.ShapeDtypeStruct((2, acc_rows, LANES), f32),
        compiler_params=_params("arbitrary", "arbitrary"),
        name="moe_down",
    )(block_e, code, nb, mid, mid, w_down)


def _final_kernel(h_ref, r0_ref, r1_ref, wgu_ref, wd_ref, lg_ref, lb_ref, o_ref):
    tm = h_ref.shape[0]
    gu = jnp.dot(h_ref[...].astype(bf16), wgu_ref[...], preferred_element_type=f32)
    g, u = gu[:, :D_EXPERT], gu[:, D_EXPERT:]
    mid = (g * jax.nn.sigmoid(g) * u).astype(bf16)
    shared = jnp.dot(mid, wd_ref[...], preferred_element_type=f32)
    halves = [pltpu.einshape("rcl->crl", r[...].reshape(tm, PACK_ROWS, LANES)) for r in (r0_ref, r1_ref)]
    routed = jnp.concatenate([hv[c] for hv in halves for c in range(PACK_ROWS)], axis=1)
    o_ref[...] = _layer_norm_rows(DEEPNORM_ALPHA * h_ref[...] + routed + shared,
                                  lg_ref[...], lb_ref[...])


def _final(h, racc, wgu, wd, ln_g, ln_b, tm=256):
    n = h.shape[0]
    row = lambda: pl.BlockSpec((tm, D_MODEL), lambda i: (i, 0))
    full = lambda a: pl.BlockSpec(a.shape, lambda i: (0,) * a.ndim)
    acc = lambda p: pl.BlockSpec((None, tm * PACK_ROWS, LANES), lambda i: (p, i, 0))
    return pl.pallas_call(
        _final_kernel,
        grid=(n // tm,),
        in_specs=[row(), acc(0), acc(1), full(wgu), full(wd), full(ln_g), full(ln_b)],
        out_specs=row(),
        out_shape=jax.ShapeDtypeStruct((n, D_MODEL), f32),
        compiler_params=_params("parallel"),
        name="shared_final",
    )(h, racc, racc, wgu, wd, ln_g, ln_b)


def _dispatch_plan(e8, pos8, cnt):
    counts = cnt[:, 0].astype(jnp.int32)
    padded = (counts + MOE_ROWS - 1) // MOE_ROWS * MOE_ROWS
    pad_end = jnp.cumsum(padded).astype(jnp.int32)
    pad_start = pad_end - padded
    seg_end = pad_start + counts
    ids = jnp.arange(N_EXPERTS, dtype=jnp.int32)
    start8 = jnp.sum(jnp.where(e8[..., None] == ids, pad_start, 0), axis=-1)
    dest = (start8 + pos8).T.reshape(-1)
    block_start = jnp.arange(MOE_BLOCKS, dtype=jnp.int32) * MOE_ROWS
    block_e = jnp.minimum(jnp.sum((pad_end[None, :] <= block_start[:, None]).astype(jnp.int32), axis=1),
                          N_EXPERTS - 1)
    used = counts > 0
    slot = (jnp.cumsum(used.astype(jnp.int32)) - 1) % 2
    later = jnp.where(used, ids, N_EXPERTS)
    nxt = jnp.concatenate([lax.cummin(later[::-1])[::-1][1:], jnp.full((1,), N_EXPERTS, jnp.int32)])
    nxt = jnp.where(nxt == N_EXPERTS, -1, nxt)
    nb = jnp.concatenate([pad_end[-1:] // MOE_ROWS, seg_end, slot, nxt]).astype(jnp.int32)
    return dest, block_e, nb


def kernel(x, w_in, ssm_log_dt, ssm_a_re, ssm_a_im, ssm_b_re, ssm_b_im, ssm_c_re, ssm_c_im, ssm_d,
           w_glu, g_ssm_out, g_attn_out, w_out, ln1_g, ln1_b, w_router, router_bias, w_gate, w_up,
           w_down, ws_gate, ws_up, ws_down, ln2_g, ln2_b):
    bsz, seq, d = x.shape
    n_tok = bsz * seq
    h = x.reshape(n_tok, d)
    for layer in range(DEPTH):
        proj = _matmul(h, w_in[layer].astype(bf16), f32)

        tables = _s5_tables(ssm_log_dt[layer], ssm_a_re[layer], ssm_a_im[layer], ssm_b_re[layer],
                            ssm_b_im[layer], ssm_c_re[layer], ssm_c_im[layer], ssm_d[layer])
        y = _s5_mixer(proj, tables, bsz, seq // S5_CHUNK)

        y_attn = _dilated_attention(proj, D_SSM // (2 * HEAD_DIM), bsz, seq)

        row2 = lambda a: a.astype(f32).reshape(1, -1)
        h, hp = _mix_out(y, y_attn, h, w_glu[layer].astype(bf16), w_out[layer].astype(bf16),
                         row2(g_ssm_out[layer]), row2(g_attn_out[layer]), row2(ln1_g[layer]), row2(ln1_b[layer]))

        assert n_tok == N_TOKENS
        e8, pos8, wtok, cnt = _router(h, w_router[layer], router_bias[layer])
        dest, block_e, nb = _dispatch_plan(e8, pos8, cnt)
        code = _invert(dest, nb, MOE_BLOCKS * MOE_ROWS)
        src_row = ((code >> 3) & (N_TOKENS - 1)) * PACK_ROWS
        dst_row = code & -PACK_ROWS
        mid = _moe_up(block_e, src_row, nb, hp, wtok, w_gate[layer], w_up[layer])
        racc = _moe_down(block_e, dst_row, nb, mid, w_down[layer])
        wgu = jnp.concatenate([ws_gate[layer], ws_up[layer]], axis=1).astype(bf16)
        h = _final(h, racc, wgu, ws_down[layer].astype(bf16), row2(ln2_g[layer]), row2(ln2_b[layer]))
    return h.reshape(bsz, seq, d)
```

```python
import functools

import jax
import jax.numpy as jnp
import numpy as np
from jax import lax
from jax.experimental import pallas as pl
from jax.experimental.pallas import tpu as pltpu

D_MODEL = 2048
D_SSM = 1024
D_ATTN = 1024
SSM_CH = 16
SSM_GROUPS = 64
SSM_STATE = 64
HEAD_DIM = 64
N_HEADS = 16
PATTERNS = ((128, 1), (512, 4), (2048, 16))
ATTN_BLOCK = 128
N_EXPERTS = 64
TOP_K = 8
N_EXPERT_GROUPS = 8
TOPK_GROUPS = 4
D_EXPERT = 512
ROUTED_SCALE = 2.5
NORM_EPS = 1e-5
DEPTH = 1
DEEPNORM_ALPHA = (2.0 * DEPTH) ** 0.25

LANES = 128
SUBLANES = 8
V7X_VMEM_BYTES = 64 * 1024 * 1024
S5_CHUNK = 16
S5_GROUPS_PER_BLOCK = LANES // SSM_CH
MOE_ROWS = 256
MASK_VALUE = -1e30
PACK_ROWS = D_MODEL // 2 // LANES
assert PACK_ROWS == SUBLANES
N_TOKENS = 8192
PAD_CODE = N_TOKENS * TOP_K
NB_SLOT = 1 + N_EXPERTS
NB_NEXT = 1 + 2 * N_EXPERTS
MOE_BLOCKS = -(-(N_TOKENS * TOP_K + N_EXPERTS * (MOE_ROWS - 1)) // MOE_ROWS)
VMEM_LIMIT = V7X_VMEM_BYTES - 8 * 1024 * 1024
MOE_UP_VMEM_LIMIT = V7X_VMEM_BYTES - 2 * 1024 * 1024

bf16 = jnp.bfloat16
f32 = jnp.float32


def _params(*sem):
    return pltpu.CompilerParams(dimension_semantics=sem, vmem_limit_bytes=VMEM_LIMIT)


MATMUL_COLS = 1024


def _matmul_kernel(a_ref, b_ref, o_ref):
    a = a_ref[...].astype(bf16)
    for j in range(o_ref.shape[1] // MATMUL_COLS):
        cols = slice(j * MATMUL_COLS, (j + 1) * MATMUL_COLS)
        o_ref[:, cols] = jnp.dot(a, b_ref[:, cols], preferred_element_type=f32).astype(o_ref.dtype)


def _matmul(a, b, out_dtype, tm=256):
    m, k = a.shape
    _, n = b.shape
    return pl.pallas_call(
        _matmul_kernel,
        grid=(m // tm,),
        in_specs=[pl.BlockSpec((tm, k), lambda i: (i, 0)),
                  pl.BlockSpec((k, n), lambda i: (0, 0), pipeline_mode=pl.Buffered(1))],
        out_specs=pl.BlockSpec((tm, n), lambda i: (i, 0)),
        out_shape=jax.ShapeDtypeStruct((m, n), out_dtype),
        compiler_params=_params("parallel"),
        name="matmul",
    )(a, b)


def _s5_tables(log_dt, a_re, a_im, b_re, b_im, c_re, c_im, d_skip):
    t = S5_CHUNK
    gpb = S5_GROUPS_PER_BLOCK
    nblk = SSM_GROUPS // gpb
    hp = lax.Precision.HIGHEST
    lr = jnp.minimum(a_re.astype(f32), -1e-4)
    li = a_im.astype(f32)
    dt = jnp.exp(log_dt.astype(f32))
    kk = jnp.arange(t + 1, dtype=f32)[:, None, None]
    mag = jnp.exp(kk * (lr * dt))
    pr = mag * jnp.cos(kk * (li * dt))
    pi = mag * jnp.sin(kk * (li * dt))
    xr, xi = pr[1] - 1.0, pi[1]
    den = lr * lr + li * li
    cr = (xr * lr + xi * li) / den
    ci = (xi * lr - xr * li) / den
    bbr = cr[..., None] * b_re - ci[..., None] * b_im
    bbi = cr[..., None] * b_im + ci[..., None] * b_re
    wr = pr[:t, :, :, None] * bbr - pi[:t, :, :, None] * bbi
    wi = pr[:t, :, :, None] * bbi + pi[:t, :, :, None] * bbr
    taps = (jnp.einsum('gop,tgpc->tgco', c_re, wr, precision=hp)
            - jnp.einsum('gop,tgpc->tgco', c_im, wi, precision=hp))
    ktab = taps.reshape(t, nblk, gpb, SSM_CH, SSM_CH).transpose(1, 0, 3, 2, 4)
    ktab = ktab.reshape(nblk, t, SSM_CH, gpb * SSM_CH)

    rev = jnp.arange(t - 1, -1, -1)
    sr = pr[rev][..., None] * bbr - pi[rev][..., None] * bbi
    si = pr[rev][..., None] * bbi + pi[rev][..., None] * bbr
    sb = jnp.stack([sr, si], axis=0).reshape(2, t, nblk, gpb, SSM_STATE, SSM_CH)
    bsrc = sb.transpose(2, 1, 0, 3, 5, 4).reshape(nblk, t, 2, gpb * SSM_CH, SSM_STATE)
    bsrc = jnp.concatenate([bsrc, bsrc], axis=-1)

    er = c_re[None] * pr[1:, :, None, :] - c_im[None] * pi[1:, :, None, :]
    ei = c_re[None] * pi[1:, :, None, :] + c_im[None] * pr[1:, :, None, :]
    eb = jnp.stack([er, -ei], axis=0).reshape(2, t, nblk, gpb, SSM_CH, SSM_STATE)
    csrc = eb.transpose(2, 1, 0, 5, 3, 4).reshape(nblk, t, 2, SSM_STATE, gpb * SSM_CH)

    a_chunk = jnp.stack([pr[t], pi[t]], axis=0).reshape(2, nblk, 1, gpb * SSM_STATE)
    a_chunk = a_chunk.transpose(1, 0, 2, 3).reshape(nblk, 2, gpb * SSM_STATE)
    dvec = jnp.tile(d_skip.astype(f32).reshape(nblk, 1, gpb * SSM_CH), (1, 1, t))
    return ktab.astype(bf16), bsrc.astype(bf16), csrc.astype(bf16), a_chunk, dvec


def _s5_kernel(u_ref, ktab_ref, bsrc_ref, csrc_ref, a_ref, d_ref, y_ref,
               toep_ref, bpow_ref, cpow_ref, s_ref, h_ref, yt_ref, *, n_batch, n_chunk):
    t = S5_CHUNK
    gpb = S5_GROUPS_PER_BLOCK
    w = gpb * SSM_CH
    ns = gpb * SSM_STATE
    zero = jnp.zeros((), bf16)

    def same_group(shape, row_size, col_size):
        r = lax.broadcasted_iota(jnp.int32, shape, 0) // row_size
        c = lax.broadcasted_iota(jnp.int32, shape, 1) // col_size
        return r == c

    tap_mask = same_group((w, w), SSM_CH, SSM_CH)
    taps = [jnp.where(tap_mask, jnp.tile(ktab_ref[0, tau], (gpb, 1)), zero) for tau in range(t)]
    for tt in range(t):
        for ss in range(tt + 1):
            toep_ref[ss * w:(ss + 1) * w, tt * w:(tt + 1) * w] = taps[tt - ss]
        if tt % 2 == 0:
            toep_ref[(tt + 1) * w:(tt + 2) * w, tt * w:(tt + 1) * w] = jnp.zeros((w, w), bf16)
    b_mask = same_group((w, ns), SSM_CH, SSM_STATE)
    c_mask = same_group((ns, w), SSM_STATE, SSM_CH)
    for ss in range(t):
        for z in range(2):
            bpow_ref[ss * w:(ss + 1) * w, z * ns:(z + 1) * ns] = jnp.where(
                b_mask, jnp.tile(bsrc_ref[0, ss, z], (1, ns // w)), zero)
            cpow_ref[z * ns:(z + 1) * ns, ss * w:(ss + 1) * w] = jnp.where(
                c_mask, jnp.tile(csrc_ref[0, ss, z], (gpb, 1)), zero)

    rows = n_batch * n_chunk
    uf3 = pltpu.einshape("rsl->srl", u_ref[...].reshape(rows, t, w))
    uf = [uf3[s] for s in range(t)]
    u = jnp.concatenate([p.astype(bf16) for p in uf], axis=1)
    s_ref[...] = jnp.dot(u, bpow_ref[...], preferred_element_type=f32)

    ar = a_ref[0, 0:1, :]
    ai = a_ref[0, 1:2, :]

    def step(j, carry):
        new = []
        for b in range(n_batch):
            hr, hi = carry[2 * b], carry[2 * b + 1]
            row = b * n_chunk + j
            h_ref[pl.ds(row, 1), 0:ns] = hr
            h_ref[pl.ds(row, 1), ns:2 * ns] = hi
            sr = s_ref[pl.ds(row, 1), 0:ns]
            si = s_ref[pl.ds(row, 1), ns:2 * ns]
            new.append(ar * hr - ai * hi + sr)
            new.append(ar * hi + ai * hr + si)
        return tuple(new)

    zero = jnp.zeros((1, ns), f32)
    lax.fori_loop(0, n_chunk, step, (zero,) * (2 * n_batch))

    hprev = h_ref[...].astype(bf16)
    for tp in range(t // 2):
        c0, c1 = 2 * tp * w, (2 * tp + 2) * w
        y = jnp.dot(u[:, :c1], toep_ref[0:c1, c0:c1], preferred_element_type=f32)
        y = y + jnp.dot(hprev, cpow_ref[:, c0:c1], preferred_element_type=f32)
        for k, tt in enumerate((2 * tp, 2 * tp + 1)):
            yk = y[:, k * w:(k + 1) * w] + d_ref[0, :, tt * w:(tt + 1) * w] * uf[tt]
            yt_ref[tt] = jax.nn.gelu(yk, approximate=True)
    y_ref[...] = pltpu.einshape("srl->rsl", yt_ref[...]).reshape(rows * t, w)


def _s5_mixer(proj, tables, n_batch, n_chunk):
    ktab, bsrc, csrc, a_chunk, dvec = tables
    nblk = ktab.shape[0]
    w = S5_GROUPS_PER_BLOCK * SSM_CH
    cols = S5_CHUNK * w
    rows = n_batch * n_chunk
    n_tok = rows * S5_CHUNK
    ns = S5_GROUPS_PER_BLOCK * SSM_STATE
    kern = functools.partial(_s5_kernel, n_batch=n_batch, n_chunk=n_chunk)
    return pl.pallas_call(
        kern,
        grid=(nblk,),
        in_specs=[pl.BlockSpec((n_tok, w), lambda g: (0, g)),
                  pl.BlockSpec((1,) + ktab.shape[1:], lambda g: (g, 0, 0, 0)),
                  pl.BlockSpec((1,) + bsrc.shape[1:], lambda g: (g, 0, 0, 0, 0)),
                  pl.BlockSpec((1,) + csrc.shape[1:], lambda g: (g, 0, 0, 0, 0)),
                  pl.BlockSpec((1, 2, ns), lambda g: (g, 0, 0)),
                  pl.BlockSpec((1, 1, cols), lambda g: (g, 0, 0))],
        out_specs=pl.BlockSpec((n_tok, w), lambda g: (0, g)),
        out_shape=jax.ShapeDtypeStruct((n_tok, D_SSM), f32),
        scratch_shapes=[pltpu.VMEM((cols, cols), bf16),
                        pltpu.VMEM((cols, 2 * ns), bf16),
                        pltpu.VMEM((2 * ns, cols), bf16),
                        pltpu.VMEM((rows, 2 * ns), f32),
                        pltpu.VMEM((rows, 2 * ns), f32),
                        pltpu.VMEM((S5_CHUNK, rows, w), f32)],
        compiler_params=_params("parallel"),
        name="s5_mixer",
    )(proj, ktab, bsrc, csrc, a_chunk, dvec)


def _attn_bias_table():
    blk = ATTN_BLOCK
    slopes = 2.0 ** (-8.0 * jnp.arange(1, N_HEADS + 1, dtype=f32) / N_HEADS)
    delta = np.arange(blk)[:, None] - (np.arange(2 * blk)[None, :] - blk)
    tabs = []
    for window, dil in PATTERNS:
        assert window // dil == blk
        valid = (delta >= 0) & (delta <= window // dil)
        dist = jnp.asarray(delta * dil, dtype=f32)
        bias = jnp.where(valid[None], -slopes[:, None, None] * dist[None], MASK_VALUE)
        tabs.append(bias.reshape(N_HEADS // 2, 2 * blk, 2 * blk))
    return jnp.stack(tabs, axis=1)


def _attn_kernel(q_ref, k_ref, v_ref, bias_ref, o_ref, *scr):
    blk = ATTN_BLOCK
    seq = q_ref.shape[0]
    first_head = lax.broadcasted_iota(jnp.int32, (blk, 2 * HEAD_DIM), 1) < HEAD_DIM
    dims = (((1,), (1,)), ((), ()))
    for pi, (_, dil) in enumerate(PATTERNS):
        o_scr, l_scr = scr[2 * pi], scr[2 * pi + 1]
        sub = seq // dil
        for r in range(dil):
            rows = (lambda st, n: pl.ds(st, n)) if dil == 1 else (lambda st, n: pl.ds(st, n, stride=dil))
            qd = (q_ref[rows(r, sub), :] * HEAD_DIM ** -0.5).astype(bf16)
            kd = k_ref[rows(r, sub), :].astype(bf16)
            vd = v_ref[rows(r, sub), :].astype(bf16)
            for i in range(sub // blk):
                qb = qd[i * blk:(i + 1) * blk]
                zero = jnp.zeros_like(qb)
                q2 = jnp.concatenate([jnp.where(first_head, qb, zero), jnp.where(first_head, zero, qb)], axis=0)
                k0 = max(i - 1, 0) * blk
                nk = (i + 1) * blk - k0
                s = lax.dot_general(q2, kd[k0:k0 + nk], dims, preferred_element_type=f32)
                s = s + bias_ref[0, pi, :, 2 * blk - nk:]
                m = jnp.max(s, axis=-1, keepdims=True)
                p = jnp.exp(s - m)
                l = jnp.sum(p, axis=-1, keepdims=True)
                o = jnp.dot(p.astype(bf16), vd[k0:k0 + nk], preferred_element_type=f32) / l
                lse = m + jnp.log(l)
                dst = rows(r + dil * blk * i, blk)
                o_scr[dst, :] = jnp.where(first_head, o[:blk], o[blk:])
                l_scr[dst, :] = jnp.where(first_head, lse[:blk], lse[blk:])
    l1, l2, l3 = scr[1][...], scr[3][...], scr[5][...]
    m = jnp.maximum(jnp.maximum(l1, l2), l3)
    e1, e2, e3 = jnp.exp(l1 - m), jnp.exp(l2 - m), jnp.exp(l3 - m)
    o_ref[...] = (e1 * scr[0][...] + e2 * scr[2][...] + e3 * scr[4][...]) / (e1 + e2 + e3)


def _dilated_attention(qkv, first, bsz, seq):
    pairs = N_HEADS // 2
    width = 2 * HEAD_DIM
    bias = _attn_bias_table()
    col = lambda off: pl.BlockSpec((seq, width), lambda hp, b: (b, first + off + hp))
    return pl.pallas_call(
        _attn_kernel,
        grid=(pairs, bsz),
        in_specs=[col(0), col(pairs), col(2 * pairs),
                  pl.BlockSpec((1,) + bias.shape[1:], lambda hp, b: (hp, 0, 0, 0))],
        out_specs=pl.BlockSpec((seq, width), lambda hp, b: (b, hp)),
        out_shape=jax.ShapeDtypeStruct((bsz * seq, D_ATTN), f32),
        scratch_shapes=[pltpu.VMEM((seq, width), f32)] * (2 * len(PATTERNS)),
        compiler_params=_params("parallel", "parallel"),
        name="dilated_attention",
    )(qkv, qkv, qkv, bias)


def _layer_norm_rows(x, g, b):
    mu = jnp.mean(x, axis=-1, keepdims=True)
    xc = x - mu
    var = jnp.mean(xc * xc, axis=-1, keepdims=True)
    return xc * lax.rsqrt(var + NORM_EPS) * g + b


def _rms_rows(x, g):
    return x * lax.rsqrt(jnp.mean(x * x, axis=-1, keepdims=True) + NORM_EPS) * g


def _mix_out_kernel(y_ref, ya_ref, x_ref, wglu_ref, w_ref, gs_ref, ga_ref, lg_ref, lb_ref, h_ref, hp_ref):
    z = jnp.dot(y_ref[...].astype(bf16), wglu_ref[...], preferred_element_type=f32)
    y_ssm = z[:, :D_SSM] * jax.nn.sigmoid(z[:, D_SSM:])
    ns = _rms_rows(y_ssm, gs_ref[...]).astype(bf16)
    na = _rms_rows(ya_ref[...], ga_ref[...]).astype(bf16)
    proj = jnp.dot(ns, w_ref[0:D_SSM, :], preferred_element_type=f32)
    proj = proj + jnp.dot(na, w_ref[D_SSM:, :], preferred_element_type=f32)
    h = _layer_norm_rows(DEEPNORM_ALPHA * x_ref[...] + proj, lg_ref[...], lb_ref[...])
    h_ref[...] = h
    half = D_MODEL // 2
    word = pltpu.pack_elementwise([h[:, :half], h[:, half:]], packed_dtype=bf16)
    tm = word.shape[0]
    chunks = jnp.stack([word[:, s * LANES:(s + 1) * LANES] for s in range(PACK_ROWS)], axis=0)
    hp_ref[...] = pltpu.einshape("srl->rsl", chunks).reshape(tm * PACK_ROWS, LANES)


def _mix_out(y, y_attn, x, w_glu, w_out, g_ssm, g_attn, ln_g, ln_b, tm=256):
    n = x.shape[0]
    row = lambda c: pl.BlockSpec((tm, c), lambda i: (i, 0))
    full = lambda a: pl.BlockSpec(a.shape, lambda i: (0,) * a.ndim)
    return pl.pallas_call(
        _mix_out_kernel,
        grid=(n // tm,),
        in_specs=[row(D_SSM), row(D_ATTN), row(D_MODEL), full(w_glu), full(w_out), full(g_ssm), full(g_attn),
                  full(ln_g), full(ln_b)],
        out_specs=[row(D_MODEL), pl.BlockSpec((tm * PACK_ROWS, LANES), lambda i: (i, 0))],
        out_shape=[jax.ShapeDtypeStruct((n, D_MODEL), f32),
                   jax.ShapeDtypeStruct((n * PACK_ROWS, LANES), jnp.int32)],
        compiler_params=_params("parallel"),
        name="mix_out",
    )(y, y_attn, x, w_glu, w_out, g_ssm, g_attn, ln_g, ln_b)


def _router_kernel(h_ref, wrt_ref, bias_ref, tri_ref, trie_ref, e8_ref, pos8_ref, wtok_ref, cnt_ref):
    gsz = N_EXPERTS // N_EXPERT_GROUPS
    tm = h_ref.shape[0]
    ninf = -jnp.inf

    @pl.when(pl.program_id(0) == 0)
    def _():
        cnt_ref[...] = jnp.zeros_like(cnt_ref)

    h = h_ref[...]
    h_hi = h.astype(bf16)
    h_lo = (h - h_hi.astype(f32)).astype(bf16)
    dims = (((1,), (1,)), ((), ()))
    w_hi, w_lo = wrt_ref[0], wrt_ref[1]
    logits = (lax.dot_general(w_hi, h_hi, dims, preferred_element_type=f32)
              + lax.dot_general(w_hi, h_lo, dims, preferred_element_type=f32)
              + lax.dot_general(w_lo, h_hi, dims, preferred_element_type=f32))
    scores = jax.nn.sigmoid(logits)
    sel = scores + bias_ref[...]
    io = lax.broadcasted_iota(jnp.int32, (gsz, tm), 0)

    blks, gs_rows = [], []
    for g in range(N_EXPERT_GROUPS):
        blk = sel[g * gsz:(g + 1) * gsz, :]
        m1 = jnp.max(blk, axis=0, keepdims=True)
        first = jnp.min(jnp.where(blk == m1, io, gsz), axis=0, keepdims=True)
        m2 = jnp.max(jnp.where(io == first, ninf, blk), axis=0, keepdims=True)
        blks.append(blk)
        gs_rows.append(m1 + m2)
    gs = jnp.concatenate(gs_rows, axis=0)

    iog = lax.broadcasted_iota(jnp.int32, (N_EXPERT_GROUPS, tm), 0)
    beaten = jnp.zeros((N_EXPERT_GROUPS, tm), f32)
    for gp in range(N_EXPERT_GROUPS):
        row = gs_rows[gp]
        tie = jnp.where(iog > gp, 1.0, 0.0)
        beaten = beaten + jnp.where(row > gs, 1.0, jnp.where(row == gs, tie, 0.0))
    keep = beaten < TOPK_GROUPS
    masked = [jnp.where(keep[g:g + 1, :], blks[g], ninf) for g in range(N_EXPERT_GROUPS)]

    cand = jnp.concatenate(masked, axis=0)
    eid = lax.broadcasted_iota(jnp.int32, (N_EXPERTS, tm), 0)
    selb = jnp.zeros((N_EXPERTS, tm), f32)
    for _ in range(TOP_K):
        best = jnp.max(cand, axis=0, keepdims=True)
        pick = jnp.min(jnp.where(cand == best, eid, N_EXPERTS), axis=0, keepdims=True)
        hit = eid == pick
        selb = jnp.where(hit, 1.0, selb)
        cand = jnp.where(hit, ninf, cand)
    wsel = selb * scores
    wn = wsel / jnp.sum(wsel, axis=0, keepdims=True) * ROUTED_SCALE

    maskb = selb.astype(bf16)
    pos = jnp.dot(maskb, tri_ref[...], preferred_element_type=f32) + cnt_ref[:, 0:1]
    cnt_ref[...] = cnt_ref[...] + jnp.sum(selb, axis=1, keepdims=True)
    slot = jnp.dot(trie_ref[...], maskb, preferred_element_type=f32)
    ioe = lax.broadcasted_iota(jnp.int32, (N_EXPERTS, tm), 0).astype(f32)
    e_rows, p_rows = [], []
    for k in range(TOP_K):
        hit = jnp.where(slot == k, selb, 0.0)
        e_rows.append(jnp.sum(hit * ioe, axis=0, keepdims=True))
        p_rows.append(jnp.sum(hit * pos, axis=0, keepdims=True))
    e8_ref[...] = jnp.concatenate(e_rows, axis=0).astype(jnp.int32)
    pos8_ref[...] = jnp.concatenate(p_rows, axis=0).astype(jnp.int32)
    wtok_ref[...] = jnp.concatenate([wn.T, jnp.zeros((tm, LANES - N_EXPERTS), f32)], axis=1)


def _router(h, w_router, router_bias, tm=512):
    n = h.shape[0]
    wt = w_router.astype(f32).T
    wt_hi = wt.astype(bf16)
    wrt = jnp.stack([wt_hi, (wt - wt_hi.astype(f32)).astype(bf16)])
    bias = router_bias.astype(f32).reshape(N_EXPERTS, 1)
    tri = (jnp.arange(tm)[:, None] < jnp.arange(tm)[None, :]).astype(bf16)
    trie = (jnp.arange(N_EXPERTS)[None, :] < jnp.arange(N_EXPERTS)[:, None]).astype(bf16)
    full = lambda a: pl.BlockSpec(a.shape, lambda i: (0,) * a.ndim)
    tok = lambda: pl.BlockSpec((TOP_K, tm), lambda i: (0, i))
    return pl.pallas_call(
        _router_kernel,
        grid=(n // tm,),
        in_specs=[pl.BlockSpec((tm, D_MODEL), lambda i: (i, 0)), full(wrt), full(bias), full(tri), full(trie)],
        out_specs=[tok(), tok(), pl.BlockSpec((tm, LANES), lambda i: (i, 0)),
                   pl.BlockSpec((N_EXPERTS, LANES), lambda i: (0, 0))],
        out_shape=[jax.ShapeDtypeStruct((TOP_K, n), jnp.int32), jax.ShapeDtypeStruct((TOP_K, n), jnp.int32),
                   jax.ShapeDtypeStruct((n, LANES), f32), jax.ShapeDtypeStruct((N_EXPERTS, LANES), f32)],
        compiler_params=_params("arbitrary"),
        name="router",
    )(h, wrt, bias, tri, trie)


INVERT_UNROLL = 16


def _invert_kernel(dest_ref, nb_ref, code_ref):
    n_rows = code_ref.shape[0]

    def fill(first_group, last_group):
        def body(k, c):
            for u in range(INVERT_UNROLL):
                code_ref[k * INVERT_UNROLL + u] = PAD_CODE
            return c
        lax.fori_loop(first_group, last_group, body, 0)

    def per_expert(e, c):
        groups = MOE_ROWS // INVERT_UNROLL + 1
        start = jnp.minimum(nb_ref[1 + e], n_rows - groups * INVERT_UNROLL) // INVERT_UNROLL
        fill(start, start + groups)
        return c
    lax.fori_loop(0, N_EXPERTS, per_expert, 0)
    fill(nb_ref[0] * (MOE_ROWS // INVERT_UNROLL), n_rows // INVERT_UNROLL)

    def body(p, c):
        code_ref[dest_ref[p]] = p
        return c
    lax.fori_loop(0, dest_ref.shape[0], body, 0, unroll=INVERT_UNROLL)


def _invert(dest, nb, n_rows):
    smem = lambda: pl.BlockSpec(memory_space=pltpu.SMEM)
    return pl.pallas_call(
        _invert_kernel,
        in_specs=[smem(), smem()],
        out_specs=smem(),
        out_shape=jax.ShapeDtypeStruct((n_rows,), jnp.int32),
        name="invert_dispatch",
    )(dest, nb)


def _expert_changed(be_ref, i):
    return jnp.logical_or(i == 0, be_ref[i] != be_ref[jnp.maximum(i - 1, 0)])


SCATTER_UNROLL = 8


def _moe_up_kernel(be_ref, code_ref, nb_ref, hp_ref, wtok_ref, wg_ref, wu_ref, mid_ref,
                   xg_scr, xa_scr, xb_scr, wra_scr, wrb_scr, wgu_scr, wstage_scr, wsem):
    i = pl.program_id(0)
    rows = MOE_ROWS

    def gather_block(blk, x_dst, wrow_dst):
        base = blk * rows
        for r in range(rows):
            off = pl.multiple_of(code_ref[base + r], PACK_ROWS)
            xg_scr[r * PACK_ROWS:(r + 1) * PACK_ROWS, :] = hp_ref[pl.ds(off, PACK_ROWS), :]
            wrow_dst[r:r + 1, :] = wtok_ref[pl.ds(off >> 3, 1), :]
        half = D_MODEL // 2
        chunks = pltpu.einshape("rsl->srl", xg_scr[...].reshape(rows, PACK_ROWS, LANES))
        for s in range(PACK_ROWS):
            wds = chunks[s]
            lo = pltpu.unpack_elementwise(wds, index=0, packed_dtype=bf16, unpacked_dtype=f32)
            hi = pltpu.unpack_elementwise(wds, index=1, packed_dtype=bf16, unpacked_dtype=f32)
            x_dst[:, s * LANES:(s + 1) * LANES] = lo.astype(bf16)
            x_dst[:, half + s * LANES:half + (s + 1) * LANES] = hi.astype(bf16)

    def weight_copies(e, s):
        return (pltpu.make_async_copy(wg_ref.at[e], wstage_scr.at[s, 0], wsem.at[s, 0]),
                pltpu.make_async_copy(wu_ref.at[e], wstage_scr.at[s, 1], wsem.at[s, 1]))

    def expert_block(blk, out_rows, x_cur, wrow_cur, x_nxt, wrow_nxt):
        last = MOE_BLOCKS - 1
        expert = be_ref[jnp.minimum(blk, last)]
        slot = nb_ref[NB_SLOT + expert]
        upcoming = nb_ref[NB_NEXT + expert]
        valid = blk < nb_ref[0]

        @pl.when(jnp.logical_and(valid, blk == 0))
        def _():
            for cp in weight_copies(expert, slot):
                cp.start()

        @pl.when(jnp.logical_and(valid, _expert_changed(be_ref, jnp.minimum(blk, last))))
        def _():
            for cp in weight_copies(expert, slot):
                cp.wait()

            @pl.when(upcoming >= 0)
            def _():
                for cp in weight_copies(upcoming, 1 - slot):
                    cp.start()

            wgu_scr[:, :D_EXPERT] = wstage_scr[slot, 0].astype(bf16)
            wgu_scr[:, D_EXPERT:] = wstage_scr[slot, 1].astype(bf16)

        @pl.when(valid)
        def _():
            gather_block(jnp.minimum(blk + 1, last), x_nxt, wrow_nxt)
            lane = lax.broadcasted_iota(jnp.int32, (rows, LANES), 1)
            w = jnp.sum(jnp.where(lane == expert, wrow_cur[...], 0.0), axis=1, keepdims=True)
            real = blk * rows + lax.broadcasted_iota(jnp.int32, (rows, 1), 0) < nb_ref[1 + expert]
            gu = jnp.dot(x_cur[...], wgu_scr[...], preferred_element_type=f32)
            g, u = gu[:, :D_EXPERT], gu[:, D_EXPERT:]
            mid_ref[out_rows, :] = (g * jax.nn.sigmoid(g) * u * jnp.where(real, w, 0.0)).astype(bf16)

        @pl.when(jnp.logical_not(valid))
        def _():
            mid_ref[out_rows, :] = jnp.zeros((rows, D_EXPERT), bf16)

    @pl.when(i == 0)
    def _():
        gather_block(0, xa_scr, wra_scr)

    expert_block(2 * i, slice(0, rows), xa_scr, wra_scr, xb_scr, wrb_scr)
    expert_block(2 * i + 1, slice(rows, 2 * rows), xb_scr, wrb_scr, xa_scr, wra_scr)


def _moe_up(block_e, code, nb, hp, wtok, w_gate, w_up):
    grid_spec = pltpu.PrefetchScalarGridSpec(
        num_scalar_prefetch=3,
        grid=(MOE_BLOCKS // 2,),
        in_specs=[pl.BlockSpec(memory_space=pltpu.VMEM),
                  pl.BlockSpec(memory_space=pltpu.VMEM),
                  pl.BlockSpec(memory_space=pl.ANY),
                  pl.BlockSpec(memory_space=pl.ANY)],
        out_specs=pl.BlockSpec((2 * MOE_ROWS, D_EXPERT), lambda i, be, cd, nb: (i, 0)),
        scratch_shapes=[pltpu.VMEM((PACK_ROWS * MOE_ROWS, LANES), jnp.int32),
                        pltpu.VMEM((MOE_ROWS, D_MODEL), bf16), pltpu.VMEM((MOE_ROWS, D_MODEL), bf16),
                        pltpu.VMEM((MOE_ROWS, LANES), f32), pltpu.VMEM((MOE_ROWS, LANES), f32),
                        pltpu.VMEM((D_MODEL, 2 * D_EXPERT), bf16),
                        pltpu.VMEM((2, 2, D_MODEL, D_EXPERT), f32),
                        pltpu.SemaphoreType.DMA((2, 2))])
    return pl.pallas_call(
        _moe_up_kernel,
        grid_spec=grid_spec,
        out_shape=jax.ShapeDtypeStruct((MOE_BLOCKS * MOE_ROWS, D_EXPERT), bf16),
        compiler_params=pltpu.CompilerParams(dimension_semantics=("arbitrary",),
                                             vmem_limit_bytes=MOE_UP_VMEM_LIMIT),
        name="moe_up",
    )(block_e, code, nb, hp, wtok, w_gate, w_up)


def _moe_down_kernel(be_ref, code_ref, nb_ref, mida_ref, midb_ref, wd_ref, acc_ref,
                     ya_scr, yb0_scr, yb1_scr, wa_scr, wb_scr, wstage_scr, wsem):
    j = pl.program_id(1)
    rows = MOE_ROWS
    nb = nb_ref[0]
    blk_a, blk_b = 2 * j, 2 * j + 1
    last = MOE_BLOCKS - 1

    @pl.when(j == 0)
    def _():
        acc_ref[...] = jnp.zeros_like(acc_ref)

    def scatter_rows(base, ybuf, r0, n):
        sums, addrs = [], []
        for k in range(n):
            a = pl.multiple_of(code_ref[base + r0 + k], PACK_ROWS)
            v = ybuf[pl.ds(pl.multiple_of((r0 + k) * PACK_ROWS, PACK_ROWS), PACK_ROWS), :]
            sums.append(acc_ref[pl.ds(a, PACK_ROWS), :] + v)
            addrs.append(a)
        for k in range(n):
            acc_ref[pl.ds(addrs[k], PACK_ROWS), :] = sums[k]

    def scatter_block(blk, ybuf):
        for g in range(rows // SCATTER_UNROLL):
            scatter_rows(blk * rows, ybuf, g * SCATTER_UNROLL, SCATTER_UNROLL)

    def scatter_block_compact(blk, ybuf):
        def body(g, c):
            scatter_rows(blk * rows, ybuf, g * SCATTER_UNROLL, SCATTER_UNROLL)
            return c
        lax.fori_loop(0, rows // SCATTER_UNROLL, body, 0)

    def down_block(mid_ref, w_scr, ybuf):
        y = jnp.dot(mid_ref[...], w_scr[...], preferred_element_type=f32)
        y3 = jnp.stack([y[:, c * LANES:(c + 1) * LANES] for c in range(PACK_ROWS)], axis=0)
        ybuf[...] = pltpu.einshape("crl->rcl", y3).reshape(rows * PACK_ROWS, LANES)

    half = D_MODEL // 2
    cols = pl.ds(pl.multiple_of(pl.program_id(0) * half, half), half)

    def weight_copy(e, s):
        return pltpu.make_async_copy(wd_ref.at[e, :, cols], wstage_scr.at[s], wsem.at[s])

    def refresh_weights(blk, w_scr):
        cur = be_ref[jnp.minimum(blk, last)]
        slot = nb_ref[NB_SLOT + cur]
        upcoming = nb_ref[NB_NEXT + cur]
        valid = blk < nb

        @pl.when(jnp.logical_and(valid, blk == 0))
        def _():
            weight_copy(cur, slot).start()

        @pl.when(jnp.logical_and(valid, _expert_changed(be_ref, jnp.minimum(blk, last))))
        def _():
            weight_copy(cur, slot).wait()

            @pl.when(upcoming >= 0)
            def _():
                weight_copy(upcoming, 1 - slot).start()

        prev = be_ref[jnp.clip(blk - 2, 0, last)]

        @pl.when(jnp.logical_and(valid, jnp.logical_or(j == 0, cur != prev)))
        def _():
            w_scr[...] = wstage_scr[slot].astype(bf16)

    refresh_weights(blk_a, wa_scr)
    refresh_weights(blk_b, wb_scr)

    for parity, yb_this, yb_prev in ((0, yb0_scr, yb1_scr), (1, yb1_scr, yb0_scr)):
        mine = j % 2 == parity

        @pl.when(jnp.logical_and(mine, jnp.logical_and(j > 0, blk_b < nb)))
        def _():
            down_block(mida_ref, wa_scr, ya_scr)
            down_block(midb_ref, wb_scr, yb_this)
            scatter_block(blk_a - 1, yb_prev)
            scatter_block(blk_a, ya_scr)

        @pl.when(jnp.logical_and(mine, jnp.logical_and(j > 0, blk_b == nb)))
        def _():
            down_block(mida_ref, wa_scr, ya_scr)
            scatter_block_compact(blk_a - 1, yb_prev)
            scatter_block_compact(blk_a, ya_scr)

        @pl.when(jnp.logical_and(mine, jnp.logical_and(j > 0, blk_a == nb)))
        def _():
            scatter_block_compact(blk_a - 1, yb_prev)

    @pl.when(jnp.logical_and(j == 0, blk_a < nb))
    def _():
        down_block(mida_ref, wa_scr, ya_scr)
        scatter_block_compact(blk_a, ya_scr)

    @pl.when(jnp.logical_and(j == 0, blk_b < nb))
    def _():
        down_block(midb_ref, wb_scr, yb0_scr)


def _moe_down(block_e, code, nb, mid, w_down):
    half = D_MODEL // 2
    acc_rows = (N_TOKENS + 1) * PACK_ROWS
    last = MOE_BLOCKS - 1
    assert MOE_BLOCKS % 2 == 0
    blk = lambda off: (lambda p, j, be, cd, nb: (jnp.minimum(2 * j + off, last), 0))
    grid_spec = pltpu.PrefetchScalarGridSpec(
        num_scalar_prefetch=3,
        grid=(2, MOE_BLOCKS // 2 + 1),
        in_specs=[pl.BlockSpec((MOE_ROWS, D_EXPERT), blk(0)), pl.BlockSpec((MOE_ROWS, D_EXPERT), blk(1)),
                  pl.BlockSpec(memory_space=pl.ANY)],
        out_specs=pl.BlockSpec((None, acc_rows, LANES), lambda p, j, be, cd, nb: (p, 0, 0),
                               pipeline_mode=pl.Buffered(1)),
        scratch_shapes=[pltpu.VMEM((PACK_ROWS * MOE_ROWS, LANES), f32)] * 3
        + [pltpu.VMEM((D_EXPERT, half), bf16)] * 2
        + [pltpu.VMEM((2, D_EXPERT, half), f32), pltpu.SemaphoreType.DMA((2,))])
    return pl.pallas_call(
        _moe_down_kernel,
        grid_spec=grid_spec,
        out_shape=jax.ShapeDtypeStruct((2, acc_rows, LANES), f32),
        compiler_params=_params("arbitrary", "arbitrary"),
        name="moe_down",
    )(block_e, code, nb, mid, mid, w_down)


def _final_kernel(h_ref, r0_ref, r1_ref, wgu_ref, wd_ref, lg_ref, lb_ref, o_ref):
    tm = h_ref.shape[0]
    gu = jnp.dot(h_ref[...].astype(bf16), wgu_ref[...], preferred_element_type=f32)
    g, u = gu[:, :D_EXPERT], gu[:, D_EXPERT:]
    mid = (g * jax.nn.sigmoid(g) * u).astype(bf16)
    shared = jnp.dot(mid, wd_ref[...], preferred_element_type=f32)
    halves = [pltpu.einshape("rcl->crl", r[...].reshape(tm, PACK_ROWS, LANES)) for r in (r0_ref, r1_ref)]
    routed = jnp.concatenate([hv[c] for hv in halves for c in range(PACK_ROWS)], axis=1)
    o_ref[...] = _layer_norm_rows(DEEPNORM_ALPHA * h_ref[...] + routed + shared,
                                  lg_ref[...], lb_ref[...])


def _final(h, racc, wgu, wd, ln_g, ln_b, tm=256):
    n = h.shape[0]
    row = lambda: pl.BlockSpec((tm, D_MODEL), lambda i: (i, 0))
    full = lambda a: pl.BlockSpec(a.shape, lambda i: (0,) * a.ndim)
    acc = lambda p: pl.BlockSpec((None, tm * PACK_ROWS, LANES), lambda i: (p, i, 0))
    return pl.pallas_call(
        _final_kernel,
        grid=(n // tm,),
        in_specs=[row(), acc(0), acc(1), full(wgu), full(wd), full(ln_g), full(ln_b)],
        out_specs=row(),
        out_shape=jax.ShapeDtypeStruct((n, D_MODEL), f32),
        compiler_params=_params("parallel"),
        name="shared_final",
    )(h, racc, racc, wgu, wd, ln_g, ln_b)


def _dispatch_plan(e8, pos8, cnt):
    counts = cnt[:, 0].astype(jnp.int32)
    padded = (counts + MOE_ROWS - 1) // MOE_ROWS * MOE_ROWS
    pad_end = jnp.cumsum(padded).astype(jnp.int32)
    pad_start = pad_end - padded
    seg_end = pad_start + counts
    ids = jnp.arange(N_EXPERTS, dtype=jnp.int32)
    start8 = jnp.sum(jnp.where(e8[..., None] == ids, pad_start, 0), axis=-1)
    dest = (start8 + pos8).T.reshape(-1)
    block_start = jnp.arange(MOE_BLOCKS, dtype=jnp.int32) * MOE_ROWS
    block_e = jnp.minimum(jnp.sum((pad_end[None, :] <= block_start[:, None]).astype(jnp.int32), axis=1),
                          N_EXPERTS - 1)
    used = counts > 0
    slot = (jnp.cumsum(used.astype(jnp.int32)) - 1) % 2
    later = jnp.where(used, ids, N_EXPERTS)
    nxt = jnp.concatenate([lax.cummin(later[::-1])[::-1][1:], jnp.full((1,), N_EXPERTS, jnp.int32)])
    nxt = jnp.where(nxt == N_EXPERTS, -1, nxt)
    nb = jnp.concatenate([pad_end[-1:] // MOE_ROWS, seg_end, slot, nxt]).astype(jnp.int32)
    return dest, block_e, nb


def kernel(x, w_in, ssm_log_dt, ssm_a_re, ssm_a_im, ssm_b_re, ssm_b_im, ssm_c_re, ssm_c_im, ssm_d,
           w_glu, g_ssm_out, g_attn_out, w_out, ln1_g, ln1_b, w_router, router_bias, w_gate, w_up,
           w_down, ws_gate, ws_up, ws_down, ln2_g, ln2_b):
    bsz, seq, d = x.shape
    n_tok = bsz * seq
    h = x.reshape(n_tok, d)
    for layer in range(DEPTH):
        proj = _matmul(h, w_in[layer].astype(bf16), f32)

        tables = _s5_tables(ssm_log_dt[layer], ssm_a_re[layer], ssm_a_im[layer], ssm_b_re[layer],
                            ssm_b_im[layer], ssm_c_re[layer], ssm_c_im[layer], ssm_d[layer])
        y = _s5_mixer(proj, tables, bsz, seq // S5_CHUNK)

        y_attn = _dilated_attention(proj, D_SSM // (2 * HEAD_DIM), bsz, seq)

        row2 = lambda a: a.astype(f32).reshape(1, -1)
        h, hp = _mix_out(y, y_attn, h, w_glu[layer].astype(bf16), w_out[layer].astype(bf16),
                         row2(g_ssm_out[layer]), row2(g_attn_out[layer]), row2(ln1_g[layer]), row2(ln1_b[layer]))

        assert n_tok == N_TOKENS
        e8, pos8, wtok, cnt = _router(h, w_router[layer], router_bias[layer])
        dest, block_e, nb = _dispatch_plan(e8, pos8, cnt)
        code = _invert(dest, nb, MOE_BLOCKS * MOE_ROWS)
        src_row = ((code >> 3) & (N_TOKENS - 1)) * PACK_ROWS
        dst_row = code & -PACK_ROWS
        mid = _moe_up(block_e, src_row, nb, hp, wtok, w_gate[layer], w_up[layer])
        racc = _moe_down(block_e, dst_row, nb, mid, w_down[layer])
        wgu = jnp.concatenate([ws_gate[layer], ws_up[layer]], axis=1).astype(bf16)
        h = _final(h, racc, wgu, ws_down[layer].astype(bf16), row2(ln2_g[layer]), row2(ln2_b[layer]))
    return h.reshape(bsz, seq, d)
```

```python
import functools

import jax
import jax.numpy as jnp
import numpy as np
from jax import lax
from jax.experimental import pallas as pl
from jax.experimental.pallas import tpu as pltpu

D_MODEL = 2048
D_SSM = 1024
D_ATTN = 1024
SSM_CH = 16
SSM_GROUPS = 64
SSM_STATE = 64
HEAD_DIM = 64
N_HEADS = 16
PATTERNS = ((128, 1), (512, 4), (2048, 16))
ATTN_BLOCK = 128
N_EXPERTS = 64
TOP_K = 8
N_EXPERT_GROUPS = 8
TOPK_GROUPS = 4
D_EXPERT = 512
ROUTED_SCALE = 2.5
NORM_EPS = 1e-5
DEPTH = 1
DEEPNORM_ALPHA = (2.0 * DEPTH) ** 0.25

LANES = 128
SUBLANES = 8
V7X_VMEM_BYTES = 64 * 1024 * 1024
S5_CHUNK = 16
S5_GROUPS_PER_BLOCK = LANES // SSM_CH
MOE_ROWS = 256
MASK_VALUE = -1e30
PACK_ROWS = D_MODEL // 2 // LANES
assert PACK_ROWS == SUBLANES
N_TOKENS = 8192
PAD_CODE = N_TOKENS * TOP_K
NB_SLOT = 1 + N_EXPERTS
NB_NEXT = 1 + 2 * N_EXPERTS
MOE_BLOCKS = -(-(N_TOKENS * TOP_K + N_EXPERTS * (MOE_ROWS - 1)) // MOE_ROWS)
VMEM_LIMIT = V7X_VMEM_BYTES - 8 * 1024 * 1024
MOE_UP_VMEM_LIMIT = V7X_VMEM_BYTES - 2 * 1024 * 1024

bf16 = jnp.bfloat16
f32 = jnp.float32


def _params(*sem):
    return pltpu.CompilerParams(dimension_semantics=sem, vmem_limit_bytes=VMEM_LIMIT)


MATMUL_COLS = 1024


def _matmul_kernel(a_ref, b_ref, o_ref):
    a = a_ref[...].astype(bf16)
    for j in range(o_ref.shape[1] // MATMUL_COLS):
        cols = slice(j * MATMUL_COLS, (j + 1) * MATMUL_COLS)
        o_ref[:, cols] = jnp.dot(a, b_ref[:, cols], preferred_element_type=f32).astype(o_ref.dtype)


def _matmul(a, b, out_dtype, tm=256):
    m, k = a.shape
    _, n = b.shape
    return pl.pallas_call(
        _matmul_kernel,
        grid=(m // tm,),
        in_specs=[pl.BlockSpec((tm, k), lambda i: (i, 0)),
                  pl.BlockSpec((k, n), lambda i: (0, 0), pipeline_mode=pl.Buffered(1))],
        out_specs=pl.BlockSpec((tm, n), lambda i: (i, 0)),
        out_shape=jax.ShapeDtypeStruct((m, n), out_dtype),
        compiler_params=_params("parallel"),
        name="matmul",
    )(a, b)


def _s5_tables(log_dt, a_re, a_im, b_re, b_im, c_re, c_im, d_skip):
    t = S5_CHUNK
    gpb = S5_GROUPS_PER_BLOCK
    nblk = SSM_GROUPS // gpb
    hp = lax.Precision.HIGHEST
    lr = jnp.minimum(a_re.astype(f32), -1e-4)
    li = a_im.astype(f32)
    dt = jnp.exp(log_dt.astype(f32))
    kk = jnp.arange(t + 1, dtype=f32)[:, None, None]
    mag = jnp.exp(kk * (lr * dt))
    pr = mag * jnp.cos(kk * (li * dt))
    pi = mag * jnp.sin(kk * (li * dt))
    xr, xi = pr[1] - 1.0, pi[1]
    den = lr * lr + li * li
    cr = (xr * lr + xi * li) / den
    ci = (xi * lr - xr * li) / den
    bbr = cr[..., None] * b_re - ci[..., None] * b_im
    bbi = cr[..., None] * b_im + ci[..., None] * b_re
    wr = pr[:t, :, :, None] * bbr - pi[:t, :, :, None] * bbi
    wi = pr[:t, :, :, None] * bbi + pi[:t, :, :, None] * bbr
    taps = (jnp.einsum('gop,tgpc->tgco', c_re, wr, precision=hp)
            - jnp.einsum('gop,tgpc->tgco', c_im, wi, precision=hp))
    ktab = taps.reshape(t, nblk, gpb, SSM_CH, SSM_CH).transpose(1, 0, 3, 2, 4)
    ktab = ktab.reshape(nblk, t, SSM_CH, gpb * SSM_CH)

    rev = jnp.arange(t - 1, -1, -1)
    sr = pr[rev][..., None] * bbr - pi[rev][..., None] * bbi
    si = pr[rev][..., None] * bbi + pi[rev][..., None] * bbr
    sb = jnp.stack([sr, si], axis=0).reshape(2, t, nblk, gpb, SSM_STATE, SSM_CH)
    bsrc = sb.transpose(2, 1, 0, 3, 5, 4).reshape(nblk, t, 2, gpb * SSM_CH, SSM_STATE)
    bsrc = jnp.concatenate([bsrc, bsrc], axis=-1)

    er = c_re[None] * pr[1:, :, None, :] - c_im[None] * pi[1:, :, None, :]
    ei = c_re[None] * pi[1:, :, None, :] + c_im[None] * pr[1:, :, None, :]
    eb = jnp.stack([er, -ei], axis=0).reshape(2, t, nblk, gpb, SSM_CH, SSM_STATE)
    csrc = eb.transpose(2, 1, 0, 5, 3, 4).reshape(nblk, t, 2, SSM_STATE, gpb * SSM_CH)

    a_chunk = jnp.stack([pr[t], pi[t]], axis=0).reshape(2, nblk, 1, gpb * SSM_STATE)
    a_chunk = a_chunk.transpose(1, 0, 2, 3).reshape(nblk, 2, gpb * SSM_STATE)
    dvec = jnp.tile(d_skip.astype(f32).reshape(nblk, 1, gpb * SSM_CH), (1, 1, t))
    return ktab.astype(bf16), bsrc.astype(bf16), csrc.astype(bf16), a_chunk, dvec


def _s5_kernel(u_ref, ktab_ref, bsrc_ref, csrc_ref, a_ref, d_ref, y_ref,
               toep_ref, bpow_ref, cpow_ref, s_ref, h_ref, yt_ref, *, n_batch, n_chunk):
    t = S5_CHUNK
    gpb = S5_GROUPS_PER_BLOCK
    w = gpb * SSM_CH
    ns = gpb * SSM_STATE
    zero = jnp.zeros((), bf16)

    def same_group(shape, row_size, col_size):
        r = lax.broadcasted_iota(jnp.int32, shape, 0) // row_size
        c = lax.broadcasted_iota(jnp.int32, shape, 1) // col_size
        return r == c

    tap_mask = same_group((w, w), SSM_CH, SSM_CH)
    taps = [jnp.where(tap_mask, jnp.tile(ktab_ref[0, tau], (gpb, 1)), zero) for tau in range(t)]
    for tt in range(t):
        for ss in range(tt + 1):
            toep_ref[ss * w:(ss + 1) * w, tt * w:(tt + 1) * w] = taps[tt - ss]
        if tt % 2 == 0:
            toep_ref[(tt + 1) * w:(tt + 2) * w, tt * w:(tt + 1) * w] = jnp.zeros((w, w), bf16)
    b_mask = same_group((w, ns), SSM_CH, SSM_STATE)
    c_mask = same_group((ns, w), SSM_STATE, SSM_CH)
    for ss in range(t):
        for z in range(2):
            bpow_ref[ss * w:(ss + 1) * w, z * ns:(z + 1) * ns] = jnp.where(
                b_mask, jnp.tile(bsrc_ref[0, ss, z], (1, ns // w)), zero)
            cpow_ref[z * ns:(z + 1) * ns, ss * w:(ss + 1) * w] = jnp.where(
                c_mask, jnp.tile(csrc_ref[0, ss, z], (gpb, 1)), zero)

    rows = n_batch * n_chunk
    uf3 = pltpu.einshape("rsl->srl", u_ref[...].reshape(rows, t, w))
    uf = [uf3[s] for s in range(t)]
    u = jnp.concatenate([p.astype(bf16) for p in uf], axis=1)
    s_ref[...] = jnp.dot(u, bpow_ref[...], preferred_element_type=f32)

    ar = a_ref[0, 0:1, :]
    ai = a_ref[0, 1:2, :]

    def step(j, carry):
        new = []
        for b in range(n_batch):
            hr, hi = carry[2 * b], carry[2 * b + 1]
            row = b * n_chunk + j
            h_ref[pl.ds(row, 1), 0:ns] = hr
            h_ref[pl.ds(row, 1), ns:2 * ns] = hi
            sr = s_ref[pl.ds(row, 1), 0:ns]
            si = s_ref[pl.ds(row, 1), ns:2 * ns]
            new.append(ar * hr - ai * hi + sr)
            new.append(ar * hi + ai * hr + si)
        return tuple(new)

    zero = jnp.zeros((1, ns), f32)
    lax.fori_loop(0, n_chunk, step, (zero,) * (2 * n_batch))

    hprev = h_ref[...].astype(bf16)
    for tp in range(t // 2):
        c0, c1 = 2 * tp * w, (2 * tp + 2) * w
        y = jnp.dot(u[:, :c1], toep_ref[0:c1, c0:c1], preferred_element_type=f32)
        y = y + jnp.dot(hprev, cpow_ref[:, c0:c1], preferred_element_type=f32)
        for k, tt in enumerate((2 * tp, 2 * tp + 1)):
            yk = y[:, k * w:(k + 1) * w] + d_ref[0, :, tt * w:(tt + 1) * w] * uf[tt]
            yt_ref[tt] = jax.nn.gelu(yk, approximate=True)
    y_ref[...] = pltpu.einshape("srl->rsl", yt_ref[...]).reshape(rows * t, w)


def _s5_mixer(proj, tables, n_batch, n_chunk):
    ktab, bsrc, csrc, a_chunk, dvec = tables
    nblk = ktab.shape[0]
    w = S5_GROUPS_PER_BLOCK * SSM_CH
    cols = S5_CHUNK * w
    rows = n_batch * n_chunk
    n_tok = rows * S5_CHUNK
    ns = S5_GROUPS_PER_BLOCK * SSM_STATE
    kern = functools.partial(_s5_kernel, n_batch=n_batch, n_chunk=n_chunk)
    return pl.pallas_call(
        kern,
        grid=(nblk,),
        in_specs=[pl.BlockSpec((n_tok, w), lambda g: (0, g)),
                  pl.BlockSpec((1,) + ktab.shape[1:], lambda g: (g, 0, 0, 0)),
                  pl.BlockSpec((1,) + bsrc.shape[1:], lambda g: (g, 0, 0, 0, 0)),
                  pl.BlockSpec((1,) + csrc.shape[1:], lambda g: (g, 0, 0, 0, 0)),
                  pl.BlockSpec((1, 2, ns), lambda g: (g, 0, 0)),
                  pl.BlockSpec((1, 1, cols), lambda g: (g, 0, 0))],
        out_specs=pl.BlockSpec((n_tok, w), lambda g: (0, g)),
        out_shape=jax.ShapeDtypeStruct((n_tok, D_SSM), f32),
        scratch_shapes=[pltpu.VMEM((cols, cols), bf16),
                        pltpu.VMEM((cols, 2 * ns), bf16),
                        pltpu.VMEM((2 * ns, cols), bf16),
                        pltpu.VMEM((rows, 2 * ns), f32),
                        pltpu.VMEM((rows, 2 * ns), f32),
                        pltpu.VMEM((S5_CHUNK, rows, w), f32)],
        compiler_params=_params("parallel"),
        name="s5_mixer",
    )(proj, ktab, bsrc, csrc, a_chunk, dvec)


def _attn_bias_table():
    blk = ATTN_BLOCK
    slopes = 2.0 ** (-8.0 * jnp.arange(1, N_HEADS + 1, dtype=f32) / N_HEADS)
    delta = np.arange(blk)[:, None] - (np.arange(2 * blk)[None, :] - blk)
    tabs = []
    for window, dil in PATTERNS:
        assert window // dil == blk
        valid = (delta >= 0) & (delta <= window // dil)
        dist = jnp.asarray(delta * dil, dtype=f32)
        bias = jnp.where(valid[None], -slopes[:, None, None] * dist[None], MASK_VALUE)
        tabs.append(bias.reshape(N_HEADS // 2, 2 * blk, 2 * blk))
    return jnp.stack(tabs, axis=1)


def _attn_kernel(q_ref, k_ref, v_ref, bias_ref, o_ref, *scr):
    blk = ATTN_BLOCK
    seq = q_ref.shape[0]
    first_head = lax.broadcasted_iota(jnp.int32, (blk, 2 * HEAD_DIM), 1) < HEAD_DIM
    dims = (((1,), (1,)), ((), ()))
    for pi, (_, dil) in enumerate(PATTERNS):
        o_scr, l_scr = scr[2 * pi], scr[2 * pi + 1]
        sub = seq // dil
        for r in range(dil):
            rows = (lambda st, n: pl.ds(st, n)) if dil == 1 else (lambda st, n: pl.ds(st, n, stride=dil))
            qd = (q_ref[rows(r, sub), :] * HEAD_DIM ** -0.5).astype(bf16)
            kd = k_ref[rows(r, sub), :].astype(bf16)
            vd = v_ref[rows(r, sub), :].astype(bf16)
            for i in range(sub // blk):
                qb = qd[i * blk:(i + 1) * blk]
                zero = jnp.zeros_like(qb)
                q2 = jnp.concatenate([jnp.where(first_head, qb, zero), jnp.where(first_head, zero, qb)], axis=0)
                k0 = max(i - 1, 0) * blk
                nk = (i + 1) * blk - k0
                s = lax.dot_general(q2, kd[k0:k0 + nk], dims, preferred_element_type=f32)
                s = s + bias_ref[0, pi, :, 2 * blk - nk:]
                m = jnp.max(s, axis=-1, keepdims=True)
                p = jnp.exp(s - m)
                l = jnp.sum(p, axis=-1, keepdims=True)
                o = jnp.dot(p.astype(bf16), vd[k0:k0 + nk], preferred_element_type=f32) / l
                lse = m + jnp.log(l)
                dst = rows(r + dil * blk * i, blk)
                o_scr[dst, :] = jnp.where(first_head, o[:blk], o[blk:])
                l_scr[dst, :] = jnp.where(first_head, lse[:blk], lse[blk:])
    l1, l2, l3 = scr[1][...], scr[3][...], scr[5][...]
    m = jnp.maximum(jnp.maximum(l1, l2), l3)
    e1, e2, e3 = jnp.exp(l1 - m), jnp.exp(l2 - m), jnp.exp(l3 - m)
    o_ref[...] = (e1 * scr[0][...] + e2 * scr[2][...] + e3 * scr[4][...]) / (e1 + e2 + e3)


def _dilated_attention(qkv, first, bsz, seq):
    pairs = N_HEADS // 2
    width = 2 * HEAD_DIM
    bias = _attn_bias_table()
    col = lambda off: pl.BlockSpec((seq, width), lambda hp, b: (b, first + off + hp))
    return pl.pallas_call(
        _attn_kernel,
        grid=(pairs, bsz),
        in_specs=[col(0), col(pairs), col(2 * pairs),
                  pl.BlockSpec((1,) + bias.shape[1:], lambda hp, b: (hp, 0, 0, 0))],
        out_specs=pl.BlockSpec((seq, width), lambda hp, b: (b, hp)),
        out_shape=jax.ShapeDtypeStruct((bsz * seq, D_ATTN), f32),
        scratch_shapes=[pltpu.VMEM((seq, width), f32)] * (2 * len(PATTERNS)),
        compiler_params=_params("parallel", "parallel"),
        name="dilated_attention",
    )(qkv, qkv, qkv, bias)


def _layer_norm_rows(x, g, b):
    mu = jnp.mean(x, axis=-1, keepdims=True)
    xc = x - mu
    var = jnp.mean(xc * xc, axis=-1, keepdims=True)
    return xc * lax.rsqrt(var + NORM_EPS) * g + b


def _rms_rows(x, g):
    return x * lax.rsqrt(jnp.mean(x * x, axis=-1, keepdims=True) + NORM_EPS) * g


def _mix_out_kernel(y_ref, ya_ref, x_ref, wglu_ref, w_ref, gs_ref, ga_ref, lg_ref, lb_ref, h_ref, hp_ref):
    tm = y_ref.shape[0] // 2
    half = D_MODEL // 2
    for part in range(2):
        rows = slice(part * tm, (part + 1) * tm)
        z = jnp.dot(y_ref[rows, :].astype(bf16), wglu_ref[...], preferred_element_type=f32)
        y_ssm = z[:, :D_SSM] * jax.nn.sigmoid(z[:, D_SSM:])
        ns = _rms_rows(y_ssm, gs_ref[...]).astype(bf16)
        na = _rms_rows(ya_ref[rows, :], ga_ref[...]).astype(bf16)
        proj = jnp.dot(ns, w_ref[0:D_SSM, :], preferred_element_type=f32)
        proj = proj + jnp.dot(na, w_ref[D_SSM:, :], preferred_element_type=f32)
        h = _layer_norm_rows(DEEPNORM_ALPHA * x_ref[rows, :] + proj, lg_ref[...], lb_ref[...])
        h_ref[rows, :] = h
        word = pltpu.pack_elementwise([h[:, :half], h[:, half:]], packed_dtype=bf16)
        chunks = jnp.stack([word[:, s * LANES:(s + 1) * LANES] for s in range(PACK_ROWS)], axis=0)
        hp_ref[part * tm * PACK_ROWS:(part + 1) * tm * PACK_ROWS, :] = (
            pltpu.einshape("srl->rsl", chunks).reshape(tm * PACK_ROWS, LANES))


def _mix_out(y, y_attn, x, w_glu, w_out, g_ssm, g_attn, ln_g, ln_b, tm=512):
    n = x.shape[0]
    row = lambda c: pl.BlockSpec((tm, c), lambda i: (i, 0))
    full = lambda a: pl.BlockSpec(a.shape, lambda i: (0,) * a.ndim)
    once = lambda a: pl.BlockSpec(a.shape, lambda i: (0,) * a.ndim, pipeline_mode=pl.Buffered(1))
    return pl.pallas_call(
        _mix_out_kernel,
        grid=(n // tm,),
        in_specs=[row(D_SSM), row(D_ATTN), row(D_MODEL), once(w_glu), once(w_out), full(g_ssm), full(g_attn),
                  full(ln_g), full(ln_b)],
        out_specs=[row(D_MODEL), pl.BlockSpec((tm * PACK_ROWS, LANES), lambda i: (i, 0))],
        out_shape=[jax.ShapeDtypeStruct((n, D_MODEL), f32),
                   jax.ShapeDtypeStruct((n * PACK_ROWS, LANES), jnp.int32)],
        compiler_params=_params("parallel"),
        name="mix_out",
    )(y, y_attn, x, w_glu, w_out, g_ssm, g_attn, ln_g, ln_b)


def _router_kernel(h_ref, wrt_ref, bias_ref, tri_ref, trie_ref, e8_ref, pos8_ref, wtok_ref, cnt_ref):
    gsz = N_EXPERTS // N_EXPERT_GROUPS
    tm = h_ref.shape[0]
    ninf = -jnp.inf

    @pl.when(pl.program_id(0) == 0)
    def _():
        cnt_ref[...] = jnp.zeros_like(cnt_ref)

    h = h_ref[...]
    h_hi = h.astype(bf16)
    h_lo = (h - h_hi.astype(f32)).astype(bf16)
    dims = (((1,), (1,)), ((), ()))
    w_hi, w_lo = wrt_ref[0], wrt_ref[1]
    logits = (lax.dot_general(w_hi, h_hi, dims, preferred_element_type=f32)
              + lax.dot_general(w_hi, h_lo, dims, preferred_element_type=f32)
              + lax.dot_general(w_lo, h_hi, dims, preferred_element_type=f32))
    scores = jax.nn.sigmoid(logits)
    sel = scores + bias_ref[...]
    io = lax.broadcasted_iota(jnp.int32, (gsz, tm), 0)

    blks, gs_rows = [], []
    for g in range(N_EXPERT_GROUPS):
        blk = sel[g * gsz:(g + 1) * gsz, :]
        m1 = jnp.max(blk, axis=0, keepdims=True)
        first = jnp.min(jnp.where(blk == m1, io, gsz), axis=0, keepdims=True)
        m2 = jnp.max(jnp.where(io == first, ninf, blk), axis=0, keepdims=True)
        blks.append(blk)
        gs_rows.append(m1 + m2)
    gs = jnp.concatenate(gs_rows, axis=0)

    iog = lax.broadcasted_iota(jnp.int32, (N_EXPERT_GROUPS, tm), 0)
    beaten = jnp.zeros((N_EXPERT_GROUPS, tm), f32)
    for gp in range(N_EXPERT_GROUPS):
        row = gs_rows[gp]
        tie = jnp.where(iog > gp, 1.0, 0.0)
        beaten = beaten + jnp.where(row > gs, 1.0, jnp.where(row == gs, tie, 0.0))
    keep = beaten < TOPK_GROUPS
    masked = [jnp.where(keep[g:g + 1, :], blks[g], ninf) for g in range(N_EXPERT_GROUPS)]

    cand = jnp.concatenate(masked, axis=0)
    eid = lax.broadcasted_iota(jnp.int32, (N_EXPERTS, tm), 0)
    selb = jnp.zeros((N_EXPERTS, tm), f32)
    for _ in range(TOP_K):
        best = jnp.max(cand, axis=0, keepdims=True)
        pick = jnp.min(jnp.where(cand == best, eid, N_EXPERTS), axis=0, keepdims=True)
        hit = eid == pick
        selb = jnp.where(hit, 1.0, selb)
        cand = jnp.where(hit, ninf, cand)
    wsel = selb * scores
    wn = wsel / jnp.sum(wsel, axis=0, keepdims=True) * ROUTED_SCALE

    maskb = selb.astype(bf16)
    pos = jnp.dot(maskb, tri_ref[...], preferred_element_type=f32) + cnt_ref[:, 0:1]
    cnt_ref[...] = cnt_ref[...] + jnp.sum(selb, axis=1, keepdims=True)
    slot = jnp.dot(trie_ref[...], maskb, preferred_element_type=f32)
    ioe = lax.broadcasted_iota(jnp.int32, (N_EXPERTS, tm), 0).astype(f32)
    e_rows, p_rows = [], []
    for k in range(TOP_K):
        hit = jnp.where(slot == k, selb, 0.0)
        e_rows.append(jnp.sum(hit * ioe, axis=0, keepdims=True))
        p_rows.append(jnp.sum(hit * pos, axis=0, keepdims=True))
    e8_ref[...] = jnp.concatenate(e_rows, axis=0).astype(jnp.int32)
    pos8_ref[...] = jnp.concatenate(p_rows, axis=0).astype(jnp.int32)
    wtok_ref[...] = jnp.concatenate([wn.T, jnp.zeros((tm, LANES - N_EXPERTS), f32)], axis=1)


def _router(h, w_router, router_bias, tm=512):
    n = h.shape[0]
    wt = w_router.astype(f32).T
    wt_hi = wt.astype(bf16)
    wrt = jnp.stack([wt_hi, (wt - wt_hi.astype(f32)).astype(bf16)])
    bias = router_bias.astype(f32).reshape(N_EXPERTS, 1)
    tri = (jnp.arange(tm)[:, None] < jnp.arange(tm)[None, :]).astype(bf16)
    trie = (jnp.arange(N_EXPERTS)[None, :] < jnp.arange(N_EXPERTS)[:, None]).astype(bf16)
    full = lambda a: pl.BlockSpec(a.shape, lambda i: (0,) * a.ndim)
    tok = lambda: pl.BlockSpec((TOP_K, tm), lambda i: (0, i))
    return pl.pallas_call(
        _router_kernel,
        grid=(n // tm,),
        in_specs=[pl.BlockSpec((tm, D_MODEL), lambda i: (i, 0)), full(wrt), full(bias), full(tri), full(trie)],
        out_specs=[tok(), tok(), pl.BlockSpec((tm, LANES), lambda i: (i, 0)),
                   pl.BlockSpec((N_EXPERTS, LANES), lambda i: (0, 0))],
        out_shape=[jax.ShapeDtypeStruct((TOP_K, n), jnp.int32), jax.ShapeDtypeStruct((TOP_K, n), jnp.int32),
                   jax.ShapeDtypeStruct((n, LANES), f32), jax.ShapeDtypeStruct((N_EXPERTS, LANES), f32)],
        compiler_params=_params("arbitrary"),
        name="router",
    )(h, wrt, bias, tri, trie)


INVERT_UNROLL = 16


def _invert_kernel(dest_ref, nb_ref, code_ref):
    n_rows = code_ref.shape[0]

    def fill(first_group, last_group):
        def body(k, c):
            for u in range(INVERT_UNROLL):
                code_ref[k * INVERT_UNROLL + u] = PAD_CODE
            return c
        lax.fori_loop(first_group, last_group, body, 0)

    def per_expert(e, c):
        groups = MOE_ROWS // INVERT_UNROLL + 1
        start = jnp.minimum(nb_ref[1 + e], n_rows - groups * INVERT_UNROLL) // INVERT_UNROLL
        fill(start, start + groups)
        return c
    lax.fori_loop(0, N_EXPERTS, per_expert, 0)
    fill(nb_ref[0] * (MOE_ROWS // INVERT_UNROLL), n_rows // INVERT_UNROLL)

    def body(p, c):
        code_ref[dest_ref[p]] = p
        return c
    lax.fori_loop(0, dest_ref.shape[0], body, 0, unroll=INVERT_UNROLL)


def _invert(dest, nb, n_rows):
    smem = lambda: pl.BlockSpec(memory_space=pltpu.SMEM)
    return pl.pallas_call(
        _invert_kernel,
        in_specs=[smem(), smem()],
        out_specs=smem(),
        out_shape=jax.ShapeDtypeStruct((n_rows,), jnp.int32),
        name="invert_dispatch",
    )(dest, nb)


def _expert_changed(be_ref, i):
    return jnp.logical_or(i == 0, be_ref[i] != be_ref[jnp.maximum(i - 1, 0)])


SCATTER_UNROLL = 8
Y_PITCH = MOE_ROWS + 4


def _moe_up_kernel(be_ref, code_ref, nb_ref, hp_ref, wtok_ref, wg_ref, wu_ref, mid_ref,
                   xg_scr, xa_scr, xb_scr, wra_scr, wrb_scr, wgu_scr, wstage_scr, wsem):
    i = pl.program_id(0)
    rows = MOE_ROWS

    def gather_block(blk, x_dst, wrow_dst):
        base = blk * rows
        for r in range(rows):
            tok = (code_ref[base + r] >> 3) & (N_TOKENS - 1)
            xg_scr[r * PACK_ROWS:(r + 1) * PACK_ROWS, :] = (
                hp_ref[pl.ds(pl.multiple_of(tok * PACK_ROWS, PACK_ROWS), PACK_ROWS), :])
            wrow_dst[r:r + 1, :] = wtok_ref[pl.ds(tok, 1), :]
        half = D_MODEL // 2
        chunks = pltpu.einshape("rsl->srl", xg_scr[...].reshape(rows, PACK_ROWS, LANES))
        for s in range(PACK_ROWS):
            wds = chunks[s]
            lo = pltpu.unpack_elementwise(wds, index=0, packed_dtype=bf16, unpacked_dtype=f32)
            hi = pltpu.unpack_elementwise(wds, index=1, packed_dtype=bf16, unpacked_dtype=f32)
            x_dst[:, s * LANES:(s + 1) * LANES] = lo.astype(bf16)
            x_dst[:, half + s * LANES:half + (s + 1) * LANES] = hi.astype(bf16)

    def weight_copies(e, s):
        return (pltpu.make_async_copy(wg_ref.at[e], wstage_scr.at[s, 0], wsem.at[s, 0]),
                pltpu.make_async_copy(wu_ref.at[e], wstage_scr.at[s, 1], wsem.at[s, 1]))

    def expert_block(blk, out_rows, x_cur, wrow_cur, x_nxt, wrow_nxt):
        last = MOE_BLOCKS - 1
        expert = be_ref[jnp.minimum(blk, last)]
        slot = nb_ref[NB_SLOT + expert]
        upcoming = nb_ref[NB_NEXT + expert]
        valid = blk < nb_ref[0]

        @pl.when(jnp.logical_and(valid, blk == 0))
        def _():
            for cp in weight_copies(expert, slot):
                cp.start()

        @pl.when(jnp.logical_and(valid, _expert_changed(be_ref, jnp.minimum(blk, last))))
        def _():
            for cp in weight_copies(expert, slot):
                cp.wait()

            @pl.when(upcoming >= 0)
            def _():
                for cp in weight_copies(upcoming, 1 - slot):
                    cp.start()

            wgu_scr[:, :D_EXPERT] = wstage_scr[slot, 0].astype(bf16)
            wgu_scr[:, D_EXPERT:] = wstage_scr[slot, 1].astype(bf16)

        @pl.when(valid)
        def _():
            gather_block(jnp.minimum(blk + 1, last), x_nxt, wrow_nxt)
            lane = lax.broadcasted_iota(jnp.int32, (rows, LANES), 1)
            w = jnp.sum(jnp.where(lane == expert, wrow_cur[...], 0.0), axis=1, keepdims=True)
            real = blk * rows + lax.broadcasted_iota(jnp.int32, (rows, 1), 0) < nb_ref[1 + expert]
            gu = jnp.dot(x_cur[...], wgu_scr[...], preferred_element_type=f32)
            g, u = gu[:, :D_EXPERT], gu[:, D_EXPERT:]
            mid_ref[out_rows, :] = (g * jax.nn.sigmoid(g) * u * jnp.where(real, w, 0.0)).astype(bf16)

        @pl.when(jnp.logical_not(valid))
        def _():
            mid_ref[out_rows, :] = jnp.zeros((rows, D_EXPERT), bf16)

    @pl.when(i == 0)
    def _():
        gather_block(0, xa_scr, wra_scr)

    expert_block(2 * i, slice(0, rows), xa_scr, wra_scr, xb_scr, wrb_scr)
    expert_block(2 * i + 1, slice(rows, 2 * rows), xb_scr, wrb_scr, xa_scr, wra_scr)


def _moe_up(block_e, code, nb, hp, wtok, w_gate, w_up):
    grid_spec = pltpu.PrefetchScalarGridSpec(
        num_scalar_prefetch=3,
        grid=(MOE_BLOCKS // 2,),
        in_specs=[pl.BlockSpec(memory_space=pltpu.VMEM),
                  pl.BlockSpec(memory_space=pltpu.VMEM),
                  pl.BlockSpec(memory_space=pl.ANY),
                  pl.BlockSpec(memory_space=pl.ANY)],
        out_specs=pl.BlockSpec((2 * MOE_ROWS, D_EXPERT), lambda i, be, cd, nb: (i, 0)),
        scratch_shapes=[pltpu.VMEM((PACK_ROWS * MOE_ROWS, LANES), jnp.int32),
                        pltpu.VMEM((MOE_ROWS, D_MODEL), bf16), pltpu.VMEM((MOE_ROWS, D_MODEL), bf16),
                        pltpu.VMEM((MOE_ROWS, LANES), f32), pltpu.VMEM((MOE_ROWS, LANES), f32),
                        pltpu.VMEM((D_MODEL, 2 * D_EXPERT), bf16),
                        pltpu.VMEM((2, 2, D_MODEL, D_EXPERT), f32),
                        pltpu.SemaphoreType.DMA((2, 2))])
    return pl.pallas_call(
        _moe_up_kernel,
        grid_spec=grid_spec,
        out_shape=jax.ShapeDtypeStruct((MOE_BLOCKS * MOE_ROWS, D_EXPERT), bf16),
        compiler_params=pltpu.CompilerParams(dimension_semantics=("arbitrary",),
                                             vmem_limit_bytes=MOE_UP_VMEM_LIMIT),
        name="moe_up",
    )(block_e, code, nb, hp, wtok, w_gate, w_up)


def _moe_down_kernel(be_ref, code_ref, nb_ref, mida_ref, midb_ref, wd_ref, acc_ref,
                     ya_scr, yb0_scr, yb1_scr, wa_scr, wb_scr, wstage_scr, wsem):
    j = pl.program_id(1)
    rows = MOE_ROWS
    nb = nb_ref[0]
    blk_a, blk_b = 2 * j, 2 * j + 1
    last = MOE_BLOCKS - 1

    @pl.when(j == 0)
    def _():
        acc_ref[...] = jnp.zeros_like(acc_ref)

    def scatter_rows(base, ybuf, r0, n):
        sums, addrs = [], []
        for k in range(n):
            a = pl.multiple_of(code_ref[base + r0 + k] & -PACK_ROWS, PACK_ROWS)
            v = ybuf[pl.ds(r0 + k, PACK_ROWS, stride=Y_PITCH), :]
            sums.append(acc_ref[pl.ds(a, PACK_ROWS), :] + v)
            addrs.append(a)
        for k in range(n):
            acc_ref[pl.ds(addrs[k], PACK_ROWS), :] = sums[k]

    def scatter_block(blk, ybuf):
        for g in range(rows // SCATTER_UNROLL):
            scatter_rows(blk * rows, ybuf, g * SCATTER_UNROLL, SCATTER_UNROLL)

    def scatter_block_compact(blk, ybuf):
        def body(g, c):
            scatter_rows(blk * rows, ybuf, g * SCATTER_UNROLL, SCATTER_UNROLL)
            return c
        lax.fori_loop(0, rows // SCATTER_UNROLL, body, 0)

    def down_block(mid_ref, w_scr, ybuf):
        y = jnp.dot(mid_ref[...], w_scr[...], preferred_element_type=f32)
        for c in range(PACK_ROWS):
            ybuf[c * Y_PITCH:c * Y_PITCH + rows, :] = y[:, c * LANES:(c + 1) * LANES]

    half = D_MODEL // 2
    cols = pl.ds(pl.multiple_of(pl.program_id(0) * half, half), half)

    def weight_copy(e, s):
        return pltpu.make_async_copy(wd_ref.at[e, :, cols], wstage_scr.at[s], wsem.at[s])

    def refresh_weights(blk, w_scr):
        cur = be_ref[jnp.minimum(blk, last)]
        slot = nb_ref[NB_SLOT + cur]
        upcoming = nb_ref[NB_NEXT + cur]
        valid = blk < nb

        @pl.when(jnp.logical_and(valid, blk == 0))
        def _():
            weight_copy(cur, slot).start()

        @pl.when(jnp.logical_and(valid, _expert_changed(be_ref, jnp.minimum(blk, last))))
        def _():
            weight_copy(cur, slot).wait()

            @pl.when(upcoming >= 0)
            def _():
                weight_copy(upcoming, 1 - slot).start()

        prev = be_ref[jnp.clip(blk - 2, 0, last)]

        @pl.when(jnp.logical_and(valid, jnp.logical_or(j == 0, cur != prev)))
        def _():
            w_scr[...] = wstage_scr[slot].astype(bf16)

    refresh_weights(blk_a, wa_scr)
    refresh_weights(blk_b, wb_scr)

    for parity, yb_this, yb_prev in ((0, yb0_scr, yb1_scr), (1, yb1_scr, yb0_scr)):
        mine = j % 2 == parity

        @pl.when(jnp.logical_and(mine, jnp.logical_and(j > 0, blk_b < nb)))
        def _():
            down_block(mida_ref, wa_scr, ya_scr)
            down_block(midb_ref, wb_scr, yb_this)
            scatter_block(blk_a - 1, yb_prev)
            scatter_block(blk_a, ya_scr)

        @pl.when(jnp.logical_and(mine, jnp.logical_and(j > 0, blk_b == nb)))
        def _():
            down_block(mida_ref, wa_scr, ya_scr)
            scatter_block_compact(blk_a - 1, yb_prev)
            scatter_block_compact(blk_a, ya_scr)

        @pl.when(jnp.logical_and(mine, jnp.logical_and(j > 0, blk_a == nb)))
        def _():
            scatter_block_compact(blk_a - 1, yb_prev)

    @pl.when(jnp.logical_and(j == 0, blk_a < nb))
    def _():
        down_block(mida_ref, wa_scr, ya_scr)
        scatter_block_compact(blk_a, ya_scr)

    @pl.when(jnp.logical_and(j == 0, blk_b < nb))
    def _():
        down_block(midb_ref, wb_scr, yb0_scr)


def _moe_down(block_e, code, nb, mid, w_down):
    half = D_MODEL // 2
    acc_rows = (N_TOKENS + 1) * PACK_ROWS
    last = MOE_BLOCKS - 1
    assert MOE_BLOCKS % 2 == 0
    blk = lambda off: (lambda p, j, be, cd, nb: (jnp.minimum(2 * j + off, last), 0))
    grid_spec = pltpu.PrefetchScalarGridSpec(
        num_scalar_prefetch=3,
        grid=(2, MOE_BLOCKS // 2 + 1),
        in_specs=[pl.BlockSpec((MOE_ROWS, D_EXPERT), blk(0)), pl.BlockSpec((MOE_ROWS, D_EXPERT), blk(1)),
                  pl.BlockSpec(memory_space=pl.ANY)],
        out_specs=pl.BlockSpec((None, acc_rows, LANES), lambda p, j, be, cd, nb: (p, 0, 0),
                               pipeline_mode=pl.Buffered(1)),
        scratch_shapes=[pltpu.VMEM((PACK_ROWS * Y_PITCH, LANES), f32)] * 3
        + [pltpu.VMEM((D_EXPERT, half), bf16)] * 2
        + [pltpu.VMEM((2, D_EXPERT, half), f32), pltpu.SemaphoreType.DMA((2,))])
    return pl.pallas_call(
        _moe_down_kernel,
        grid_spec=grid_spec,
        out_shape=jax.ShapeDtypeStruct((2, acc_rows, LANES), f32),
        compiler_params=_params("arbitrary", "arbitrary"),
        name="moe_down",
    )(block_e, code, nb, mid, mid, w_down)


def _final_kernel(h_ref, r0_ref, r1_ref, wgu_ref, wd_ref, lg_ref, lb_ref, o_ref):
    tm = h_ref.shape[0]
    gu = jnp.dot(h_ref[...].astype(bf16), wgu_ref[...], preferred_element_type=f32)
    g, u = gu[:, :D_EXPERT], gu[:, D_EXPERT:]
    mid = (g * jax.nn.sigmoid(g) * u).astype(bf16)
    shared = jnp.dot(mid, wd_ref[...], preferred_element_type=f32)
    halves = [pltpu.einshape("rcl->crl", r[...].reshape(tm, PACK_ROWS, LANES)) for r in (r0_ref, r1_ref)]
    routed = jnp.concatenate([hv[c] for hv in halves for c in range(PACK_ROWS)], axis=1)
    o_ref[...] = _layer_norm_rows(DEEPNORM_ALPHA * h_ref[...] + routed + shared,
                                  lg_ref[...], lb_ref[...])


def _final(h, racc, wgu, wd, ln_g, ln_b, tm=256):
    n = h.shape[0]
    row = lambda: pl.BlockSpec((tm, D_MODEL), lambda i: (i, 0))
    full = lambda a: pl.BlockSpec(a.shape, lambda i: (0,) * a.ndim)
    acc = lambda p: pl.BlockSpec((None, tm * PACK_ROWS, LANES), lambda i: (p, i, 0))
    return pl.pallas_call(
        _final_kernel,
        grid=(n // tm,),
        in_specs=[row(), acc(0), acc(1), full(wgu), full(wd), full(ln_g), full(ln_b)],
        out_specs=row(),
        out_shape=jax.ShapeDtypeStruct((n, D_MODEL), f32),
        compiler_params=_params("parallel"),
        name="shared_final",
    )(h, racc, racc, wgu, wd, ln_g, ln_b)


def _dispatch_plan(e8, pos8, cnt):
    counts = cnt[:, 0].astype(jnp.int32)
    padded = (counts + MOE_ROWS - 1) // MOE_ROWS * MOE_ROWS
    pad_end = jnp.cumsum(padded).astype(jnp.int32)
    pad_start = pad_end - padded
    seg_end = pad_start + counts
    ids = jnp.arange(N_EXPERTS, dtype=jnp.int32)
    start8 = jnp.sum(jnp.where(e8[..., None] == ids, pad_start, 0), axis=-1)
    dest = (start8 + pos8).T.reshape(-1)
    block_start = jnp.arange(MOE_BLOCKS, dtype=jnp.int32) * MOE_ROWS
    block_e = jnp.minimum(jnp.sum((pad_end[None, :] <= block_start[:, None]).astype(jnp.int32), axis=1),
                          N_EXPERTS - 1)
    used = counts > 0
    slot = (jnp.cumsum(used.astype(jnp.int32)) - 1) % 2
    later = jnp.where(used, ids, N_EXPERTS)
    nxt = jnp.concatenate([lax.cummin(later[::-1])[::-1][1:], jnp.full((1,), N_EXPERTS, jnp.int32)])
    nxt = jnp.where(nxt == N_EXPERTS, -1, nxt)
    nb = jnp.concatenate([pad_end[-1:] // MOE_ROWS, seg_end, slot, nxt]).astype(jnp.int32)
    return dest, block_e, nb


def kernel(x, w_in, ssm_log_dt, ssm_a_re, ssm_a_im, ssm_b_re, ssm_b_im, ssm_c_re, ssm_c_im, ssm_d,
           w_glu, g_ssm_out, g_attn_out, w_out, ln1_g, ln1_b, w_router, router_bias, w_gate, w_up,
           w_down, ws_gate, ws_up, ws_down, ln2_g, ln2_b):
    bsz, seq, d = x.shape
    n_tok = bsz * seq
    h = x.reshape(n_tok, d)
    for layer in range(DEPTH):
        proj = _matmul(h, w_in[layer].astype(bf16), f32)

        tables = _s5_tables(ssm_log_dt[layer], ssm_a_re[layer], ssm_a_im[layer], ssm_b_re[layer],
                            ssm_b_im[layer], ssm_c_re[layer], ssm_c_im[layer], ssm_d[layer])
        y = _s5_mixer(proj, tables, bsz, seq // S5_CHUNK)

        y_attn = _dilated_attention(proj, D_SSM // (2 * HEAD_DIM), bsz, seq)

        row2 = lambda a: a.astype(f32).reshape(1, -1)
        h, hp = _mix_out(y, y_attn, h, w_glu[layer].astype(bf16), w_out[layer].astype(bf16),
                         row2(g_ssm_out[layer]), row2(g_attn_out[layer]), row2(ln1_g[layer]), row2(ln1_b[layer]))

        assert n_tok == N_TOKENS
        e8, pos8, wtok, cnt = _router(h, w_router[layer], router_bias[layer])
        dest, block_e, nb = _dispatch_plan(e8, pos8, cnt)
        code = _invert(dest, nb, MOE_BLOCKS * MOE_ROWS)
        mid = _moe_up(block_e, code, nb, hp, wtok, w_gate[layer], w_up[layer])
        racc = _moe_down(block_e, code, nb, mid, w_down[layer])
        wgu = jnp.concatenate([ws_gate[layer], ws_up[layer]], axis=1).astype(bf16)
        h = _final(h, racc, wgu, ws_down[layer].astype(bf16), row2(ln2_g[layer]), row2(ln2_b[layer]))
    return h.reshape(bsz, seq, d)
```
